```python
import jax
import jax.numpy as jnp
from jax import lax
import numpy as np

D_MODEL = 1024
BATCH = 8
SEQ = 2048
DEPTH = 1

GRID_W = 64
CTX_LEN = 256
EPS = 1e-6

GLA_HEADS = 4
GLA_DK = 64
GLA_DV = 128
GLA_RANK = 16
GLA_TAU = 16.0
GLA_CHUNK = 64
GLA_QK_W = GLA_HEADS * GLA_DK
GLA_V_W = GLA_HEADS * GLA_DV

MLA_HEADS = 8
MLA_Q_RANK = 256
MLA_KV_RANK = 128
MLA_NOPE = 64
MLA_ROPE = 32
MLA_V = 64
MLA_QK = MLA_NOPE + MLA_ROPE
MLA_SCALE = MLA_QK ** -0.5
ROPE_BASE = 10000.0
Q_BLOCK = 128

N_EXPERTS = 32
TOP_K = 4
D_FF = 1024
SWIGLU_LIMIT = 7.0
SWIGLU_ALPHA = 1.702

IN_SPLITS = (GLA_QK_W, GLA_QK_W, GLA_V_W, GLA_V_W, GLA_RANK, GLA_RANK,
             MLA_Q_RANK, MLA_KV_RANK, MLA_ROPE, D_MODEL, D_MODEL)
D_IN = sum(IN_SPLITS)

kernel_name = "hybrid_gla_mla_moe_diffusion_block"


def _rmsnorm(x, g):
    xf = x.astype(jnp.float32)
    y = xf * lax.rsqrt(jnp.mean(xf * xf, axis=-1, keepdims=True) + EPS)
    return (y * g.astype(jnp.float32)).astype(x.dtype)


def _modulate(h, shift, scale):
    return h * (1.0 + scale) + shift


def _heads(t, n_heads):
    b, t_len, _ = t.shape
    return t.reshape(b, t_len, n_heads, -1).transpose(0, 2, 1, 3)


def _merge_heads(t):
    b, h, t_len, d = t.shape
    return t.transpose(0, 2, 1, 3).reshape(b, t_len, h * d)


def _flip(t):
    return t[:, :, ::-1]


def _axial_rope_angles(rows):
    r, col = jnp.meshgrid(jnp.arange(rows, dtype=jnp.float32),
                          jnp.arange(GRID_W, dtype=jnp.float32), indexing="ij")
    half = MLA_ROPE // 2
    inv_freq = ROPE_BASE ** (-jnp.arange(0, half, 2, dtype=jnp.float32) / half)
    return r.reshape(-1)[:, None] * inv_freq, col.reshape(-1)[:, None] * inv_freq


def _rotate(v, ang):
    v1, v2 = jnp.split(v, 2, axis=-1)
    cos = jnp.cos(ang).astype(v.dtype)
    sin = jnp.sin(ang).astype(v.dtype)
    return jnp.concatenate([v1 * cos - v2 * sin, v2 * cos + v1 * sin], axis=-1)


def _axial_rope(v, ang_row, ang_col):
    v_row, v_col = jnp.split(v, 2, axis=-1)
    return jnp.concatenate([_rotate(v_row, ang_row), _rotate(v_col, ang_col)], axis=-1)


def _gla_chunked(q, k, v, log_a, s0):
    b, h, t_len, dk = q.shape
    dv = v.shape[-1]
    n = t_len // GLA_CHUNK

    def rs(t):
        return t.reshape(b, h, n, GLA_CHUNK, t.shape[-1]).astype(jnp.float32)

    qc, kc, vc, la = rs(q), rs(k), rs(v), rs(log_a)
    cum = jnp.cumsum(la, axis=3)
    cum_last = cum[:, :, :, -1:, :]
    q_dec = qc * jnp.exp(cum)
    k_inv = kc * jnp.exp(-cum)
    k_to_end = kc * jnp.exp(cum_last - cum)
    mask = jnp.tril(jnp.ones((GLA_CHUNK, GLA_CHUNK), dtype=bool))
    att = jnp.where(mask, jnp.einsum("bhncd,bhnsd->bhncs", q_dec, k_inv), 0.0)
    o_intra = jnp.einsum("bhncs,bhnse->bhnce", att, vc)
    ds = jnp.einsum("bhncd,bhnce->bhnde", k_to_end, vc)
    decay = jnp.exp(cum_last[:, :, :, 0, :])

    def step(s, inp):
        d, dsn = inp
        return d[..., None] * s + dsn, s

    s_final, s_in = lax.scan(step, s0, (jnp.moveaxis(decay, 2, 0), jnp.moveaxis(ds, 2, 0)))
    s_in = jnp.moveaxis(s_in, 0, 2)
    o_inter = jnp.einsum("bhncd,bhnde->bhnce", q_dec, s_in)
    o = (o_intra + o_inter).reshape(b, h, t_len, dv)
    return o.astype(v.dtype), s_final


def _block_attention(q, k, v):
    b, h, t_len, dq = q.shape
    nb = t_len // Q_BLOCK
    qb = q.reshape(b, h, nb, Q_BLOCK, dq).transpose(2, 0, 1, 3, 4)

    def one_block(qi):
        s = jnp.einsum("bhqd,bhkd->bhqk", qi, k).astype(jnp.float32) * MLA_SCALE
        p = jax.nn.softmax(s, axis=-1)
        return jnp.einsum("bhqk,bhkd->bhqd", p.astype(v.dtype), v)

    o = lax.map(one_block, qb)
    return o.transpose(1, 2, 0, 3, 4).reshape(b, h, t_len, v.shape[-1])


def _mixer_features(z, p, ang):
    offsets = np.cumsum(IN_SPLITS)[:-1].tolist()
    (zq, zk, zv, zg, za_f, za_b, zdq, zdkv, zkr, zm_gla, zm_mla) = jnp.split(z, offsets, axis=-1)
    b, t_len, _ = z.shape
    gq = _heads(zq, GLA_HEADS) * (GLA_DK ** -0.5)
    gk = _heads(zk, GLA_HEADS)
    gv = _heads(zv, GLA_HEADS)
    la_f = _heads(jax.nn.log_sigmoid((za_f @ p["gla_w_a2_f"] + p["gla_b_a_f"]).astype(jnp.float32)) / GLA_TAU, GLA_HEADS)
    la_b = _heads(jax.nn.log_sigmoid((za_b @ p["gla_w_a2_b"] + p["gla_b_a_b"]).astype(jnp.float32)) / GLA_TAU, GLA_HEADS)
    mq = _heads(_rmsnorm(zdq, p["mla_g_q"]) @ p["mla_w_uq"], MLA_HEADS)
    q_nope, q_rope = mq[..., :MLA_NOPE], mq[..., MLA_NOPE:]
    ckv = _rmsnorm(zdkv, p["mla_g_kv"])
    k_nope = _heads(ckv @ p["mla_w_uk"], MLA_HEADS)
    mv = _heads(ckv @ p["mla_w_uv"], MLA_HEADS)
    k_rope = zkr[:, None]
    if ang is not None:
        q_rope = _axial_rope(q_rope, *ang)
        k_rope = _axial_rope(k_rope, *ang)
    mq = jnp.concatenate([q_nope, q_rope], axis=-1)
    mk = jnp.concatenate([k_nope, jnp.broadcast_to(k_rope, (b, MLA_HEADS, t_len, MLA_ROPE))], axis=-1)
    return {"gq": gq, "gk": gk, "gv": gv, "la_f": la_f, "la_b": la_b, "zg": zg,
            "mq": mq, "mk": mk, "mv": mv, "zm_gla": zm_gla, "zm_mla": zm_mla}


def _mixer_output(o_gla, o_mla, f, p):
    og = _merge_heads(_rmsnorm(o_gla, p["gla_g_norm"])) * jax.nn.silu(f["zg"])
    br_gla = og @ p["w_br_gla"]
    br_mla = _merge_heads(o_mla) @ p["w_br_mla"]
    merged = jax.nn.sigmoid(f["zm_gla"]) * br_gla + jax.nn.sigmoid(f["zm_mla"]) * br_mla
    return merged @ p["w_out"]


def _moe(h, p):
    b, t_len, d = h.shape
    tok = h.reshape(-1, d)
    logits = (tok @ p["router_w"] + p["router_b"]).astype(jnp.float32)
    top_v, top_i = lax.top_k(logits, TOP_K)
    wts = jax.nn.softmax(top_v, axis=-1)
    combine = jnp.sum(jax.nn.one_hot(top_i, N_EXPERTS, dtype=jnp.float32) * wts[..., None], axis=1)
    combine = combine.astype(tok.dtype)
    out = jnp.zeros_like(tok)
    for e in range(N_EXPERTS):
        gate = jnp.minimum(tok @ p["w_gate"][e] + p["b_gate"][e], SWIGLU_LIMIT)
        up = jnp.clip(tok @ p["w_up"][e] + p["b_up"][e], -SWIGLU_LIMIT, SWIGLU_LIMIT)
        act = (up + 1.0) * gate * jax.nn.sigmoid(SWIGLU_ALPHA * gate)
        out = out + combine[:, e:e + 1] * (act @ p["w_down"][e] + p["b_down"][e])
    return out.reshape(b, t_len, d)


def _layer(x, ctx, mod_x, mod_c, p, ang, ctx_out):
    sh_a, sc_a, gt_a, sh_f, sc_f, gt_f = mod_x
    csh_a, csc_a, cgt_a, csh_f, csc_f, cgt_f = mod_c
    hx = _modulate(_rmsnorm(x, p["g_pre_mix"]), sh_a, sc_a)
    hc = _modulate(_rmsnorm(ctx, p["g_pre_mix"]), csh_a, csc_a)
    fx = _mixer_features(hx @ p["w_in"], p, ang)
    fc = _mixer_features(hc @ p["w_in"], p, None)
    s0 = jnp.zeros((x.shape[0], GLA_HEADS, GLA_DK, GLA_DV), jnp.float32)
    oc_fwd, s_ctx_fwd = _gla_chunked(fc["gq"], fc["gk"], fc["gv"], fc["la_f"], s0)
    ox_fwd, _ = _gla_chunked(fx["gq"], fx["gk"], fx["gv"], fx["la_f"], s_ctx_fwd)
    oc_bwd, s_ctx_bwd = _gla_chunked(_flip(fc["gq"]), _flip(fc["gk"]), _flip(fc["gv"]), _flip(fc["la_b"]), s0)
    ox_bwd, _ = _gla_chunked(_flip(fx["gq"]), _flip(fx["gk"]), _flip(fx["gv"]), _flip(fx["la_b"]), s_ctx_bwd)
    ox_gla = ox_fwd + _flip(ox_bwd)
    k_all = jnp.concatenate([fx["mk"], fc["mk"]], axis=2)
    v_all = jnp.concatenate([fx["mv"], fc["mv"]], axis=2)
    ox_mla = _block_attention(fx["mq"], k_all, v_all)
    x = x + gt_a * _rmsnorm(_mixer_output(ox_gla, ox_mla, fx, p), p["g_post_mix"])
    hx = _modulate(_rmsnorm(x, p["g_pre_ffn"]), sh_f, sc_f)
    x = x + gt_f * _rmsnorm(_moe(hx, p), p["g_post_ffn"])
    if ctx_out:
        oc_gla = oc_fwd + _flip(oc_bwd)
        oc_mla = _block_attention(fc["mq"], fc["mk"], fc["mv"])
        ctx = ctx + cgt_a * _rmsnorm(_mixer_output(oc_gla, oc_mla, fc, p), p["g_post_mix"])
        hc = _modulate(_rmsnorm(ctx, p["g_pre_ffn"]), csh_f, csc_f)
        ctx = ctx + cgt_f * _rmsnorm(_moe(hc, p), p["g_post_ffn"])
    return x, ctx


def setup_inputs(seed: int = 0) -> dict:
    key = jax.random.key(seed)
    ks = jax.random.split(key, 32)
    f32 = jnp.float32
    L = DEPTH

    def nrm(k, shape, scale=1.0):
        return jax.random.normal(k, shape, f32) * scale

    def gain(k, n):
        return 1.0 + nrm(k, (L, n), 0.1)

    return {
        "x": nrm(ks[0], (BATCH, SEQ, D_MODEL)),
        "c": nrm(ks[1], (BATCH, D_MODEL)),
        "ctx": nrm(ks[2], (BATCH, CTX_LEN, D_MODEL)),
        "c_ctx": nrm(ks[3], (D_MODEL,)),
        "w_mod": nrm(ks[4], (L, D_MODEL, 6 * D_MODEL), 0.5 * D_MODEL ** -0.5),
        "b_mod": nrm(ks[5], (L, 6 * D_MODEL), 0.02),
        "g_pre_mix": gain(ks[6], D_MODEL),
        "g_post_mix": gain(ks[7], D_MODEL),
        "g_pre_ffn": gain(ks[8], D_MODEL),
        "g_post_ffn": gain(ks[9], D_MODEL),
        "w_in": nrm(ks[10], (L, D_MODEL, D_IN), D_MODEL ** -0.5),
        "gla_w_a2_f": nrm(ks[11], (L, GLA_RANK, GLA_QK_W), GLA_RANK ** -0.5),
        "gla_b_a_f": nrm(ks[12], (L, GLA_QK_W), 0.02),
        "gla_w_a2_b": nrm(ks[13], (L, GLA_RANK, GLA_QK_W), GLA_RANK ** -0.5),
        "gla_b_a_b": nrm(ks[14], (L, GLA_QK_W), 0.02),
        "gla_g_norm": gain(ks[15], GLA_DV),
        "mla_g_q": gain(ks[16], MLA_Q_RANK),
        "mla_w_uq": nrm(ks[17], (L, MLA_Q_RANK, MLA_HEADS * MLA_QK), MLA_Q_RANK ** -0.5),
        "mla_g_kv": gain(ks[18], MLA_KV_RANK),
        "mla_w_uk": nrm(ks[19], (L, MLA_KV_RANK, MLA_HEADS * MLA_NOPE), MLA_KV_RANK ** -0.5),
        "mla_w_uv": nrm(ks[20], (L, MLA_KV_RANK, MLA_HEADS * MLA_V), MLA_KV_RANK ** -0.5),
        "w_br_gla": nrm(ks[21], (L, GLA_V_W, D_MODEL), GLA_V_W ** -0.5),
        "w_br_mla": nrm(ks[22], (L, MLA_HEADS * MLA_V, D_MODEL), (MLA_HEADS * MLA_V) ** -0.5),
        "w_out": nrm(ks[23], (L, D_MODEL, D_MODEL), D_MODEL ** -0.5),
        "router_w": nrm(ks[24], (L, D_MODEL, N_EXPERTS), D_MODEL ** -0.5),
        "router_b": nrm(ks[25], (L, N_EXPERTS), 0.01),
        "w_gate": nrm(ks[26], (L, N_EXPERTS, D_MODEL, D_FF), D_MODEL ** -0.5),
        "b_gate": nrm(ks[27], (L, N_EXPERTS, D_FF), 0.02),
        "w_up": nrm(ks[28], (L, N_EXPERTS, D_MODEL, D_FF), D_MODEL ** -0.5),
        "b_up": nrm(ks[29], (L, N_EXPERTS, D_FF), 0.02),
        "w_down": nrm(ks[30], (L, N_EXPERTS, D_FF, D_MODEL), D_FF ** -0.5),
        "b_down": nrm(ks[31], (L, N_EXPERTS, D_MODEL), 0.02),
    }


def reference(x, c, ctx, c_ctx, w_mod, b_mod, g_pre_mix, g_post_mix, g_pre_ffn, g_post_ffn,
              w_in, gla_w_a2_f, gla_b_a_f, gla_w_a2_b, gla_b_a_b, gla_g_norm,
              mla_g_q, mla_w_uq, mla_g_kv, mla_w_uk, mla_w_uv,
              w_br_gla, w_br_mla, w_out, router_w, router_b,
              w_gate, b_gate, w_up, b_up, w_down, b_down):
    ROWS = x.shape[1] // GRID_W
    ang = _axial_rope_angles(ROWS)
    for l in range(DEPTH):
        p = {
            "g_pre_mix": g_pre_mix[l], "g_post_mix": g_post_mix[l],
            "g_pre_ffn": g_pre_ffn[l], "g_post_ffn": g_post_ffn[l],
            "w_in": w_in[l],
            "gla_w_a2_f": gla_w_a2_f[l], "gla_b_a_f": gla_b_a_f[l],
            "gla_w_a2_b": gla_w_a2_b[l], "gla_b_a_b": gla_b_a_b[l],
            "gla_g_norm": gla_g_norm[l],
            "mla_g_q": mla_g_q[l], "mla_w_uq": mla_w_uq[l],
            "mla_g_kv": mla_g_kv[l], "mla_w_uk": mla_w_uk[l], "mla_w_uv": mla_w_uv[l],
            "w_br_gla": w_br_gla[l], "w_br_mla": w_br_mla[l], "w_out": w_out[l],
            "router_w": router_w[l], "router_b": router_b[l],
            "w_gate": w_gate[l], "b_gate": b_gate[l], "w_up": w_up[l], "b_up": b_up[l],
            "w_down": w_down[l], "b_down": b_down[l],
        }
        mod_x = jnp.split((jax.nn.silu(c) @ w_mod[l] + b_mod[l])[:, None, :], 6, axis=-1)
        mod_c = jnp.split(jax.nn.silu(c_ctx) @ w_mod[l] + b_mod[l], 6, axis=-1)
        x, ctx = _layer(x, ctx, mod_x, mod_c, p, ang, l < DEPTH - 1)
    return x
```

```python
import functools

import jax
import jax.numpy as jnp
import numpy as np
from jax import lax
from jax.experimental import pallas as pl
from jax.experimental.pallas import tpu as pltpu

F32 = jnp.float32
BF16 = jnp.bfloat16

D = 1024
EPS = 1e-6
GRID_W = 64

GLA_H = 4
GLA_DK = 64
GLA_DV = 128
GLA_RANK = 16
GLA_TAU = 16.0
GLA_CHUNK = 64
GLA_QK = GLA_H * GLA_DK
GLA_V = GLA_H * GLA_DV

MLA_H = 8
MLA_QR = 256
MLA_KVR = 128
MLA_NOPE = 64
MLA_ROPE = 32
MLA_V = 64
MLA_QKD = MLA_NOPE + MLA_ROPE
MLA_SCALE = MLA_QKD ** -0.5
ROPE_BASE = 10000.0
HEAD_SLOT = 128
MLA_W = MLA_H * HEAD_SLOT

N_EXP = 32
TOP_K = 4
D_FF = 1024
SWIGLU_LIMIT = 7.0
SWIGLU_ALPHA = 1.702

LANES = 128
TM_PROJ = 512
TQ = 512
TM_FFN = 512
TM_ROW = 256

_OFF = np.cumsum([0, GLA_QK, GLA_QK, GLA_V, GLA_V, GLA_RANK, GLA_RANK,
                  MLA_QR, MLA_KVR, MLA_ROPE, D, D])
(O_Q, O_K, O_V, O_G, O_AF, O_AB, O_DQ, O_DKV, O_KR, O_MG, O_MM, _) = _OFF.tolist()

X_AF, X_AB, X_KR, X_KRS = 0, 16, 32, 64

VMEM_LIMIT = 56 * 1024 * 1024


def _cparams(sem):
    return pltpu.CompilerParams(dimension_semantics=sem, vmem_limit_bytes=VMEM_LIMIT)


def _resident(arr):
    nd = arr.ndim
    return pl.BlockSpec(arr.shape, lambda *_: (0,) * nd, pipeline_mode=pl.Buffered(1))


def _rms(x, g):
    return x * lax.rsqrt(jnp.mean(x * x, axis=-1, keepdims=True) + EPS) * g


def _sigmoid(x):
    return 1.0 / (1.0 + jnp.exp(-x))


def _log_sigmoid(x):
    return jnp.minimum(x, 0.0) - jnp.log1p(jnp.exp(-jnp.abs(x)))


def _dot(a, b):
    return jnp.dot(a, b, preferred_element_type=F32)


def _dot_nt(a, b):
    return lax.dot_general(a, b, (((1,), (1,)), ((), ())), preferred_element_type=F32)


def _dot_tn(a, b):
    return lax.dot_general(a, b, (((0,), (0,)), ((), ())), preferred_element_type=F32)


def _mod_kernel(c_ref, w_ref, b_ref, o_ref):
    c = c_ref[...]
    s = (c * _sigmoid(c)).astype(BF16)
    o_ref[...] = _dot(s, w_ref[...].astype(BF16)) + b_ref[...]


def _mod_call(cc, w_mod, b_mod):
    n = w_mod.shape[1]
    tn = 1536
    return pl.pallas_call(
        _mod_kernel,
        grid=(n // tn,),
        in_specs=[pl.BlockSpec((16, D), lambda j: (0, 0)),
                  pl.BlockSpec((D, tn), lambda j: (0, j)),
                  pl.BlockSpec((1, tn), lambda j: (0, j))],
        out_specs=pl.BlockSpec((16, tn), lambda j: (0, j)),
        out_shape=jax.ShapeDtypeStruct((16, n), F32),
        compiler_params=_cparams(("arbitrary",)),
        name="mod",
    )(cc, w_mod, b_mod.reshape(1, n))


def _inproj_kernel(with_q, x_ref, mod_ref, gpre_ref, w1_ref, wa_ref, ba_ref, wk_ref,
                   wv_ref, gkv_ref, t1_ref, *rest):
    if with_q:
        (w2_ref, gq_ref, wq_ref, cq_ref, sq_ref,
         q_o, k_o, v_o, lf_o, lb_o, mk_o, mv_o, sz_o, mq_o, sg_o, sm_o) = rest
    else:
        (q_o, k_o, v_o, lf_o, lb_o, mk_o, mv_o) = rest
    x = x_ref[...]
    mod = mod_ref[...]
    sh = mod[:, 0:D]
    sc = mod[:, D:2 * D]
    h = (_rms(x, gpre_ref[...]) * (1.0 + sc) + sh).astype(BF16)

    z1 = _dot(h, w1_ref[...])
    q_o[...] = (z1[:, 0:GLA_QK] * (GLA_DK ** -0.5)).astype(BF16)
    k_o[...] = z1[:, GLA_QK:2 * GLA_QK].astype(BF16)
    v_o[...] = z1[:, 2 * GLA_QK:2 * GLA_QK + GLA_V].astype(BF16)
    o_dkv = 2 * GLA_QK + GLA_V
    ckv = _rms(z1[:, o_dkv:o_dkv + MLA_KVR], gkv_ref[...])
    xs = z1[:, o_dkv + MLA_KVR:o_dkv + MLA_KVR + LANES]

    la = _log_sigmoid(_dot(xs.astype(BF16), wa_ref[...]) + ba_ref[...]) * (1.0 / GLA_TAU)
    lf_o[...] = la[:, 0:GLA_QK]
    lb_o[...] = la[:, GLA_QK:2 * GLA_QK]

    lhs_k = jnp.concatenate([ckv, xs * t1_ref[...]], axis=-1).astype(BF16)
    mk_o[...] = _dot(lhs_k, wk_ref[...]).astype(BF16)
    mv_o[...] = _dot(ckv.astype(BF16), wv_ref[...]).astype(BF16)

    if with_q:
        zg = _dot(h, w2_ref[:, 0:GLA_V])
        sz_o[...] = (zg * _sigmoid(zg)).astype(BF16)
        n = _rms(_dot(h, w2_ref[:, GLA_V:GLA_V + MLA_QR]), gq_ref[...]).astype(BF16)
        mq_o[...] = (_dot(n, wq_ref[:, 0:MLA_W]) * cq_ref[...]
                     + _dot(n, wq_ref[:, MLA_W:2 * MLA_W]) * sq_ref[...]).astype(BF16)
        o_mg = GLA_V + MLA_QR
        sg_o[...] = _sigmoid(_dot(h, w2_ref[:, o_mg:o_mg + D])).astype(BF16)
        sm_o[...] = _sigmoid(_dot(h, w2_ref[:, o_mg + D:o_mg + 2 * D])).astype(BF16)


def _inproj_call(with_q, xf, mod3, mod_row_fn, tiles_per_seq, gpre, w1, wa, ba, wk, wv,
                 gkv, t1, extra=()):
    n_tok = xf.shape[0]
    tm = TM_PROJ
    grid = (n_tok // tm,)
    row = lambda i: (i, 0)
    tab = lambda i: (i % tiles_per_seq, 0)
    in_specs = [
        pl.BlockSpec((tm, D), row),
        pl.BlockSpec((None, 1, 6 * D), lambda i: (mod_row_fn(i), 0, 0)),
        _resident(gpre), _resident(w1), _resident(wa), _resident(ba), _resident(wk),
        _resident(wv), _resident(gkv),
        pl.BlockSpec((tm, LANES), tab),
    ]
    widths = [(GLA_QK, BF16), (GLA_QK, BF16), (GLA_V, BF16), (GLA_QK, F32), (GLA_QK, F32),
              (MLA_W, BF16), (MLA_W, BF16)]
    args = [xf, mod3, gpre, w1, wa, ba, wk, wv, gkv, t1]
    if with_q:
        w2, gq, wq, cq, sq = extra
        in_specs += [_resident(w2), _resident(gq), _resident(wq),
                     pl.BlockSpec((tm, MLA_W), tab), pl.BlockSpec((tm, MLA_W), tab)]
        args += [w2, gq, wq, cq, sq]
        widths += [(GLA_V, BF16), (MLA_W, BF16), (D, BF16), (D, BF16)]
    return pl.pallas_call(
        functools.partial(_inproj_kernel, with_q),
        grid=grid,
        in_specs=in_specs,
        out_specs=[pl.BlockSpec((tm, w), row) for w, _ in widths],
        out_shape=[jax.ShapeDtypeStruct((n_tok, w), dt) for w, dt in widths],
        compiler_params=_cparams(("arbitrary",)),
        name="inproj_x" if with_q else "inproj_ctx",
    )(*args)


def _gla_kernel(qx, kx, vx, lfx, lbx, qc, kc, vc, lfc, lbc, o_ref, sf_ref, sb_ref):
    C = GLA_CHUNK
    r = lax.broadcasted_iota(jnp.int32, (C, C), 0)
    c = lax.broadcasted_iota(jnp.int32, (C, C), 1)
    tri_f = c <= r
    tri_b = c >= r
    trif = tri_f.astype(F32)
    trib = tri_b.astype(F32)
    hv = lax.broadcasted_iota(jnp.int32, (GLA_V, GLA_QK), 0) // GLA_DV
    hk = lax.broadcasted_iota(jnp.int32, (GLA_V, GLA_QK), 1) // GLA_DK
    mbd = hv == hk
    lane_head = lax.broadcasted_iota(jnp.int32, (C, GLA_QK), 1) // GLA_DK

    def chunk(q_ref, k_ref, v_ref, la_ref, row0, fwd, s_ref, emit):
        sl = pl.ds(row0, C)
        la = la_ref[sl, :]
        cum = jnp.dot(trif if fwd else trib, la, preferred_element_type=F32,
                      precision=lax.Precision.HIGHEST)
        tot = cum[C - 1:C, :] if fwd else cum[0:1, :]
        q = q_ref[sl, :].astype(F32)
        k = k_ref[sl, :].astype(F32)
        v = v_ref[sl, :]
        ke = (k * jnp.exp(tot - cum)).astype(BF16)
        st = s_ref[...]
        if emit:
            qd = (q * jnp.exp(cum)).astype(BF16)
            ki = (k * jnp.exp(-cum)).astype(BF16)
            o_inter = _dot_nt(qd, st.astype(BF16))
            parts = []
            for hh in range(GLA_H):
                qh = jnp.where(lane_head == hh, qd, jnp.zeros_like(qd))
                att = _dot_nt(qh, ki)
                att = jnp.where(tri_f if fwd else tri_b, att, 0.0).astype(BF16)
                parts.append(_dot(att, v[:, hh * GLA_DV:(hh + 1) * GLA_DV]))
            o_ref[sl, :] += o_inter + jnp.concatenate(parts, axis=-1)
        ds_t = _dot_tn(v, ke)
        s_ref[...] = st * jnp.exp(tot) + jnp.where(mbd, ds_t, 0.0)

    sf_ref[...] = jnp.zeros_like(sf_ref)
    sb_ref[...] = jnp.zeros_like(sb_ref)
    o_ref[...] = jnp.zeros_like(o_ref)
    n_ctx = qc.shape[0] // C
    n_x = qx.shape[0] // C
    for i in range(n_ctx):
        chunk(qc, kc, vc, lfc, i * C, True, sf_ref, False)
        chunk(qc, kc, vc, lbc, (n_ctx - 1 - i) * C, False, sb_ref, False)

    def body(i, carry):
        chunk(qx, kx, vx, lfx, pl.multiple_of(i * C, C), True, sf_ref, True)
        chunk(qx, kx, vx, lbx, pl.multiple_of((n_x - 1 - i) * C, C), False, sb_ref, True)
        return carry

    lax.fori_loop(0, n_x, body, 0)


def _gla_call(fx, fc, batch, seq, ctx_len):
    qx, kx, vx, lfx, lbx = fx
    qc, kc, vc, lfc, lbc = fc

    def spec(rows, w):
        return pl.BlockSpec((rows, w), lambda b: (b, 0))

    return pl.pallas_call(
        _gla_kernel,
        grid=(batch,),
        in_specs=[spec(seq, GLA_QK), spec(seq, GLA_QK), spec(seq, GLA_V), spec(seq, GLA_QK),
                  spec(seq, GLA_QK),
                  spec(ctx_len, GLA_QK), spec(ctx_len, GLA_QK), spec(ctx_len, GLA_V),
                  spec(ctx_len, GLA_QK), spec(ctx_len, GLA_QK)],
        out_specs=spec(seq, GLA_V),
        out_shape=jax.ShapeDtypeStruct((batch * seq, GLA_V), F32),
        scratch_shapes=[pltpu.VMEM((GLA_V, GLA_QK), F32), pltpu.VMEM((GLA_V, GLA_QK), F32)],
        compiler_params=_cparams(("arbitrary",)),
        name="gla",
    )(qx, kx, vx, lfx, lbx, qc, kc, vc, lfc, lbc)


def _mla_kernel(q_ref, kx_ref, vx_ref, kc_ref, vc_ref, o_ref):
    for j in range(MLA_H // 2):
        acc = None
        for hh in (2 * j, 2 * j + 1):
            sl = slice(hh * HEAD_SLOT, (hh + 1) * HEAD_SLOT)
            q = q_ref[:, sl]
            sx = _dot_nt(q, kx_ref[:, sl])
            sc = _dot_nt(q, kc_ref[:, sl])
            m = jnp.maximum(jnp.max(sx, axis=-1, keepdims=True),
                            jnp.max(sc, axis=-1, keepdims=True))
            px = jnp.exp(sx - m)
            pc = jnp.exp(sc - m)
            l = jnp.sum(px, axis=-1, keepdims=True) + jnp.sum(pc, axis=-1, keepdims=True)
            o = _dot(px.astype(BF16), vx_ref[:, sl]) + _dot(pc.astype(BF16), vc_ref[:, sl])
            o = o / l
            acc = o if acc is None else acc + o
        o_ref[:, j * HEAD_SLOT:(j + 1) * HEAD_SLOT] = acc.astype(BF16)


def _mla_call(mq, mkx, mvx, mkc, mvc, batch, seq, ctx_len):
    nq = seq // TQ
    return pl.pallas_call(
        _mla_kernel,
        grid=(batch, nq),
        in_specs=[pl.BlockSpec((TQ, MLA_W), lambda b, i: (b * nq + i, 0)),
                  pl.BlockSpec((seq, MLA_W), lambda b, i: (b, 0)),
                  pl.BlockSpec((seq, MLA_W), lambda b, i: (b, 0)),
                  pl.BlockSpec((ctx_len, MLA_W), lambda b, i: (b, 0)),
                  pl.BlockSpec((ctx_len, MLA_W), lambda b, i: (b, 0))],
        out_specs=pl.BlockSpec((TQ, MLA_H * MLA_V), lambda b, i: (b * nq + i, 0)),
        out_shape=jax.ShapeDtypeStruct((batch * seq, MLA_H * MLA_V), BF16),
        compiler_params=_cparams(("arbitrary", "arbitrary")),
        name="mla",
    )(mq, mkx, mvx, mkc, mvc)


def _mixout_kernel(og_ref, sz_ref, om_ref, sg_ref, sm_ref, x_ref, mod_ref, gn_ref, wbg_ref,
                   wbm_ref, wo_ref, gpost_ref, gffn_ref, wr_ref, br_ref,
                   x1_o, h2_o, route_o, cnt_o, carry_ref):
    i = pl.program_id(0)
    tm = x_ref.shape[0]

    @pl.when(i == 0)
    def _():
        carry_ref[...] = jnp.zeros_like(carry_ref)

    mod = mod_ref[...]
    gt_a = mod[:, 2 * D:3 * D]
    sh_f = mod[:, 3 * D:4 * D]
    sc_f = mod[:, 4 * D:5 * D]

    og = og_ref[...]
    gn = gn_ref[...]
    parts = [_rms(og[:, hh * GLA_DV:(hh + 1) * GLA_DV], gn) for hh in range(GLA_H)]
    a = (jnp.concatenate(parts, axis=-1) * sz_ref[...].astype(F32)).astype(BF16)
    br_g = _dot(a, wbg_ref[...])
    br_m = _dot(om_ref[...], wbm_ref[...])
    merged = (sg_ref[...].astype(F32) * br_g + sm_ref[...].astype(F32) * br_m).astype(BF16)
    mo = _dot(merged, wo_ref[...])
    x1 = x_ref[...] + gt_a * _rms(mo, gpost_ref[...])
    x1_o[...] = x1
    h2 = _rms(x1, gffn_ref[...]) * (1.0 + sc_f) + sh_f
    h2_o[...] = h2

    lane = lax.broadcasted_iota(jnp.int32, (tm, LANES), 1)
    logits = jnp.dot(h2, wr_ref[...], preferred_element_type=F32,
                     precision=lax.Precision.HIGHEST) + br_ref[...]
    neg = jnp.float32(-jnp.inf)
    lg = jnp.where(lane < N_EXP, logits, neg)
    lane_f = lane.astype(F32)
    hots, vals = [], []
    for _k in range(TOP_K):
        mx = jnp.max(lg, axis=-1, keepdims=True)
        idx = jnp.min(jnp.where(lg == mx, lane_f, float(LANES)), axis=-1, keepdims=True)
        hot = lane_f == idx
        lg = jnp.where(hot, neg, lg)
        hots.append(hot)
        vals.append(mx)
    es = [jnp.exp(v - vals[0]) for v in vals]
    den = es[0] + es[1] + es[2] + es[3]
    ws = [e / den for e in es]

    msum = jnp.zeros((tm, LANES), F32)
    for hot in hots:
        msum = msum + hot.astype(F32)
    rr = lax.broadcasted_iota(jnp.int32, (tm, tm), 0)
    cc = lax.broadcasted_iota(jnp.int32, (tm, tm), 1)
    lower = (cc < rr).astype(BF16)
    prior = _dot(lower, msum.astype(BF16)) + carry_ref[...]

    route = jnp.zeros((tm, LANES), F32)
    for kk in range(TOP_K):
        hotf = hots[kk].astype(F32)
        e_col = jnp.sum(hotf * lane_f, axis=-1, keepdims=True)
        r_col = jnp.sum(hotf * prior, axis=-1, keepdims=True)
        route = jnp.where(lane == kk, e_col, route)
        route = jnp.where(lane == TOP_K + kk, ws[kk], route)
        route = jnp.where(lane == 2 * TOP_K + kk, r_col, route)
    route_o[...] = route

    carry_ref[...] = carry_ref[...] + jnp.sum(msum, axis=0, keepdims=True)
    cnt_o[...] = jnp.broadcast_to(carry_ref[...], cnt_o.shape)


def _mixout_call(og, sz, om, sg, sm, xf, mod3, tiles_per_seq, gn, wbg, wbm, wo, gpost, gffn,
                 wr, br):
    n_tok = xf.shape[0]
    tm = TM_PROJ
    const = lambda i: (0, 0)
    row = lambda i: (i, 0)

    def rs(w):
        return pl.BlockSpec((tm, w), row)

    return pl.pallas_call(
        _mixout_kernel,
        grid=(n_tok // tm,),
        in_specs=[rs(GLA_V), rs(GLA_V), rs(MLA_H * MLA_V), rs(D), rs(D), rs(D),
                  pl.BlockSpec((None, 1, 6 * D), lambda i: (i // tiles_per_seq, 0, 0)),
                  _resident(gn), _resident(wbg), _resident(wbm), _resident(wo),
                  _resident(gpost), _resident(gffn), _resident(wr), _resident(br)],
        out_specs=[rs(D), rs(D), rs(LANES), pl.BlockSpec((8, LANES), const)],
        out_shape=[jax.ShapeDtypeStruct((n_tok, D), F32), jax.ShapeDtypeStruct((n_tok, D), F32),
                   jax.ShapeDtypeStruct((n_tok, LANES), F32),
                   jax.ShapeDtypeStruct((8, LANES), F32)],
        scratch_shapes=[pltpu.VMEM((1, LANES), F32)],
        compiler_params=_cparams(("arbitrary",)),
        name="mixout",
    )(og, sz, om, sg, sm, xf, mod3, gn, wbg, wbm, wo, gpost, gffn, wr, br)


def _dispatch_kernel(pos_ref, h_ref, xs_in, xs_out, sem):
    del xs_in
    tm = h_ref.shape[0]

    def issue(t, carry):
        for kk in range(TOP_K):
            p = pos_ref[t * TOP_K + kk]
            pltpu.make_async_copy(h_ref.at[pl.ds(t, 1)], xs_out.at[pl.ds(p, 1)], sem).start()
        return carry

    lax.fori_loop(0, tm, issue, 0)
    for _k in range(TOP_K):
        pltpu.make_async_copy(h_ref, xs_out.at[pl.ds(0, tm)], sem).wait()


def _dispatch_call(pos_flat, h2, xs_zero):
    n_tok = h2.shape[0]
    tm = TM_ROW
    return pl.pallas_call(
        _dispatch_kernel,
        grid=(n_tok // tm,),
        in_specs=[pl.BlockSpec((tm * TOP_K,), lambda i: (i,), memory_space=pltpu.SMEM),
                  pl.BlockSpec((tm, D), lambda i: (i, 0)),
                  pl.BlockSpec(memory_space=pl.ANY)],
        out_specs=pl.BlockSpec(memory_space=pl.ANY),
        out_shape=jax.ShapeDtypeStruct(xs_zero.shape, xs_zero.dtype),
        scratch_shapes=[pltpu.SemaphoreType.DMA(())],
        input_output_aliases={2: 0},
        compiler_params=_cparams(("arbitrary",)),
        name="dispatch",
    )(pos_flat, h2, xs_zero)


def _ffn_kernel(te_ref, nu_ref, x_ref, wg_ref, wu_ref, wd_ref, bg_ref, bu_ref, bd_ref, y_ref,
                wgb, wub, wdb):
    i = pl.program_id(0)
    first = jnp.logical_or(i == 0, te_ref[i] != te_ref[jnp.maximum(i - 1, 0)])
    used = i < nu_ref[0]

    @pl.when(jnp.logical_and(used, first))
    def _():
        wgb[...] = wg_ref[...].astype(BF16)
        wub[...] = wu_ref[...].astype(BF16)
        wdb[...] = wd_ref[...].astype(BF16)

    @pl.when(used)
    def _():
        x = x_ref[...].astype(BF16)
        gate = jnp.minimum(_dot(x, wgb[...]) + bg_ref[...], SWIGLU_LIMIT)
        up = jnp.clip(_dot(x, wub[...]) + bu_ref[...], -SWIGLU_LIMIT, SWIGLU_LIMIT)
        act = ((up + 1.0) * gate * _sigmoid(SWIGLU_ALPHA * gate)).astype(BF16)
        y_ref[...] = _dot(act, wdb[...]) + bd_ref[...]

    @pl.when(jnp.logical_not(used))
    def _():
        y_ref[...] = jnp.zeros_like(y_ref)


def _ffn_call(tile_exp, n_used, xs, w_gate, w_up, w_down, b_gate, b_up, b_down):
    n_rows = xs.shape[0]
    tm = TM_FFN
    n_tiles = n_rows // tm

    def xrow(i, te, nu):
        return (jnp.minimum(i, nu[0] - 1), 0)

    def wsel(i, te, nu):
        return (te[i], 0, 0)

    grid_spec = pltpu.PrefetchScalarGridSpec(
        num_scalar_prefetch=2,
        grid=(n_tiles,),
        in_specs=[pl.BlockSpec((tm, D), xrow),
                  pl.BlockSpec((None, D, D_FF), wsel),
                  pl.BlockSpec((None, D, D_FF), wsel),
                  pl.BlockSpec((None, D_FF, D), wsel),
                  pl.BlockSpec((None, 1, D_FF), wsel),
                  pl.BlockSpec((None, 1, D_FF), wsel),
                  pl.BlockSpec((None, 1, D), wsel)],
        out_specs=pl.BlockSpec((tm, D), lambda i, te, nu: (i, 0)),
        scratch_shapes=[pltpu.VMEM((D, D_FF), BF16), pltpu.VMEM((D, D_FF), BF16),
                        pltpu.VMEM((D_FF, D), BF16)],
    )
    return pl.pallas_call(
        _ffn_kernel,
        grid_spec=grid_spec,
        out_shape=jax.ShapeDtypeStruct((n_rows, D), F32),
        compiler_params=_cparams(("arbitrary",)),
        name="ffn",
    )(tile_exp, n_used, xs, w_gate, w_up, w_down,
      b_gate.reshape(N_EXP, 1, D_FF), b_up.reshape(N_EXP, 1, D_FF), b_down.reshape(N_EXP, 1, D))


def _combine_kernel(pos_ref, y_hbm, route_ref, x1_ref, mod_ref, gpost_ref, o_ref, buf, sem):
    tm = x1_ref.shape[0]

    def issue(t, carry):
        for kk in range(TOP_K):
            p = pos_ref[t * TOP_K + kk]
            pltpu.make_async_copy(y_hbm.at[pl.ds(p, 1)], buf.at[kk, pl.ds(t, 1)], sem).start()
        return carry

    lax.fori_loop(0, tm, issue, 0)
    for kk in range(TOP_K):
        pltpu.make_async_copy(y_hbm.at[pl.ds(0, tm)], buf.at[kk], sem).wait()

    route = route_ref[...]
    lane = lax.broadcasted_iota(jnp.int32, route.shape, 1)
    moe = jnp.zeros((tm, D), F32)
    for kk in range(TOP_K):
        w = jnp.sum(jnp.where(lane == TOP_K + kk, route, 0.0), axis=-1, keepdims=True)
        moe = moe + w * buf[kk]
    gt_f = mod_ref[...][:, 5 * D:6 * D]
    o_ref[...] = x1_ref[...] + gt_f * _rms(moe, gpost_ref[...])


def _combine_call(pos_flat, y, route, x1, mod3, tiles_per_seq, gpost):
    n_tok = x1.shape[0]
    tm = TM_ROW
    return pl.pallas_call(
        _combine_kernel,
        grid=(n_tok // tm,),
        in_specs=[pl.BlockSpec((tm * TOP_K,), lambda i: (i,), memory_space=pltpu.SMEM),
                  pl.BlockSpec(memory_space=pl.ANY),
                  pl.BlockSpec((tm, LANES), lambda i: (i, 0)),
                  pl.BlockSpec((tm, D), lambda i: (i, 0)),
                  pl.BlockSpec((None, 1, 6 * D), lambda i: (i // tiles_per_seq, 0, 0)),
                  pl.BlockSpec((1, D), lambda i: (0, 0))],
        out_specs=pl.BlockSpec((tm, D), lambda i: (i, 0)),
        out_shape=jax.ShapeDtypeStruct((n_tok, D), F32),
        scratch_shapes=[pltpu.VMEM((TOP_K, tm, D), F32), pltpu.SemaphoreType.DMA(())],
        compiler_params=_cparams(("arbitrary",)),
        name="combine",
    )(pos_flat, y, route, x1, mod3, gpost)


def _rope_perm():
    q = MLA_ROPE // 4
    perm = np.concatenate([np.arange(q, 2 * q), np.arange(0, q),
                           np.arange(3 * q, 4 * q), np.arange(2 * q, 3 * q)])
    sign = np.concatenate([-np.ones(q), np.ones(q), -np.ones(q), np.ones(q)]).astype(np.float32)
    return perm, sign


def _rope_tables(seq):
    rows = seq // GRID_W
    r, col = jnp.meshgrid(jnp.arange(rows, dtype=F32), jnp.arange(GRID_W, dtype=F32),
                          indexing="ij")
    half = MLA_ROPE // 2
    inv_freq = ROPE_BASE ** (-jnp.arange(0, half, 2, dtype=F32) / half)
    ar = r.reshape(-1)[:, None] * inv_freq
    ac = col.reshape(-1)[:, None] * inv_freq
    cos = jnp.concatenate([jnp.cos(ar), jnp.cos(ar), jnp.cos(ac), jnp.cos(ac)], axis=-1)
    sin = jnp.concatenate([jnp.sin(ar), jnp.sin(ar), jnp.sin(ac), jnp.sin(ac)], axis=-1)
    return cos, sin


def _head_slots(parts):
    cols = []
    for p in parts:
        pad = HEAD_SLOT - p.shape[1]
        cols.append(jnp.pad(p, ((0, 0), (0, pad))))
    return jnp.concatenate(cols, axis=1)


def kernel(x, c, ctx, c_ctx, w_mod, b_mod, g_pre_mix, g_post_mix, g_pre_ffn, g_post_ffn, w_in,
           gla_w_a2_f, gla_b_a_f, gla_w_a2_b, gla_b_a_b, gla_g_norm, mla_g_q, mla_w_uq, mla_g_kv,
           mla_w_uk, mla_w_uv, w_br_gla, w_br_mla, w_out, router_w, router_b, w_gate, b_gate,
           w_up, b_up, w_down, b_down):
    depth = w_mod.shape[0]
    assert depth == 1, "single-layer block"
    batch, seq, d = x.shape
    ctx_len = ctx.shape[1]
    assert d == D and seq % TM_PROJ == 0 and (batch * ctx_len) % TM_PROJ == 0
    assert TM_PROJ % ctx_len == 0 or ctx_len % TM_PROJ == 0
    n_tok = batch * seq
    perm, sign = _rope_perm()
    sign = jnp.asarray(sign)

    cc = jnp.zeros((16, D), F32).at[:batch].set(c).at[batch].set(c_ctx)
    mod = _mod_call(cc, w_mod[0], b_mod[0])
    mod3 = mod.reshape(16, 1, 6 * D)

    wi = w_in[0]
    kr = wi[:, O_KR:O_KR + MLA_ROPE]
    small = jnp.concatenate([wi[:, O_AF:O_AF + GLA_RANK], wi[:, O_AB:O_AB + GLA_RANK], kr,
                             kr[:, perm] * sign, jnp.zeros((D, LANES - X_KRS - MLA_ROPE), F32)],
                            axis=1)
    w1 = jnp.concatenate([wi[:, O_Q:O_G], wi[:, O_DKV:O_DKV + MLA_KVR], small], axis=1).astype(BF16)
    w2 = jnp.concatenate([wi[:, O_G:O_G + GLA_V], wi[:, O_DQ:O_DQ + MLA_QR], wi[:, O_MG:]],
                         axis=1).astype(BF16)
    wa = jnp.zeros((LANES, 2 * GLA_QK), F32)
    wa = wa.at[X_AF:X_AF + GLA_RANK, 0:GLA_QK].set(gla_w_a2_f[0])
    wa = wa.at[X_AB:X_AB + GLA_RANK, GLA_QK:].set(gla_w_a2_b[0]).astype(BF16)
    ba = jnp.concatenate([gla_b_a_f[0], gla_b_a_b[0]]).reshape(1, 2 * GLA_QK)

    uk = mla_w_uk[0]
    uv = mla_w_uv[0]
    uq = mla_w_uq[0]
    eye = jnp.eye(MLA_ROPE, dtype=F32)
    wk_top = _head_slots([uk[:, h * MLA_NOPE:(h + 1) * MLA_NOPE] for h in range(MLA_H)])
    place = _head_slots([jnp.pad(eye, ((0, 0), (MLA_NOPE, 0))) for _ in range(MLA_H)])
    wk_bot = jnp.zeros((LANES, MLA_W), F32)
    wk_bot = wk_bot.at[X_KR:X_KR + MLA_ROPE].set(place).at[X_KRS:X_KRS + MLA_ROPE].set(place)
    wk = jnp.concatenate([wk_top, wk_bot], axis=0).astype(BF16)
    wv = _head_slots([jnp.pad(uv[:, h * MLA_V:(h + 1) * MLA_V], ((0, 0), ((h % 2) * MLA_V, 0)))
                      for h in range(MLA_H)]).astype(BF16)
    wq_a = _head_slots([uq[:, h * MLA_QKD:(h + 1) * MLA_QKD] for h in range(MLA_H)])
    wq_b = _head_slots([jnp.pad(uq[:, h * MLA_QKD + MLA_NOPE:(h + 1) * MLA_QKD][:, perm] * sign,
                                ((0, 0), (MLA_NOPE, 0))) for h in range(MLA_H)])
    wq = jnp.concatenate([wq_a, wq_b], axis=1).astype(BF16)

    cos, sin = _rope_tables(seq)
    zeros32 = jnp.zeros((seq, MLA_ROPE), F32)
    t1_x = jnp.concatenate([zeros32, cos, sin, zeros32], axis=1)
    t1_c = jnp.broadcast_to(
        jnp.concatenate([jnp.zeros((MLA_ROPE,)), jnp.ones((MLA_ROPE,)),
                         jnp.zeros((2 * MLA_ROPE,))]).astype(F32), (TM_PROJ, LANES))
    ones64 = jnp.ones((seq, MLA_NOPE), F32)
    cq = jnp.tile(jnp.concatenate([ones64, cos, zeros32], axis=1), (1, MLA_H)) * MLA_SCALE
    sq = jnp.tile(jnp.concatenate([jnp.zeros((seq, MLA_NOPE), F32), sin, zeros32], axis=1),
                  (1, MLA_H)) * MLA_SCALE

    gpre = g_pre_mix[0].reshape(1, D)
    gkv = mla_g_kv[0].reshape(1, MLA_KVR)
    gq = mla_g_q[0].reshape(1, MLA_QR)
    tiles_per_seq = seq // TM_PROJ

    xf = x.reshape(n_tok, D)
    cf = ctx.reshape(batch * ctx_len, D)
    (gqx, gkx, gvx, lfx, lbx, mkx, mvx, sz, mq, sg, sm) = _inproj_call(
        True, xf, mod3, lambda i: i // tiles_per_seq, tiles_per_seq, gpre, w1, wa, ba, wk, wv,
        gkv, t1_x, extra=(w2, gq, wq, cq, sq))
    (gqc, gkc, gvc, lfc, lbc, mkc, mvc) = _inproj_call(
        False, cf, mod3, lambda i: batch, 1, gpre, w1, wa, ba, wk, wv, gkv, t1_c)

    og = _gla_call((gqx, gkx, gvx, lfx, lbx), (gqc, gkc, gvc, lfc, lbc), batch, seq, ctx_len)
    om = _mla_call(mq, mkx, mvx, mkc, mvc, batch, seq, ctx_len)

    wr = jnp.pad(router_w[0], ((0, 0), (0, LANES - N_EXP)))
    br = jnp.pad(router_b[0], (0, LANES - N_EXP)).reshape(1, LANES)
    x1, h2, route, cnt = _mixout_call(
        og, sz, om, sg, sm, xf, mod3, tiles_per_seq, gla_g_norm[0].reshape(1, GLA_DV),
        w_br_gla[0].astype(BF16), w_br_mla[0].astype(BF16), w_out[0].astype(BF16),
        g_post_mix[0].reshape(1, D), g_pre_ffn[0].reshape(1, D), wr, br)

    e_idx = route[:, 0:TOP_K].astype(jnp.int32)
    rank = route[:, 2 * TOP_K:3 * TOP_K].astype(jnp.int32)
    counts = cnt[0, :N_EXP].astype(jnp.int32)
    padded = ((counts + TM_FFN - 1) // TM_FFN) * TM_FFN
    ends = jnp.cumsum(padded)
    off = ends - padded
    pos = jnp.sum(jnp.where(e_idx[..., None] == jnp.arange(N_EXP), off, 0), axis=-1) + rank
    pos_flat = pos.reshape(-1)
    n_tiles = (n_tok * TOP_K) // TM_FFN + N_EXP
    tile_exp = jnp.minimum(
        jnp.searchsorted(ends, jnp.arange(n_tiles, dtype=jnp.int32) * TM_FFN, side="right"),
        N_EXP - 1).astype(jnp.int32)
    n_used = (ends[-1:] // TM_FFN).astype(jnp.int32)

    xs = _dispatch_call(pos_flat, h2, jnp.zeros((n_tiles * TM_FFN, D), F32))
    y = _ffn_call(tile_exp, n_used, xs, w_gate[0], w_up[0], w_down[0], b_gate[0], b_up[0],
                  b_down[0])
    out = _combine_call(pos_flat, y, route, x1, mod3, seq // TM_ROW,
                        g_post_ffn[0].reshape(1, D))
    return out.reshape(batch, seq, D)
```

```python
import functools

import jax
import jax.numpy as jnp
import numpy as np
from jax import lax
from jax.experimental import pallas as pl
from jax.experimental.pallas import tpu as pltpu

F32 = jnp.float32
BF16 = jnp.bfloat16

D = 1024
EPS = 1e-6
GRID_W = 64

GLA_H = 4
GLA_DK = 64
GLA_DV = 128
GLA_RANK = 16
GLA_TAU = 16.0
GLA_CHUNK = 64
GLA_BLOCK = 256
GLA_QK = GLA_H * GLA_DK
GLA_V = GLA_H * GLA_DV

MLA_H = 8
MLA_QR = 256
MLA_KVR = 128
MLA_NOPE = 64
MLA_ROPE = 32
MLA_V = 64
MLA_QKD = MLA_NOPE + MLA_ROPE
MLA_SCALE = MLA_QKD ** -0.5
ROPE_BASE = 10000.0
HEAD_SLOT = 128
MLA_W = MLA_H * HEAD_SLOT

N_EXP = 32
TOP_K = 4
D_FF = 1024
SWIGLU_LIMIT = 7.0
SWIGLU_ALPHA = 1.702

LANES = 128
SUBLANES = 8
TM_PROJ = 512
TQ = 512
TM_FFN = 512
TM_ROW = 256
ZERO_ROWS = 256

_OFF = np.cumsum([0, GLA_QK, GLA_QK, GLA_V, GLA_V, GLA_RANK, GLA_RANK,
                  MLA_QR, MLA_KVR, MLA_ROPE, D, D])
(O_Q, O_K, O_V, O_G, O_AF, O_AB, O_DQ, O_DKV, O_KR, O_MG, O_MM, _) = _OFF.tolist()

X_AF, X_AB, X_KR, X_KRS = 0, 16, 32, 64

VMEM_LIMIT = 56 * 1024 * 1024


def _cparams(sem):
    return pltpu.CompilerParams(dimension_semantics=sem, vmem_limit_bytes=VMEM_LIMIT)


def _resident(arr):
    nd = arr.ndim
    return pl.BlockSpec(arr.shape, lambda *_: (0,) * nd, pipeline_mode=pl.Buffered(1))


def _rms(x, g):
    return x * lax.rsqrt(jnp.mean(x * x, axis=-1, keepdims=True) + EPS) * g


def _sigmoid(x):
    return 1.0 / (1.0 + jnp.exp(-x))


def _log_sigmoid(x):
    return jnp.minimum(x, 0.0) - jnp.log1p(jnp.exp(-jnp.abs(x)))


def _dot(a, b):
    return jnp.dot(a, b, preferred_element_type=F32)


def _dot_nt(a, b):
    return lax.dot_general(a, b, (((1,), (1,)), ((), ())), preferred_element_type=F32)


def _dot_tn(a, b):
    return lax.dot_general(a, b, (((0,), (0,)), ((), ())), preferred_element_type=F32)


def _mod_kernel(c_ref, w_ref, b_ref, o_ref):
    c = c_ref[...]
    s = (c * _sigmoid(c)).astype(BF16)
    o_ref[...] = _dot(s, w_ref[...].astype(BF16)) + b_ref[...]


def _mod_call(cc, w_mod, b_mod):
    n = w_mod.shape[1]
    tn = 1536
    return pl.pallas_call(
        _mod_kernel,
        grid=(n // tn,),
        in_specs=[pl.BlockSpec((16, D), lambda j: (0, 0)),
                  pl.BlockSpec((D, tn), lambda j: (0, j)),
                  pl.BlockSpec((1, tn), lambda j: (0, j))],
        out_specs=pl.BlockSpec((16, tn), lambda j: (0, j)),
        out_shape=jax.ShapeDtypeStruct((16, n), F32),
        compiler_params=_cparams(("arbitrary",)),
        name="mod",
    )(cc, w_mod, b_mod.reshape(1, n))


def _inproj_kernel(with_q, x_ref, mod_ref, gpre_ref, w1_ref, wa_ref, ba_ref, wk_ref,
                   wv_ref, gkv_ref, t1_ref, *rest):
    if with_q:
        (w2_ref, gq_ref, wq_ref, cq_ref, sq_ref,
         q_o, k_o, v_o, lf_o, lb_o, mk_o, mv_o, sz_o, mq_o, sg_o, sm_o) = rest
    else:
        (q_o, k_o, v_o, lf_o, lb_o, mk_o, mv_o) = rest
    x = x_ref[...]
    mod = mod_ref[...]
    sh = mod[:, 0:D]
    sc = mod[:, D:2 * D]
    h = (_rms(x, gpre_ref[...]) * (1.0 + sc) + sh).astype(BF16)

    z1 = _dot(h, w1_ref[...])
    q_o[...] = (z1[:, 0:GLA_QK] * (GLA_DK ** -0.5)).astype(BF16)
    k_o[...] = z1[:, GLA_QK:2 * GLA_QK].astype(BF16)
    v_o[...] = z1[:, 2 * GLA_QK:2 * GLA_QK + GLA_V].astype(BF16)
    o_dkv = 2 * GLA_QK + GLA_V
    ckv = _rms(z1[:, o_dkv:o_dkv + MLA_KVR], gkv_ref[...])
    xs = z1[:, o_dkv + MLA_KVR:o_dkv + MLA_KVR + LANES]

    la = _log_sigmoid(_dot(xs.astype(BF16), wa_ref[...]) + ba_ref[...]) * (1.0 / GLA_TAU)
    lf_o[...] = la[:, 0:GLA_QK]
    lb_o[...] = la[:, GLA_QK:2 * GLA_QK]

    lhs_k = jnp.concatenate([ckv, xs * t1_ref[...]], axis=-1).astype(BF16)
    mk_o[...] = _dot(lhs_k, wk_ref[...]).astype(BF16)
    mv_o[...] = _dot(ckv.astype(BF16), wv_ref[...]).astype(BF16)

    if with_q:
        zg = _dot(h, w2_ref[:, 0:GLA_V])
        sz_o[...] = (zg * _sigmoid(zg)).astype(BF16)
        n = _rms(_dot(h, w2_ref[:, GLA_V:GLA_V + MLA_QR]), gq_ref[...]).astype(BF16)
        mq_o[...] = (_dot(n, wq_ref[:, 0:MLA_W]) * cq_ref[...]
                     + _dot(n, wq_ref[:, MLA_W:2 * MLA_W]) * sq_ref[...]).astype(BF16)
        o_mg = GLA_V + MLA_QR
        sg_o[...] = _sigmoid(_dot(h, w2_ref[:, o_mg:o_mg + D])).astype(BF16)
        sm_o[...] = _sigmoid(_dot(h, w2_ref[:, o_mg + D:o_mg + 2 * D])).astype(BF16)


def _inproj_call(with_q, xf, mod3, mod_row_fn, tiles_per_seq, gpre, w1, wa, ba, wk, wv,
                 gkv, t1, extra=()):
    n_tok = xf.shape[0]
    tm = TM_PROJ
    grid = (n_tok // tm,)
    row = lambda i: (i, 0)
    tab = lambda i: (i % tiles_per_seq, 0)
    in_specs = [
        pl.BlockSpec((tm, D), row),
        pl.BlockSpec((None, 1, 6 * D), lambda i: (mod_row_fn(i), 0, 0)),
        _resident(gpre), _resident(w1), _resident(wa), _resident(ba), _resident(wk),
        _resident(wv), _resident(gkv),
        pl.BlockSpec((tm, LANES), tab),
    ]
    widths = [(GLA_QK, BF16), (GLA_QK, BF16), (GLA_V, BF16), (GLA_QK, F32), (GLA_QK, F32),
              (MLA_W, BF16), (MLA_W, BF16)]
    args = [xf, mod3, gpre, w1, wa, ba, wk, wv, gkv, t1]
    if with_q:
        w2, gq, wq, cq, sq = extra
        in_specs += [_resident(w2), _resident(gq), _resident(wq),
                     pl.BlockSpec((tm, MLA_W), tab), pl.BlockSpec((tm, MLA_W), tab)]
        args += [w2, gq, wq, cq, sq]
        widths += [(GLA_V, BF16), (MLA_W, BF16), (D, BF16), (D, BF16)]
    return pl.pallas_call(
        functools.partial(_inproj_kernel, with_q),
        grid=grid,
        in_specs=in_specs,
        out_specs=[pl.BlockSpec((tm, w), row) for w, _ in widths],
        out_shape=[jax.ShapeDtypeStruct((n_tok, w), dt) for w, dt in widths],
        compiler_params=_cparams(("arbitrary",)),
        name="inproj_x" if with_q else "inproj_ctx",
    )(*args)


def _gla_kernel(qx, kx, vx, lfx, lbx, qc, kc, vc, lfc, lbc, o_ref, sf_ref, sb_ref):
    C = GLA_CHUNK
    R = GLA_BLOCK
    n_sub = R // C
    rr = lax.broadcasted_iota(jnp.int32, (R, R), 0)
    cc = lax.broadcasted_iota(jnp.int32, (R, R), 1)
    same = (rr // C) == (cc // C)
    tbd_f = jnp.logical_and(same, cc <= rr).astype(BF16)
    tbd_b = jnp.logical_and(same, cc >= rr).astype(BF16)
    row = lax.broadcasted_iota(jnp.int32, (C, GLA_QK), 0)
    col = lax.broadcasted_iota(jnp.int32, (C, GLA_QK), 1)
    head_qk = col // GLA_DK
    tri4_f = (col % C) <= row
    tri4_b = (col % C) >= row
    head_v = lax.broadcasted_iota(jnp.int32, (C, GLA_V), 1) // GLA_DV

    def stack_masked(x, head_of_lane):
        return jnp.concatenate(
            [jnp.where(head_of_lane == hh, x, jnp.zeros_like(x)) for hh in range(GLA_H)], axis=0)

    def block(q_ref, k_ref, v_ref, la_ref, row0, fwd, s_ref, emit):
        sl = pl.ds(row0, R)
        la = la_ref[sl, :]
        la_hi = la.astype(BF16)
        la_lo = (la - la_hi.astype(F32)).astype(BF16)
        tbd = tbd_f if fwd else tbd_b
        cum = _dot(tbd, la_hi) + _dot(tbd, la_lo)
        k = k_ref[sl, :].astype(F32)
        v = v_ref[sl, :]
        if emit:
            q = q_ref[sl, :].astype(F32)
        st = s_ref[...]
        outs = [None] * n_sub
        for ci in (range(n_sub) if fwd else reversed(range(n_sub))):
            rs = slice(ci * C, (ci + 1) * C)
            cum_c = cum[rs]
            tot = cum_c[C - 1:C, :] if fwd else cum_c[0:1, :]
            k_c = k[rs]
            v_c = v[rs]
            ke_bd = stack_masked((k_c * jnp.exp(tot - cum_c)).astype(BF16), head_qk)
            v_stack = jnp.concatenate(
                [v_c[:, hh * GLA_DV:(hh + 1) * GLA_DV] for hh in range(GLA_H)], axis=0)
            if emit:
                qd = (q[rs] * jnp.exp(cum_c)).astype(BF16)
                ki_bd = stack_masked((k_c * jnp.exp(-cum_c)).astype(BF16), head_qk)
                att = _dot_nt(qd, ki_bd)
                att = jnp.where(tri4_f if fwd else tri4_b, att, 0.0).astype(BF16)
                o_intra = _dot(att, stack_masked(v_c, head_v))
                oi = _dot_nt(stack_masked(qd, head_qk), st.astype(BF16))
                o_inter = jnp.concatenate([oi[hh * C:(hh + 1) * C] for hh in range(GLA_H)],
                                          axis=-1)
                outs[ci] = o_intra + o_inter
            st = st * jnp.exp(tot) + _dot_tn(v_stack, ke_bd)
        s_ref[...] = st
        if emit:
            o_ref[sl, :] += jnp.concatenate(outs, axis=0)

    sf_ref[...] = jnp.zeros_like(sf_ref)
    sb_ref[...] = jnp.zeros_like(sb_ref)
    o_ref[...] = jnp.zeros_like(o_ref)
    n_ctx = qc.shape[0] // R
    n_x = qx.shape[0] // R
    for i in range(n_ctx):
        block(qc, kc, vc, lfc, i * R, True, sf_ref, False)
        block(qc, kc, vc, lbc, (n_ctx - 1 - i) * R, False, sb_ref, False)

    def body(i, carry):
        block(qx, kx, vx, lfx, pl.multiple_of(i * R, R), True, sf_ref, True)
        block(qx, kx, vx, lbx, pl.multiple_of((n_x - 1 - i) * R, R), False, sb_ref, True)
        return carry

    lax.fori_loop(0, n_x, body, 0)


def _gla_call(fx, fc, batch, seq, ctx_len):
    qx, kx, vx, lfx, lbx = fx
    qc, kc, vc, lfc, lbc = fc

    def spec(rows, w):
        return pl.BlockSpec((rows, w), lambda b: (b, 0))

    return pl.pallas_call(
        _gla_kernel,
        grid=(batch,),
        in_specs=[spec(seq, GLA_QK), spec(seq, GLA_QK), spec(seq, GLA_V), spec(seq, GLA_QK),
                  spec(seq, GLA_QK),
                  spec(ctx_len, GLA_QK), spec(ctx_len, GLA_QK), spec(ctx_len, GLA_V),
                  spec(ctx_len, GLA_QK), spec(ctx_len, GLA_QK)],
        out_specs=spec(seq, GLA_V),
        out_shape=jax.ShapeDtypeStruct((batch * seq, GLA_V), F32),
        scratch_shapes=[pltpu.VMEM((GLA_DV, GLA_QK), F32), pltpu.VMEM((GLA_DV, GLA_QK), F32)],
        compiler_params=_cparams(("arbitrary",)),
        name="gla",
    )(qx, kx, vx, lfx, lbx, qc, kc, vc, lfc, lbc)


def _mla_kernel(q_ref, kx_ref, vx_ref, kc_ref, vc_ref, o_ref):
    for j in range(MLA_H // 2):
        acc = None
        for hh in (2 * j, 2 * j + 1):
            sl = slice(hh * HEAD_SLOT, (hh + 1) * HEAD_SLOT)
            q = q_ref[:, sl]
            sx = _dot_nt(q, kx_ref[:, sl])
            sc = _dot_nt(q, kc_ref[:, sl])
            m = jnp.maximum(jnp.max(sx, axis=-1, keepdims=True),
                            jnp.max(sc, axis=-1, keepdims=True))
            px = jnp.exp(sx - m)
            pc = jnp.exp(sc - m)
            l = jnp.sum(px, axis=-1, keepdims=True) + jnp.sum(pc, axis=-1, keepdims=True)
            o = _dot(px.astype(BF16), vx_ref[:, sl]) + _dot(pc.astype(BF16), vc_ref[:, sl])
            o = o / l
            acc = o if acc is None else acc + o
        o_ref[:, j * HEAD_SLOT:(j + 1) * HEAD_SLOT] = acc.astype(BF16)


def _mla_call(mq, mkx, mvx, mkc, mvc, batch, seq, ctx_len):
    nq = seq // TQ
    return pl.pallas_call(
        _mla_kernel,
        grid=(batch, nq),
        in_specs=[pl.BlockSpec((TQ, MLA_W), lambda b, i: (b * nq + i, 0)),
                  pl.BlockSpec((seq, MLA_W), lambda b, i: (b, 0)),
                  pl.BlockSpec((seq, MLA_W), lambda b, i: (b, 0)),
                  pl.BlockSpec((ctx_len, MLA_W), lambda b, i: (b, 0)),
                  pl.BlockSpec((ctx_len, MLA_W), lambda b, i: (b, 0))],
        out_specs=pl.BlockSpec((TQ, MLA_H * MLA_V), lambda b, i: (b * nq + i, 0)),
        out_shape=jax.ShapeDtypeStruct((batch * seq, MLA_H * MLA_V), BF16),
        compiler_params=_cparams(("arbitrary", "arbitrary")),
        name="mla",
    )(mq, mkx, mvx, mkc, mvc)


def _mixout_kernel(og_ref, sz_ref, om_ref, sg_ref, sm_ref, x_ref, mod_ref, gn_ref, wbg_ref,
                   wbm_ref, wo_ref, gpost_ref, gffn_ref, wrh_ref, wrl_ref, br_ref,
                   x1_o, h2_o, route_o, cnt_o, carry_ref):
    i = pl.program_id(0)
    tm = x_ref.shape[0]

    @pl.when(i == 0)
    def _():
        carry_ref[...] = jnp.zeros_like(carry_ref)

    mod = mod_ref[...]
    gt_a = mod[:, 2 * D:3 * D]
    sh_f = mod[:, 3 * D:4 * D]
    sc_f = mod[:, 4 * D:5 * D]

    og = og_ref[...]
    gn = gn_ref[...]
    parts = [_rms(og[:, hh * GLA_DV:(hh + 1) * GLA_DV], gn) for hh in range(GLA_H)]
    a = (jnp.concatenate(parts, axis=-1) * sz_ref[...].astype(F32)).astype(BF16)
    br_g = _dot(a, wbg_ref[...])
    br_m = _dot(om_ref[...], wbm_ref[...])
    merged = (sg_ref[...].astype(F32) * br_g + sm_ref[...].astype(F32) * br_m).astype(BF16)
    mo = _dot(merged, wo_ref[...])
    x1 = x_ref[...] + gt_a * _rms(mo, gpost_ref[...])
    x1_o[...] = x1
    h2 = _rms(x1, gffn_ref[...]) * (1.0 + sc_f) + sh_f
    h2_o[...] = h2

    lane = lax.broadcasted_iota(jnp.int32, (tm, LANES), 1)
    h_hi = h2.astype(BF16)
    h_lo = (h2 - h_hi.astype(F32)).astype(BF16)
    logits = (_dot(h_hi, wrh_ref[...]) + _dot(h_lo, wrh_ref[...]) + _dot(h_hi, wrl_ref[...])
              + _dot(h_lo, wrl_ref[...]) + br_ref[...])
    neg = jnp.float32(-jnp.inf)
    lg = jnp.where(lane < N_EXP, logits, neg)
    lane_f = lane.astype(F32)
    hots, vals = [], []
    for _k in range(TOP_K):
        mx = jnp.max(lg, axis=-1, keepdims=True)
        idx = jnp.min(jnp.where(lg == mx, lane_f, float(LANES)), axis=-1, keepdims=True)
        hot = lane_f == idx
        lg = jnp.where(hot, neg, lg)
        hots.append(hot)
        vals.append(mx)
    es = [jnp.exp(v - vals[0]) for v in vals]
    den = es[0] + es[1] + es[2] + es[3]
    ws = [e / den for e in es]

    msum = jnp.zeros((tm, LANES), F32)
    for hot in hots:
        msum = msum + hot.astype(F32)
    rr = lax.broadcasted_iota(jnp.int32, (tm, tm), 0)
    cc = lax.broadcasted_iota(jnp.int32, (tm, tm), 1)
    lower = (cc < rr).astype(BF16)
    prior = _dot(lower, msum.astype(BF16)) + carry_ref[...]

    route = jnp.zeros((tm, LANES), F32)
    for kk in range(TOP_K):
        hotf = hots[kk].astype(F32)
        e_col = jnp.sum(hotf * lane_f, axis=-1, keepdims=True)
        r_col = jnp.sum(hotf * prior, axis=-1, keepdims=True)
        route = jnp.where(lane == kk, e_col, route)
        route = jnp.where(lane == TOP_K + kk, ws[kk], route)
        route = jnp.where(lane == 2 * TOP_K + kk, r_col, route)
    route_o[...] = route

    carry_ref[...] = carry_ref[...] + jnp.sum(msum, axis=0, keepdims=True)
    cnt_o[...] = jnp.broadcast_to(carry_ref[...], cnt_o.shape)


def _mixout_call(og, sz, om, sg, sm, xf, mod3, tiles_per_seq, gn, wbg, wbm, wo, gpost, gffn,
                 wrh, wrl, br):
    n_tok = xf.shape[0]
    tm = TM_PROJ
    const = lambda i: (0, 0)
    row = lambda i: (i, 0)

    def rs(w):
        return pl.BlockSpec((tm, w), row)

    return pl.pallas_call(
        _mixout_kernel,
        grid=(n_tok // tm,),
        in_specs=[rs(GLA_V), rs(GLA_V), rs(MLA_H * MLA_V), rs(D), rs(D), rs(D),
                  pl.BlockSpec((None, 1, 6 * D), lambda i: (i // tiles_per_seq, 0, 0)),
                  _resident(gn), _resident(wbg), _resident(wbm), _resident(wo),
                  _resident(gpost), _resident(gffn), _resident(wrh), _resident(wrl),
                  _resident(br)],
        out_specs=[rs(D), rs(D), rs(LANES), pl.BlockSpec((8, LANES), const)],
        out_shape=[jax.ShapeDtypeStruct((n_tok, D), F32), jax.ShapeDtypeStruct((n_tok, D), F32),
                   jax.ShapeDtypeStruct((n_tok, LANES), F32),
                   jax.ShapeDtypeStruct((8, LANES), F32)],
        scratch_shapes=[pltpu.VMEM((1, LANES), F32)],
        compiler_params=_cparams(("arbitrary",)),
        name="mixout",
    )(og, sz, om, sg, sm, xf, mod3, gn, wbg, wbm, wo, gpost, gffn, wrh, wrl, br)


def _dispatch_steps(n_rows, tm):
    return (n_rows - N_EXP * TM_FFN) // (tm * TOP_K)


def _zero_rows(zero_ref, dst_ref, start, end, sem):
    first = jnp.minimum(((start + SUBLANES - 1) // SUBLANES) * SUBLANES, end)
    n_single = first - start
    length = end - first
    for wait in (False, True):
        def run(cp):
            if wait:
                cp.wait()
            else:
                cp.start()

        for j in range(SUBLANES - 1):
            @pl.when(j < n_single)
            def _():
                run(pltpu.make_async_copy(zero_ref.at[pl.ds(0, 1)],
                                          dst_ref.at[pl.ds(start + j, 1)], sem))

        for b in reversed(range(SUBLANES.bit_length() - 1, ZERO_ROWS.bit_length())):
            size = 1 << b
            off = (length >> (b + 1)) << (b + 1)

            @pl.when((length & size) != 0)
            def _():
                run(pltpu.make_async_copy(
                    zero_ref.at[pl.ds(0, size)],
                    dst_ref.at[pl.ds(pl.multiple_of(first + off, SUBLANES), size)], sem))


def _dispatch_kernel(pad_ref, pos_ref, h_ref, xs_out, zero_ref, sem, zsem):
    i = pl.program_id(0)
    tm = h_ref.shape[0]
    n_rows = xs_out.shape[0]
    n_steps = _dispatch_steps(n_rows, tm)

    @pl.when(i == 0)
    def _():
        zero_ref[...] = jnp.zeros_like(zero_ref)

    def issue(t, carry):
        for kk in range(TOP_K):
            p = pos_ref[t * TOP_K + kk]
            pltpu.make_async_copy(h_ref.at[pl.ds(t, 1)], xs_out.at[pl.ds(p, 1)], sem).start()
        return carry

    lax.fori_loop(0, tm, issue, 0)

    for j in range(-(-N_EXP // n_steps)):
        e = i * (-(-N_EXP // n_steps)) + j

        @pl.when(e < N_EXP)
        def _():
            ec = jnp.minimum(e, N_EXP - 1)
            _zero_rows(zero_ref, xs_out, pad_ref[ec], pad_ref[N_EXP + 1 + ec], zsem)

    total_end = pad_ref[N_EXP]
    tail_rows = N_EXP * TM_FFN
    per_step = tail_rows // n_steps
    for j in range(per_step // ZERO_ROWS):
        seg_end = (n_rows - tail_rows) + i * per_step + (j + 1) * ZERO_ROWS
        zlen = jnp.clip(seg_end - total_end, 0, ZERO_ROWS)
        _zero_rows(zero_ref, xs_out, seg_end - zlen, seg_end, zsem)

    for _k in range(TOP_K):
        pltpu.make_async_copy(h_ref, xs_out.at[pl.ds(0, tm)], sem).wait()


def _dispatch_call(pad_info, pos_flat, h2, n_rows):
    n_tok = h2.shape[0]
    tm = TM_ROW
    n_steps = n_tok // tm
    assert _dispatch_steps(n_rows, tm) == n_steps
    assert (N_EXP * TM_FFN) % (n_steps * ZERO_ROWS) == 0 and TM_FFN <= 2 * ZERO_ROWS
    grid_spec = pltpu.PrefetchScalarGridSpec(
        num_scalar_prefetch=1,
        grid=(n_steps,),
        in_specs=[pl.BlockSpec((tm * TOP_K,), lambda i, pad: (i,), memory_space=pltpu.SMEM),
                  pl.BlockSpec((tm, D), lambda i, pad: (i, 0))],
        out_specs=pl.BlockSpec(memory_space=pl.ANY),
        scratch_shapes=[pltpu.VMEM((ZERO_ROWS, D), F32), pltpu.SemaphoreType.DMA(()),
                        pltpu.SemaphoreType.DMA(())],
    )
    return pl.pallas_call(
        _dispatch_kernel,
        grid_spec=grid_spec,
        out_shape=jax.ShapeDtypeStruct((n_rows, D), F32),
        compiler_params=_cparams(("arbitrary",)),
        name="dispatch",
    )(pad_info, pos_flat, h2)


def _ffn_kernel(te_ref, nu_ref, x_ref, wg_ref, wu_ref, wd_ref, bg_ref, bu_ref, bd_ref, y_ref,
                wgb, wub, wdb):
    i = pl.program_id(0)
    first = jnp.logical_or(i == 0, te_ref[i] != te_ref[jnp.maximum(i - 1, 0)])
    used = i < nu_ref[0]

    @pl.when(jnp.logical_and(used, first))
    def _():
        wgb[...] = wg_ref[...].astype(BF16)
        wub[...] = wu_ref[...].astype(BF16)
        wdb[...] = wd_ref[...].astype(BF16)

    @pl.when(used)
    def _():
        x = x_ref[...].astype(BF16)
        gate = jnp.minimum(_dot(x, wgb[...]) + bg_ref[...], SWIGLU_LIMIT)
        up = jnp.clip(_dot(x, wub[...]) + bu_ref[...], -SWIGLU_LIMIT, SWIGLU_LIMIT)
        act = ((up + 1.0) * gate * _sigmoid(SWIGLU_ALPHA * gate)).astype(BF16)
        y_ref[...] = _dot(act, wdb[...]) + bd_ref[...]

    @pl.when(jnp.logical_not(used))
    def _():
        y_ref[...] = jnp.zeros_like(y_ref)


def _ffn_call(tile_exp, n_used, xs, w_gate, w_up, w_down, b_gate, b_up, b_down):
    n_rows = xs.shape[0]
    tm = TM_FFN
    n_tiles = n_rows // tm

    def xrow(i, te, nu):
        return (jnp.minimum(i, nu[0] - 1), 0)

    def wsel(i, te, nu):
        return (te[i], 0, 0)

    grid_spec = pltpu.PrefetchScalarGridSpec(
        num_scalar_prefetch=2,
        grid=(n_tiles,),
        in_specs=[pl.BlockSpec((tm, D), xrow),
                  pl.BlockSpec((None, D, D_FF), wsel),
                  pl.BlockSpec((None, D, D_FF), wsel),
                  pl.BlockSpec((None, D_FF, D), wsel),
                  pl.BlockSpec((None, 1, D_FF), wsel),
                  pl.BlockSpec((None, 1, D_FF), wsel),
                  pl.BlockSpec((None, 1, D), wsel)],
        out_specs=pl.BlockSpec((tm, D), lambda i, te, nu: (i, 0)),
        scratch_shapes=[pltpu.VMEM((D, D_FF), BF16), pltpu.VMEM((D, D_FF), BF16),
                        pltpu.VMEM((D_FF, D), BF16)],
    )
    return pl.pallas_call(
        _ffn_kernel,
        grid_spec=grid_spec,
        out_shape=jax.ShapeDtypeStruct((n_rows, D), F32),
        compiler_params=_cparams(("arbitrary",)),
        name="ffn",
    )(tile_exp, n_used, xs, w_gate, w_up, w_down,
      b_gate.reshape(N_EXP, 1, D_FF), b_up.reshape(N_EXP, 1, D_FF), b_down.reshape(N_EXP, 1, D))


def _combine_kernel(pos_ref, y_hbm, route_ref, x1_ref, mod_ref, gpost_ref, o_ref, buf, sem):
    tm = x1_ref.shape[0]

    def issue(t, carry):
        for kk in range(TOP_K):
            p = pos_ref[t * TOP_K + kk]
            pltpu.make_async_copy(y_hbm.at[pl.ds(p, 1)], buf.at[kk, pl.ds(t, 1)], sem).start()
        return carry

    lax.fori_loop(0, tm, issue, 0)
    for kk in range(TOP_K):
        pltpu.make_async_copy(y_hbm.at[pl.ds(0, tm)], buf.at[kk], sem).wait()

    route = route_ref[...]
    lane = lax.broadcasted_iota(jnp.int32, route.shape, 1)
    moe = jnp.zeros((tm, D), F32)
    for kk in range(TOP_K):
        w = jnp.sum(jnp.where(lane == TOP_K + kk, route, 0.0), axis=-1, keepdims=True)
        moe = moe + w * buf[kk]
    gt_f = mod_ref[...][:, 5 * D:6 * D]
    o_ref[...] = x1_ref[...] + gt_f * _rms(moe, gpost_ref[...])


def _combine_call(pos_flat, y, route, x1, mod3, tiles_per_seq, gpost):
    n_tok = x1.shape[0]
    tm = TM_ROW
    return pl.pallas_call(
        _combine_kernel,
        grid=(n_tok // tm,),
        in_specs=[pl.BlockSpec((tm * TOP_K,), lambda i: (i,), memory_space=pltpu.SMEM),
                  pl.BlockSpec(memory_space=pl.ANY),
                  pl.BlockSpec((tm, LANES), lambda i: (i, 0)),
                  pl.BlockSpec((tm, D), lambda i: (i, 0)),
                  pl.BlockSpec((None, 1, 6 * D), lambda i: (i // tiles_per_seq, 0, 0)),
                  pl.BlockSpec((1, D), lambda i: (0, 0))],
        out_specs=pl.BlockSpec((tm, D), lambda i: (i, 0)),
        out_shape=jax.ShapeDtypeStruct((n_tok, D), F32),
        scratch_shapes=[pltpu.VMEM((TOP_K, tm, D), F32), pltpu.SemaphoreType.DMA(())],
        compiler_params=_cparams(("arbitrary",)),
        name="combine",
    )(pos_flat, y, route, x1, mod3, gpost)


def _rope_perm():
    q = MLA_ROPE // 4
    perm = np.concatenate([np.arange(q, 2 * q), np.arange(0, q),
                           np.arange(3 * q, 4 * q), np.arange(2 * q, 3 * q)])
    sign = np.concatenate([-np.ones(q), np.ones(q), -np.ones(q), np.ones(q)]).astype(np.float32)
    return perm, sign


def _rope_tables(seq):
    rows = seq // GRID_W
    r, col = jnp.meshgrid(jnp.arange(rows, dtype=F32), jnp.arange(GRID_W, dtype=F32),
                          indexing="ij")
    half = MLA_ROPE // 2
    inv_freq = ROPE_BASE ** (-jnp.arange(0, half, 2, dtype=F32) / half)
    ar = r.reshape(-1)[:, None] * inv_freq
    ac = col.reshape(-1)[:, None] * inv_freq
    cos = jnp.concatenate([jnp.cos(ar), jnp.cos(ar), jnp.cos(ac), jnp.cos(ac)], axis=-1)
    sin = jnp.concatenate([jnp.sin(ar), jnp.sin(ar), jnp.sin(ac), jnp.sin(ac)], axis=-1)
    return cos, sin


def _head_slots(parts):
    cols = []
    for p in parts:
        pad = HEAD_SLOT - p.shape[1]
        cols.append(jnp.pad(p, ((0, 0), (0, pad))))
    return jnp.concatenate(cols, axis=1)


def kernel(x, c, ctx, c_ctx, w_mod, b_mod, g_pre_mix, g_post_mix, g_pre_ffn, g_post_ffn, w_in,
           gla_w_a2_f, gla_b_a_f, gla_w_a2_b, gla_b_a_b, gla_g_norm, mla_g_q, mla_w_uq, mla_g_kv,
           mla_w_uk, mla_w_uv, w_br_gla, w_br_mla, w_out, router_w, router_b, w_gate, b_gate,
           w_up, b_up, w_down, b_down):
    depth = w_mod.shape[0]
    assert depth == 1, "single-layer block"
    batch, seq, d = x.shape
    ctx_len = ctx.shape[1]
    assert d == D and seq % TM_PROJ == 0 and (batch * ctx_len) % TM_PROJ == 0
    assert TM_PROJ % ctx_len == 0 or ctx_len % TM_PROJ == 0
    n_tok = batch * seq
    perm, sign = _rope_perm()
    sign = jnp.asarray(sign)

    cc = jnp.zeros((16, D), F32).at[:batch].set(c).at[batch].set(c_ctx)
    mod = _mod_call(cc, w_mod[0], b_mod[0])
    mod3 = mod.reshape(16, 1, 6 * D)

    wi = w_in[0]
    kr = wi[:, O_KR:O_KR + MLA_ROPE]
    small = jnp.concatenate([wi[:, O_AF:O_AF + GLA_RANK], wi[:, O_AB:O_AB + GLA_RANK], kr,
                             kr[:, perm] * sign, jnp.zeros((D, LANES - X_KRS - MLA_ROPE), F32)],
                            axis=1)
    w1 = jnp.concatenate([wi[:, O_Q:O_G], wi[:, O_DKV:O_DKV + MLA_KVR], small], axis=1).astype(BF16)
    w2 = jnp.concatenate([wi[:, O_G:O_G + GLA_V], wi[:, O_DQ:O_DQ + MLA_QR], wi[:, O_MG:]],
                         axis=1).astype(BF16)
    wa = jnp.zeros((LANES, 2 * GLA_QK), F32)
    wa = wa.at[X_AF:X_AF + GLA_RANK, 0:GLA_QK].set(gla_w_a2_f[0])
    wa = wa.at[X_AB:X_AB + GLA_RANK, GLA_QK:].set(gla_w_a2_b[0]).astype(BF16)
    ba = jnp.concatenate([gla_b_a_f[0], gla_b_a_b[0]]).reshape(1, 2 * GLA_QK)

    uk = mla_w_uk[0]
    uv = mla_w_uv[0]
    uq = mla_w_uq[0]
    eye = jnp.eye(MLA_ROPE, dtype=F32)
    wk_top = _head_slots([uk[:, h * MLA_NOPE:(h + 1) * MLA_NOPE] for h in range(MLA_H)])
    place = _head_slots([jnp.pad(eye, ((0, 0), (MLA_NOPE, 0))) for _ in range(MLA_H)])
    wk_bot = jnp.zeros((LANES, MLA_W), F32)
    wk_bot = wk_bot.at[X_KR:X_KR + MLA_ROPE].set(place).at[X_KRS:X_KRS + MLA_ROPE].set(place)
    wk = jnp.concatenate([wk_top, wk_bot], axis=0).astype(BF16)
    wv = _head_slots([jnp.pad(uv[:, h * MLA_V:(h + 1) * MLA_V], ((0, 0), ((h % 2) * MLA_V, 0)))
                      for h in range(MLA_H)]).astype(BF16)
    wq_a = _head_slots([uq[:, h * MLA_QKD:(h + 1) * MLA_QKD] for h in range(MLA_H)])
    wq_b = _head_slots([jnp.pad(uq[:, h * MLA_QKD + MLA_NOPE:(h + 1) * MLA_QKD][:, perm] * sign,
                                ((0, 0), (MLA_NOPE, 0))) for h in range(MLA_H)])
    wq = jnp.concatenate([wq_a, wq_b], axis=1).astype(BF16)

    cos, sin = _rope_tables(seq)
    zeros32 = jnp.zeros((seq, MLA_ROPE), F32)
    t1_x = jnp.concatenate([zeros32, cos, sin, zeros32], axis=1)
    t1_c = jnp.broadcast_to(
        jnp.concatenate([jnp.zeros((MLA_ROPE,)), jnp.ones((MLA_ROPE,)),
                         jnp.zeros((2 * MLA_ROPE,))]).astype(F32), (TM_PROJ, LANES))
    ones64 = jnp.ones((seq, MLA_NOPE), F32)
    cq = jnp.tile(jnp.concatenate([ones64, cos, zeros32], axis=1), (1, MLA_H)) * MLA_SCALE
    sq = jnp.tile(jnp.concatenate([jnp.zeros((seq, MLA_NOPE), F32), sin, zeros32], axis=1),
                  (1, MLA_H)) * MLA_SCALE

    gpre = g_pre_mix[0].reshape(1, D)
    gkv = mla_g_kv[0].reshape(1, MLA_KVR)
    gq = mla_g_q[0].reshape(1, MLA_QR)
    tiles_per_seq = seq // TM_PROJ

    xf = x.reshape(n_tok, D)
    cf = ctx.reshape(batch * ctx_len, D)
    (gqx, gkx, gvx, lfx, lbx, mkx, mvx, sz, mq, sg, sm) = _inproj_call(
        True, xf, mod3, lambda i: i // tiles_per_seq, tiles_per_seq, gpre, w1, wa, ba, wk, wv,
        gkv, t1_x, extra=(w2, gq, wq, cq, sq))
    (gqc, gkc, gvc, lfc, lbc, mkc, mvc) = _inproj_call(
        False, cf, mod3, lambda i: batch, 1, gpre, w1, wa, ba, wk, wv, gkv, t1_c)

    og = _gla_call((gqx, gkx, gvx, lfx, lbx), (gqc, gkc, gvc, lfc, lbc), batch, seq, ctx_len)
    om = _mla_call(mq, mkx, mvx, mkc, mvc, batch, seq, ctx_len)

    wr = jnp.pad(router_w[0], ((0, 0), (0, LANES - N_EXP)))
    wrh = wr.astype(BF16)
    wrl = (wr - wrh.astype(F32)).astype(BF16)
    br =jnp.pad(router_b[0], (0, LANES - N_EXP)).reshape(1, LANES)
    x1, h2, route, cnt = _mixout_call(
        og, sz, om, sg, sm, xf, mod3, tiles_per_seq, gla_g_norm[0].reshape(1, GLA_DV),
        w_br_gla[0].astype(BF16), w_br_mla[0].astype(BF16), w_out[0].astype(BF16),
        g_post_mix[0].reshape(1, D), g_pre_ffn[0].reshape(1, D), wrh, wrl, br)

    e_idx = route[:, 0:TOP_K].astype(jnp.int32)
    rank = route[:, 2 * TOP_K:3 * TOP_K].astype(jnp.int32)
    counts = cnt[0, :N_EXP].astype(jnp.int32)
    padded = ((counts + TM_FFN - 1) // TM_FFN) * TM_FFN
    ends = jnp.cumsum(padded)
    off = ends - padded
    pos = jnp.sum(jnp.where(e_idx[..., None] == jnp.arange(N_EXP), off, 0), axis=-1) + rank
    pos_flat = pos.reshape(-1)
    n_tiles = (n_tok * TOP_K) // TM_FFN + N_EXP
    tile_start = jnp.arange(n_tiles, dtype=jnp.int32) * TM_FFN
    tile_exp = jnp.minimum(jnp.sum((ends[None, :] <= tile_start[:, None]).astype(jnp.int32), axis=1),
                           N_EXP - 1)
    n_used = (ends[-1:] // TM_FFN).astype(jnp.int32)

    pad_info = jnp.concatenate([off + counts, ends[-1:], ends]).astype(jnp.int32)
    xs = _dispatch_call(pad_info, pos_flat, h2, n_tiles * TM_FFN)
    y = _ffn_call(tile_exp, n_used, xs, w_gate[0], w_up[0], w_down[0], b_gate[0], b_up[0],
                  b_down[0])
    out = _combine_call(pos_flat, y, route, x1, mod3, seq // TM_ROW,
                        g_post_ffn[0].reshape(1, D))
    return out.reshape(batch, seq, D)
```

```python
import functools

import jax
import jax.numpy as jnp
import numpy as np
from jax import lax
from jax.experimental import pallas as pl
from jax.experimental.pallas import tpu as pltpu

F32 = jnp.float32
BF16 = jnp.bfloat16

D = 1024
EPS = 1e-6
GRID_W = 64

GLA_H = 4
GLA_DK = 64
GLA_DV = 128
GLA_RANK = 16
GLA_TAU = 16.0
GLA_CHUNK = 64
GLA_BLOCK = 256
GLA_QK = GLA_H * GLA_DK
GLA_V = GLA_H * GLA_DV

MLA_H = 8
MLA_QR = 256
MLA_KVR = 128
MLA_NOPE = 64
MLA_ROPE = 32
MLA_V = 64
MLA_QKD = MLA_NOPE + MLA_ROPE
MLA_SCALE = MLA_QKD ** -0.5
ROPE_BASE = 10000.0
HEAD_SLOT = 128
MLA_W = MLA_H * HEAD_SLOT

N_EXP = 32
TOP_K = 4
D_FF = 1024
SWIGLU_LIMIT = 7.0
SWIGLU_ALPHA = 1.702

LANES = 128
SUBLANES = 8
TM_PROJ = 512
TQ = 512
TM_FFN = 512
ZERO_ROWS = 256
STAGE_ROWS = -(-(TM_PROJ * TOP_K + N_EXP * (SUBLANES - 1)) // 256) * 256
ROUTE_E, ROUTE_W, ROUTE_P = 0, TOP_K, 2 * TOP_K

_OFF = np.cumsum([0, GLA_QK, GLA_QK, GLA_V, GLA_V, GLA_RANK, GLA_RANK,
                  MLA_QR, MLA_KVR, MLA_ROPE, D, D])
(O_Q, O_K, O_V, O_G, O_AF, O_AB, O_DQ, O_DKV, O_KR, O_MG, O_MM, _) = _OFF.tolist()

X_AF, X_AB, X_KR, X_KRS = 0, 16, 32, 64

VMEM_LIMIT = 56 * 1024 * 1024


def _cparams(sem):
    return pltpu.CompilerParams(dimension_semantics=sem, vmem_limit_bytes=VMEM_LIMIT)


def _resident(arr):
    nd = arr.ndim
    return pl.BlockSpec(arr.shape, lambda *_: (0,) * nd, pipeline_mode=pl.Buffered(1))


def _rms(x, g):
    return x * lax.rsqrt(jnp.mean(x * x, axis=-1, keepdims=True) + EPS) * g


def _sigmoid(x):
    return 1.0 / (1.0 + jnp.exp(-x))


def _log_sigmoid(x):
    return jnp.minimum(x, 0.0) - jnp.log1p(jnp.exp(-jnp.abs(x)))


def _dot(a, b):
    return jnp.dot(a, b, preferred_element_type=F32)


def _dot_nt(a, b):
    return lax.dot_general(a, b, (((1,), (1,)), ((), ())), preferred_element_type=F32)


def _dot_tn(a, b):
    return lax.dot_general(a, b, (((0,), (0,)), ((), ())), preferred_element_type=F32)


def _mod_kernel(c_ref, w_ref, b_ref, o_ref):
    c = c_ref[...]
    s = (c * _sigmoid(c)).astype(BF16)
    o_ref[...] = _dot(s, w_ref[...].astype(BF16)) + b_ref[...]


def _mod_call(cc, w_mod, b_mod):
    n = w_mod.shape[1]
    tn = 1536
    return pl.pallas_call(
        _mod_kernel,
        grid=(n // tn,),
        in_specs=[pl.BlockSpec((16, D), lambda j: (0, 0)),
                  pl.BlockSpec((D, tn), lambda j: (0, j)),
                  pl.BlockSpec((1, tn), lambda j: (0, j))],
        out_specs=pl.BlockSpec((16, tn), lambda j: (0, j)),
        out_shape=jax.ShapeDtypeStruct((16, n), F32),
        compiler_params=_cparams(("arbitrary",)),
        name="mod",
    )(cc, w_mod, b_mod.reshape(1, n))


def _inproj_kernel(with_q, x_ref, mod_ref, gpre_ref, w1_ref, wa_ref, ba_ref, wk_ref,
                   wv_ref, gkv_ref, t1_ref, *rest):
    if with_q:
        (w2_ref, gq_ref, wq_ref, cq_ref, sq_ref,
         q_o, k_o, v_o, lf_o, lb_o, mk_o, mv_o, sz_o, mq_o, sg_o, sm_o) = rest
    else:
        (q_o, k_o, v_o, lf_o, lb_o, mk_o, mv_o) = rest
    x = x_ref[...]
    mod = mod_ref[...]
    sh = mod[:, 0:D]
    sc = mod[:, D:2 * D]
    h = (_rms(x, gpre_ref[...]) * (1.0 + sc) + sh).astype(BF16)

    z1 = _dot(h, w1_ref[...])
    q_o[...] = (z1[:, 0:GLA_QK] * (GLA_DK ** -0.5)).astype(BF16)
    k_o[...] = z1[:, GLA_QK:2 * GLA_QK].astype(BF16)
    v_o[...] = z1[:, 2 * GLA_QK:2 * GLA_QK + GLA_V].astype(BF16)
    o_dkv = 2 * GLA_QK + GLA_V
    ckv = _rms(z1[:, o_dkv:o_dkv + MLA_KVR], gkv_ref[...])
    xs = z1[:, o_dkv + MLA_KVR:o_dkv + MLA_KVR + LANES]

    la = _log_sigmoid(_dot(xs.astype(BF16), wa_ref[...]) + ba_ref[...]) * (1.0 / GLA_TAU)
    lf_o[...] = la[:, 0:GLA_QK]
    lb_o[...] = la[:, GLA_QK:2 * GLA_QK]

    lhs_k = jnp.concatenate([ckv, xs * t1_ref[...]], axis=-1).astype(BF16)
    mk_o[...] = _dot(lhs_k, wk_ref[...]).astype(BF16)
    mv_o[...] = _dot(ckv.astype(BF16), wv_ref[...]).astype(BF16)

    if with_q:
        zg = _dot(h, w2_ref[:, 0:GLA_V])
        sz_o[...] = (zg * _sigmoid(zg)).astype(BF16)
        n = _rms(_dot(h, w2_ref[:, GLA_V:GLA_V + MLA_QR]), gq_ref[...]).astype(BF16)
        mq_o[...] = (_dot(n, wq_ref[:, 0:MLA_W]) * cq_ref[...]
                     + _dot(n, wq_ref[:, MLA_W:2 * MLA_W]) * sq_ref[...]).astype(BF16)
        o_mg = GLA_V + MLA_QR
        sg_o[...] = _sigmoid(_dot(h, w2_ref[:, o_mg:o_mg + D])).astype(BF16)
        sm_o[...] = _sigmoid(_dot(h, w2_ref[:, o_mg + D:o_mg + 2 * D])).astype(BF16)


def _inproj_call(with_q, xf, mod3, mod_row_fn, tiles_per_seq, gpre, w1, wa, ba, wk, wv,
                 gkv, t1, extra=()):
    n_tok = xf.shape[0]
    tm = TM_PROJ
    grid = (n_tok // tm,)
    row = lambda i: (i, 0)
    tab = lambda i: (i % tiles_per_seq, 0)
    in_specs = [
        pl.BlockSpec((tm, D), row),
        pl.BlockSpec((None, 1, 6 * D), lambda i: (mod_row_fn(i), 0, 0)),
        _resident(gpre), _resident(w1), _resident(wa), _resident(ba), _resident(wk),
        _resident(wv), _resident(gkv),
        pl.BlockSpec((tm, LANES), tab),
    ]
    widths = [(GLA_QK, BF16), (GLA_QK, BF16), (GLA_V, BF16), (GLA_QK, F32), (GLA_QK, F32),
              (MLA_W, BF16), (MLA_W, BF16)]
    args = [xf, mod3, gpre, w1, wa, ba, wk, wv, gkv, t1]
    if with_q:
        w2, gq, wq, cq, sq = extra
        in_specs += [_resident(w2), _resident(gq), _resident(wq),
                     pl.BlockSpec((tm, MLA_W), tab), pl.BlockSpec((tm, MLA_W), tab)]
        args += [w2, gq, wq, cq, sq]
        widths += [(GLA_V, BF16), (MLA_W, BF16), (D, BF16), (D, BF16)]
    return pl.pallas_call(
        functools.partial(_inproj_kernel, with_q),
        grid=grid,
        in_specs=in_specs,
        out_specs=[pl.BlockSpec((tm, w), row) for w, _ in widths],
        out_shape=[jax.ShapeDtypeStruct((n_tok, w), dt) for w, dt in widths],
        compiler_params=_cparams(("arbitrary",)),
        name="inproj_x" if with_q else "inproj_ctx",
    )(*args)


def _gla_kernel(qx, kx, vx, lfx, lbx, qc, kc, vc, lfc, lbc, o_ref, sf_ref, sb_ref):
    C = GLA_CHUNK
    R = GLA_BLOCK
    n_sub = R // C
    rr = lax.broadcasted_iota(jnp.int32, (R, R), 0)
    cc = lax.broadcasted_iota(jnp.int32, (R, R), 1)
    same = (rr // C) == (cc // C)
    tbd_f = jnp.logical_and(same, cc <= rr).astype(BF16)
    tbd_b = jnp.logical_and(same, cc >= rr).astype(BF16)
    row = lax.broadcasted_iota(jnp.int32, (C, GLA_QK), 0)
    col = lax.broadcasted_iota(jnp.int32, (C, GLA_QK), 1)
    head_qk = col // GLA_DK
    tri4_f = (col % C) <= row
    tri4_b = (col % C) >= row
    head_v = lax.broadcasted_iota(jnp.int32, (C, GLA_V), 1) // GLA_DV

    def stack_masked(x, head_of_lane):
        return jnp.concatenate(
            [jnp.where(head_of_lane == hh, x, jnp.zeros_like(x)) for hh in range(GLA_H)], axis=0)

    def block(q_ref, k_ref, v_ref, la_ref, row0, fwd, s_ref, emit):
        sl = pl.ds(row0, R)
        la = la_ref[sl, :]
        la_hi = la.astype(BF16)
        la_lo = (la - la_hi.astype(F32)).astype(BF16)
        tbd = tbd_f if fwd else tbd_b
        cum = _dot(tbd, la_hi) + _dot(tbd, la_lo)
        k = k_ref[sl, :].astype(F32)
        v = v_ref[sl, :]
        if emit:
            q = q_ref[sl, :].astype(F32)
        st = s_ref[...]
        outs = [None] * n_sub
        for ci in (range(n_sub) if fwd else reversed(range(n_sub))):
            rs = slice(ci * C, (ci + 1) * C)
            cum_c = cum[rs]
            tot = cum_c[C - 1:C, :] if fwd else cum_c[0:1, :]
            k_c = k[rs]
            v_c = v[rs]
            ke_bd = stack_masked((k_c * jnp.exp(tot - cum_c)).astype(BF16), head_qk)
            v_stack = jnp.concatenate(
                [v_c[:, hh * GLA_DV:(hh + 1) * GLA_DV] for hh in range(GLA_H)], axis=0)
            if emit:
                qd = (q[rs] * jnp.exp(cum_c)).astype(BF16)
                ki_bd = stack_masked((k_c * jnp.exp(-cum_c)).astype(BF16), head_qk)
                att = _dot_nt(qd, ki_bd)
                att = jnp.where(tri4_f if fwd else tri4_b, att, 0.0).astype(BF16)
                o_intra = _dot(att, stack_masked(v_c, head_v))
                oi = _dot_nt(stack_masked(qd, head_qk), st.astype(BF16))
                o_inter = jnp.concatenate([oi[hh * C:(hh + 1) * C] for hh in range(GLA_H)],
                                          axis=-1)
                outs[ci] = o_intra + o_inter
            st = st * jnp.exp(tot) + _dot_tn(v_stack, ke_bd)
        s_ref[...] = st
        if emit:
            o_ref[sl, :] += jnp.concatenate(outs, axis=0)

    sf_ref[...] = jnp.zeros_like(sf_ref)
    sb_ref[...] = jnp.zeros_like(sb_ref)
    o_ref[...] = jnp.zeros_like(o_ref)
    n_ctx = qc.shape[0] // R
    n_x = qx.shape[0] // R
    for i in range(n_ctx):
        block(qc, kc, vc, lfc, i * R, True, sf_ref, False)
        block(qc, kc, vc, lbc, (n_ctx - 1 - i) * R, False, sb_ref, False)

    def body(i, carry):
        block(qx, kx, vx, lfx, pl.multiple_of(i * R, R), True, sf_ref, True)
        block(qx, kx, vx, lbx, pl.multiple_of((n_x - 1 - i) * R, R), False, sb_ref, True)
        return carry

    lax.fori_loop(0, n_x, body, 0)


def _gla_call(fx, fc, batch, seq, ctx_len):
    qx, kx, vx, lfx, lbx = fx
    qc, kc, vc, lfc, lbc = fc

    def spec(rows, w):
        return pl.BlockSpec((rows, w), lambda b: (b, 0))

    return pl.pallas_call(
        _gla_kernel,
        grid=(batch,),
        in_specs=[spec(seq, GLA_QK), spec(seq, GLA_QK), spec(seq, GLA_V), spec(seq, GLA_QK),
                  spec(seq, GLA_QK),
                  spec(ctx_len, GLA_QK), spec(ctx_len, GLA_QK), spec(ctx_len, GLA_V),
                  spec(ctx_len, GLA_QK), spec(ctx_len, GLA_QK)],
        out_specs=spec(seq, GLA_V),
        out_shape=jax.ShapeDtypeStruct((batch * seq, GLA_V), F32),
        scratch_shapes=[pltpu.VMEM((GLA_DV, GLA_QK), F32), pltpu.VMEM((GLA_DV, GLA_QK), F32)],
        compiler_params=_cparams(("arbitrary",)),
        name="gla",
    )(qx, kx, vx, lfx, lbx, qc, kc, vc, lfc, lbc)


def _mla_kernel(q_ref, kx_ref, vx_ref, kc_ref, vc_ref, o_ref):
    for j in range(MLA_H // 2):
        acc = None
        for hh in (2 * j, 2 * j + 1):
            sl = slice(hh * HEAD_SLOT, (hh + 1) * HEAD_SLOT)
            q = q_ref[:, sl]
            sx = _dot_nt(q, kx_ref[:, sl])
            sc = _dot_nt(q, kc_ref[:, sl])
            m = jnp.maximum(jnp.max(sx, axis=-1, keepdims=True),
                            jnp.max(sc, axis=-1, keepdims=True))
            px = jnp.exp(sx - m)
            pc = jnp.exp(sc - m)
            l = jnp.sum(px, axis=-1, keepdims=True) + jnp.sum(pc, axis=-1, keepdims=True)
            o = _dot(px.astype(BF16), vx_ref[:, sl]) + _dot(pc.astype(BF16), vc_ref[:, sl])
            o = o / l
            acc = o if acc is None else acc + o
        o_ref[:, j * HEAD_SLOT:(j + 1) * HEAD_SLOT] = acc.astype(BF16)


def _mla_call(mq, mkx, mvx, mkc, mvc, batch, seq, ctx_len):
    nq = seq // TQ
    return pl.pallas_call(
        _mla_kernel,
        grid=(batch, nq),
        in_specs=[pl.BlockSpec((TQ, MLA_W), lambda b, i: (b * nq + i, 0)),
                  pl.BlockSpec((seq, MLA_W), lambda b, i: (b, 0)),
                  pl.BlockSpec((seq, MLA_W), lambda b, i: (b, 0)),
                  pl.BlockSpec((ctx_len, MLA_W), lambda b, i: (b, 0)),
                  pl.BlockSpec((ctx_len, MLA_W), lambda b, i: (b, 0))],
        out_specs=pl.BlockSpec((TQ, MLA_H * MLA_V), lambda b, i: (b * nq + i, 0)),
        out_shape=jax.ShapeDtypeStruct((batch * seq, MLA_H * MLA_V), BF16),
        compiler_params=_cparams(("arbitrary", "arbitrary")),
        name="mla",
    )(mq, mkx, mvx, mkc, mvc)


def _mixout_kernel(og_ref, sz_ref, om_ref, sg_ref, sm_ref, x_ref, mod_ref, gn_ref, wbg_ref,
                   wbm_ref, wo_ref, gpost_ref, gffn_ref, wrh_ref, wrl_ref, br_ref,
                   x1_o, h2_o, route_o, cnt_o):
    tm = x_ref.shape[0]
    mod = mod_ref[...]
    gt_a = mod[:, 2 * D:3 * D]
    sh_f = mod[:, 3 * D:4 * D]
    sc_f = mod[:, 4 * D:5 * D]

    og = og_ref[...]
    gn = gn_ref[...]
    parts = [_rms(og[:, hh * GLA_DV:(hh + 1) * GLA_DV], gn) for hh in range(GLA_H)]
    a = (jnp.concatenate(parts, axis=-1) * sz_ref[...].astype(F32)).astype(BF16)
    br_g = _dot(a, wbg_ref[...])
    br_m = _dot(om_ref[...], wbm_ref[...])
    merged = (sg_ref[...].astype(F32) * br_g + sm_ref[...].astype(F32) * br_m).astype(BF16)
    mo = _dot(merged, wo_ref[...])
    x1 = x_ref[...] + gt_a * _rms(mo, gpost_ref[...])
    x1_o[...] = x1
    h2 = _rms(x1, gffn_ref[...]) * (1.0 + sc_f) + sh_f
    h2_o[...] = h2.astype(BF16)

    lane = lax.broadcasted_iota(jnp.int32, (tm, LANES), 1)
    h_hi = h2.astype(BF16)
    h_lo = (h2 - h_hi.astype(F32)).astype(BF16)
    logits = (_dot(h_hi, wrh_ref[...]) + _dot(h_lo, wrh_ref[...]) + _dot(h_hi, wrl_ref[...])
              + _dot(h_lo, wrl_ref[...]) + br_ref[...])
    neg = jnp.float32(-jnp.inf)
    lg = jnp.where(lane < N_EXP, logits, neg)
    lane_f = lane.astype(F32)
    hots, vals = [], []
    for _k in range(TOP_K):
        mx = jnp.max(lg, axis=-1, keepdims=True)
        idx = jnp.min(jnp.where(lg == mx, lane_f, float(LANES)), axis=-1, keepdims=True)
        hot = lane_f == idx
        lg = jnp.where(hot, neg, lg)
        hots.append(hot)
        vals.append(mx)
    es = [jnp.exp(v - vals[0]) for v in vals]
    den = es[0] + es[1] + es[2] + es[3]
    ws = [e / den for e in es]

    msum = jnp.zeros((tm, LANES), F32)
    for hot in hots:
        msum = msum + hot.astype(F32)
    rr = lax.broadcasted_iota(jnp.int32, (tm, tm), 0)
    cc = lax.broadcasted_iota(jnp.int32, (tm, tm), 1)
    lower = (cc < rr).astype(BF16)
    prior = _dot(lower, msum.astype(BF16))
    cnt = jnp.sum(msum, axis=0, keepdims=True)
    seg = jnp.floor((cnt + (SUBLANES - 1.0)) * (1.0 / SUBLANES)) * SUBLANES
    er = lax.broadcasted_iota(jnp.int32, (LANES, LANES), 0)
    ec = lax.broadcasted_iota(jnp.int32, (LANES, LANES), 1)
    before = (er < ec).astype(BF16)
    seg_start = _dot(jnp.broadcast_to(seg, (SUBLANES, LANES)).astype(BF16), before)[0:1, :]
    slot_of = prior + seg_start

    route = jnp.zeros((tm, LANES), F32)
    for kk in range(TOP_K):
        hotf = hots[kk].astype(F32)
        e_col = jnp.sum(hotf * lane_f, axis=-1, keepdims=True)
        p_col = jnp.sum(hotf * slot_of, axis=-1, keepdims=True)
        route = jnp.where(lane == ROUTE_E + kk, e_col, route)
        route = jnp.where(lane == ROUTE_W + kk, ws[kk], route)
        route = jnp.where(lane == ROUTE_P + kk, p_col, route)
    route_o[...] = route
    cnt_o[...] = jnp.broadcast_to(cnt, cnt_o.shape)


def _mixout_call(og, sz, om, sg, sm, xf, mod3, tiles_per_seq, gn, wbg, wbm, wo, gpost, gffn,
                 wrh, wrl, br):
    n_tok = xf.shape[0]
    tm = TM_PROJ
    row = lambda i: (i, 0)

    def rs(w):
        return pl.BlockSpec((tm, w), row)

    return pl.pallas_call(
        _mixout_kernel,
        grid=(n_tok // tm,),
        in_specs=[rs(GLA_V), rs(GLA_V), rs(MLA_H * MLA_V), rs(D), rs(D), rs(D),
                  pl.BlockSpec((None, 1, 6 * D), lambda i: (i // tiles_per_seq, 0, 0)),
                  _resident(gn), _resident(wbg), _resident(wbm), _resident(wo),
                  _resident(gpost), _resident(gffn), _resident(wrh), _resident(wrl),
                  _resident(br)],
        out_specs=[rs(D), rs(D), rs(LANES),
                   pl.BlockSpec((None, SUBLANES, LANES), lambda i: (i, 0, 0))],
        out_shape=[jax.ShapeDtypeStruct((n_tok, D), F32), jax.ShapeDtypeStruct((n_tok, D), BF16),
                   jax.ShapeDtypeStruct((n_tok, LANES), F32),
                   jax.ShapeDtypeStruct((n_tok // tm, SUBLANES, LANES), F32)],
        compiler_params=_cparams(("arbitrary",)),
        name="mixout",
    )(og, sz, om, sg, sm, xf, mod3, gn, wbg, wbm, wo, gpost, gffn, wrh, wrl, br)


def _sorted_rows(n_tok):
    n_steps = n_tok // TM_PROJ
    worst_pad = n_steps * N_EXP * (SUBLANES - 1) + N_EXP * TM_FFN
    per_step = -(-worst_pad // (n_steps * ZERO_ROWS)) * ZERO_ROWS
    n_rows = n_tok * TOP_K + per_step * n_steps
    assert n_rows % TM_FFN == 0
    return n_rows


def _seg_copies(src_ref, dst_ref, src0, dst0, length, sem):
    for b in reversed(range(SUBLANES.bit_length() - 1, TM_PROJ.bit_length())):
        size = 1 << b
        off = (length >> (b + 1)) << (b + 1)

        @pl.when((length & size) != 0)
        def _():
            pltpu.make_async_copy(
                src_ref.at[pl.ds(pl.multiple_of(src0 + off, SUBLANES), size)],
                dst_ref.at[pl.ds(pl.multiple_of(dst0 + off, SUBLANES), size)], sem).start()


def _wait_rows(src_ref, dst_ref, total, sem):
    for b in reversed(range(SUBLANES.bit_length() - 1, STAGE_ROWS.bit_length())):
        size = 1 << b

        @pl.when((total & size) != 0)
        def _():
            pltpu.make_async_copy(src_ref.at[pl.ds(0, size)], dst_ref.at[pl.ds(0, size)],
                                  sem).wait()


def _zero_rows(zero_ref, dst_ref, start, end, sem):
    first = jnp.minimum(((start + SUBLANES - 1) // SUBLANES) * SUBLANES, end)
    n_single = first - start
    length = end - first
    for wait in (False, True):
        def run(cp):
            if wait:
                cp.wait()
            else:
                cp.start()

        for j in range(SUBLANES - 1):
            @pl.when(j < n_single)
            def _():
                run(pltpu.make_async_copy(zero_ref.at[pl.ds(0, 1)],
                                          dst_ref.at[pl.ds(start + j, 1)], sem))

        for b in reversed(range(SUBLANES.bit_length() - 1, ZERO_ROWS.bit_length())):
            size = 1 << b
            off = (length >> (b + 1)) << (b + 1)

            @pl.when((length & size) != 0)
            def _():
                run(pltpu.make_async_copy(
                    zero_ref.at[pl.ds(0, size)],
                    dst_ref.at[pl.ds(pl.multiple_of(first + off, SUBLANES), size)], sem))


def _slot_onehot(route_t, n_slots):
    tm = route_t.shape[1]
    slot = lax.broadcasted_iota(jnp.int32, (n_slots, tm), 0).astype(F32)
    hit = slot == route_t[ROUTE_P:ROUTE_P + 1, :]
    for kk in range(1, TOP_K):
        hit = jnp.logical_or(hit, slot == route_t[ROUTE_P + kk:ROUTE_P + kk + 1, :])
    return jnp.where(hit, 1.0, 0.0).astype(BF16)


def _dispatch_kernel(meta_ref, pad_ref, route_ref, h_ref, xs_out, stage, zero_ref, sems, zsem):
    i = pl.program_id(0)
    n_steps = pl.num_programs(0)
    n_rows = xs_out.shape[0]
    n_seg = n_steps * N_EXP
    slot = i % 2

    @pl.when(i == 0)
    def _():
        zero_ref[...] = jnp.zeros_like(zero_ref)

    onehot = _slot_onehot(route_ref[...].T, STAGE_ROWS)
    stage[slot] = _dot(onehot, h_ref[...])

    def issue(e, carry):
        s = i * N_EXP + e
        _seg_copies(stage.at[slot], xs_out, meta_ref[s], meta_ref[2 * n_seg + s],
                    meta_ref[n_seg + s], sems.at[slot])
        return carry

    lax.fori_loop(0, N_EXP, issue, 0)

    per = -(-N_EXP // n_steps)
    for j in range(per):
        e = i * per + j

        @pl.when(e < N_EXP)
        def _():
            ec = jnp.minimum(e, N_EXP - 1)
            _zero_rows(zero_ref, xs_out, pad_ref[ec], pad_ref[N_EXP + 1 + ec], zsem)

    total_end = pad_ref[N_EXP]
    tail_rows = n_rows - n_steps * TM_PROJ * TOP_K
    per_step = tail_rows // n_steps
    for j in range(per_step // ZERO_ROWS):
        seg_end = (n_rows - tail_rows) + i * per_step + (j + 1) * ZERO_ROWS
        zlen = jnp.clip(seg_end - total_end, 0, ZERO_ROWS)
        _zero_rows(zero_ref, xs_out, seg_end - zlen, seg_end, zsem)

    @pl.when(i >= 1)
    def _():
        _wait_rows(stage.at[1 - slot], xs_out, meta_ref[3 * n_seg + i - 1], sems.at[1 - slot])

    @pl.when(i == n_steps - 1)
    def _():
        _wait_rows(stage.at[slot], xs_out, meta_ref[3 * n_seg + i], sems.at[slot])


def _dispatch_call(meta, pad_info, route, h2, n_rows):
    n_tok = h2.shape[0]
    tm = TM_PROJ
    n_steps = n_tok // tm
    assert (n_rows - n_tok * TOP_K) % (n_steps * ZERO_ROWS) == 0 and TM_FFN <= 2 * ZERO_ROWS
    grid_spec = pltpu.PrefetchScalarGridSpec(
        num_scalar_prefetch=2,
        grid=(n_steps,),
        in_specs=[pl.BlockSpec((tm, LANES), lambda i, m, p: (i, 0)),
                  pl.BlockSpec((tm, D), lambda i, m, p: (i, 0))],
        out_specs=pl.BlockSpec(memory_space=pl.ANY),
        scratch_shapes=[pltpu.VMEM((2, STAGE_ROWS, D), F32), pltpu.VMEM((ZERO_ROWS, D), F32),
                        pltpu.SemaphoreType.DMA((2,)), pltpu.SemaphoreType.DMA(())],
    )
    return pl.pallas_call(
        _dispatch_kernel,
        grid_spec=grid_spec,
        out_shape=jax.ShapeDtypeStruct((n_rows, D), F32),
        compiler_params=_cparams(("arbitrary",)),
        name="dispatch",
    )(meta, pad_info, route, h2)


def _ffn_kernel(te_ref, nu_ref, x_ref, wg_ref, wu_ref, wd_ref, bg_ref, bu_ref, bd_ref, y_ref,
                wgb, wub, wdb):
    i = pl.program_id(0)
    first = jnp.logical_or(i == 0, te_ref[i] != te_ref[jnp.maximum(i - 1, 0)])
    used = i < nu_ref[0]

    @pl.when(jnp.logical_and(used, first))
    def _():
        wgb[...] = wg_ref[...].astype(BF16)
        wub[...] = wu_ref[...].astype(BF16)
        wdb[...] = wd_ref[...].astype(BF16)

    @pl.when(used)
    def _():
        x = x_ref[...].astype(BF16)
        gate = jnp.minimum(_dot(x, wgb[...]) + bg_ref[...], SWIGLU_LIMIT)
        up = jnp.clip(_dot(x, wub[...]) + bu_ref[...], -SWIGLU_LIMIT, SWIGLU_LIMIT)
        act = ((up + 1.0) * gate * _sigmoid(SWIGLU_ALPHA * gate)).astype(BF16)
        y_ref[...] = _dot(act, wdb[...]) + bd_ref[...]

    @pl.when(jnp.logical_not(used))
    def _():
        y_ref[...] = jnp.zeros_like(y_ref)


def _ffn_call(tile_exp, n_used, xs, w_gate, w_up, w_down, b_gate, b_up, b_down):
    n_rows = xs.shape[0]
    tm = TM_FFN
    n_tiles = n_rows // tm

    def xrow(i, te, nu):
        return (jnp.maximum(jnp.minimum(i, nu[0] - 1), 0), 0)

    def wsel(i, te, nu):
        return (te[i], 0, 0)

    grid_spec = pltpu.PrefetchScalarGridSpec(
        num_scalar_prefetch=2,
        grid=(n_tiles,),
        in_specs=[pl.BlockSpec((tm, D), xrow),
                  pl.BlockSpec((None, D, D_FF), wsel),
                  pl.BlockSpec((None, D, D_FF), wsel),
                  pl.BlockSpec((None, D_FF, D), wsel),
                  pl.BlockSpec((None, 1, D_FF), wsel),
                  pl.BlockSpec((None, 1, D_FF), wsel),
                  pl.BlockSpec((None, 1, D), wsel)],
        out_specs=pl.BlockSpec((tm, D), lambda i, te, nu: (i, 0)),
        scratch_shapes=[pltpu.VMEM((D, D_FF), BF16), pltpu.VMEM((D, D_FF), BF16),
                        pltpu.VMEM((D_FF, D), BF16)],
    )
    return pl.pallas_call(
        _ffn_kernel,
        grid_spec=grid_spec,
        out_shape=jax.ShapeDtypeStruct((n_rows, D), F32),
        compiler_params=_cparams(("arbitrary",)),
        name="ffn",
    )(tile_exp, n_used, xs, w_gate, w_up, w_down,
      b_gate.reshape(N_EXP, 1, D_FF), b_up.reshape(N_EXP, 1, D_FF), b_down.reshape(N_EXP, 1, D))


def _combine_kernel(meta_ref, y_hbm, route_ref, x1_ref, mod_ref, gpost_ref, o_ref, stage, sems):
    i = pl.program_id(0)
    n_steps = pl.num_programs(0)
    n_seg = n_steps * N_EXP
    tm = x1_ref.shape[0]
    slot = i % 2

    def fetch(tile, into):
        def issue(e, carry):
            s = tile * N_EXP + e
            _seg_copies(y_hbm, stage.at[into], meta_ref[2 * n_seg + s], meta_ref[s],
                        meta_ref[n_seg + s], sems.at[into])
            return carry

        lax.fori_loop(0, N_EXP, issue, 0)

    @pl.when(i == 0)
    def _():
        stage[...] = jnp.zeros_like(stage)
        fetch(0, 0)

    @pl.when(i + 1 < n_steps)
    def _():
        fetch(i + 1, 1 - slot)

    _wait_rows(y_hbm, stage.at[slot], meta_ref[3 * n_seg + i], sems.at[slot])

    route = route_ref[...]
    col = lax.broadcasted_iota(jnp.int32, (tm, STAGE_ROWS), 1).astype(F32)
    wmat = jnp.zeros((tm, STAGE_ROWS), F32)
    for kk in range(TOP_K):
        hit = col == route[:, ROUTE_P + kk:ROUTE_P + kk + 1]
        wmat = jnp.where(hit, route[:, ROUTE_W + kk:ROUTE_W + kk + 1], wmat)
    moe = _dot(wmat.astype(BF16), stage[slot].astype(BF16))
    gt_f = mod_ref[...][:, 5 * D:6 * D]
    o_ref[...] = x1_ref[...] + gt_f * _rms(moe, gpost_ref[...])


def _combine_call(meta, y, route, x1, mod3, tiles_per_seq, gpost):
    n_tok = x1.shape[0]
    tm = TM_PROJ
    grid_spec = pltpu.PrefetchScalarGridSpec(
        num_scalar_prefetch=1,
        grid=(n_tok // tm,),
        in_specs=[pl.BlockSpec(memory_space=pl.ANY),
                  pl.BlockSpec((tm, LANES), lambda i, m: (i, 0)),
                  pl.BlockSpec((tm, D), lambda i, m: (i, 0)),
                  pl.BlockSpec((None, 1, 6 * D), lambda i, m: (i // tiles_per_seq, 0, 0)),
                  pl.BlockSpec((1, D), lambda i, m: (0, 0))],
        out_specs=pl.BlockSpec((tm, D), lambda i, m: (i, 0)),
        scratch_shapes=[pltpu.VMEM((2, STAGE_ROWS, D), F32), pltpu.SemaphoreType.DMA((2,))],
    )
    return pl.pallas_call(
        _combine_kernel,
        grid_spec=grid_spec,
        out_shape=jax.ShapeDtypeStruct((n_tok, D), F32),
        compiler_params=_cparams(("arbitrary",)),
        name="combine",
    )(meta, y, route, x1, mod3, gpost)


def _rope_perm():
    q = MLA_ROPE // 4
    perm = np.concatenate([np.arange(q, 2 * q), np.arange(0, q),
                           np.arange(3 * q, 4 * q), np.arange(2 * q, 3 * q)])
    sign = np.concatenate([-np.ones(q), np.ones(q), -np.ones(q), np.ones(q)]).astype(np.float32)
    return perm, sign


def _rope_tables(seq):
    rows = seq // GRID_W
    r, col = jnp.meshgrid(jnp.arange(rows, dtype=F32), jnp.arange(GRID_W, dtype=F32),
                          indexing="ij")
    half = MLA_ROPE // 2
    inv_freq = ROPE_BASE ** (-jnp.arange(0, half, 2, dtype=F32) / half)
    ar = r.reshape(-1)[:, None] * inv_freq
    ac = col.reshape(-1)[:, None] * inv_freq
    cos = jnp.concatenate([jnp.cos(ar), jnp.cos(ar), jnp.cos(ac), jnp.cos(ac)], axis=-1)
    sin = jnp.concatenate([jnp.sin(ar), jnp.sin(ar), jnp.sin(ac), jnp.sin(ac)], axis=-1)
    return cos, sin


def _head_slots(parts):
    cols = []
    for p in parts:
        pad = HEAD_SLOT - p.shape[1]
        cols.append(jnp.pad(p, ((0, 0), (0, pad))))
    return jnp.concatenate(cols, axis=1)


def kernel(x, c, ctx, c_ctx, w_mod, b_mod, g_pre_mix, g_post_mix, g_pre_ffn, g_post_ffn, w_in,
           gla_w_a2_f, gla_b_a_f, gla_w_a2_b, gla_b_a_b, gla_g_norm, mla_g_q, mla_w_uq, mla_g_kv,
           mla_w_uk, mla_w_uv, w_br_gla, w_br_mla, w_out, router_w, router_b, w_gate, b_gate,
           w_up, b_up, w_down, b_down):
    depth = w_mod.shape[0]
    assert depth == 1, "single-layer block"
    batch, seq, d = x.shape
    ctx_len = ctx.shape[1]
    assert d == D and seq % TM_PROJ == 0 and (batch * ctx_len) % TM_PROJ == 0
    assert TM_PROJ % ctx_len == 0 or ctx_len % TM_PROJ == 0
    n_tok = batch * seq
    perm, sign = _rope_perm()
    sign = jnp.asarray(sign)

    cc = jnp.zeros((16, D), F32).at[:batch].set(c).at[batch].set(c_ctx)
    mod = _mod_call(cc, w_mod[0], b_mod[0])
    mod3 = mod.reshape(16, 1, 6 * D)

    wi = w_in[0]
    kr = wi[:, O_KR:O_KR + MLA_ROPE]
    small = jnp.concatenate([wi[:, O_AF:O_AF + GLA_RANK], wi[:, O_AB:O_AB + GLA_RANK], kr,
                             kr[:, perm] * sign, jnp.zeros((D, LANES - X_KRS - MLA_ROPE), F32)],
                            axis=1)
    w1 = jnp.concatenate([wi[:, O_Q:O_G], wi[:, O_DKV:O_DKV + MLA_KVR], small], axis=1).astype(BF16)
    w2 = jnp.concatenate([wi[:, O_G:O_G + GLA_V], wi[:, O_DQ:O_DQ + MLA_QR], wi[:, O_MG:]],
                         axis=1).astype(BF16)
    wa = jnp.zeros((LANES, 2 * GLA_QK), F32)
    wa = wa.at[X_AF:X_AF + GLA_RANK, 0:GLA_QK].set(gla_w_a2_f[0])
    wa = wa.at[X_AB:X_AB + GLA_RANK, GLA_QK:].set(gla_w_a2_b[0]).astype(BF16)
    ba = jnp.concatenate([gla_b_a_f[0], gla_b_a_b[0]]).reshape(1, 2 * GLA_QK)

    uk = mla_w_uk[0]
    uv = mla_w_uv[0]
    uq = mla_w_uq[0]
    eye = jnp.eye(MLA_ROPE, dtype=F32)
    wk_top = _head_slots([uk[:, h * MLA_NOPE:(h + 1) * MLA_NOPE] for h in range(MLA_H)])
    place = _head_slots([jnp.pad(eye, ((0, 0), (MLA_NOPE, 0))) for _ in range(MLA_H)])
    wk_bot = jnp.zeros((LANES, MLA_W), F32)
    wk_bot = wk_bot.at[X_KR:X_KR + MLA_ROPE].set(place).at[X_KRS:X_KRS + MLA_ROPE].set(place)
    wk = jnp.concatenate([wk_top, wk_bot], axis=0).astype(BF16)
    wv = _head_slots([jnp.pad(uv[:, h * MLA_V:(h + 1) * MLA_V], ((0, 0), ((h % 2) * MLA_V, 0)))
                      for h in range(MLA_H)]).astype(BF16)
    wq_a = _head_slots([uq[:, h * MLA_QKD:(h + 1) * MLA_QKD] for h in range(MLA_H)])
    wq_b = _head_slots([jnp.pad(uq[:, h * MLA_QKD + MLA_NOPE:(h + 1) * MLA_QKD][:, perm] * sign,
                                ((0, 0), (MLA_NOPE, 0))) for h in range(MLA_H)])
    wq = jnp.concatenate([wq_a, wq_b], axis=1).astype(BF16)

    cos, sin = _rope_tables(seq)
    zeros32 = jnp.zeros((seq, MLA_ROPE), F32)
    t1_x = jnp.concatenate([zeros32, cos, sin, zeros32], axis=1)
    t1_c = jnp.broadcast_to(
        jnp.concatenate([jnp.zeros((MLA_ROPE,)), jnp.ones((MLA_ROPE,)),
                         jnp.zeros((2 * MLA_ROPE,))]).astype(F32), (TM_PROJ, LANES))
    ones64 = jnp.ones((seq, MLA_NOPE), F32)
    cq = jnp.tile(jnp.concatenate([ones64, cos, zeros32], axis=1), (1, MLA_H)) * MLA_SCALE
    sq = jnp.tile(jnp.concatenate([jnp.zeros((seq, MLA_NOPE), F32), sin, zeros32], axis=1),
                  (1, MLA_H)) * MLA_SCALE

    gpre = g_pre_mix[0].reshape(1, D)
    gkv = mla_g_kv[0].reshape(1, MLA_KVR)
    gq = mla_g_q[0].reshape(1, MLA_QR)
    tiles_per_seq = seq // TM_PROJ

    xf = x.reshape(n_tok, D)
    cf = ctx.reshape(batch * ctx_len, D)
    (gqx, gkx, gvx, lfx, lbx, mkx, mvx, sz, mq, sg, sm) = _inproj_call(
        True, xf, mod3, lambda i: i // tiles_per_seq, tiles_per_seq, gpre, w1, wa, ba, wk, wv,
        gkv, t1_x, extra=(w2, gq, wq, cq, sq))
    (gqc, gkc, gvc, lfc, lbc, mkc, mvc) = _inproj_call(
        False, cf, mod3, lambda i: batch, 1, gpre, w1, wa, ba, wk, wv, gkv, t1_c)

    og = _gla_call((gqx, gkx, gvx, lfx, lbx), (gqc, gkc, gvc, lfc, lbc), batch, seq, ctx_len)
    om = _mla_call(mq, mkx, mvx, mkc, mvc, batch, seq, ctx_len)

    wr = jnp.pad(router_w[0], ((0, 0), (0, LANES - N_EXP)))
    wrh = wr.astype(BF16)
    wrl = (wr - wrh.astype(F32)).astype(BF16)
    br =jnp.pad(router_b[0], (0, LANES - N_EXP)).reshape(1, LANES)
    x1, h2, route, cnt = _mixout_call(
        og, sz, om, sg, sm, xf, mod3, tiles_per_seq, gla_g_norm[0].reshape(1, GLA_DV),
        w_br_gla[0].astype(BF16), w_br_mla[0].astype(BF16), w_out[0].astype(BF16),
        g_post_mix[0].reshape(1, D), g_pre_ffn[0].reshape(1, D), wrh, wrl, br)

    seg = -(-cnt[:, 0, :N_EXP].astype(jnp.int32) // SUBLANES) * SUBLANES
    stage_start = jnp.cumsum(seg, axis=1) - seg
    in_group = jnp.cumsum(seg, axis=0) - seg
    group = jnp.sum(seg, axis=0)
    padded = -(-group // TM_FFN) * TM_FFN
    ends = jnp.cumsum(padded)
    starts = ends - padded
    sorted_row = starts[None, :] + in_group
    meta = jnp.concatenate([stage_start.reshape(-1), seg.reshape(-1), sorted_row.reshape(-1),
                            jnp.sum(seg, axis=1)]).astype(jnp.int32)
    pad_info = jnp.concatenate([starts + group, ends[-1:], ends]).astype(jnp.int32)
    n_rows = _sorted_rows(n_tok)
    tile_start = jnp.arange(n_rows // TM_FFN, dtype=jnp.int32) * TM_FFN
    tile_exp = jnp.minimum(jnp.sum((ends[None, :] <= tile_start[:, None]).astype(jnp.int32), axis=1),
                           N_EXP - 1)
    n_used = (ends[-1:] // TM_FFN).astype(jnp.int32)

    xs = _dispatch_call(meta, pad_info, route, h2, n_rows)
    y = _ffn_call(tile_exp, n_used, xs, w_gate[0], w_up[0], w_down[0], b_gate[0], b_up[0],
                  b_down[0])
    out = _combine_call(meta, y, route, x1, mod3, tiles_per_seq, g_post_ffn[0].reshape(1, D))
    return out.reshape(batch, seq, D)
```

```python
import functools

import jax
import jax.numpy as jnp
import numpy as np
from jax import lax
from jax.experimental import pallas as pl
from jax.experimental.pallas import tpu as pltpu

F32 = jnp.float32
BF16 = jnp.bfloat16

D = 1024
EPS = 1e-6
GRID_W = 64

GLA_H = 4
GLA_DK = 64
GLA_DV = 128
GLA_RANK = 16
GLA_TAU = 16.0
GLA_CHUNK = 64
GLA_BLOCK = 256
GLA_QK = GLA_H * GLA_DK
GLA_V = GLA_H * GLA_DV

MLA_H = 8
MLA_QR = 256
MLA_KVR = 128
MLA_NOPE = 64
MLA_ROPE = 32
MLA_V = 64
MLA_QKD = MLA_NOPE + MLA_ROPE
MLA_SCALE = MLA_QKD ** -0.5
LOG2_E = 1.4426950408889634
ROPE_BASE = 10000.0
HEAD_SLOT = 128
MLA_W = MLA_H * HEAD_SLOT

N_EXP = 32
TOP_K = 4
D_FF = 1024
SWIGLU_LIMIT = 7.0
SWIGLU_ALPHA = 1.702

LANES = 128
SUBLANES = 8
TM_PROJ = 512
TQ = 512
TM_FFN = 512
ZERO_ROWS = 256
STAGE_ROWS = -(-(TM_PROJ * TOP_K + N_EXP * (SUBLANES - 1)) // 256) * 256
ROUTE_E, ROUTE_W, ROUTE_P = 0, TOP_K, 2 * TOP_K

_OFF = np.cumsum([0, GLA_QK, GLA_QK, GLA_V, GLA_V, GLA_RANK, GLA_RANK,
                  MLA_QR, MLA_KVR, MLA_ROPE, D, D])
(O_Q, O_K, O_V, O_G, O_AF, O_AB, O_DQ, O_DKV, O_KR, O_MG, O_MM, _) = _OFF.tolist()

X_AF, X_AB, X_KR, X_KRS = 0, 16, 32, 64

VMEM_LIMIT = 56 * 1024 * 1024


def _cparams(sem):
    return pltpu.CompilerParams(dimension_semantics=sem, vmem_limit_bytes=VMEM_LIMIT)


def _resident(arr):
    nd = arr.ndim
    return pl.BlockSpec(arr.shape, lambda *_: (0,) * nd, pipeline_mode=pl.Buffered(1))


def _rms(x, g):
    return x * lax.rsqrt(jnp.mean(x * x, axis=-1, keepdims=True) + EPS) * g


def _sigmoid(x):
    return 1.0 / (1.0 + jnp.exp(-x))


def _log_sigmoid(x):
    return jnp.minimum(x, 0.0) - jnp.log1p(jnp.exp(-jnp.abs(x)))


def _dot(a, b):
    return jnp.dot(a, b, preferred_element_type=F32)


def _dot_nt(a, b):
    return lax.dot_general(a, b, (((1,), (1,)), ((), ())), preferred_element_type=F32)


def _dot_tn(a, b):
    return lax.dot_general(a, b, (((0,), (0,)), ((), ())), preferred_element_type=F32)


def _mod_kernel(c_ref, w_ref, b_ref, o_ref):
    c = c_ref[...]
    s = (c * _sigmoid(c)).astype(BF16)
    o_ref[...] = _dot(s, w_ref[...].astype(BF16)) + b_ref[...]


def _mod_call(cc, w_mod, b_mod):
    n = w_mod.shape[1]
    tn = 1536
    return pl.pallas_call(
        _mod_kernel,
        grid=(n // tn,),
        in_specs=[pl.BlockSpec((16, D), lambda j: (0, 0)),
                  pl.BlockSpec((D, tn), lambda j: (0, j)),
                  pl.BlockSpec((1, tn), lambda j: (0, j))],
        out_specs=pl.BlockSpec((16, tn), lambda j: (0, j)),
        out_shape=jax.ShapeDtypeStruct((16, n), F32),
        compiler_params=_cparams(("arbitrary",)),
        name="mod",
    )(cc, w_mod, b_mod.reshape(1, n))


def _inproj_kernel(with_q, x_ref, mod_ref, gpre_ref, w1_ref, wa_ref, ba_ref, wk_ref,
                   wv_ref, gkv_ref, t1_ref, *rest):
    if with_q:
        (w2_ref, gq_ref, wq_ref, cq_ref, sq_ref,
         q_o, k_o, v_o, lf_o, lb_o, mk_o, mv_o, sz_o, mq_o, sg_o, sm_o) = rest
    else:
        (q_o, k_o, v_o, lf_o, lb_o, mk_o, mv_o) = rest
    x = x_ref[...]
    mod = mod_ref[...]
    sh = mod[:, 0:D]
    sc = mod[:, D:2 * D]
    h = (_rms(x, gpre_ref[...]) * (1.0 + sc) + sh).astype(BF16)

    z1 = _dot(h, w1_ref[...])
    q_o[...] = (z1[:, 0:GLA_QK] * (GLA_DK ** -0.5)).astype(BF16)
    k_o[...] = z1[:, GLA_QK:2 * GLA_QK].astype(BF16)
    v_o[...] = z1[:, 2 * GLA_QK:2 * GLA_QK + GLA_V].astype(BF16)
    o_dkv = 2 * GLA_QK + GLA_V
    ckv = _rms(z1[:, o_dkv:o_dkv + MLA_KVR], gkv_ref[...])
    xs = z1[:, o_dkv + MLA_KVR:o_dkv + MLA_KVR + LANES]

    la = _log_sigmoid(_dot(xs.astype(BF16), wa_ref[...]) + ba_ref[...]) * (1.0 / GLA_TAU)
    lf_o[...] = la[:, 0:GLA_QK]
    lb_o[...] = la[:, GLA_QK:2 * GLA_QK]

    lhs_k = jnp.concatenate([ckv, xs * t1_ref[...]], axis=-1).astype(BF16)
    mk_o[...] = _dot(lhs_k, wk_ref[...]).astype(BF16)
    mv_o[...] = _dot(ckv.astype(BF16), wv_ref[...]).astype(BF16)

    if with_q:
        zg = _dot(h, w2_ref[:, 0:GLA_V])
        sz_o[...] = (zg * _sigmoid(zg)).astype(BF16)
        n = _rms(_dot(h, w2_ref[:, GLA_V:GLA_V + MLA_QR]), gq_ref[...]).astype(BF16)
        mq_o[...] = (_dot(n, wq_ref[:, 0:MLA_W]) * cq_ref[...]
                     + _dot(n, wq_ref[:, MLA_W:2 * MLA_W]) * sq_ref[...]).astype(BF16)
        o_mg = GLA_V + MLA_QR
        sg_o[...] = _sigmoid(_dot(h, w2_ref[:, o_mg:o_mg + D])).astype(BF16)
        sm_o[...] = _sigmoid(_dot(h, w2_ref[:, o_mg + D:o_mg + 2 * D])).astype(BF16)


def _inproj_call(with_q, xf, mod3, mod_row_fn, tiles_per_seq, gpre, w1, wa, ba, wk, wv,
                 gkv, t1, extra=()):
    n_tok = xf.shape[0]
    tm = TM_PROJ
    grid = (n_tok // tm,)
    row = lambda i: (i, 0)
    tab = lambda i: (i % tiles_per_seq, 0)
    in_specs = [
        pl.BlockSpec((tm, D), row),
        pl.BlockSpec((None, 1, 6 * D), lambda i: (mod_row_fn(i), 0, 0)),
        _resident(gpre), _resident(w1), _resident(wa), _resident(ba), _resident(wk),
        _resident(wv), _resident(gkv),
        pl.BlockSpec((tm, LANES), tab),
    ]
    widths = [(GLA_QK, BF16), (GLA_QK, BF16), (GLA_V, BF16), (GLA_QK, F32), (GLA_QK, F32),
              (MLA_W, BF16), (MLA_W, BF16)]
    args = [xf, mod3, gpre, w1, wa, ba, wk, wv, gkv, t1]
    if with_q:
        w2, gq, wq, cq, sq = extra
        in_specs += [_resident(w2), _resident(gq), _resident(wq),
                     pl.BlockSpec((tm, MLA_W), tab), pl.BlockSpec((tm, MLA_W), tab)]
        args += [w2, gq, wq, cq, sq]
        widths += [(GLA_V, BF16), (MLA_W, BF16), (D, BF16), (D, BF16)]
    return pl.pallas_call(
        functools.partial(_inproj_kernel, with_q),
        grid=grid,
        in_specs=in_specs,
        out_specs=[pl.BlockSpec((tm, w), row) for w, _ in widths],
        out_shape=[jax.ShapeDtypeStruct((n_tok, w), dt) for w, dt in widths],
        compiler_params=_cparams(("arbitrary",)),
        name="inproj_x" if with_q else "inproj_ctx",
    )(*args)


def _gla_kernel(qx, kx, vx, lfx, lbx, qc, kc, vc, lfc, lbc, o_ref, sf_ref, sb_ref):
    C = GLA_CHUNK
    R = GLA_BLOCK
    n_sub = R // C
    rr = lax.broadcasted_iota(jnp.int32, (R, R), 0)
    cc = lax.broadcasted_iota(jnp.int32, (R, R), 1)
    same = (rr // C) == (cc // C)
    tbd_f = jnp.logical_and(same, cc <= rr).astype(BF16)
    tbd_b = jnp.logical_and(same, cc >= rr).astype(BF16)
    row = lax.broadcasted_iota(jnp.int32, (C, GLA_QK), 0)
    col = lax.broadcasted_iota(jnp.int32, (C, GLA_QK), 1)
    head_qk = col // GLA_DK
    tri4_f = (col % C) <= row
    tri4_b = (col % C) >= row
    head_v = lax.broadcasted_iota(jnp.int32, (C, GLA_V), 1) // GLA_DV

    def stack_masked(x, head_of_lane):
        return jnp.concatenate(
            [jnp.where(head_of_lane == hh, x, jnp.zeros_like(x)) for hh in range(GLA_H)], axis=0)

    def block(q_ref, k_ref, v_ref, la_ref, row0, fwd, s_ref, emit):
        sl = pl.ds(row0, R)
        la = la_ref[sl, :]
        la_hi = la.astype(BF16)
        la_lo = (la - la_hi.astype(F32)).astype(BF16)
        tbd = tbd_f if fwd else tbd_b
        cum = _dot(tbd, la_hi) + _dot(tbd, la_lo)
        k = k_ref[sl, :].astype(F32)
        v = v_ref[sl, :]
        if emit:
            q = q_ref[sl, :].astype(F32)
        st = s_ref[...]
        outs = [None] * n_sub
        for ci in (range(n_sub) if fwd else reversed(range(n_sub))):
            rs = slice(ci * C, (ci + 1) * C)
            cum_c = cum[rs]
            tot = cum_c[C - 1:C, :] if fwd else cum_c[0:1, :]
            k_c = k[rs]
            v_c = v[rs]
            ke_bd = stack_masked((k_c * jnp.exp(tot - cum_c)).astype(BF16), head_qk)
            v_stack = jnp.concatenate(
                [v_c[:, hh * GLA_DV:(hh + 1) * GLA_DV] for hh in range(GLA_H)], axis=0)
            if emit:
                qd = (q[rs] * jnp.exp(cum_c)).astype(BF16)
                ki_bd = stack_masked((k_c * jnp.exp(-cum_c)).astype(BF16), head_qk)
                att = _dot_nt(qd, ki_bd)
                att = jnp.where(tri4_f if fwd else tri4_b, att, 0.0).astype(BF16)
                o_intra = _dot(att, stack_masked(v_c, head_v))
                oi = _dot_nt(stack_masked(qd, head_qk), st.astype(BF16))
                o_inter = jnp.concatenate([oi[hh * C:(hh + 1) * C] for hh in range(GLA_H)],
                                          axis=-1)
                outs[ci] = o_intra + o_inter
            st = st * jnp.exp(tot) + _dot_tn(v_stack, ke_bd)
        s_ref[...] = st
        if emit:
            o_ref[sl, :] += jnp.concatenate(outs, axis=0)

    sf_ref[...] = jnp.zeros_like(sf_ref)
    sb_ref[...] = jnp.zeros_like(sb_ref)
    o_ref[...] = jnp.zeros_like(o_ref)
    n_ctx = qc.shape[0] // R
    n_x = qx.shape[0] // R
    for i in range(n_ctx):
        block(qc, kc, vc, lfc, i * R, True, sf_ref, False)
        block(qc, kc, vc, lbc, (n_ctx - 1 - i) * R, False, sb_ref, False)

    def body(i, carry):
        block(qx, kx, vx, lfx, pl.multiple_of(i * R, R), True, sf_ref, True)
        block(qx, kx, vx, lbx, pl.multiple_of((n_x - 1 - i) * R, R), False, sb_ref, True)
        return carry

    lax.fori_loop(0, n_x, body, 0)


def _gla_call(fx, fc, batch, seq, ctx_len):
    qx, kx, vx, lfx, lbx = fx
    qc, kc, vc, lfc, lbc = fc

    def spec(rows, w):
        return pl.BlockSpec((rows, w), lambda b: (b, 0))

    return pl.pallas_call(
        _gla_kernel,
        grid=(batch,),
        in_specs=[spec(seq, GLA_QK), spec(seq, GLA_QK), spec(seq, GLA_V), spec(seq, GLA_QK),
                  spec(seq, GLA_QK),
                  spec(ctx_len, GLA_QK), spec(ctx_len, GLA_QK), spec(ctx_len, GLA_V),
                  spec(ctx_len, GLA_QK), spec(ctx_len, GLA_QK)],
        out_specs=spec(seq, GLA_V),
        out_shape=jax.ShapeDtypeStruct((batch * seq, GLA_V), F32),
        scratch_shapes=[pltpu.VMEM((GLA_DV, GLA_QK), F32), pltpu.VMEM((GLA_DV, GLA_QK), F32)],
        compiler_params=_cparams(("arbitrary",)),
        name="gla",
    )(qx, kx, vx, lfx, lbx, qc, kc, vc, lfc, lbc)


def _mla_kernel(q_ref, kx_ref, vx_ref, kc_ref, vc_ref, o_ref):
    for j in range(MLA_H // 2):
        acc = None
        for hh in (2 * j, 2 * j + 1):
            sl = slice(hh * HEAD_SLOT, (hh + 1) * HEAD_SLOT)
            q = q_ref[:, sl]
            sx = _dot_nt(q, kx_ref[:, sl])
            sc = _dot_nt(q, kc_ref[:, sl])
            m = jnp.maximum(jnp.max(sx, axis=-1, keepdims=True),
                            jnp.max(sc, axis=-1, keepdims=True))
            px = jnp.exp2(sx - m)
            pc = jnp.exp2(sc - m)
            l = jnp.sum(px, axis=-1, keepdims=True) + jnp.sum(pc, axis=-1, keepdims=True)
            o = _dot(px.astype(BF16), vx_ref[:, sl]) + _dot(pc.astype(BF16), vc_ref[:, sl])
            o = o / l
            acc = o if acc is None else acc + o
        o_ref[:, j * HEAD_SLOT:(j + 1) * HEAD_SLOT] = acc.astype(BF16)


def _mla_call(mq, mkx, mvx, mkc, mvc, batch, seq, ctx_len):
    nq = seq // TQ
    return pl.pallas_call(
        _mla_kernel,
        grid=(batch, nq),
        in_specs=[pl.BlockSpec((TQ, MLA_W), lambda b, i: (b * nq + i, 0)),
                  pl.BlockSpec((seq, MLA_W), lambda b, i: (b, 0)),
                  pl.BlockSpec((seq, MLA_W), lambda b, i: (b, 0)),
                  pl.BlockSpec((ctx_len, MLA_W), lambda b, i: (b, 0)),
                  pl.BlockSpec((ctx_len, MLA_W), lambda b, i: (b, 0))],
        out_specs=pl.BlockSpec((TQ, MLA_H * MLA_V), lambda b, i: (b * nq + i, 0)),
        out_shape=jax.ShapeDtypeStruct((batch * seq, MLA_H * MLA_V), BF16),
        compiler_params=_cparams(("arbitrary", "arbitrary")),
        name="mla",
    )(mq, mkx, mvx, mkc, mvc)


def _mixout_kernel(og_ref, sz_ref, om_ref, sg_ref, sm_ref, x_ref, mod_ref, gn_ref, wbg_ref,
                   wbm_ref, wo_ref, gpost_ref, gffn_ref, wrh_ref, wrl_ref, br_ref,
                   x1_o, h2_o, route_o, cnt_o):
    tm = x_ref.shape[0]
    mod = mod_ref[...]
    gt_a = mod[:, 2 * D:3 * D]
    sh_f = mod[:, 3 * D:4 * D]
    sc_f = mod[:, 4 * D:5 * D]

    og = og_ref[...]
    gn = gn_ref[...]
    parts = [_rms(og[:, hh * GLA_DV:(hh + 1) * GLA_DV], gn) for hh in range(GLA_H)]
    a = (jnp.concatenate(parts, axis=-1) * sz_ref[...].astype(F32)).astype(BF16)
    br_g = _dot(a, wbg_ref[...])
    br_m = _dot(om_ref[...], wbm_ref[...])
    merged = (sg_ref[...].astype(F32) * br_g + sm_ref[...].astype(F32) * br_m).astype(BF16)
    mo = _dot(merged, wo_ref[...])
    x1 = x_ref[...] + gt_a * _rms(mo, gpost_ref[...])
    x1_o[...] = x1
    h2 = _rms(x1, gffn_ref[...]) * (1.0 + sc_f) + sh_f
    h2_o[...] = h2.astype(BF16)

    lane = lax.broadcasted_iota(jnp.int32, (tm, LANES), 1)
    h_hi = h2.astype(BF16)
    h_lo = (h2 - h_hi.astype(F32)).astype(BF16)
    logits = (_dot(h_hi, wrh_ref[...]) + _dot(h_lo, wrh_ref[...]) + _dot(h_hi, wrl_ref[...])
              + _dot(h_lo, wrl_ref[...]) + br_ref[...])
    neg = jnp.float32(-jnp.inf)
    lg = jnp.where(lane < N_EXP, logits, neg)
    lane_f = lane.astype(F32)
    hots, vals = [], []
    for _k in range(TOP_K):
        mx = jnp.max(lg, axis=-1, keepdims=True)
        idx = jnp.min(jnp.where(lg == mx, lane_f, float(LANES)), axis=-1, keepdims=True)
        hot = lane_f == idx
        lg = jnp.where(hot, neg, lg)
        hots.append(hot)
        vals.append(mx)
    es = [jnp.exp(v - vals[0]) for v in vals]
    den = es[0] + es[1] + es[2] + es[3]
    ws = [e / den for e in es]

    msum = jnp.zeros((tm, LANES), F32)
    for hot in hots:
        msum = msum + hot.astype(F32)
    rr = lax.broadcasted_iota(jnp.int32, (tm, tm), 0)
    cc = lax.broadcasted_iota(jnp.int32, (tm, tm), 1)
    lower = (cc < rr).astype(BF16)
    prior = _dot(lower, msum.astype(BF16))
    cnt = jnp.sum(msum, axis=0, keepdims=True)
    seg = jnp.floor((cnt + (SUBLANES - 1.0)) * (1.0 / SUBLANES)) * SUBLANES
    er = lax.broadcasted_iota(jnp.int32, (LANES, LANES), 0)
    ec = lax.broadcasted_iota(jnp.int32, (LANES, LANES), 1)
    before = (er < ec).astype(BF16)
    seg_start = _dot(jnp.broadcast_to(seg, (SUBLANES, LANES)).astype(BF16), before)[0:1, :]
    slot_of = prior + seg_start

    route = jnp.zeros((tm, LANES), F32)
    for kk in range(TOP_K):
        hotf = hots[kk].astype(F32)
        e_col = jnp.sum(hotf * lane_f, axis=-1, keepdims=True)
        p_col = jnp.sum(hotf * slot_of, axis=-1, keepdims=True)
        route = jnp.where(lane == ROUTE_E + kk, e_col, route)
        route = jnp.where(lane == ROUTE_W + kk, ws[kk], route)
        route = jnp.where(lane == ROUTE_P + kk, p_col, route)
    route_o[...] = route
    cnt_o[...] = jnp.broadcast_to(cnt, cnt_o.shape)


def _mixout_call(og, sz, om, sg, sm, xf, mod3, tiles_per_seq, gn, wbg, wbm, wo, gpost, gffn,
                 wrh, wrl, br):
    n_tok = xf.shape[0]
    tm = TM_PROJ
    row = lambda i: (i, 0)

    def rs(w):
        return pl.BlockSpec((tm, w), row)

    return pl.pallas_call(
        _mixout_kernel,
        grid=(n_tok // tm,),
        in_specs=[rs(GLA_V), rs(GLA_V), rs(MLA_H * MLA_V), rs(D), rs(D), rs(D),
                  pl.BlockSpec((None, 1, 6 * D), lambda i: (i // tiles_per_seq, 0, 0)),
                  _resident(gn), _resident(wbg), _resident(wbm), _resident(wo),
                  _resident(gpost), _resident(gffn), _resident(wrh), _resident(wrl),
                  _resident(br)],
        out_specs=[rs(D), rs(D), rs(LANES),
                   pl.BlockSpec((None, SUBLANES, LANES), lambda i: (i, 0, 0))],
        out_shape=[jax.ShapeDtypeStruct((n_tok, D), F32), jax.ShapeDtypeStruct((n_tok, D), BF16),
                   jax.ShapeDtypeStruct((n_tok, LANES), F32),
                   jax.ShapeDtypeStruct((n_tok // tm, SUBLANES, LANES), F32)],
        compiler_params=_cparams(("arbitrary",)),
        name="mixout",
    )(og, sz, om, sg, sm, xf, mod3, gn, wbg, wbm, wo, gpost, gffn, wrh, wrl, br)


def _sorted_rows(n_tok):
    n_steps = n_tok // TM_PROJ
    worst_pad = n_steps * N_EXP * (SUBLANES - 1) + N_EXP * TM_FFN
    per_step = -(-worst_pad // (n_steps * ZERO_ROWS)) * ZERO_ROWS
    n_rows = n_tok * TOP_K + per_step * n_steps
    assert n_rows % TM_FFN == 0
    return n_rows


def _seg_copies(src_ref, dst_ref, src0, dst0, length, sem):
    for b in reversed(range(SUBLANES.bit_length() - 1, TM_PROJ.bit_length())):
        size = 1 << b
        off = (length >> (b + 1)) << (b + 1)

        @pl.when((length & size) != 0)
        def _():
            pltpu.make_async_copy(
                src_ref.at[pl.ds(pl.multiple_of(src0 + off, SUBLANES), size)],
                dst_ref.at[pl.ds(pl.multiple_of(dst0 + off, SUBLANES), size)], sem).start()


def _wait_rows(src_ref, dst_ref, total, sem):
    for b in reversed(range(SUBLANES.bit_length() - 1, STAGE_ROWS.bit_length())):
        size = 1 << b

        @pl.when((total & size) != 0)
        def _():
            pltpu.make_async_copy(src_ref.at[pl.ds(0, size)], dst_ref.at[pl.ds(0, size)],
                                  sem).wait()


def _zero_rows(zero_ref, dst_ref, start, end, sem, wait):
    length = end - start
    for b in reversed(range(SUBLANES.bit_length() - 1, ZERO_ROWS.bit_length())):
        size = 1 << b
        off = (length >> (b + 1)) << (b + 1)

        @pl.when((length & size) != 0)
        def _():
            cp = pltpu.make_async_copy(
                zero_ref.at[pl.ds(0, size)],
                dst_ref.at[pl.ds(pl.multiple_of(start + off, SUBLANES), size)], sem)
            if wait:
                cp.wait()
            else:
                cp.start()


def _slot_onehot(route_t, n_slots):
    tm = route_t.shape[1]
    slot = lax.broadcasted_iota(jnp.int32, (n_slots, tm), 0).astype(F32)
    hit = slot == route_t[ROUTE_P:ROUTE_P + 1, :]
    for kk in range(1, TOP_K):
        hit = jnp.logical_or(hit, slot == route_t[ROUTE_P + kk:ROUTE_P + kk + 1, :])
    return jnp.where(hit, 1.0, 0.0).astype(BF16)


def _dispatch_kernel(meta_ref, pad_ref, route_ref, h_ref, xs_out, stage, zero_ref, sems, zsem):
    i = pl.program_id(0)
    n_steps = pl.num_programs(0)
    n_rows = xs_out.shape[0]
    n_seg = n_steps * N_EXP
    slot = i % 2

    @pl.when(i == 0)
    def _():
        zero_ref[...] = jnp.zeros_like(zero_ref)

    def clear_uncovered(wait):
        per = -(-N_EXP // n_steps)
        for j in range(per):
            e = i * per + j

            @pl.when(e < N_EXP)
            def _():
                ec = jnp.minimum(e, N_EXP - 1)
                _zero_rows(zero_ref, xs_out, pad_ref[ec], pad_ref[N_EXP + 1 + ec], zsem, wait)

        total_end = pad_ref[N_EXP]
        tail_rows = n_rows - n_steps * TM_PROJ * TOP_K
        per_step = tail_rows // n_steps
        for j in range(per_step // ZERO_ROWS):
            seg_end = (n_rows - tail_rows) + i * per_step + (j + 1) * ZERO_ROWS
            zlen = jnp.clip(seg_end - total_end, 0, ZERO_ROWS)
            _zero_rows(zero_ref, xs_out, seg_end - zlen, seg_end, zsem, wait)

    clear_uncovered(wait=False)

    onehot = _slot_onehot(route_ref[...].T, STAGE_ROWS)
    stage[slot] = _dot(onehot, h_ref[...])

    def issue(e, carry):
        s = i * N_EXP + e
        _seg_copies(stage.at[slot], xs_out, meta_ref[s], meta_ref[2 * n_seg + s],
                    meta_ref[n_seg + s], sems.at[slot])
        return carry

    lax.fori_loop(0, N_EXP, issue, 0)
    clear_uncovered(wait=True)

    @pl.when(i >= 1)
    def _():
        _wait_rows(stage.at[1 - slot], xs_out, meta_ref[3 * n_seg + i - 1], sems.at[1 - slot])

    @pl.when(i == n_steps - 1)
    def _():
        _wait_rows(stage.at[slot], xs_out, meta_ref[3 * n_seg + i], sems.at[slot])


def _dispatch_call(meta, pad_info, route, h2, n_rows):
    n_tok = h2.shape[0]
    tm = TM_PROJ
    n_steps = n_tok // tm
    assert (n_rows - n_tok * TOP_K) % (n_steps * ZERO_ROWS) == 0 and TM_FFN <= 2 * ZERO_ROWS
    grid_spec = pltpu.PrefetchScalarGridSpec(
        num_scalar_prefetch=2,
        grid=(n_steps,),
        in_specs=[pl.BlockSpec((tm, LANES), lambda i, m, p: (i, 0)),
                  pl.BlockSpec((tm, D), lambda i, m, p: (i, 0))],
        out_specs=pl.BlockSpec(memory_space=pl.ANY),
        scratch_shapes=[pltpu.VMEM((2, STAGE_ROWS, D), F32), pltpu.VMEM((ZERO_ROWS, D), F32),
                        pltpu.SemaphoreType.DMA((2,)), pltpu.SemaphoreType.DMA(())],
    )
    return pl.pallas_call(
        _dispatch_kernel,
        grid_spec=grid_spec,
        out_shape=jax.ShapeDtypeStruct((n_rows, D), F32),
        compiler_params=_cparams(("arbitrary",)),
        name="dispatch",
    )(meta, pad_info, route, h2)


def _ffn_kernel(te_ref, nu_ref, x_ref, wg_ref, wu_ref, wd_ref, bg_ref, bu_ref, bd_ref, y_ref,
                wgb, wub, wdb):
    i = pl.program_id(0)
    first = jnp.logical_or(i == 0, te_ref[i] != te_ref[jnp.maximum(i - 1, 0)])
    used = i < nu_ref[0]

    @pl.when(jnp.logical_and(used, first))
    def _():
        wgb[...] = wg_ref[...].astype(BF16)
        wub[...] = wu_ref[...].astype(BF16)
        wdb[...] = wd_ref[...].astype(BF16)

    @pl.when(used)
    def _():
        x = x_ref[...].astype(BF16)
        gate = jnp.minimum(_dot(x, wgb[...]) + bg_ref[...], SWIGLU_LIMIT)
        up = jnp.clip(_dot(x, wub[...]) + bu_ref[...], -SWIGLU_LIMIT, SWIGLU_LIMIT)
        act = ((up + 1.0) * gate * _sigmoid(SWIGLU_ALPHA * gate)).astype(BF16)
        y_ref[...] = _dot(act, wdb[...]) + bd_ref[...]

    @pl.when(jnp.logical_not(used))
    def _():
        y_ref[...] = jnp.zeros_like(y_ref)


def _ffn_call(tile_exp, n_used, xs, w_gate, w_up, w_down, b_gate, b_up, b_down):
    n_rows = xs.shape[0]
    tm = TM_FFN
    n_tiles = n_rows // tm

    def xrow(i, te, nu):
        return (jnp.maximum(jnp.minimum(i, nu[0] - 1), 0), 0)

    def wsel(i, te, nu):
        return (te[i], 0, 0)

    grid_spec = pltpu.PrefetchScalarGridSpec(
        num_scalar_prefetch=2,
        grid=(n_tiles,),
        in_specs=[pl.BlockSpec((tm, D), xrow),
                  pl.BlockSpec((None, D, D_FF), wsel),
                  pl.BlockSpec((None, D, D_FF), wsel),
                  pl.BlockSpec((None, D_FF, D), wsel),
                  pl.BlockSpec((None, 1, D_FF), wsel),
                  pl.BlockSpec((None, 1, D_FF), wsel),
                  pl.BlockSpec((None, 1, D), wsel)],
        out_specs=pl.BlockSpec((tm, D), lambda i, te, nu: (i, 0)),
        scratch_shapes=[pltpu.VMEM((D, D_FF), BF16), pltpu.VMEM((D, D_FF), BF16),
                        pltpu.VMEM((D_FF, D), BF16)],
    )
    return pl.pallas_call(
        _ffn_kernel,
        grid_spec=grid_spec,
        out_shape=jax.ShapeDtypeStruct((n_rows, D), F32),
        compiler_params=_cparams(("arbitrary",)),
        name="ffn",
    )(tile_exp, n_used, xs, w_gate, w_up, w_down,
      b_gate.reshape(N_EXP, 1, D_FF), b_up.reshape(N_EXP, 1, D_FF), b_down.reshape(N_EXP, 1, D))


def _combine_kernel(meta_ref, y_hbm, route_ref, x1_ref, mod_ref, gpost_ref, o_ref, stage, sems):
    i = pl.program_id(0)
    n_steps = pl.num_programs(0)
    n_seg = n_steps * N_EXP
    tm = x1_ref.shape[0]
    slot = i % 2

    def fetch(tile, into):
        def issue(e, carry):
            s = tile * N_EXP + e
            _seg_copies(y_hbm, stage.at[into], meta_ref[2 * n_seg + s], meta_ref[s],
                        meta_ref[n_seg + s], sems.at[into])
            return carry

        lax.fori_loop(0, N_EXP, issue, 0)

    @pl.when(i == 0)
    def _():
        stage[...] = jnp.zeros_like(stage)
        fetch(0, 0)

    @pl.when(i + 1 < n_steps)
    def _():
        fetch(i + 1, 1 - slot)

    _wait_rows(y_hbm, stage.at[slot], meta_ref[3 * n_seg + i], sems.at[slot])

    route = route_ref[...]
    col = lax.broadcasted_iota(jnp.int32, (tm, STAGE_ROWS), 1).astype(F32)
    wmat = jnp.zeros((tm, STAGE_ROWS), F32)
    for kk in range(TOP_K):
        hit = col == route[:, ROUTE_P + kk:ROUTE_P + kk + 1]
        wmat = jnp.where(hit, route[:, ROUTE_W + kk:ROUTE_W + kk + 1], wmat)
    moe = _dot(wmat.astype(BF16), stage[slot].astype(BF16))
    gt_f = mod_ref[...][:, 5 * D:6 * D]
    o_ref[...] = x1_ref[...] + gt_f * _rms(moe, gpost_ref[...])


def _combine_call(meta, y, route, x1, mod3, tiles_per_seq, gpost):
    n_tok = x1.shape[0]
    tm = TM_PROJ
    grid_spec = pltpu.PrefetchScalarGridSpec(
        num_scalar_prefetch=1,
        grid=(n_tok // tm,),
        in_specs=[pl.BlockSpec(memory_space=pl.ANY),
                  pl.BlockSpec((tm, LANES), lambda i, m: (i, 0)),
                  pl.BlockSpec((tm, D), lambda i, m: (i, 0)),
                  pl.BlockSpec((None, 1, 6 * D), lambda i, m: (i // tiles_per_seq, 0, 0)),
                  pl.BlockSpec((1, D), lambda i, m: (0, 0))],
        out_specs=pl.BlockSpec((tm, D), lambda i, m: (i, 0)),
        scratch_shapes=[pltpu.VMEM((2, STAGE_ROWS, D), F32), pltpu.SemaphoreType.DMA((2,))],
    )
    return pl.pallas_call(
        _combine_kernel,
        grid_spec=grid_spec,
        out_shape=jax.ShapeDtypeStruct((n_tok, D), F32),
        compiler_params=_cparams(("arbitrary",)),
        name="combine",
    )(meta, y, route, x1, mod3, gpost)


def _rope_perm():
    q = MLA_ROPE // 4
    perm = np.concatenate([np.arange(q, 2 * q), np.arange(0, q),
                           np.arange(3 * q, 4 * q), np.arange(2 * q, 3 * q)])
    sign = np.concatenate([-np.ones(q), np.ones(q), -np.ones(q), np.ones(q)]).astype(np.float32)
    return perm, sign


def _rope_tables(seq):
    rows = seq // GRID_W
    r, col = jnp.meshgrid(jnp.arange(rows, dtype=F32), jnp.arange(GRID_W, dtype=F32),
                          indexing="ij")
    half = MLA_ROPE // 2
    inv_freq = ROPE_BASE ** (-jnp.arange(0, half, 2, dtype=F32) / half)
    ar = r.reshape(-1)[:, None] * inv_freq
    ac = col.reshape(-1)[:, None] * inv_freq
    cos = jnp.concatenate([jnp.cos(ar), jnp.cos(ar), jnp.cos(ac), jnp.cos(ac)], axis=-1)
    sin = jnp.concatenate([jnp.sin(ar), jnp.sin(ar), jnp.sin(ac), jnp.sin(ac)], axis=-1)
    return cos, sin


def _head_slots(parts):
    cols = []
    for p in parts:
        pad = HEAD_SLOT - p.shape[1]
        cols.append(jnp.pad(p, ((0, 0), (0, pad))))
    return jnp.concatenate(cols, axis=1)


def kernel(x, c, ctx, c_ctx, w_mod, b_mod, g_pre_mix, g_post_mix, g_pre_ffn, g_post_ffn, w_in,
           gla_w_a2_f, gla_b_a_f, gla_w_a2_b, gla_b_a_b, gla_g_norm, mla_g_q, mla_w_uq, mla_g_kv,
           mla_w_uk, mla_w_uv, w_br_gla, w_br_mla, w_out, router_w, router_b, w_gate, b_gate,
           w_up, b_up, w_down, b_down):
    depth = w_mod.shape[0]
    assert depth == 1, "single-layer block"
    batch, seq, d = x.shape
    ctx_len = ctx.shape[1]
    assert d == D and seq % TM_PROJ == 0 and (batch * ctx_len) % TM_PROJ == 0
    assert TM_PROJ % ctx_len == 0 or ctx_len % TM_PROJ == 0
    n_tok = batch * seq
    perm, sign = _rope_perm()
    sign = jnp.asarray(sign)

    cc = jnp.zeros((16, D), F32).at[:batch].set(c).at[batch].set(c_ctx)
    mod = _mod_call(cc, w_mod[0], b_mod[0])
    mod3 = mod.reshape(16, 1, 6 * D)

    wi = w_in[0]
    kr = wi[:, O_KR:O_KR + MLA_ROPE]
    small = jnp.concatenate([wi[:, O_AF:O_AF + GLA_RANK], wi[:, O_AB:O_AB + GLA_RANK], kr,
                             kr[:, perm] * sign, jnp.zeros((D, LANES - X_KRS - MLA_ROPE), F32)],
                            axis=1)
    w1 = jnp.concatenate([wi[:, O_Q:O_G], wi[:, O_DKV:O_DKV + MLA_KVR], small], axis=1).astype(BF16)
    w2 = jnp.concatenate([wi[:, O_G:O_G + GLA_V], wi[:, O_DQ:O_DQ + MLA_QR], wi[:, O_MG:]],
                         axis=1).astype(BF16)
    wa = jnp.zeros((LANES, 2 * GLA_QK), F32)
    wa = wa.at[X_AF:X_AF + GLA_RANK, 0:GLA_QK].set(gla_w_a2_f[0])
    wa = wa.at[X_AB:X_AB + GLA_RANK, GLA_QK:].set(gla_w_a2_b[0]).astype(BF16)
    ba = jnp.concatenate([gla_b_a_f[0], gla_b_a_b[0]]).reshape(1, 2 * GLA_QK)

    uk = mla_w_uk[0]
    uv = mla_w_uv[0]
    uq = mla_w_uq[0]
    eye = jnp.eye(MLA_ROPE, dtype=F32)
    wk_top = _head_slots([uk[:, h * MLA_NOPE:(h + 1) * MLA_NOPE] for h in range(MLA_H)])
    place = _head_slots([jnp.pad(eye, ((0, 0), (MLA_NOPE, 0))) for _ in range(MLA_H)])
    wk_bot = jnp.zeros((LANES, MLA_W), F32)
    wk_bot = wk_bot.at[X_KR:X_KR + MLA_ROPE].set(place).at[X_KRS:X_KRS + MLA_ROPE].set(place)
    wk = jnp.concatenate([wk_top, wk_bot], axis=0).astype(BF16)
    wv = _head_slots([jnp.pad(uv[:, h * MLA_V:(h + 1) * MLA_V], ((0, 0), ((h % 2) * MLA_V, 0)))
                      for h in range(MLA_H)]).astype(BF16)
    wq_a = _head_slots([uq[:, h * MLA_QKD:(h + 1) * MLA_QKD] for h in range(MLA_H)])
    wq_b = _head_slots([jnp.pad(uq[:, h * MLA_QKD + MLA_NOPE:(h + 1) * MLA_QKD][:, perm] * sign,
                                ((0, 0), (MLA_NOPE, 0))) for h in range(MLA_H)])
    wq = jnp.concatenate([wq_a, wq_b], axis=1).astype(BF16)

    cos, sin = _rope_tables(seq)
    zeros32 = jnp.zeros((seq, MLA_ROPE), F32)
    t1_x = jnp.concatenate([zeros32, cos, sin, zeros32], axis=1)
    t1_c = jnp.broadcast_to(
        jnp.concatenate([jnp.zeros((MLA_ROPE,)), jnp.ones((MLA_ROPE,)),
                         jnp.zeros((2 * MLA_ROPE,))]).astype(F32), (TM_PROJ, LANES))
    ones64 = jnp.ones((seq, MLA_NOPE), F32)
    q_scale = MLA_SCALE * LOG2_E
    cq = jnp.tile(jnp.concatenate([ones64, cos, zeros32], axis=1), (1, MLA_H)) * q_scale
    sq = jnp.tile(jnp.concatenate([jnp.zeros((seq, MLA_NOPE), F32), sin, zeros32], axis=1),
                  (1, MLA_H)) * q_scale

    gpre = g_pre_mix[0].reshape(1, D)
    gkv = mla_g_kv[0].reshape(1, MLA_KVR)
    gq = mla_g_q[0].reshape(1, MLA_QR)
    tiles_per_seq = seq // TM_PROJ

    xf = x.reshape(n_tok, D)
    cf = ctx.reshape(batch * ctx_len, D)
    (gqx, gkx, gvx, lfx, lbx, mkx, mvx, sz, mq, sg, sm) = _inproj_call(
        True, xf, mod3, lambda i: i // tiles_per_seq, tiles_per_seq, gpre, w1, wa, ba, wk, wv,
        gkv, t1_x, extra=(w2, gq, wq, cq, sq))
    (gqc, gkc, gvc, lfc, lbc, mkc, mvc) = _inproj_call(
        False, cf, mod3, lambda i: batch, 1, gpre, w1, wa, ba, wk, wv, gkv, t1_c)

    og = _gla_call((gqx, gkx, gvx, lfx, lbx), (gqc, gkc, gvc, lfc, lbc), batch, seq, ctx_len)
    om = _mla_call(mq, mkx, mvx, mkc, mvc, batch, seq, ctx_len)

    wr = jnp.pad(router_w[0], ((0, 0), (0, LANES - N_EXP)))
    wrh = wr.astype(BF16)
    wrl = (wr - wrh.astype(F32)).astype(BF16)
    br =jnp.pad(router_b[0], (0, LANES - N_EXP)).reshape(1, LANES)
    x1, h2, route, cnt = _mixout_call(
        og, sz, om, sg, sm, xf, mod3, tiles_per_seq, gla_g_norm[0].reshape(1, GLA_DV),
        w_br_gla[0].astype(BF16), w_br_mla[0].astype(BF16), w_out[0].astype(BF16),
        g_post_mix[0].reshape(1, D), g_pre_ffn[0].reshape(1, D), wrh, wrl, br)

    seg = -(-cnt[:, 0, :N_EXP].astype(jnp.int32) // SUBLANES) * SUBLANES
    stage_start = jnp.cumsum(seg, axis=1) - seg
    in_group = jnp.cumsum(seg, axis=0) - seg
    group = jnp.sum(seg, axis=0)
    padded = -(-group // TM_FFN) * TM_FFN
    ends = jnp.cumsum(padded)
    starts = ends - padded
    sorted_row = starts[None, :] + in_group
    meta = jnp.concatenate([stage_start.reshape(-1), seg.reshape(-1), sorted_row.reshape(-1),
                            jnp.sum(seg, axis=1)]).astype(jnp.int32)
    pad_info = jnp.concatenate([starts + group, ends[-1:], ends]).astype(jnp.int32)
    n_rows = _sorted_rows(n_tok)
    tile_start = jnp.arange(n_rows // TM_FFN, dtype=jnp.int32) * TM_FFN
    tile_exp = jnp.minimum(jnp.sum((ends[None, :] <= tile_start[:, None]).astype(jnp.int32), axis=1),
                           N_EXP - 1)
    n_used = (ends[-1:] // TM_FFN).astype(jnp.int32)

    xs = _dispatch_call(meta, pad_info, route, h2, n_rows)
    y = _ffn_call(tile_exp, n_used, xs, w_gate[0], w_up[0], w_down[0], b_gate[0], b_up[0],
                  b_down[0])
    out = _combine_call(meta, y, route, x1, mod3, tiles_per_seq, g_post_ffn[0].reshape(1, D))
    return out.reshape(batch, seq, D)
```

```python
import functools

import jax
import jax.numpy as jnp
import numpy as np
from jax import lax
from jax.experimental import pallas as pl
from jax.experimental.pallas import tpu as pltpu

F32 = jnp.float32
BF16 = jnp.bfloat16

D = 1024
EPS = 1e-6
GRID_W = 64

GLA_H = 4
GLA_DK = 64
GLA_DV = 128
GLA_RANK = 16
GLA_TAU = 16.0
GLA_CHUNK = 64
GLA_BLOCK = 256
GLA_QK = GLA_H * GLA_DK
GLA_V = GLA_H * GLA_DV

MLA_H = 8
MLA_QR = 256
MLA_KVR = 128
MLA_NOPE = 64
MLA_ROPE = 32
MLA_V = 64
MLA_QKD = MLA_NOPE + MLA_ROPE
MLA_SCALE = MLA_QKD ** -0.5
LOG2_E = 1.4426950408889634
ROPE_BASE = 10000.0
HEAD_SLOT = 128
MLA_W = MLA_H * HEAD_SLOT
SUM_LANE_EVEN, SUM_LANE_ODD = MLA_V, 0

N_EXP = 32
TOP_K = 4
D_FF = 1024
SWIGLU_LIMIT = 7.0
SWIGLU_ALPHA = 1.702

LANES = 128
SUBLANES = 8
TM_PROJ = 512
TQ = 512
TM_FFN = 512
ZERO_ROWS = 256
STAGE_ROWS = -(-(TM_PROJ * TOP_K + N_EXP * (SUBLANES - 1)) // 256) * 256
ROUTE_E, ROUTE_W, ROUTE_P = 0, TOP_K, 2 * TOP_K

_OFF = np.cumsum([0, GLA_QK, GLA_QK, GLA_V, GLA_V, GLA_RANK, GLA_RANK,
                  MLA_QR, MLA_KVR, MLA_ROPE, D, D])
(O_Q, O_K, O_V, O_G, O_AF, O_AB, O_DQ, O_DKV, O_KR, O_MG, O_MM, _) = _OFF.tolist()

X_AF, X_AB, X_KR, X_KRS = 0, 16, 32, 64

VMEM_LIMIT = 56 * 1024 * 1024


def _cparams(sem):
    return pltpu.CompilerParams(dimension_semantics=sem, vmem_limit_bytes=VMEM_LIMIT)


def _resident(arr):
    nd = arr.ndim
    return pl.BlockSpec(arr.shape, lambda *_: (0,) * nd, pipeline_mode=pl.Buffered(1))


def _rms(x, g):
    return x * lax.rsqrt(jnp.mean(x * x, axis=-1, keepdims=True) + EPS) * g


def _sigmoid(x):
    return 1.0 / (1.0 + jnp.exp(-x))


def _log_sigmoid(x):
    return jnp.minimum(x, 0.0) - jnp.log1p(jnp.exp(-jnp.abs(x)))


def _dot(a, b):
    return jnp.dot(a, b, preferred_element_type=F32)


def _dot_nt(a, b):
    return lax.dot_general(a, b, (((1,), (1,)), ((), ())), preferred_element_type=F32)


def _dot_tn(a, b):
    return lax.dot_general(a, b, (((0,), (0,)), ((), ())), preferred_element_type=F32)


def _mod_kernel(c_ref, w_ref, b_ref, o_ref):
    c = c_ref[...]
    s = (c * _sigmoid(c)).astype(BF16)
    o_ref[...] = _dot(s, w_ref[...].astype(BF16)) + b_ref[...]


def _mod_call(cc, w_mod, b_mod):
    n = w_mod.shape[1]
    tn = 1536
    return pl.pallas_call(
        _mod_kernel,
        grid=(n // tn,),
        in_specs=[pl.BlockSpec((16, D), lambda j: (0, 0)),
                  pl.BlockSpec((D, tn), lambda j: (0, j)),
                  pl.BlockSpec((1, tn), lambda j: (0, j))],
        out_specs=pl.BlockSpec((16, tn), lambda j: (0, j)),
        out_shape=jax.ShapeDtypeStruct((16, n), F32),
        compiler_params=_cparams(("arbitrary",)),
        name="mod",
    )(cc, w_mod, b_mod.reshape(1, n))


def _inproj_kernel(with_q, x_ref, mod_ref, gpre_ref, w1_ref, wa_ref, ba_ref, wk_ref,
                   wv_ref, gkv_ref, t1_ref, *rest):
    if with_q:
        (w2_ref, gq_ref, wq_ref, cq_ref, sq_ref,
         q_o, k_o, v_o, lf_o, lb_o, mk_o, mv_o, sz_o, mq_o, sg_o, sm_o) = rest
    else:
        (q_o, k_o, v_o, lf_o, lb_o, mk_o, mv_o) = rest
    x = x_ref[...]
    mod = mod_ref[...]
    sh = mod[:, 0:D]
    sc = mod[:, D:2 * D]
    h = (_rms(x, gpre_ref[...]) * (1.0 + sc) + sh).astype(BF16)

    z1 = _dot(h, w1_ref[...])
    q_o[...] = (z1[:, 0:GLA_QK] * (GLA_DK ** -0.5)).astype(BF16)
    k_o[...] = z1[:, GLA_QK:2 * GLA_QK].astype(BF16)
    v_o[...] = z1[:, 2 * GLA_QK:2 * GLA_QK + GLA_V].astype(BF16)
    o_dkv = 2 * GLA_QK + GLA_V
    ckv = _rms(z1[:, o_dkv:o_dkv + MLA_KVR], gkv_ref[...])
    xs = z1[:, o_dkv + MLA_KVR:o_dkv + MLA_KVR + LANES]

    la = _log_sigmoid(_dot(xs.astype(BF16), wa_ref[...]) + ba_ref[...]) * (1.0 / GLA_TAU)
    lf_o[...] = la[:, 0:GLA_QK]
    lb_o[...] = la[:, GLA_QK:2 * GLA_QK]

    lhs_k = jnp.concatenate([ckv, xs * t1_ref[...]], axis=-1).astype(BF16)
    mk_o[...] = _dot(lhs_k, wk_ref[...]).astype(BF16)
    lane = lax.broadcasted_iota(jnp.int32, (x.shape[0], MLA_W), 1)
    mv = _dot(ckv.astype(BF16), wv_ref[...])
    mv_o[...] = jnp.where(lane % (2 * HEAD_SLOT) == SUM_LANE_EVEN, 1.0,
                          jnp.where(lane % (2 * HEAD_SLOT) == HEAD_SLOT + SUM_LANE_ODD, 1.0,
                                    mv)).astype(BF16)

    if with_q:
        zg = _dot(h, w2_ref[:, 0:GLA_V])
        sz_o[...] = (zg * _sigmoid(zg)).astype(BF16)
        n = _rms(_dot(h, w2_ref[:, GLA_V:GLA_V + MLA_QR]), gq_ref[...]).astype(BF16)
        cq = jnp.concatenate([cq_ref[...]] * MLA_H, axis=-1)
        sq = jnp.concatenate([sq_ref[...]] * MLA_H, axis=-1)
        mq_o[...] = (_dot(n, wq_ref[:, 0:MLA_W]) * cq
                     + _dot(n, wq_ref[:, MLA_W:2 * MLA_W]) * sq).astype(BF16)
        o_mg = GLA_V + MLA_QR
        sg_o[...] = _sigmoid(_dot(h, w2_ref[:, o_mg:o_mg + D])).astype(BF16)
        sm_o[...] = _sigmoid(_dot(h, w2_ref[:, o_mg + D:o_mg + 2 * D])).astype(BF16)


def _inproj_call(with_q, xf, mod3, mod_row_fn, tiles_per_seq, gpre, w1, wa, ba, wk, wv,
                 gkv, t1, extra=()):
    n_tok = xf.shape[0]
    tm = TM_PROJ
    grid = (n_tok // tm,)
    row = lambda i: (i, 0)
    tab = lambda i: (i % tiles_per_seq, 0)
    in_specs = [
        pl.BlockSpec((tm, D), row),
        pl.BlockSpec((None, 1, 6 * D), lambda i: (mod_row_fn(i), 0, 0)),
        _resident(gpre), _resident(w1), _resident(wa), _resident(ba), _resident(wk),
        _resident(wv), _resident(gkv),
        pl.BlockSpec((tm, LANES), tab),
    ]
    widths = [(GLA_QK, BF16), (GLA_QK, BF16), (GLA_V, BF16), (GLA_QK, F32), (GLA_QK, F32),
              (MLA_W, BF16), (MLA_W, BF16)]
    args = [xf, mod3, gpre, w1, wa, ba, wk, wv, gkv, t1]
    if with_q:
        w2, gq, wq, cq, sq = extra
        in_specs += [_resident(w2), _resident(gq), _resident(wq),
                     pl.BlockSpec((tm, HEAD_SLOT), tab), pl.BlockSpec((tm, HEAD_SLOT), tab)]
        args += [w2, gq, wq, cq, sq]
        widths += [(GLA_V, BF16), (MLA_W, BF16), (D, BF16), (D, BF16)]
    return pl.pallas_call(
        functools.partial(_inproj_kernel, with_q),
        grid=grid,
        in_specs=in_specs,
        out_specs=[pl.BlockSpec((tm, w), row) for w, _ in widths],
        out_shape=[jax.ShapeDtypeStruct((n_tok, w), dt) for w, dt in widths],
        compiler_params=_cparams(("arbitrary",)),
        name="inproj_x" if with_q else "inproj_ctx",
    )(*args)


def _gla_kernel(qx, kx, vx, lfx, lbx, qc, kc, vc, lfc, lbc, o_ref, sf_ref, sb_ref):
    C = GLA_CHUNK
    R = GLA_BLOCK
    n_sub = R // C
    rr = lax.broadcasted_iota(jnp.int32, (R, R), 0)
    cc = lax.broadcasted_iota(jnp.int32, (R, R), 1)
    same = (rr // C) == (cc // C)
    tbd_f = jnp.logical_and(same, cc <= rr).astype(BF16)
    tbd_b = jnp.logical_and(same, cc >= rr).astype(BF16)
    row = lax.broadcasted_iota(jnp.int32, (C, GLA_QK), 0)
    col = lax.broadcasted_iota(jnp.int32, (C, GLA_QK), 1)
    head_qk = col // GLA_DK
    tri4_f = (col % C) <= row
    tri4_b = (col % C) >= row
    head_v = lax.broadcasted_iota(jnp.int32, (C, GLA_V), 1) // GLA_DV

    def stack_masked(x, head_of_lane):
        return jnp.concatenate(
            [jnp.where(head_of_lane == hh, x, jnp.zeros_like(x)) for hh in range(GLA_H)], axis=0)

    def block(q_ref, k_ref, v_ref, la_ref, row0, fwd, s_ref, emit):
        sl = pl.ds(row0, R)
        la = la_ref[sl, :]
        la_hi = la.astype(BF16)
        la_lo = (la - la_hi.astype(F32)).astype(BF16)
        tbd = tbd_f if fwd else tbd_b
        cum = _dot(tbd, la_hi) + _dot(tbd, la_lo)
        k = k_ref[sl, :].astype(F32)
        v = v_ref[sl, :]
        if emit:
            q = q_ref[sl, :].astype(F32)
        st = s_ref[...]
        outs = [None] * n_sub
        for ci in (range(n_sub) if fwd else reversed(range(n_sub))):
            rs = slice(ci * C, (ci + 1) * C)
            cum_c = cum[rs]
            tot = cum_c[C - 1:C, :] if fwd else cum_c[0:1, :]
            k_c = k[rs]
            v_c = v[rs]
            ke_bd = stack_masked((k_c * jnp.exp(tot - cum_c)).astype(BF16), head_qk)
            v_stack = jnp.concatenate(
                [v_c[:, hh * GLA_DV:(hh + 1) * GLA_DV] for hh in range(GLA_H)], axis=0)
            if emit:
                qd = (q[rs] * jnp.exp(cum_c)).astype(BF16)
                ki_bd = stack_masked((k_c * jnp.exp(-cum_c)).astype(BF16), head_qk)
                att = _dot_nt(qd, ki_bd)
                att = jnp.where(tri4_f if fwd else tri4_b, att, 0.0).astype(BF16)
                o_intra = _dot(att, stack_masked(v_c, head_v))
                oi = _dot_nt(stack_masked(qd, head_qk), st.astype(BF16))
                o_inter = jnp.concatenate([oi[hh * C:(hh + 1) * C] for hh in range(GLA_H)],
                                          axis=-1)
                outs[ci] = o_intra + o_inter
            st = st * jnp.exp(tot) + _dot_tn(v_stack, ke_bd)
        s_ref[...] = st
        if emit:
            o_ref[sl, :] += jnp.concatenate(outs, axis=0)

    sf_ref[...] = jnp.zeros_like(sf_ref)
    sb_ref[...] = jnp.zeros_like(sb_ref)
    o_ref[...] = jnp.zeros_like(o_ref)
    n_ctx = qc.shape[0] // R
    n_x = qx.shape[0] // R
    for i in range(n_ctx):
        block(qc, kc, vc, lfc, i * R, True, sf_ref, False)
        block(qc, kc, vc, lbc, (n_ctx - 1 - i) * R, False, sb_ref, False)

    def body(i, carry):
        block(qx, kx, vx, lfx, pl.multiple_of(i * R, R), True, sf_ref, True)
        block(qx, kx, vx, lbx, pl.multiple_of((n_x - 1 - i) * R, R), False, sb_ref, True)
        return carry

    lax.fori_loop(0, n_x, body, 0)


def _gla_call(fx, fc, batch, seq, ctx_len):
    qx, kx, vx, lfx, lbx = fx
    qc, kc, vc, lfc, lbc = fc

    def spec(rows, w):
        return pl.BlockSpec((rows, w), lambda b: (b, 0))

    return pl.pallas_call(
        _gla_kernel,
        grid=(batch,),
        in_specs=[spec(seq, GLA_QK), spec(seq, GLA_QK), spec(seq, GLA_V), spec(seq, GLA_QK),
                  spec(seq, GLA_QK),
                  spec(ctx_len, GLA_QK), spec(ctx_len, GLA_QK), spec(ctx_len, GLA_V),
                  spec(ctx_len, GLA_QK), spec(ctx_len, GLA_QK)],
        out_specs=spec(seq, GLA_V),
        out_shape=jax.ShapeDtypeStruct((batch * seq, GLA_V), F32),
        scratch_shapes=[pltpu.VMEM((GLA_DV, GLA_QK), F32), pltpu.VMEM((GLA_DV, GLA_QK), F32)],
        compiler_params=_cparams(("arbitrary",)),
        name="gla",
    )(qx, kx, vx, lfx, lbx, qc, kc, vc, lfc, lbc)


def _mla_kernel(q_ref, kx_ref, vx_ref, kc_ref, vc_ref, o_ref):
    lane = lax.broadcasted_iota(jnp.int32, (q_ref.shape[0], HEAD_SLOT), 1)
    for j in range(MLA_H // 2):
        pair = []
        for hh, sum_lane in ((2 * j, SUM_LANE_EVEN), (2 * j + 1, SUM_LANE_ODD)):
            sl = slice(hh * HEAD_SLOT, (hh + 1) * HEAD_SLOT)
            q = q_ref[:, sl]
            sx = _dot_nt(q, kx_ref[:, sl])
            sc = _dot_nt(q, kc_ref[:, sl])
            m = jnp.maximum(jnp.max(sx, axis=-1, keepdims=True),
                            jnp.max(sc, axis=-1, keepdims=True))
            px = jnp.exp2(sx - m).astype(BF16)
            pc = jnp.exp2(sc - m).astype(BF16)
            o = _dot(px, vx_ref[:, sl]) + _dot(pc, vc_ref[:, sl])
            pair.append(o / o[:, sum_lane:sum_lane + 1])
        o_ref[:, j * HEAD_SLOT:(j + 1) * HEAD_SLOT] = jnp.where(
            lane < MLA_V, pair[0], pair[1]).astype(BF16)


def _mla_call(mq, mkx, mvx, mkc, mvc, batch, seq, ctx_len):
    nq = seq // TQ
    return pl.pallas_call(
        _mla_kernel,
        grid=(batch, nq),
        in_specs=[pl.BlockSpec((TQ, MLA_W), lambda b, i: (b * nq + i, 0)),
                  pl.BlockSpec((seq, MLA_W), lambda b, i: (b, 0)),
                  pl.BlockSpec((seq, MLA_W), lambda b, i: (b, 0)),
                  pl.BlockSpec((ctx_len, MLA_W), lambda b, i: (b, 0)),
                  pl.BlockSpec((ctx_len, MLA_W), lambda b, i: (b, 0))],
        out_specs=pl.BlockSpec((TQ, MLA_H * MLA_V), lambda b, i: (b * nq + i, 0)),
        out_shape=jax.ShapeDtypeStruct((batch * seq, MLA_H * MLA_V), BF16),
        compiler_params=_cparams(("arbitrary", "arbitrary")),
        name="mla",
    )(mq, mkx, mvx, mkc, mvc)


def _mixout_kernel(og_ref, sz_ref, om_ref, sg_ref, sm_ref, x_ref, mod_ref, gn_ref, wbg_ref,
                   wbm_ref, wo_ref, gpost_ref, gffn_ref, wrh_ref, wrl_ref, br_ref,
                   x1_o, h2_o, route_o, cnt_o):
    tm = x_ref.shape[0]
    mod = mod_ref[...]
    gt_a = mod[:, 2 * D:3 * D]
    sh_f = mod[:, 3 * D:4 * D]
    sc_f = mod[:, 4 * D:5 * D]

    og = og_ref[...]
    gn = gn_ref[...]
    parts = [_rms(og[:, hh * GLA_DV:(hh + 1) * GLA_DV], gn) for hh in range(GLA_H)]
    a = (jnp.concatenate(parts, axis=-1) * sz_ref[...].astype(F32)).astype(BF16)
    br_g = _dot(a, wbg_ref[...])
    br_m = _dot(om_ref[...], wbm_ref[...])
    merged = (sg_ref[...].astype(F32) * br_g + sm_ref[...].astype(F32) * br_m).astype(BF16)
    mo = _dot(merged, wo_ref[...])
    x1 = x_ref[...] + gt_a * _rms(mo, gpost_ref[...])
    x1_o[...] = x1
    h2 = _rms(x1, gffn_ref[...]) * (1.0 + sc_f) + sh_f
    h2_o[...] = h2.astype(BF16)

    lane = lax.broadcasted_iota(jnp.int32, (tm, LANES), 1)
    h_hi = h2.astype(BF16)
    h_lo = (h2 - h_hi.astype(F32)).astype(BF16)
    logits = (_dot(h_hi, wrh_ref[...]) + _dot(h_lo, wrh_ref[...]) + _dot(h_hi, wrl_ref[...])
              + _dot(h_lo, wrl_ref[...]) + br_ref[...])
    neg = jnp.float32(-jnp.inf)
    lg = jnp.where(lane < N_EXP, logits, neg)
    lane_f = lane.astype(F32)
    hots, vals = [], []
    for _k in range(TOP_K):
        mx = jnp.max(lg, axis=-1, keepdims=True)
        idx = jnp.min(jnp.where(lg == mx, lane_f, float(LANES)), axis=-1, keepdims=True)
        hot = lane_f == idx
        lg = jnp.where(hot, neg, lg)
        hots.append(hot)
        vals.append(mx)
    es = [jnp.exp(v - vals[0]) for v in vals]
    den = es[0] + es[1] + es[2] + es[3]
    ws = [e / den for e in es]

    msum = jnp.zeros((tm, LANES), F32)
    for hot in hots:
        msum = msum + hot.astype(F32)
    rr = lax.broadcasted_iota(jnp.int32, (tm, tm), 0)
    cc = lax.broadcasted_iota(jnp.int32, (tm, tm), 1)
    lower = (cc < rr).astype(BF16)
    prior = _dot(lower, msum.astype(BF16))
    cnt = jnp.sum(msum, axis=0, keepdims=True)
    seg = jnp.floor((cnt + (SUBLANES - 1.0)) * (1.0 / SUBLANES)) * SUBLANES
    er = lax.broadcasted_iota(jnp.int32, (LANES, LANES), 0)
    ec = lax.broadcasted_iota(jnp.int32, (LANES, LANES), 1)
    before = (er < ec).astype(BF16)
    seg_start = _dot(jnp.broadcast_to(seg, (SUBLANES, LANES)).astype(BF16), before)[0:1, :]
    slot_of = prior + seg_start

    route = jnp.zeros((tm, LANES), F32)
    for kk in range(TOP_K):
        hotf = hots[kk].astype(F32)
        e_col = jnp.sum(hotf * lane_f, axis=-1, keepdims=True)
        p_col = jnp.sum(hotf * slot_of, axis=-1, keepdims=True)
        route = jnp.where(lane == ROUTE_E + kk, e_col, route)
        route = jnp.where(lane == ROUTE_W + kk, ws[kk], route)
        route = jnp.where(lane == ROUTE_P + kk, p_col, route)
    route_o[...] = route
    cnt_o[...] = jnp.broadcast_to(cnt, cnt_o.shape)


def _mixout_call(og, sz, om, sg, sm, xf, mod3, tiles_per_seq, gn, wbg, wbm, wo, gpost, gffn,
                 wrh, wrl, br):
    n_tok = xf.shape[0]
    tm = TM_PROJ
    row = lambda i: (i, 0)

    def rs(w):
        return pl.BlockSpec((tm, w), row)

    return pl.pallas_call(
        _mixout_kernel,
        grid=(n_tok // tm,),
        in_specs=[rs(GLA_V), rs(GLA_V), rs(MLA_H * MLA_V), rs(D), rs(D), rs(D),
                  pl.BlockSpec((None, 1, 6 * D), lambda i: (i // tiles_per_seq, 0, 0)),
                  _resident(gn), _resident(wbg), _resident(wbm), _resident(wo),
                  _resident(gpost), _resident(gffn), _resident(wrh), _resident(wrl),
                  _resident(br)],
        out_specs=[rs(D), rs(D), rs(LANES),
                   pl.BlockSpec((None, SUBLANES, LANES), lambda i: (i, 0, 0))],
        out_shape=[jax.ShapeDtypeStruct((n_tok, D), F32), jax.ShapeDtypeStruct((n_tok, D), BF16),
                   jax.ShapeDtypeStruct((n_tok, LANES), F32),
                   jax.ShapeDtypeStruct((n_tok // tm, SUBLANES, LANES), F32)],
        compiler_params=_cparams(("arbitrary",)),
        name="mixout",
    )(og, sz, om, sg, sm, xf, mod3, gn, wbg, wbm, wo, gpost, gffn, wrh, wrl, br)


def _sorted_rows(n_tok):
    n_steps = n_tok // TM_PROJ
    worst_pad = n_steps * N_EXP * (SUBLANES - 1) + N_EXP * TM_FFN
    per_step = -(-worst_pad // (n_steps * ZERO_ROWS)) * ZERO_ROWS
    n_rows = n_tok * TOP_K + per_step * n_steps
    assert n_rows % TM_FFN == 0
    return n_rows


def _seg_copies(src_ref, dst_ref, src0, dst0, length, sem):
    for b in reversed(range(SUBLANES.bit_length() - 1, TM_PROJ.bit_length())):
        size = 1 << b
        off = (length >> (b + 1)) << (b + 1)

        @pl.when((length & size) != 0)
        def _():
            pltpu.make_async_copy(
                src_ref.at[pl.ds(pl.multiple_of(src0 + off, SUBLANES), size)],
                dst_ref.at[pl.ds(pl.multiple_of(dst0 + off, SUBLANES), size)], sem).start()


def _wait_rows(src_ref, dst_ref, total, sem):
    for b in reversed(range(SUBLANES.bit_length() - 1, STAGE_ROWS.bit_length())):
        size = 1 << b

        @pl.when((total & size) != 0)
        def _():
            pltpu.make_async_copy(src_ref.at[pl.ds(0, size)], dst_ref.at[pl.ds(0, size)],
                                  sem).wait()


def _zero_rows(zero_ref, dst_ref, start, end, sem, wait):
    length = end - start
    for b in reversed(range(SUBLANES.bit_length() - 1, ZERO_ROWS.bit_length())):
        size = 1 << b
        off = (length >> (b + 1)) << (b + 1)

        @pl.when((length & size) != 0)
        def _():
            cp = pltpu.make_async_copy(
                zero_ref.at[pl.ds(0, size)],
                dst_ref.at[pl.ds(pl.multiple_of(start + off, SUBLANES), size)], sem)
            if wait:
                cp.wait()
            else:
                cp.start()


def _slot_onehot(route_t, n_slots):
    tm = route_t.shape[1]
    slot = lax.broadcasted_iota(jnp.int32, (n_slots, tm), 0).astype(F32)
    acc = jnp.zeros((n_slots, tm), F32)
    for kk in range(TOP_K):
        acc = jnp.where(slot == route_t[ROUTE_P + kk:ROUTE_P + kk + 1, :], 1.0, acc)
    return acc.astype(BF16)


def _dispatch_kernel(meta_ref, pad_ref, route_ref, h_ref, xs_out, stage, zero_ref, sems, zsem):
    i = pl.program_id(0)
    n_steps = pl.num_programs(0)
    n_rows = xs_out.shape[0]
    n_seg = n_steps * N_EXP
    slot = i % 2

    @pl.when(i == 0)
    def _():
        zero_ref[...] = jnp.zeros_like(zero_ref)

    def clear_uncovered(wait):
        per = -(-N_EXP // n_steps)
        for j in range(per):
            e = i * per + j

            @pl.when(e < N_EXP)
            def _():
                ec = jnp.minimum(e, N_EXP - 1)
                _zero_rows(zero_ref, xs_out, pad_ref[ec], pad_ref[N_EXP + 1 + ec], zsem, wait)

        total_end = pad_ref[N_EXP]
        tail_rows = n_rows - n_steps * TM_PROJ * TOP_K
        per_step = tail_rows // n_steps
        for j in range(per_step // ZERO_ROWS):
            seg_end = (n_rows - tail_rows) + i * per_step + (j + 1) * ZERO_ROWS
            zlen = jnp.clip(seg_end - total_end, 0, ZERO_ROWS)
            _zero_rows(zero_ref, xs_out, seg_end - zlen, seg_end, zsem, wait)

    clear_uncovered(wait=False)

    onehot = _slot_onehot(route_ref[...].T, STAGE_ROWS)
    stage[slot] = _dot(onehot, h_ref[...])

    def issue(e, carry):
        s = i * N_EXP + e
        _seg_copies(stage.at[slot], xs_out, meta_ref[s], meta_ref[2 * n_seg + s],
                    meta_ref[n_seg + s], sems.at[slot])
        return carry

    lax.fori_loop(0, N_EXP, issue, 0)
    clear_uncovered(wait=True)

    @pl.when(i >= 1)
    def _():
        _wait_rows(stage.at[1 - slot], xs_out, meta_ref[3 * n_seg + i - 1], sems.at[1 - slot])

    @pl.when(i == n_steps - 1)
    def _():
        _wait_rows(stage.at[slot], xs_out, meta_ref[3 * n_seg + i], sems.at[slot])


def _dispatch_call(meta, pad_info, route, h2, n_rows):
    n_tok = h2.shape[0]
    tm = TM_PROJ
    n_steps = n_tok // tm
    assert (n_rows - n_tok * TOP_K) % (n_steps * ZERO_ROWS) == 0 and TM_FFN <= 2 * ZERO_ROWS
    grid_spec = pltpu.PrefetchScalarGridSpec(
        num_scalar_prefetch=2,
        grid=(n_steps,),
        in_specs=[pl.BlockSpec((tm, LANES), lambda i, m, p: (i, 0)),
                  pl.BlockSpec((tm, D), lambda i, m, p: (i, 0))],
        out_specs=pl.BlockSpec(memory_space=pl.ANY),
        scratch_shapes=[pltpu.VMEM((2, STAGE_ROWS, D), F32), pltpu.VMEM((ZERO_ROWS, D), F32),
                        pltpu.SemaphoreType.DMA((2,)), pltpu.SemaphoreType.DMA(())],
    )
    return pl.pallas_call(
        _dispatch_kernel,
        grid_spec=grid_spec,
        out_shape=jax.ShapeDtypeStruct((n_rows, D), F32),
        compiler_params=_cparams(("arbitrary",)),
        name="dispatch",
    )(meta, pad_info, route, h2)


def _ffn_kernel(te_ref, nu_ref, x_ref, wg_ref, wu_ref, wd_ref, bg_ref, bu_ref, bd_ref, y_ref,
                wgb, wub, wdb):
    i = pl.program_id(0)
    first = jnp.logical_or(i == 0, te_ref[i] != te_ref[jnp.maximum(i - 1, 0)])
    used = i < nu_ref[0]

    @pl.when(jnp.logical_and(used, first))
    def _():
        wgb[...] = wg_ref[...].astype(BF16)
        wub[...] = wu_ref[...].astype(BF16)
        wdb[...] = wd_ref[...].astype(BF16)

    @pl.when(used)
    def _():
        x = x_ref[...].astype(BF16)
        gate = jnp.minimum(_dot(x, wgb[...]) + bg_ref[...], SWIGLU_LIMIT)
        up = jnp.clip(_dot(x, wub[...]) + bu_ref[...], -SWIGLU_LIMIT, SWIGLU_LIMIT)
        act = ((up + 1.0) * gate * _sigmoid(SWIGLU_ALPHA * gate)).astype(BF16)
        y_ref[...] = _dot(act, wdb[...]) + bd_ref[...]

    @pl.when(jnp.logical_not(used))
    def _():
        y_ref[...] = jnp.zeros_like(y_ref)


def _ffn_call(tile_exp, n_used, xs, w_gate, w_up, w_down, b_gate, b_up, b_down):
    n_rows = xs.shape[0]
    tm = TM_FFN
    n_tiles = n_rows // tm

    def xrow(i, te, nu):
        return (jnp.maximum(jnp.minimum(i, nu[0] - 1), 0), 0)

    def wsel(i, te, nu):
        return (te[i], 0, 0)

    grid_spec = pltpu.PrefetchScalarGridSpec(
        num_scalar_prefetch=2,
        grid=(n_tiles,),
        in_specs=[pl.BlockSpec((tm, D), xrow),
                  pl.BlockSpec((None, D, D_FF), wsel),
                  pl.BlockSpec((None, D, D_FF), wsel),
                  pl.BlockSpec((None, D_FF, D), wsel),
                  pl.BlockSpec((None, 1, D_FF), wsel),
                  pl.BlockSpec((None, 1, D_FF), wsel),
                  pl.BlockSpec((None, 1, D), wsel)],
        out_specs=pl.BlockSpec((tm, D), lambda i, te, nu: (i, 0)),
        scratch_shapes=[pltpu.VMEM((D, D_FF), BF16), pltpu.VMEM((D, D_FF), BF16),
                        pltpu.VMEM((D_FF, D), BF16)],
    )
    return pl.pallas_call(
        _ffn_kernel,
        grid_spec=grid_spec,
        out_shape=jax.ShapeDtypeStruct((n_rows, D), F32),
        compiler_params=_cparams(("arbitrary",)),
        name="ffn",
    )(tile_exp, n_used, xs, w_gate, w_up, w_down,
      b_gate.reshape(N_EXP, 1, D_FF), b_up.reshape(N_EXP, 1, D_FF), b_down.reshape(N_EXP, 1, D))


def _combine_kernel(meta_ref, y_hbm, route_ref, x1_ref, mod_ref, gpost_ref, o_ref, stage, sems):
    i = pl.program_id(0)
    n_steps = pl.num_programs(0)
    n_seg = n_steps * N_EXP
    tm = x1_ref.shape[0]
    slot = i % 2

    def fetch(tile, into):
        def issue(e, carry):
            s = tile * N_EXP + e
            _seg_copies(y_hbm, stage.at[into], meta_ref[2 * n_seg + s], meta_ref[s],
                        meta_ref[n_seg + s], sems.at[into])
            return carry

        lax.fori_loop(0, N_EXP, issue, 0)

    @pl.when(i == 0)
    def _():
        stage[...] = jnp.zeros_like(stage)
        fetch(0, 0)

    @pl.when(i + 1 < n_steps)
    def _():
        fetch(i + 1, 1 - slot)

    _wait_rows(y_hbm, stage.at[slot], meta_ref[3 * n_seg + i], sems.at[slot])

    route = route_ref[...]
    col = lax.broadcasted_iota(jnp.int32, (tm, STAGE_ROWS), 1).astype(F32)
    wmat = jnp.zeros((tm, STAGE_ROWS), F32)
    for kk in range(TOP_K):
        hit = col == route[:, ROUTE_P + kk:ROUTE_P + kk + 1]
        wmat = jnp.where(hit, route[:, ROUTE_W + kk:ROUTE_W + kk + 1], wmat)
    moe = _dot(wmat.astype(BF16), stage[slot].astype(BF16))
    gt_f = mod_ref[...][:, 5 * D:6 * D]
    o_ref[...] = x1_ref[...] + gt_f * _rms(moe, gpost_ref[...])


def _combine_call(meta, y, route, x1, mod3, tiles_per_seq, gpost):
    n_tok = x1.shape[0]
    tm = TM_PROJ
    grid_spec = pltpu.PrefetchScalarGridSpec(
        num_scalar_prefetch=1,
        grid=(n_tok // tm,),
        in_specs=[pl.BlockSpec(memory_space=pl.ANY),
                  pl.BlockSpec((tm, LANES), lambda i, m: (i, 0)),
                  pl.BlockSpec((tm, D), lambda i, m: (i, 0)),
                  pl.BlockSpec((None, 1, 6 * D), lambda i, m: (i // tiles_per_seq, 0, 0)),
                  pl.BlockSpec((1, D), lambda i, m: (0, 0))],
        out_specs=pl.BlockSpec((tm, D), lambda i, m: (i, 0)),
        scratch_shapes=[pltpu.VMEM((2, STAGE_ROWS, D), F32), pltpu.SemaphoreType.DMA((2,))],
    )
    return pl.pallas_call(
        _combine_kernel,
        grid_spec=grid_spec,
        out_shape=jax.ShapeDtypeStruct((n_tok, D), F32),
        compiler_params=_cparams(("arbitrary",)),
        name="combine",
    )(meta, y, route, x1, mod3, gpost)


def _rope_swap(w):
    q = MLA_ROPE // 4
    return jnp.concatenate([-w[..., q:2 * q], w[..., 0:q], -w[..., 3 * q:4 * q],
                            w[..., 2 * q:3 * q]], axis=-1)


def _rope_tables(seq):
    rows = seq // GRID_W
    r, col = np.meshgrid(np.arange(rows, dtype=np.float64), np.arange(GRID_W, dtype=np.float64),
                         indexing="ij")
    half = MLA_ROPE // 2
    inv_freq = ROPE_BASE ** (-np.arange(0, half, 2, dtype=np.float64) / half)
    ar = r.reshape(-1)[:, None] * inv_freq
    ac = col.reshape(-1)[:, None] * inv_freq
    cos = np.concatenate([np.cos(ar), np.cos(ar), np.cos(ac), np.cos(ac)], axis=-1)
    sin = np.concatenate([np.sin(ar), np.sin(ar), np.sin(ac), np.sin(ac)], axis=-1)
    return cos, sin


def _head_slots(w3, lead):
    rows, heads, width = w3.shape
    return jnp.pad(w3, ((0, 0), (0, 0), (lead, HEAD_SLOT - lead - width))).reshape(
        rows, heads * HEAD_SLOT)


def kernel(x, c, ctx, c_ctx, w_mod, b_mod, g_pre_mix, g_post_mix, g_pre_ffn, g_post_ffn, w_in,
           gla_w_a2_f, gla_b_a_f, gla_w_a2_b, gla_b_a_b, gla_g_norm, mla_g_q, mla_w_uq, mla_g_kv,
           mla_w_uk, mla_w_uv, w_br_gla, w_br_mla, w_out, router_w, router_b, w_gate, b_gate,
           w_up, b_up, w_down, b_down):
    depth = w_mod.shape[0]
    assert depth == 1, "single-layer block"
    batch, seq, d = x.shape
    ctx_len = ctx.shape[1]
    assert d == D and seq % TM_PROJ == 0 and (batch * ctx_len) % TM_PROJ == 0
    assert TM_PROJ % ctx_len == 0 or ctx_len % TM_PROJ == 0
    n_tok = batch * seq

    cc = jnp.zeros((16, D), F32).at[:batch].set(c).at[batch].set(c_ctx)
    mod = _mod_call(cc, w_mod[0], b_mod[0])
    mod3 = mod.reshape(16, 1, 6 * D)

    wi = w_in[0]
    kr = wi[:, O_KR:O_KR + MLA_ROPE]
    small = jnp.concatenate([wi[:, O_AF:O_AF + GLA_RANK], wi[:, O_AB:O_AB + GLA_RANK], kr,
                             _rope_swap(kr), jnp.zeros((D, LANES - X_KRS - MLA_ROPE), F32)],
                            axis=1)
    w1 = jnp.concatenate([wi[:, O_Q:O_G], wi[:, O_DKV:O_DKV + MLA_KVR], small], axis=1).astype(BF16)
    w2 = jnp.concatenate([wi[:, O_G:O_G + GLA_V], wi[:, O_DQ:O_DQ + MLA_QR], wi[:, O_MG:]],
                         axis=1).astype(BF16)
    wa = jnp.zeros((LANES, 2 * GLA_QK), F32)
    wa = wa.at[X_AF:X_AF + GLA_RANK, 0:GLA_QK].set(gla_w_a2_f[0])
    wa = wa.at[X_AB:X_AB + GLA_RANK, GLA_QK:].set(gla_w_a2_b[0]).astype(BF16)
    ba = jnp.concatenate([gla_b_a_f[0], gla_b_a_b[0]]).reshape(1, 2 * GLA_QK)

    uk = mla_w_uk[0]
    uv = mla_w_uv[0]
    uq = mla_w_uq[0]
    wk_top = _head_slots(uk.reshape(MLA_KVR, MLA_H, MLA_NOPE), 0)
    place = np.zeros((LANES, MLA_H, HEAD_SLOT), np.float32)
    for j in range(MLA_ROPE):
        place[X_KR + j, :, MLA_NOPE + j] = 1.0
        place[X_KRS + j, :, MLA_NOPE + j] = 1.0
    wk = jnp.concatenate([wk_top, jnp.asarray(place.reshape(LANES, MLA_W))],
                         axis=0).astype(BF16)
    uv4 = uv.reshape(MLA_KVR, MLA_H // 2, 2, MLA_V)
    zv = jnp.zeros_like(uv4[:, :, 0])
    wv = jnp.stack([jnp.concatenate([uv4[:, :, 0], zv], axis=-1),
                    jnp.concatenate([zv, uv4[:, :, 1]], axis=-1)],
                   axis=2).reshape(MLA_KVR, MLA_W).astype(BF16)
    uq3 = uq.reshape(MLA_QR, MLA_H, MLA_QKD)
    wq_a = _head_slots(uq3, 0)
    wq_b = _head_slots(_rope_swap(uq3[:, :, MLA_NOPE:]), MLA_NOPE)
    wq = jnp.concatenate([wq_a, wq_b], axis=1).astype(BF16)

    cos, sin = _rope_tables(seq)
    zeros32 = np.zeros((seq, MLA_ROPE))
    t1_x = jnp.asarray(np.concatenate([zeros32, cos, sin, zeros32], axis=1), F32)
    t1_c = jnp.asarray(np.broadcast_to(
        np.concatenate([np.zeros(MLA_ROPE), np.ones(MLA_ROPE), np.zeros(2 * MLA_ROPE)]),
        (TM_PROJ, LANES)), F32)
    q_scale = MLA_SCALE * LOG2_E
    cq = jnp.asarray(np.concatenate([np.ones((seq, MLA_NOPE)), cos, zeros32], axis=1) * q_scale, F32)
    sq = jnp.asarray(np.concatenate([np.zeros((seq, MLA_NOPE)), sin, zeros32], axis=1) * q_scale,
                     F32)

    gpre = g_pre_mix[0].reshape(1, D)
    gkv = mla_g_kv[0].reshape(1, MLA_KVR)
    gq = mla_g_q[0].reshape(1, MLA_QR)
    tiles_per_seq = seq // TM_PROJ

    xf = x.reshape(n_tok, D)
    cf = ctx.reshape(batch * ctx_len, D)
    (gqx, gkx, gvx, lfx, lbx, mkx, mvx, sz, mq, sg, sm) = _inproj_call(
        True, xf, mod3, lambda i: i // tiles_per_seq, tiles_per_seq, gpre, w1, wa, ba, wk, wv,
        gkv, t1_x, extra=(w2, gq, wq, cq, sq))
    (gqc, gkc, gvc, lfc, lbc, mkc, mvc) = _inproj_call(
        False, cf, mod3, lambda i: batch, 1, gpre, w1, wa, ba, wk, wv, gkv, t1_c)

    og = _gla_call((gqx, gkx, gvx, lfx, lbx), (gqc, gkc, gvc, lfc, lbc), batch, seq, ctx_len)
    om = _mla_call(mq, mkx, mvx, mkc, mvc, batch, seq, ctx_len)

    wr = jnp.pad(router_w[0], ((0, 0), (0, LANES - N_EXP)))
    wrh = wr.astype(BF16)
    wrl = (wr - wrh.astype(F32)).astype(BF16)
    br =jnp.pad(router_b[0], (0, LANES - N_EXP)).reshape(1, LANES)
    x1, h2, route, cnt = _mixout_call(
        og, sz, om, sg, sm, xf, mod3, tiles_per_seq, gla_g_norm[0].reshape(1, GLA_DV),
        w_br_gla[0].astype(BF16), w_br_mla[0].astype(BF16), w_out[0].astype(BF16),
        g_post_mix[0].reshape(1, D), g_pre_ffn[0].reshape(1, D), wrh, wrl, br)

    seg = -(-cnt[:, 0, :N_EXP].astype(jnp.int32) // SUBLANES) * SUBLANES
    stage_start = jnp.cumsum(seg, axis=1) - seg
    in_group = jnp.cumsum(seg, axis=0) - seg
    group = jnp.sum(seg, axis=0)
    padded = -(-group // TM_FFN) * TM_FFN
    ends = jnp.cumsum(padded)
    starts = ends - padded
    sorted_row = starts[None, :] + in_group
    meta = jnp.concatenate([stage_start.reshape(-1), seg.reshape(-1), sorted_row.reshape(-1),
                            jnp.sum(seg, axis=1)]).astype(jnp.int32)
    pad_info = jnp.concatenate([starts + group, ends[-1:], ends]).astype(jnp.int32)
    n_rows = _sorted_rows(n_tok)
    tile_start = jnp.arange(n_rows // TM_FFN, dtype=jnp.int32) * TM_FFN
    tile_exp = jnp.minimum(jnp.sum((ends[None, :] <= tile_start[:, None]).astype(jnp.int32), axis=1),
                           N_EXP - 1)
    n_used = (ends[-1:] // TM_FFN).astype(jnp.int32)

    xs = _dispatch_call(meta, pad_info, route, h2, n_rows)
    y = _ffn_call(tile_exp, n_used, xs, w_gate[0], w_up[0], w_down[0], b_gate[0], b_up[0],
                  b_down[0])
    out = _combine_call(meta, y, route, x1, mod3, tiles_per_seq, g_post_ffn[0].reshape(1, D))
    return out.reshape(batch, seq, D)
```

```python
import functools

import jax
import jax.numpy as jnp
import numpy as np
from jax import lax
from jax.experimental import pallas as pl
from jax.experimental.pallas import tpu as pltpu

F32 = jnp.float32
BF16 = jnp.bfloat16

D = 1024
EPS = 1e-6
GRID_W = 64

GLA_H = 4
GLA_DK = 64
GLA_DV = 128
GLA_RANK = 16
GLA_TAU = 16.0
GLA_CHUNK = 64
GLA_BLOCK = 256
GLA_QK = GLA_H * GLA_DK
GLA_V = GLA_H * GLA_DV

MLA_H = 8
MLA_QR = 256
MLA_KVR = 128
MLA_NOPE = 64
MLA_ROPE = 32
MLA_V = 64
MLA_QKD = MLA_NOPE + MLA_ROPE
MLA_SCALE = MLA_QKD ** -0.5
LOG2_E = 1.4426950408889634
ROPE_BASE = 10000.0
HEAD_SLOT = 128
MLA_W = MLA_H * HEAD_SLOT
SUM_LANE_EVEN, SUM_LANE_ODD = MLA_V, 0

N_EXP = 32
TOP_K = 4
D_FF = 1024
SWIGLU_LIMIT = 7.0
SWIGLU_ALPHA = 1.702

LANES = 128
SUBLANES = 8
TM_PROJ = 512
TQ = 512
TM_FFN = 512
ZERO_ROWS = 256
STAGE_ROWS = -(-(TM_PROJ * TOP_K + N_EXP * (SUBLANES - 1)) // 256) * 256
ROUTE_E, ROUTE_W, ROUTE_P = 0, TOP_K, 2 * TOP_K

_OFF = np.cumsum([0, GLA_QK, GLA_QK, GLA_V, GLA_V, GLA_RANK, GLA_RANK,
                  MLA_QR, MLA_KVR, MLA_ROPE, D, D])
(O_Q, O_K, O_V, O_G, O_AF, O_AB, O_DQ, O_DKV, O_KR, O_MG, O_MM, _) = _OFF.tolist()

X_AF, X_AB, X_KR, X_KRS = 0, 16, 32, 64

VMEM_LIMIT = 56 * 1024 * 1024


def _cparams(sem):
    return pltpu.CompilerParams(dimension_semantics=sem, vmem_limit_bytes=VMEM_LIMIT)


def _resident(arr):
    nd = arr.ndim
    return pl.BlockSpec(arr.shape, lambda *_: (0,) * nd, pipeline_mode=pl.Buffered(1))


def _rms(x, g):
    return x * lax.rsqrt(jnp.mean(x * x, axis=-1, keepdims=True) + EPS) * g


def _sigmoid(x):
    return 1.0 / (1.0 + jnp.exp(-x))


def _log_sigmoid(x):
    return jnp.minimum(x, 0.0) - jnp.log1p(jnp.exp(-jnp.abs(x)))


def _dot(a, b):
    return jnp.dot(a, b, preferred_element_type=F32)


def _dot_nt(a, b):
    return lax.dot_general(a, b, (((1,), (1,)), ((), ())), preferred_element_type=F32)


def _dot_tn(a, b):
    return lax.dot_general(a, b, (((0,), (0,)), ((), ())), preferred_element_type=F32)


def _mod_kernel(c_ref, w_ref, b_ref, o_ref):
    c = c_ref[...]
    s = (c * _sigmoid(c)).astype(BF16)
    o_ref[...] = _dot(s, w_ref[...].astype(BF16)) + b_ref[...]


def _mod_call(cc, w_mod, b_mod):
    n = w_mod.shape[1]
    tn = 1536
    return pl.pallas_call(
        _mod_kernel,
        grid=(n // tn,),
        in_specs=[pl.BlockSpec((16, D), lambda j: (0, 0)),
                  pl.BlockSpec((D, tn), lambda j: (0, j)),
                  pl.BlockSpec((1, tn), lambda j: (0, j))],
        out_specs=pl.BlockSpec((16, tn), lambda j: (0, j)),
        out_shape=jax.ShapeDtypeStruct((16, n), F32),
        compiler_params=_cparams(("arbitrary",)),
        name="mod",
    )(cc, w_mod, b_mod.reshape(1, n))


def _inproj_kernel(with_q, x_ref, mod_ref, gpre_ref, w1_ref, wa_ref, ba_ref, wk_ref,
                   wv_ref, gkv_ref, t1_ref, *rest):
    if with_q:
        (w2_ref, gq_ref, wq_ref, cq_ref, sq_ref,
         q_o, k_o, v_o, lf_o, lb_o, mk_o, mv_o, sz_o, mq_o, sg_o, sm_o) = rest
    else:
        (q_o, k_o, v_o, lf_o, lb_o, mk_o, mv_o) = rest
    x = x_ref[...]
    mod = mod_ref[...]
    sh = mod[:, 0:D]
    sc = mod[:, D:2 * D]
    h = (_rms(x, gpre_ref[...]) * (1.0 + sc) + sh).astype(BF16)

    z1 = _dot(h, w1_ref[...])
    q_o[...] = (z1[:, 0:GLA_QK] * (GLA_DK ** -0.5)).astype(BF16)
    k_o[...] = z1[:, GLA_QK:2 * GLA_QK].astype(BF16)
    v_o[...] = z1[:, 2 * GLA_QK:2 * GLA_QK + GLA_V].astype(BF16)
    o_dkv = 2 * GLA_QK + GLA_V
    ckv = _rms(z1[:, o_dkv:o_dkv + MLA_KVR], gkv_ref[...])
    xs = z1[:, o_dkv + MLA_KVR:o_dkv + MLA_KVR + LANES]

    la = _log_sigmoid(_dot(xs.astype(BF16), wa_ref[...]) + ba_ref[...]) * (1.0 / GLA_TAU)
    lf_o[...] = la[:, 0:GLA_QK]
    lb_o[...] = la[:, GLA_QK:2 * GLA_QK]

    lhs_k = jnp.concatenate([ckv, xs * t1_ref[...]], axis=-1).astype(BF16)
    mk_o[...] = _dot(lhs_k, wk_ref[...]).astype(BF16)
    lane = lax.broadcasted_iota(jnp.int32, (x.shape[0], MLA_W), 1)
    mv = _dot(ckv.astype(BF16), wv_ref[...])
    mv_o[...] = jnp.where(lane % (2 * HEAD_SLOT) == SUM_LANE_EVEN, 1.0,
                          jnp.where(lane % (2 * HEAD_SLOT) == HEAD_SLOT + SUM_LANE_ODD, 1.0,
                                    mv)).astype(BF16)

    if with_q:
        zg = _dot(h, w2_ref[:, 0:GLA_V])
        sz_o[...] = (zg * _sigmoid(zg)).astype(BF16)
        n = _rms(_dot(h, w2_ref[:, GLA_V:GLA_V + MLA_QR]), gq_ref[...]).astype(BF16)
        cq = jnp.concatenate([cq_ref[...]] * MLA_H, axis=-1)
        sq = jnp.concatenate([sq_ref[...]] * MLA_H, axis=-1)
        mq_o[...] = (_dot(n, wq_ref[:, 0:MLA_W]) * cq
                     + _dot(n, wq_ref[:, MLA_W:2 * MLA_W]) * sq).astype(BF16)
        o_mg = GLA_V + MLA_QR
        sg_o[...] = _sigmoid(_dot(h, w2_ref[:, o_mg:o_mg + D])).astype(BF16)
        sm_o[...] = _sigmoid(_dot(h, w2_ref[:, o_mg + D:o_mg + 2 * D])).astype(BF16)


def _inproj_call(with_q, xf, mod3, mod_row_fn, tiles_per_seq, gpre, w1, wa, ba, wk, wv,
                 gkv, t1, extra=()):
    n_tok = xf.shape[0]
    tm = TM_PROJ
    grid = (n_tok // tm,)
    row = lambda i: (i, 0)
    tab = lambda i: (i % tiles_per_seq, 0)
    in_specs = [
        pl.BlockSpec((tm, D), row),
        pl.BlockSpec((None, 1, 6 * D), lambda i: (mod_row_fn(i), 0, 0)),
        _resident(gpre), _resident(w1), _resident(wa), _resident(ba), _resident(wk),
        _resident(wv), _resident(gkv),
        pl.BlockSpec((tm, LANES), tab),
    ]
    widths = [(GLA_QK, BF16), (GLA_QK, BF16), (GLA_V, BF16), (GLA_QK, F32), (GLA_QK, F32),
              (MLA_W, BF16), (MLA_W, BF16)]
    args = [xf, mod3, gpre, w1, wa, ba, wk, wv, gkv, t1]
    if with_q:
        w2, gq, wq, cq, sq = extra
        in_specs += [_resident(w2), _resident(gq), _resident(wq),
                     pl.BlockSpec((tm, HEAD_SLOT), tab), pl.BlockSpec((tm, HEAD_SLOT), tab)]
        args += [w2, gq, wq, cq, sq]
        widths += [(GLA_V, BF16), (MLA_W, BF16), (D, BF16), (D, BF16)]
    return pl.pallas_call(
        functools.partial(_inproj_kernel, with_q),
        grid=grid,
        in_specs=in_specs,
        out_specs=[pl.BlockSpec((tm, w), row) for w, _ in widths],
        out_shape=[jax.ShapeDtypeStruct((n_tok, w), dt) for w, dt in widths],
        compiler_params=_cparams(("arbitrary",)),
        name="inproj_x" if with_q else "inproj_ctx",
    )(*args)


def _gla_kernel(qx, kx, vx, lfx, lbx, qc, kc, vc, lfc, lbc, o_ref, sf_ref, sb_ref):
    C = GLA_CHUNK
    R = GLA_BLOCK
    n_sub = R // C
    rr = lax.broadcasted_iota(jnp.int32, (R, R), 0)
    cc = lax.broadcasted_iota(jnp.int32, (R, R), 1)
    same = (rr // C) == (cc // C)
    tbd_f = jnp.logical_and(same, cc <= rr).astype(BF16)
    tbd_b = jnp.logical_and(same, cc >= rr).astype(BF16)
    row = lax.broadcasted_iota(jnp.int32, (C, GLA_QK), 0)
    col = lax.broadcasted_iota(jnp.int32, (C, GLA_QK), 1)
    head_qk = col // GLA_DK
    tri4_f = (col % C) <= row
    tri4_b = (col % C) >= row
    head_v = lax.broadcasted_iota(jnp.int32, (C, GLA_V), 1) // GLA_DV

    def stack_masked(x, head_of_lane):
        return jnp.concatenate(
            [jnp.where(head_of_lane == hh, x, jnp.zeros_like(x)) for hh in range(GLA_H)], axis=0)

    def block(q_ref, k_ref, v_ref, la_ref, row0, fwd, s_ref, emit):
        sl = pl.ds(row0, R)
        la = la_ref[sl, :]
        la_hi = la.astype(BF16)
        la_lo = (la - la_hi.astype(F32)).astype(BF16)
        tbd = tbd_f if fwd else tbd_b
        cum = _dot(tbd, la_hi) + _dot(tbd, la_lo)
        k = k_ref[sl, :].astype(F32)
        v = v_ref[sl, :]
        if emit:
            q = q_ref[sl, :].astype(F32)
        st = s_ref[...]
        outs = [None] * n_sub
        for ci in (range(n_sub) if fwd else reversed(range(n_sub))):
            rs = slice(ci * C, (ci + 1) * C)
            cum_c = cum[rs]
            tot = cum_c[C - 1:C, :] if fwd else cum_c[0:1, :]
            k_c = k[rs]
            v_c = v[rs]
            ke_bd = stack_masked((k_c * jnp.exp(tot - cum_c)).astype(BF16), head_qk)
            v_stack = jnp.concatenate(
                [v_c[:, hh * GLA_DV:(hh + 1) * GLA_DV] for hh in range(GLA_H)], axis=0)
            if emit:
                qd = (q[rs] * jnp.exp(cum_c)).astype(BF16)
                ki_bd = stack_masked((k_c * jnp.exp(-cum_c)).astype(BF16), head_qk)
                att = _dot_nt(qd, ki_bd)
                att = jnp.where(tri4_f if fwd else tri4_b, att, 0.0).astype(BF16)
                o_intra = _dot(att, stack_masked(v_c, head_v))
                oi = _dot_nt(stack_masked(qd, head_qk), st.astype(BF16))
                o_inter = jnp.concatenate([oi[hh * C:(hh + 1) * C] for hh in range(GLA_H)],
                                          axis=-1)
                outs[ci] = o_intra + o_inter
            st = st * jnp.exp(tot) + _dot_tn(v_stack, ke_bd)
        s_ref[...] = st
        if emit:
            o_ref[sl, :] += jnp.concatenate(outs, axis=0)

    sf_ref[...] = jnp.zeros_like(sf_ref)
    sb_ref[...] = jnp.zeros_like(sb_ref)
    o_ref[...] = jnp.zeros_like(o_ref)
    n_ctx = qc.shape[0] // R
    n_x = qx.shape[0] // R
    for i in range(n_ctx):
        block(qc, kc, vc, lfc, i * R, True, sf_ref, False)
        block(qc, kc, vc, lbc, (n_ctx - 1 - i) * R, False, sb_ref, False)

    def body(i, carry):
        block(qx, kx, vx, lfx, pl.multiple_of(i * R, R), True, sf_ref, True)
        block(qx, kx, vx, lbx, pl.multiple_of((n_x - 1 - i) * R, R), False, sb_ref, True)
        return carry

    lax.fori_loop(0, n_x, body, 0)


def _gla_call(fx, fc, batch, seq, ctx_len):
    qx, kx, vx, lfx, lbx = fx
    qc, kc, vc, lfc, lbc = fc

    def spec(rows, w):
        return pl.BlockSpec((rows, w), lambda b: (b, 0))

    return pl.pallas_call(
        _gla_kernel,
        grid=(batch,),
        in_specs=[spec(seq, GLA_QK), spec(seq, GLA_QK), spec(seq, GLA_V), spec(seq, GLA_QK),
                  spec(seq, GLA_QK),
                  spec(ctx_len, GLA_QK), spec(ctx_len, GLA_QK), spec(ctx_len, GLA_V),
                  spec(ctx_len, GLA_QK), spec(ctx_len, GLA_QK)],
        out_specs=spec(seq, GLA_V),
        out_shape=jax.ShapeDtypeStruct((batch * seq, GLA_V), F32),
        scratch_shapes=[pltpu.VMEM((GLA_DV, GLA_QK), F32), pltpu.VMEM((GLA_DV, GLA_QK), F32)],
        compiler_params=_cparams(("arbitrary",)),
        name="gla",
    )(qx, kx, vx, lfx, lbx, qc, kc, vc, lfc, lbc)


def _mla_kernel(q_ref, kx_ref, vx_ref, kc_ref, vc_ref, o_ref):
    lane = lax.broadcasted_iota(jnp.int32, (q_ref.shape[0], HEAD_SLOT), 1)
    for j in range(MLA_H // 2):
        pair = []
        for hh, sum_lane in ((2 * j, SUM_LANE_EVEN), (2 * j + 1, SUM_LANE_ODD)):
            sl = slice(hh * HEAD_SLOT, (hh + 1) * HEAD_SLOT)
            q = q_ref[:, sl]
            sx = _dot_nt(q, kx_ref[:, sl])
            sc = _dot_nt(q, kc_ref[:, sl])
            m = jnp.maximum(jnp.max(sx, axis=-1, keepdims=True),
                            jnp.max(sc, axis=-1, keepdims=True))
            px = jnp.exp2(sx - m).astype(BF16)
            pc = jnp.exp2(sc - m).astype(BF16)
            o = _dot(px, vx_ref[:, sl]) + _dot(pc, vc_ref[:, sl])
            pair.append(o / o[:, sum_lane:sum_lane + 1])
        o_ref[:, j * HEAD_SLOT:(j + 1) * HEAD_SLOT] = jnp.where(
            lane < MLA_V, pair[0], pair[1]).astype(BF16)


def _mla_call(mq, mkx, mvx, mkc, mvc, batch, seq, ctx_len):
    nq = seq // TQ
    return pl.pallas_call(
        _mla_kernel,
        grid=(batch, nq),
        in_specs=[pl.BlockSpec((TQ, MLA_W), lambda b, i: (b * nq + i, 0)),
                  pl.BlockSpec((seq, MLA_W), lambda b, i: (b, 0)),
                  pl.BlockSpec((seq, MLA_W), lambda b, i: (b, 0)),
                  pl.BlockSpec((ctx_len, MLA_W), lambda b, i: (b, 0)),
                  pl.BlockSpec((ctx_len, MLA_W), lambda b, i: (b, 0))],
        out_specs=pl.BlockSpec((TQ, MLA_H * MLA_V), lambda b, i: (b * nq + i, 0)),
        out_shape=jax.ShapeDtypeStruct((batch * seq, MLA_H * MLA_V), BF16),
        compiler_params=_cparams(("arbitrary", "arbitrary")),
        name="mla",
    )(mq, mkx, mvx, mkc, mvc)


def _mixout_kernel(og_ref, sz_ref, om_ref, sg_ref, sm_ref, x_ref, mod_ref, gn_ref, wbg_ref,
                   wbm_ref, wo_ref, gpost_ref, gffn_ref, wrh_ref, wrl_ref, br_ref,
                   x1_o, h2_o, route_o, cnt_o):
    tm = x_ref.shape[0]
    mod = mod_ref[...]
    gt_a = mod[:, 2 * D:3 * D]
    sh_f = mod[:, 3 * D:4 * D]
    sc_f = mod[:, 4 * D:5 * D]

    og = og_ref[...]
    gn = gn_ref[...]
    parts = [_rms(og[:, hh * GLA_DV:(hh + 1) * GLA_DV], gn) for hh in range(GLA_H)]
    a = (jnp.concatenate(parts, axis=-1) * sz_ref[...].astype(F32)).astype(BF16)
    br_g = _dot(a, wbg_ref[...])
    br_m = _dot(om_ref[...], wbm_ref[...])
    merged = (sg_ref[...].astype(F32) * br_g + sm_ref[...].astype(F32) * br_m).astype(BF16)
    mo = _dot(merged, wo_ref[...])
    x1 = x_ref[...] + gt_a * _rms(mo, gpost_ref[...])
    x1_o[...] = x1
    h2 = _rms(x1, gffn_ref[...]) * (1.0 + sc_f) + sh_f
    h2_o[...] = h2.astype(BF16)

    lane = lax.broadcasted_iota(jnp.int32, (tm, LANES), 1)
    h_hi = h2.astype(BF16)
    h_lo = (h2 - h_hi.astype(F32)).astype(BF16)
    logits = (_dot(h_hi, wrh_ref[...]) + _dot(h_lo, wrh_ref[...]) + _dot(h_hi, wrl_ref[...])
              + _dot(h_lo, wrl_ref[...]) + br_ref[...])
    neg = jnp.float32(-jnp.inf)
    lg = jnp.where(lane < N_EXP, logits, neg)
    lane_f = lane.astype(F32)
    hots, vals = [], []
    for _k in range(TOP_K):
        mx = jnp.max(lg, axis=-1, keepdims=True)
        idx = jnp.min(jnp.where(lg == mx, lane_f, float(LANES)), axis=-1, keepdims=True)
        hot = lane_f == idx
        lg = jnp.where(hot, neg, lg)
        hots.append(hot)
        vals.append(mx)
    es = [jnp.exp(v - vals[0]) for v in vals]
    den = es[0] + es[1] + es[2] + es[3]
    ws = [e / den for e in es]

    msum = jnp.zeros((tm, LANES), F32)
    for hot in hots:
        msum = msum + hot.astype(F32)
    rr = lax.broadcasted_iota(jnp.int32, (tm, tm), 0)
    cc = lax.broadcasted_iota(jnp.int32, (tm, tm), 1)
    lower = (cc < rr).astype(BF16)
    prior = _dot(lower, msum.astype(BF16))
    cnt = jnp.sum(msum, axis=0, keepdims=True)
    seg = jnp.floor((cnt + (SUBLANES - 1.0)) * (1.0 / SUBLANES)) * SUBLANES
    er = lax.broadcasted_iota(jnp.int32, (LANES, LANES), 0)
    ec = lax.broadcasted_iota(jnp.int32, (LANES, LANES), 1)
    before = (er < ec).astype(BF16)
    seg_start = _dot(jnp.broadcast_to(seg, (SUBLANES, LANES)).astype(BF16), before)[0:1, :]
    slot_of = prior + seg_start

    route = jnp.zeros((tm, LANES), F32)
    for kk in range(TOP_K):
        hotf = hots[kk].astype(F32)
        e_col = jnp.sum(hotf * lane_f, axis=-1, keepdims=True)
        p_col = jnp.sum(hotf * slot_of, axis=-1, keepdims=True)
        route = jnp.where(lane == ROUTE_E + kk, e_col, route)
        route = jnp.where(lane == ROUTE_W + kk, ws[kk], route)
        route = jnp.where(lane == ROUTE_P + kk, p_col, route)
    route_o[...] = route
    cnt_o[...] = jnp.broadcast_to(cnt, cnt_o.shape)


def _mixout_call(og, sz, om, sg, sm, xf, mod3, tiles_per_seq, gn, wbg, wbm, wo, gpost, gffn,
                 wrh, wrl, br):
    n_tok = xf.shape[0]
    tm = TM_PROJ
    row = lambda i: (i, 0)

    def rs(w):
        return pl.BlockSpec((tm, w), row)

    return pl.pallas_call(
        _mixout_kernel,
        grid=(n_tok // tm,),
        in_specs=[rs(GLA_V), rs(GLA_V), rs(MLA_H * MLA_V), rs(D), rs(D), rs(D),
                  pl.BlockSpec((None, 1, 6 * D), lambda i: (i // tiles_per_seq, 0, 0)),
                  _resident(gn), _resident(wbg), _resident(wbm), _resident(wo),
                  _resident(gpost), _resident(gffn), _resident(wrh), _resident(wrl),
                  _resident(br)],
        out_specs=[rs(D), rs(D), rs(LANES),
                   pl.BlockSpec((None, SUBLANES, LANES), lambda i: (i, 0, 0))],
        out_shape=[jax.ShapeDtypeStruct((n_tok, D), F32), jax.ShapeDtypeStruct((n_tok, D), BF16),
                   jax.ShapeDtypeStruct((n_tok, LANES), F32),
                   jax.ShapeDtypeStruct((n_tok // tm, SUBLANES, LANES), F32)],
        compiler_params=_cparams(("arbitrary",)),
        name="mixout",
    )(og, sz, om, sg, sm, xf, mod3, gn, wbg, wbm, wo, gpost, gffn, wrh, wrl, br)


def _sorted_rows(n_tok):
    n_steps = n_tok // TM_PROJ
    worst_pad = n_steps * N_EXP * (SUBLANES - 1) + N_EXP * TM_FFN
    per_step = -(-worst_pad // (n_steps * ZERO_ROWS)) * ZERO_ROWS
    n_rows = n_tok * TOP_K + per_step * n_steps
    assert n_rows % TM_FFN == 0
    return n_rows


def _seg_copies(src_ref, dst_ref, src0, dst0, length, sem):
    for b in reversed(range(SUBLANES.bit_length() - 1, TM_PROJ.bit_length())):
        size = 1 << b
        off = (length >> (b + 1)) << (b + 1)

        @pl.when((length & size) != 0)
        def _():
            pltpu.make_async_copy(
                src_ref.at[pl.ds(pl.multiple_of(src0 + off, SUBLANES), size)],
                dst_ref.at[pl.ds(pl.multiple_of(dst0 + off, SUBLANES), size)], sem).start()


def _wait_rows(src_ref, dst_ref, total, sem):
    for b in reversed(range(SUBLANES.bit_length() - 1, STAGE_ROWS.bit_length())):
        size = 1 << b

        @pl.when((total & size) != 0)
        def _():
            pltpu.make_async_copy(src_ref.at[pl.ds(0, size)], dst_ref.at[pl.ds(0, size)],
                                  sem).wait()


def _zero_rows(zero_ref, dst_ref, start, end, sem, wait):
    length = end - start
    for b in reversed(range(SUBLANES.bit_length() - 1, ZERO_ROWS.bit_length())):
        size = 1 << b
        off = (length >> (b + 1)) << (b + 1)

        @pl.when((length & size) != 0)
        def _():
            cp = pltpu.make_async_copy(
                zero_ref.at[pl.ds(0, size)],
                dst_ref.at[pl.ds(pl.multiple_of(start + off, SUBLANES), size)], sem)
            if wait:
                cp.wait()
            else:
                cp.start()


def _slot_onehot(route_t, n_slots):
    tm = route_t.shape[1]
    slot = lax.broadcasted_iota(jnp.int32, (n_slots, tm), 0).astype(F32)
    acc = jnp.zeros((n_slots, tm), F32)
    for kk in range(TOP_K):
        acc = jnp.where(slot == route_t[ROUTE_P + kk:ROUTE_P + kk + 1, :], 1.0, acc)
    return acc.astype(BF16)


def _dispatch_kernel(meta_ref, pad_ref, route_ref, h_ref, xs_out, stage, zero_ref, sems, zsem):
    i = pl.program_id(0)
    n_steps = pl.num_programs(0)
    n_rows = xs_out.shape[0]
    n_seg = n_steps * N_EXP
    slot = i % 2

    @pl.when(i == 0)
    def _():
        zero_ref[...] = jnp.zeros_like(zero_ref)

    def clear_uncovered(wait):
        per = -(-N_EXP // n_steps)
        for j in range(per):
            e = i * per + j

            @pl.when(e < N_EXP)
            def _():
                ec = jnp.minimum(e, N_EXP - 1)
                _zero_rows(zero_ref, xs_out, pad_ref[ec], pad_ref[N_EXP + 1 + ec], zsem, wait)

        total_end = pad_ref[N_EXP]
        tail_rows = n_rows - n_steps * TM_PROJ * TOP_K
        per_step = tail_rows // n_steps
        for j in range(per_step // ZERO_ROWS):
            seg_end = (n_rows - tail_rows) + i * per_step + (j + 1) * ZERO_ROWS
            zlen = jnp.clip(seg_end - total_end, 0, ZERO_ROWS)
            _zero_rows(zero_ref, xs_out, seg_end - zlen, seg_end, zsem, wait)

    clear_uncovered(wait=False)

    onehot = _slot_onehot(route_ref[...].T, STAGE_ROWS)
    stage[slot] = _dot(onehot, h_ref[...])

    def issue(e, carry):
        s = i * N_EXP + e
        _seg_copies(stage.at[slot], xs_out, meta_ref[s], meta_ref[2 * n_seg + s],
                    meta_ref[n_seg + s], sems.at[slot])
        return carry

    lax.fori_loop(0, N_EXP, issue, 0)
    clear_uncovered(wait=True)

    @pl.when(i >= 1)
    def _():
        _wait_rows(stage.at[1 - slot], xs_out, meta_ref[3 * n_seg + i - 1], sems.at[1 - slot])

    @pl.when(i == n_steps - 1)
    def _():
        _wait_rows(stage.at[slot], xs_out, meta_ref[3 * n_seg + i], sems.at[slot])


def _dispatch_call(meta, pad_info, route, h2, n_rows):
    n_tok = h2.shape[0]
    tm = TM_PROJ
    n_steps = n_tok // tm
    assert (n_rows - n_tok * TOP_K) % (n_steps * ZERO_ROWS) == 0 and TM_FFN <= 2 * ZERO_ROWS
    grid_spec = pltpu.PrefetchScalarGridSpec(
        num_scalar_prefetch=2,
        grid=(n_steps,),
        in_specs=[pl.BlockSpec((tm, LANES), lambda i, m, p: (i, 0)),
                  pl.BlockSpec((tm, D), lambda i, m, p: (i, 0))],
        out_specs=pl.BlockSpec(memory_space=pl.ANY),
        scratch_shapes=[pltpu.VMEM((2, STAGE_ROWS, D), F32), pltpu.VMEM((ZERO_ROWS, D), F32),
                        pltpu.SemaphoreType.DMA((2,)), pltpu.SemaphoreType.DMA(())],
    )
    return pl.pallas_call(
        _dispatch_kernel,
        grid_spec=grid_spec,
        out_shape=jax.ShapeDtypeStruct((n_rows, D), F32),
        compiler_params=_cparams(("arbitrary",)),
        name="dispatch",
    )(meta, pad_info, route, h2)


def _ffn_kernel(te_ref, nx_ref, nu_ref, x_ref, wg_hbm, wu_hbm, wd_hbm, bg_ref, bu_ref, bd_ref,
                y_ref, land, wgb, wub, wdb, sems):
    i = pl.program_id(0)
    e = te_ref[i]
    first = jnp.logical_or(i == 0, e != te_ref[jnp.maximum(i - 1, 0)])
    used = i < nu_ref[0]
    w_hbm = (wg_hbm, wu_hbm, wd_hbm)

    def fetch(expert):
        for j in range(3):
            pltpu.make_async_copy(w_hbm[j].at[expert], land.at[j], sems.at[j]).start()

    @pl.when(jnp.logical_and(used, i == 0))
    def _():
        fetch(e)

    @pl.when(jnp.logical_and(used, first))
    def _():
        for j, dst in enumerate((wgb, wub, wdb)):
            pltpu.make_async_copy(w_hbm[j].at[0], land.at[j], sems.at[j]).wait()
            dst[...] = land[j].astype(BF16)
        nxt = nx_ref[i]

        @pl.when(nxt >= 0)
        def _():
            fetch(jnp.maximum(nxt, 0))

    @pl.when(used)
    def _():
        x = x_ref[...].astype(BF16)
        gate = jnp.minimum(_dot(x, wgb[...]) + bg_ref[...], SWIGLU_LIMIT)
        up = jnp.clip(_dot(x, wub[...]) + bu_ref[...], -SWIGLU_LIMIT, SWIGLU_LIMIT)
        act = ((up + 1.0) * gate * _sigmoid(SWIGLU_ALPHA * gate)).astype(BF16)
        y_ref[...] = _dot(act, wdb[...]) + bd_ref[...]

    @pl.when(jnp.logical_not(used))
    def _():
        y_ref[...] = jnp.zeros_like(y_ref)


def _ffn_call(tile_exp, next_exp, n_used, xs, w_gate, w_up, w_down, b_gate, b_up, b_down):
    n_rows = xs.shape[0]
    tm = TM_FFN
    n_tiles = n_rows // tm
    assert D == D_FF

    def xrow(i, te, nx, nu):
        return (jnp.maximum(jnp.minimum(i, nu[0] - 1), 0), 0)

    def bsel(i, te, nx, nu):
        return (te[i], 0, 0)

    grid_spec = pltpu.PrefetchScalarGridSpec(
        num_scalar_prefetch=3,
        grid=(n_tiles,),
        in_specs=[pl.BlockSpec((tm, D), xrow),
                  pl.BlockSpec(memory_space=pl.ANY),
                  pl.BlockSpec(memory_space=pl.ANY),
                  pl.BlockSpec(memory_space=pl.ANY),
                  pl.BlockSpec((None, 1, D_FF), bsel),
                  pl.BlockSpec((None, 1, D_FF), bsel),
                  pl.BlockSpec((None, 1, D), bsel)],
        out_specs=pl.BlockSpec((tm, D), lambda i, te, nx, nu: (i, 0)),
        scratch_shapes=[pltpu.VMEM((3, D, D_FF), F32),
                        pltpu.VMEM((D, D_FF), BF16), pltpu.VMEM((D, D_FF), BF16),
                        pltpu.VMEM((D_FF, D), BF16), pltpu.SemaphoreType.DMA((3,))],
    )
    return pl.pallas_call(
        _ffn_kernel,
        grid_spec=grid_spec,
        out_shape=jax.ShapeDtypeStruct((n_rows, D), F32),
        compiler_params=_cparams(("arbitrary",)),
        name="ffn",
    )(tile_exp, next_exp, n_used, xs, w_gate, w_up, w_down,
      b_gate.reshape(N_EXP, 1, D_FF), b_up.reshape(N_EXP, 1, D_FF), b_down.reshape(N_EXP, 1, D))


def _combine_kernel(meta_ref, y_hbm, route_ref, x1_ref, mod_ref, gpost_ref, o_ref, stage, sems):
    i = pl.program_id(0)
    n_steps = pl.num_programs(0)
    n_seg = n_steps * N_EXP
    tm = x1_ref.shape[0]
    slot = i % 2

    def fetch(tile, into):
        def issue(e, carry):
            s = tile * N_EXP + e
            _seg_copies(y_hbm, stage.at[into], meta_ref[2 * n_seg + s], meta_ref[s],
                        meta_ref[n_seg + s], sems.at[into])
            return carry

        lax.fori_loop(0, N_EXP, issue, 0)

    @pl.when(i == 0)
    def _():
        stage[...] = jnp.zeros_like(stage)
        fetch(0, 0)

    @pl.when(i + 1 < n_steps)
    def _():
        fetch(i + 1, 1 - slot)

    _wait_rows(y_hbm, stage.at[slot], meta_ref[3 * n_seg + i], sems.at[slot])

    route = route_ref[...]
    col = lax.broadcasted_iota(jnp.int32, (tm, STAGE_ROWS), 1).astype(F32)
    wmat = jnp.zeros((tm, STAGE_ROWS), F32)
    for kk in range(TOP_K):
        hit = col == route[:, ROUTE_P + kk:ROUTE_P + kk + 1]
        wmat = jnp.where(hit, route[:, ROUTE_W + kk:ROUTE_W + kk + 1], wmat)
    moe = _dot(wmat.astype(BF16), stage[slot].astype(BF16))
    gt_f = mod_ref[...][:, 5 * D:6 * D]
    o_ref[...] = x1_ref[...] + gt_f * _rms(moe, gpost_ref[...])


def _combine_call(meta, y, route, x1, mod3, tiles_per_seq, gpost):
    n_tok = x1.shape[0]
    tm = TM_PROJ
    grid_spec = pltpu.PrefetchScalarGridSpec(
        num_scalar_prefetch=1,
        grid=(n_tok // tm,),
        in_specs=[pl.BlockSpec(memory_space=pl.ANY),
                  pl.BlockSpec((tm, LANES), lambda i, m: (i, 0)),
                  pl.BlockSpec((tm, D), lambda i, m: (i, 0)),
                  pl.BlockSpec((None, 1, 6 * D), lambda i, m: (i // tiles_per_seq, 0, 0)),
                  pl.BlockSpec((1, D), lambda i, m: (0, 0))],
        out_specs=pl.BlockSpec((tm, D), lambda i, m: (i, 0)),
        scratch_shapes=[pltpu.VMEM((2, STAGE_ROWS, D), F32), pltpu.SemaphoreType.DMA((2,))],
    )
    return pl.pallas_call(
        _combine_kernel,
        grid_spec=grid_spec,
        out_shape=jax.ShapeDtypeStruct((n_tok, D), F32),
        compiler_params=_cparams(("arbitrary",)),
        name="combine",
    )(meta, y, route, x1, mod3, gpost)


def _rope_swap(w):
    q = MLA_ROPE // 4
    return jnp.concatenate([-w[..., q:2 * q], w[..., 0:q], -w[..., 3 * q:4 * q],
                            w[..., 2 * q:3 * q]], axis=-1)


def _rope_tables(seq):
    rows = seq // GRID_W
    r, col = np.meshgrid(np.arange(rows, dtype=np.float64), np.arange(GRID_W, dtype=np.float64),
                         indexing="ij")
    half = MLA_ROPE // 2
    inv_freq = ROPE_BASE ** (-np.arange(0, half, 2, dtype=np.float64) / half)
    ar = r.reshape(-1)[:, None] * inv_freq
    ac = col.reshape(-1)[:, None] * inv_freq
    cos = np.concatenate([np.cos(ar), np.cos(ar), np.cos(ac), np.cos(ac)], axis=-1)
    sin = np.concatenate([np.sin(ar), np.sin(ar), np.sin(ac), np.sin(ac)], axis=-1)
    return cos, sin


def _head_slots(w3, lead):
    rows, heads, width = w3.shape
    return jnp.pad(w3, ((0, 0), (0, 0), (lead, HEAD_SLOT - lead - width))).reshape(
        rows, heads * HEAD_SLOT)


def kernel(x, c, ctx, c_ctx, w_mod, b_mod, g_pre_mix, g_post_mix, g_pre_ffn, g_post_ffn, w_in,
           gla_w_a2_f, gla_b_a_f, gla_w_a2_b, gla_b_a_b, gla_g_norm, mla_g_q, mla_w_uq, mla_g_kv,
           mla_w_uk, mla_w_uv, w_br_gla, w_br_mla, w_out, router_w, router_b, w_gate, b_gate,
           w_up, b_up, w_down, b_down):
    depth = w_mod.shape[0]
    assert depth == 1, "single-layer block"
    batch, seq, d = x.shape
    ctx_len = ctx.shape[1]
    assert d == D and seq % TM_PROJ == 0 and (batch * ctx_len) % TM_PROJ == 0
    assert TM_PROJ % ctx_len == 0 or ctx_len % TM_PROJ == 0
    n_tok = batch * seq

    cc = jnp.zeros((16, D), F32).at[:batch].set(c).at[batch].set(c_ctx)
    mod = _mod_call(cc, w_mod[0], b_mod[0])
    mod3 = mod.reshape(16, 1, 6 * D)

    wi = w_in[0]
    kr = wi[:, O_KR:O_KR + MLA_ROPE]
    small = jnp.concatenate([wi[:, O_AF:O_AF + GLA_RANK], wi[:, O_AB:O_AB + GLA_RANK], kr,
                             _rope_swap(kr), jnp.zeros((D, LANES - X_KRS - MLA_ROPE), F32)],
                            axis=1)
    w1 = jnp.concatenate([wi[:, O_Q:O_G], wi[:, O_DKV:O_DKV + MLA_KVR], small], axis=1).astype(BF16)
    w2 = jnp.concatenate([wi[:, O_G:O_G + GLA_V], wi[:, O_DQ:O_DQ + MLA_QR], wi[:, O_MG:]],
                         axis=1).astype(BF16)
    wa = jnp.zeros((LANES, 2 * GLA_QK), F32)
    wa = wa.at[X_AF:X_AF + GLA_RANK, 0:GLA_QK].set(gla_w_a2_f[0])
    wa = wa.at[X_AB:X_AB + GLA_RANK, GLA_QK:].set(gla_w_a2_b[0]).astype(BF16)
    ba = jnp.concatenate([gla_b_a_f[0], gla_b_a_b[0]]).reshape(1, 2 * GLA_QK)

    uk = mla_w_uk[0]
    uv = mla_w_uv[0]
    uq = mla_w_uq[0]
    wk_top = _head_slots(uk.reshape(MLA_KVR, MLA_H, MLA_NOPE), 0)
    place = np.zeros((LANES, MLA_H, HEAD_SLOT), np.float32)
    for j in range(MLA_ROPE):
        place[X_KR + j, :, MLA_NOPE + j] = 1.0
        place[X_KRS + j, :, MLA_NOPE + j] = 1.0
    wk = jnp.concatenate([wk_top, jnp.asarray(place.reshape(LANES, MLA_W))],
                         axis=0).astype(BF16)
    uv4 = uv.reshape(MLA_KVR, MLA_H // 2, 2, MLA_V)
    zv = jnp.zeros_like(uv4[:, :, 0])
    wv = jnp.stack([jnp.concatenate([uv4[:, :, 0], zv], axis=-1),
                    jnp.concatenate([zv, uv4[:, :, 1]], axis=-1)],
                   axis=2).reshape(MLA_KVR, MLA_W).astype(BF16)
    uq3 = uq.reshape(MLA_QR, MLA_H, MLA_QKD)
    wq_a = _head_slots(uq3, 0)
    wq_b = _head_slots(_rope_swap(uq3[:, :, MLA_NOPE:]), MLA_NOPE)
    wq = jnp.concatenate([wq_a, wq_b], axis=1).astype(BF16)

    cos, sin = _rope_tables(seq)
    zeros32 = np.zeros((seq, MLA_ROPE))
    t1_x = jnp.asarray(np.concatenate([zeros32, cos, sin, zeros32], axis=1), F32)
    t1_c = jnp.asarray(np.broadcast_to(
        np.concatenate([np.zeros(MLA_ROPE), np.ones(MLA_ROPE), np.zeros(2 * MLA_ROPE)]),
        (TM_PROJ, LANES)), F32)
    q_scale = MLA_SCALE * LOG2_E
    cq = jnp.asarray(np.concatenate([np.ones((seq, MLA_NOPE)), cos, zeros32], axis=1) * q_scale, F32)
    sq = jnp.asarray(np.concatenate([np.zeros((seq, MLA_NOPE)), sin, zeros32], axis=1) * q_scale,
                     F32)

    gpre = g_pre_mix[0].reshape(1, D)
    gkv = mla_g_kv[0].reshape(1, MLA_KVR)
    gq = mla_g_q[0].reshape(1, MLA_QR)
    tiles_per_seq = seq // TM_PROJ

    xf = x.reshape(n_tok, D)
    cf = ctx.reshape(batch * ctx_len, D)
    (gqx, gkx, gvx, lfx, lbx, mkx, mvx, sz, mq, sg, sm) = _inproj_call(
        True, xf, mod3, lambda i: i // tiles_per_seq, tiles_per_seq, gpre, w1, wa, ba, wk, wv,
        gkv, t1_x, extra=(w2, gq, wq, cq, sq))
    (gqc, gkc, gvc, lfc, lbc, mkc, mvc) = _inproj_call(
        False, cf, mod3, lambda i: batch, 1, gpre, w1, wa, ba, wk, wv, gkv, t1_c)

    og = _gla_call((gqx, gkx, gvx, lfx, lbx), (gqc, gkc, gvc, lfc, lbc), batch, seq, ctx_len)
    om = _mla_call(mq, mkx, mvx, mkc, mvc, batch, seq, ctx_len)

    wr = jnp.pad(router_w[0], ((0, 0), (0, LANES - N_EXP)))
    wrh = wr.astype(BF16)
    wrl = (wr - wrh.astype(F32)).astype(BF16)
    br =jnp.pad(router_b[0], (0, LANES - N_EXP)).reshape(1, LANES)
    x1, h2, route, cnt = _mixout_call(
        og, sz, om, sg, sm, xf, mod3, tiles_per_seq, gla_g_norm[0].reshape(1, GLA_DV),
        w_br_gla[0].astype(BF16), w_br_mla[0].astype(BF16), w_out[0].astype(BF16),
        g_post_mix[0].reshape(1, D), g_pre_ffn[0].reshape(1, D), wrh, wrl, br)

    seg = -(-cnt[:, 0, :N_EXP].astype(jnp.int32) // SUBLANES) * SUBLANES
    stage_start = jnp.cumsum(seg, axis=1) - seg
    in_group = jnp.cumsum(seg, axis=0) - seg
    group = jnp.sum(seg, axis=0)
    padded = -(-group // TM_FFN) * TM_FFN
    ends = jnp.cumsum(padded)
    starts = ends - padded
    sorted_row = starts[None, :] + in_group
    meta = jnp.concatenate([stage_start.reshape(-1), seg.reshape(-1), sorted_row.reshape(-1),
                            jnp.sum(seg, axis=1)]).astype(jnp.int32)
    pad_info = jnp.concatenate([starts + group, ends[-1:], ends]).astype(jnp.int32)
    n_rows = _sorted_rows(n_tok)
    tile_start = jnp.arange(n_rows // TM_FFN, dtype=jnp.int32) * TM_FFN
    tile_exp = jnp.minimum(jnp.sum((ends[None, :] <= tile_start[:, None]).astype(jnp.int32), axis=1),
                           N_EXP - 1)
    n_used = (ends[-1:] // TM_FFN).astype(jnp.int32)
    ids = jnp.arange(N_EXP, dtype=jnp.int32)
    later = jnp.logical_and(padded[None, :] > 0, ids[None, :] > ids[:, None])
    next_owner = jnp.min(jnp.where(later, ids[None, :], N_EXP), axis=1)
    next_owner = jnp.where(next_owner == N_EXP, -1, next_owner)
    next_exp = jnp.sum(jnp.where(tile_exp[:, None] == ids[None, :], next_owner[None, :], 0),
                       axis=1).astype(jnp.int32)

    xs = _dispatch_call(meta, pad_info, route, h2, n_rows)
    y = _ffn_call(tile_exp, next_exp, n_used, xs, w_gate[0], w_up[0], w_down[0], b_gate[0], b_up[0],
                  b_down[0])
    out = _combine_call(meta, y, route, x1, mod3, tiles_per_seq, g_post_ffn[0].reshape(1, D))
    return out.reshape(batch, seq, D)
```

```python
import functools

import jax
import jax.numpy as jnp
import numpy as np
from jax import lax
from jax.experimental import pallas as pl
from jax.experimental.pallas import tpu as pltpu

F32 = jnp.float32
BF16 = jnp.bfloat16

D = 1024
EPS = 1e-6
GRID_W = 64

GLA_H = 4
GLA_DK = 64
GLA_DV = 128
GLA_RANK = 16
GLA_TAU = 16.0
GLA_CHUNK = 64
GLA_BLOCK = 256
GLA_QK = GLA_H * GLA_DK
GLA_V = GLA_H * GLA_DV

MLA_H = 8
MLA_QR = 256
MLA_KVR = 128
MLA_NOPE = 64
MLA_ROPE = 32
MLA_V = 64
MLA_QKD = MLA_NOPE + MLA_ROPE
MLA_SCALE = MLA_QKD ** -0.5
LOG2_E = 1.4426950408889634
ROPE_BASE = 10000.0
HEAD_SLOT = 128
MLA_W = MLA_H * HEAD_SLOT
SUM_LANE_EVEN, SUM_LANE_ODD = MLA_V, 0

N_EXP = 32
TOP_K = 4
D_FF = 1024
SWIGLU_LIMIT = 7.0
SWIGLU_ALPHA = 1.702

LANES = 128
SUBLANES = 8
TM_PROJ = 512
TQ = 512
TM_FFN = 512
ZERO_ROWS = 256
STAGE_ROWS = -(-(TM_PROJ * TOP_K + N_EXP * (SUBLANES - 1)) // 256) * 256
ROUTE_E, ROUTE_W, ROUTE_P = 0, TOP_K, 2 * TOP_K

_OFF = np.cumsum([0, GLA_QK, GLA_QK, GLA_V, GLA_V, GLA_RANK, GLA_RANK,
                  MLA_QR, MLA_KVR, MLA_ROPE, D, D])
(O_Q, O_K, O_V, O_G, O_AF, O_AB, O_DQ, O_DKV, O_KR, O_MG, O_MM, _) = _OFF.tolist()

X_AF, X_AB, X_KR, X_KRS = 0, 16, 32, 64

VMEM_LIMIT = 56 * 1024 * 1024


def _cparams(sem):
    return pltpu.CompilerParams(dimension_semantics=sem, vmem_limit_bytes=VMEM_LIMIT)


def _resident(arr):
    nd = arr.ndim
    return pl.BlockSpec(arr.shape, lambda *_: (0,) * nd, pipeline_mode=pl.Buffered(1))


def _rms(x, g):
    return x * lax.rsqrt(jnp.mean(x * x, axis=-1, keepdims=True) + EPS) * g


def _sigmoid(x):
    return 1.0 / (1.0 + jnp.exp(-x))


def _log_sigmoid(x):
    return jnp.minimum(x, 0.0) - jnp.log1p(jnp.exp(-jnp.abs(x)))


def _dot(a, b):
    return jnp.dot(a, b, preferred_element_type=F32)


def _dot_nt(a, b):
    return lax.dot_general(a, b, (((1,), (1,)), ((), ())), preferred_element_type=F32)


def _dot_tn(a, b):
    return lax.dot_general(a, b, (((0,), (0,)), ((), ())), preferred_element_type=F32)


def _mod_kernel(c_ref, w_ref, b_ref, o_ref):
    c = c_ref[...]
    s = (c * _sigmoid(c)).astype(BF16)
    o_ref[...] = _dot(s, w_ref[...].astype(BF16)) + b_ref[...]


def _mod_call(cc, w_mod, b_mod):
    n = w_mod.shape[1]
    tn = 1536
    return pl.pallas_call(
        _mod_kernel,
        grid=(n // tn,),
        in_specs=[pl.BlockSpec((16, D), lambda j: (0, 0)),
                  pl.BlockSpec((D, tn), lambda j: (0, j)),
                  pl.BlockSpec((1, tn), lambda j: (0, j))],
        out_specs=pl.BlockSpec((16, tn), lambda j: (0, j)),
        out_shape=jax.ShapeDtypeStruct((16, n), F32),
        compiler_params=_cparams(("arbitrary",)),
        name="mod",
    )(cc, w_mod, b_mod.reshape(1, n))


def _inproj_kernel(with_q, x_ref, mod_ref, gpre_ref, w1_ref, wa_ref, ba_ref, wk_ref,
                   wv_ref, gkv_ref, t1_ref, *rest):
    if with_q:
        (w2_ref, gq_ref, wq_ref, cq_ref, sq_ref,
         q_o, k_o, v_o, lf_o, lb_o, mk_o, mv_o, sz_o, mq_o, sg_o, sm_o) = rest
    else:
        (q_o, k_o, v_o, lf_o, lb_o, mk_o, mv_o) = rest
    x = x_ref[...]
    mod = mod_ref[...]
    sh = mod[:, 0:D]
    sc = mod[:, D:2 * D]
    h = (_rms(x, gpre_ref[...]) * (1.0 + sc) + sh).astype(BF16)

    z1 = _dot(h, w1_ref[...])
    q_o[...] = (z1[:, 0:GLA_QK] * (GLA_DK ** -0.5)).astype(BF16)
    k_o[...] = z1[:, GLA_QK:2 * GLA_QK].astype(BF16)
    v_o[...] = z1[:, 2 * GLA_QK:2 * GLA_QK + GLA_V].astype(BF16)
    o_dkv = 2 * GLA_QK + GLA_V
    ckv = _rms(z1[:, o_dkv:o_dkv + MLA_KVR], gkv_ref[...])
    xs = z1[:, o_dkv + MLA_KVR:o_dkv + MLA_KVR + LANES]

    la = _log_sigmoid(_dot(xs.astype(BF16), wa_ref[...]) + ba_ref[...]) * (1.0 / GLA_TAU)
    lf_o[...] = la[:, 0:GLA_QK]
    lb_o[...] = la[:, GLA_QK:2 * GLA_QK]

    lhs_k = jnp.concatenate([ckv, xs * t1_ref[...]], axis=-1).astype(BF16)
    mk_o[...] = _dot(lhs_k, wk_ref[...]).astype(BF16)
    lane = lax.broadcasted_iota(jnp.int32, (x.shape[0], MLA_W), 1)
    mv = _dot(ckv.astype(BF16), wv_ref[...])
    mv_o[...] = jnp.where(lane % (2 * HEAD_SLOT) == SUM_LANE_EVEN, 1.0,
                          jnp.where(lane % (2 * HEAD_SLOT) == HEAD_SLOT + SUM_LANE_ODD, 1.0,
                                    mv)).astype(BF16)

    if with_q:
        zg = _dot(h, w2_ref[:, 0:GLA_V])
        sz_o[...] = (zg * _sigmoid(zg)).astype(BF16)
        n = _rms(_dot(h, w2_ref[:, GLA_V:GLA_V + MLA_QR]), gq_ref[...]).astype(BF16)
        cq = jnp.concatenate([cq_ref[...]] * MLA_H, axis=-1)
        sq = jnp.concatenate([sq_ref[...]] * MLA_H, axis=-1)
        mq_o[...] = (_dot(n, wq_ref[:, 0:MLA_W]) * cq
                     + _dot(n, wq_ref[:, MLA_W:2 * MLA_W]) * sq).astype(BF16)
        o_mg = GLA_V + MLA_QR
        sg_o[...] = _sigmoid(_dot(h, w2_ref[:, o_mg:o_mg + D])).astype(BF16)
        sm_o[...] = _sigmoid(_dot(h, w2_ref[:, o_mg + D:o_mg + 2 * D])).astype(BF16)


def _inproj_call(with_q, xf, mod3, mod_row_fn, tiles_per_seq, gpre, w1, wa, ba, wk, wv,
                 gkv, t1, extra=()):
    n_tok = xf.shape[0]
    tm = TM_PROJ
    grid = (n_tok // tm,)
    row = lambda i: (i, 0)
    tab = lambda i: (i % tiles_per_seq, 0)
    in_specs = [
        pl.BlockSpec((tm, D), row),
        pl.BlockSpec((None, 1, 6 * D), lambda i: (mod_row_fn(i), 0, 0)),
        _resident(gpre), _resident(w1), _resident(wa), _resident(ba), _resident(wk),
        _resident(wv), _resident(gkv),
        pl.BlockSpec((tm, LANES), tab),
    ]
    widths = [(GLA_QK, BF16), (GLA_QK, BF16), (GLA_V, BF16), (GLA_QK, F32), (GLA_QK, F32),
              (MLA_W, BF16), (MLA_W, BF16)]
    args = [xf, mod3, gpre, w1, wa, ba, wk, wv, gkv, t1]
    if with_q:
        w2, gq, wq, cq, sq = extra
        in_specs += [_resident(w2), _resident(gq), _resident(wq),
                     pl.BlockSpec((tm, HEAD_SLOT), tab), pl.BlockSpec((tm, HEAD_SLOT), tab)]
        args += [w2, gq, wq, cq, sq]
        widths += [(GLA_V, BF16), (MLA_W, BF16), (D, BF16), (D, BF16)]
    return pl.pallas_call(
        functools.partial(_inproj_kernel, with_q),
        grid=grid,
        in_specs=in_specs,
        out_specs=[pl.BlockSpec((tm, w), row) for w, _ in widths],
        out_shape=[jax.ShapeDtypeStruct((n_tok, w), dt) for w, dt in widths],
        compiler_params=_cparams(("arbitrary",)),
        name="inproj_x" if with_q else "inproj_ctx",
    )(*args)


def _gla_kernel(qx, kx, vx, lfx, lbx, qc, kc, vc, lfc, lbc, o_ref, sf_ref, sb_ref):
    C = GLA_CHUNK
    R = GLA_BLOCK
    n_sub = R // C
    rr = lax.broadcasted_iota(jnp.int32, (R, R), 0)
    cc = lax.broadcasted_iota(jnp.int32, (R, R), 1)
    same = (rr // C) == (cc // C)
    tbd_f = jnp.logical_and(same, cc <= rr).astype(BF16)
    tbd_b = jnp.logical_and(same, cc >= rr).astype(BF16)
    row = lax.broadcasted_iota(jnp.int32, (C, GLA_QK), 0)
    col = lax.broadcasted_iota(jnp.int32, (C, GLA_QK), 1)
    head_qk = col // GLA_DK
    tri4_f = (col % C) <= row
    tri4_b = (col % C) >= row
    head_v = lax.broadcasted_iota(jnp.int32, (C, GLA_V), 1) // GLA_DV

    def stack_masked(x, head_of_lane):
        return jnp.concatenate(
            [jnp.where(head_of_lane == hh, x, jnp.zeros_like(x)) for hh in range(GLA_H)], axis=0)

    def block(q_ref, k_ref, v_ref, la_ref, row0, fwd, s_ref, emit):
        sl = pl.ds(row0, R)
        la = la_ref[sl, :]
        la_hi = la.astype(BF16)
        la_lo = (la - la_hi.astype(F32)).astype(BF16)
        tbd = tbd_f if fwd else tbd_b
        cum = _dot(tbd, la_hi) + _dot(tbd, la_lo)
        k = k_ref[sl, :].astype(F32)
        v = v_ref[sl, :]
        if emit:
            q = q_ref[sl, :].astype(F32)
        st = s_ref[...]
        outs = [None] * n_sub
        for ci in (range(n_sub) if fwd else reversed(range(n_sub))):
            rs = slice(ci * C, (ci + 1) * C)
            cum_c = cum[rs]
            tot = cum_c[C - 1:C, :] if fwd else cum_c[0:1, :]
            k_c = k[rs]
            v_c = v[rs]
            ke_bd = stack_masked((k_c * jnp.exp(tot - cum_c)).astype(BF16), head_qk)
            v_stack = jnp.concatenate(
                [v_c[:, hh * GLA_DV:(hh + 1) * GLA_DV] for hh in range(GLA_H)], axis=0)
            if emit:
                qd = (q[rs] * jnp.exp(cum_c)).astype(BF16)
                ki_bd = stack_masked((k_c * jnp.exp(-cum_c)).astype(BF16), head_qk)
                att = _dot_nt(qd, ki_bd)
                att = jnp.where(tri4_f if fwd else tri4_b, att, 0.0).astype(BF16)
                o_intra = _dot(att, stack_masked(v_c, head_v))
                oi = _dot_nt(stack_masked(qd, head_qk), st.astype(BF16))
                o_inter = jnp.concatenate([oi[hh * C:(hh + 1) * C] for hh in range(GLA_H)],
                                          axis=-1)
                outs[ci] = o_intra + o_inter
            st = st * jnp.exp(tot) + _dot_tn(v_stack, ke_bd)
        s_ref[...] = st
        if emit:
            o_ref[sl, :] += jnp.concatenate(outs, axis=0)

    sf_ref[...] = jnp.zeros_like(sf_ref)
    sb_ref[...] = jnp.zeros_like(sb_ref)
    o_ref[...] = jnp.zeros_like(o_ref)
    n_ctx = qc.shape[0] // R
    n_x = qx.shape[0] // R
    for i in range(n_ctx):
        block(qc, kc, vc, lfc, i * R, True, sf_ref, False)
        block(qc, kc, vc, lbc, (n_ctx - 1 - i) * R, False, sb_ref, False)

    def body(i, carry):
        block(qx, kx, vx, lfx, pl.multiple_of(i * R, R), True, sf_ref, True)
        block(qx, kx, vx, lbx, pl.multiple_of((n_x - 1 - i) * R, R), False, sb_ref, True)
        return carry

    lax.fori_loop(0, n_x, body, 0)


def _gla_call(fx, fc, batch, seq, ctx_len):
    qx, kx, vx, lfx, lbx = fx
    qc, kc, vc, lfc, lbc = fc

    def spec(rows, w):
        return pl.BlockSpec((rows, w), lambda b: (b, 0))

    return pl.pallas_call(
        _gla_kernel,
        grid=(batch,),
        in_specs=[spec(seq, GLA_QK), spec(seq, GLA_QK), spec(seq, GLA_V), spec(seq, GLA_QK),
                  spec(seq, GLA_QK),
                  spec(ctx_len, GLA_QK), spec(ctx_len, GLA_QK), spec(ctx_len, GLA_V),
                  spec(ctx_len, GLA_QK), spec(ctx_len, GLA_QK)],
        out_specs=spec(seq, GLA_V),
        out_shape=jax.ShapeDtypeStruct((batch * seq, GLA_V), F32),
        scratch_shapes=[pltpu.VMEM((GLA_DV, GLA_QK), F32), pltpu.VMEM((GLA_DV, GLA_QK), F32)],
        compiler_params=_cparams(("arbitrary",)),
        name="gla",
    )(qx, kx, vx, lfx, lbx, qc, kc, vc, lfc, lbc)


def _mla_kernel(q_ref, kx_ref, vx_ref, kc_ref, vc_ref, o_ref):
    lane = lax.broadcasted_iota(jnp.int32, (q_ref.shape[0], HEAD_SLOT), 1)
    for j in range(MLA_H // 2):
        pair = []
        for hh, sum_lane in ((2 * j, SUM_LANE_EVEN), (2 * j + 1, SUM_LANE_ODD)):
            sl = slice(hh * HEAD_SLOT, (hh + 1) * HEAD_SLOT)
            q = q_ref[:, sl]
            sx = _dot_nt(q, kx_ref[:, sl])
            sc = _dot_nt(q, kc_ref[:, sl])
            m = jnp.maximum(jnp.max(sx, axis=-1, keepdims=True),
                            jnp.max(sc, axis=-1, keepdims=True))
            px = jnp.exp2(sx - m).astype(BF16)
            pc = jnp.exp2(sc - m).astype(BF16)
            o = _dot(px, vx_ref[:, sl]) + _dot(pc, vc_ref[:, sl])
            pair.append(o / o[:, sum_lane:sum_lane + 1])
        o_ref[:, j * HEAD_SLOT:(j + 1) * HEAD_SLOT] = jnp.where(
            lane < MLA_V, pair[0], pair[1]).astype(BF16)


def _mla_call(mq, mkx, mvx, mkc, mvc, batch, seq, ctx_len):
    nq = seq // TQ
    return pl.pallas_call(
        _mla_kernel,
        grid=(batch, nq),
        in_specs=[pl.BlockSpec((TQ, MLA_W), lambda b, i: (b * nq + i, 0)),
                  pl.BlockSpec((seq, MLA_W), lambda b, i: (b, 0)),
                  pl.BlockSpec((seq, MLA_W), lambda b, i: (b, 0)),
                  pl.BlockSpec((ctx_len, MLA_W), lambda b, i: (b, 0)),
                  pl.BlockSpec((ctx_len, MLA_W), lambda b, i: (b, 0))],
        out_specs=pl.BlockSpec((TQ, MLA_H * MLA_V), lambda b, i: (b * nq + i, 0)),
        out_shape=jax.ShapeDtypeStruct((batch * seq, MLA_H * MLA_V), BF16),
        compiler_params=_cparams(("arbitrary", "arbitrary")),
        name="mla",
    )(mq, mkx, mvx, mkc, mvc)


def _mixout_kernel(og_ref, sz_ref, om_ref, sg_ref, sm_ref, x_ref, mod_ref, gn_ref, wbg_ref,
                   wbm_ref, wo_ref, gpost_ref, gffn_ref, wrh_ref, wrl_ref, br_ref,
                   x1_o, h2_o, route_o, cnt_o):
    tm = x_ref.shape[0]
    mod = mod_ref[...]
    gt_a = mod[:, 2 * D:3 * D]
    sh_f = mod[:, 3 * D:4 * D]
    sc_f = mod[:, 4 * D:5 * D]

    og = og_ref[...]
    gn = gn_ref[...]
    parts = [_rms(og[:, hh * GLA_DV:(hh + 1) * GLA_DV], gn) for hh in range(GLA_H)]
    a = (jnp.concatenate(parts, axis=-1) * sz_ref[...].astype(F32)).astype(BF16)
    br_g = _dot(a, wbg_ref[...])
    br_m = _dot(om_ref[...], wbm_ref[...])
    merged = (sg_ref[...].astype(F32) * br_g + sm_ref[...].astype(F32) * br_m).astype(BF16)
    mo = _dot(merged, wo_ref[...])
    x1 = x_ref[...] + gt_a * _rms(mo, gpost_ref[...])
    x1_o[...] = x1
    h2 = _rms(x1, gffn_ref[...]) * (1.0 + sc_f) + sh_f
    h2_o[...] = h2.astype(BF16)

    lane = lax.broadcasted_iota(jnp.int32, (tm, LANES), 1)
    h_hi = h2.astype(BF16)
    h_lo = (h2 - h_hi.astype(F32)).astype(BF16)
    logits = (_dot(h_hi, wrh_ref[...]) + _dot(h_lo, wrh_ref[...]) + _dot(h_hi, wrl_ref[...])
              + _dot(h_lo, wrl_ref[...]) + br_ref[...])
    neg = jnp.float32(-jnp.inf)
    lg = jnp.where(lane < N_EXP, logits, neg)
    lane_f = lane.astype(F32)
    hots, vals = [], []
    for _k in range(TOP_K):
        mx = jnp.max(lg, axis=-1, keepdims=True)
        idx = jnp.min(jnp.where(lg == mx, lane_f, float(LANES)), axis=-1, keepdims=True)
        hot = lane_f == idx
        lg = jnp.where(hot, neg, lg)
        hots.append(hot)
        vals.append(mx)
    es = [jnp.exp(v - vals[0]) for v in vals]
    den = es[0] + es[1] + es[2] + es[3]
    ws = [e / den for e in es]

    msum = jnp.zeros((tm, LANES), F32)
    for hot in hots:
        msum = msum + hot.astype(F32)
    rr = lax.broadcasted_iota(jnp.int32, (tm, tm), 0)
    cc = lax.broadcasted_iota(jnp.int32, (tm, tm), 1)
    lower = (cc < rr).astype(BF16)
    prior = _dot(lower, msum.astype(BF16))
    cnt = jnp.sum(msum, axis=0, keepdims=True)
    seg = jnp.floor((cnt + (SUBLANES - 1.0)) * (1.0 / SUBLANES)) * SUBLANES
    er = lax.broadcasted_iota(jnp.int32, (LANES, LANES), 0)
    ec = lax.broadcasted_iota(jnp.int32, (LANES, LANES), 1)
    before = (er < ec).astype(BF16)
    seg_start = _dot(jnp.broadcast_to(seg, (SUBLANES, LANES)).astype(BF16), before)[0:1, :]
    slot_of = prior + seg_start

    route = jnp.zeros((tm, LANES), F32)
    for kk in range(TOP_K):
        hotf = hots[kk].astype(F32)
        e_col = jnp.sum(hotf * lane_f, axis=-1, keepdims=True)
        p_col = jnp.sum(hotf * slot_of, axis=-1, keepdims=True)
        route = jnp.where(lane == ROUTE_E + kk, e_col, route)
        route = jnp.where(lane == ROUTE_W + kk, ws[kk], route)
        route = jnp.where(lane == ROUTE_P + kk, p_col, route)
    route_o[...] = route
    cnt_o[...] = jnp.broadcast_to(cnt, cnt_o.shape)


def _mixout_call(og, sz, om, sg, sm, xf, mod3, tiles_per_seq, gn, wbg, wbm, wo, gpost, gffn,
                 wrh, wrl, br):
    n_tok = xf.shape[0]
    tm = TM_PROJ
    row = lambda i: (i, 0)

    def rs(w):
        return pl.BlockSpec((tm, w), row)

    return pl.pallas_call(
        _mixout_kernel,
        grid=(n_tok // tm,),
        in_specs=[rs(GLA_V), rs(GLA_V), rs(MLA_H * MLA_V), rs(D), rs(D), rs(D),
                  pl.BlockSpec((None, 1, 6 * D), lambda i: (i // tiles_per_seq, 0, 0)),
                  _resident(gn), _resident(wbg), _resident(wbm), _resident(wo),
                  _resident(gpost), _resident(gffn), _resident(wrh), _resident(wrl),
                  _resident(br)],
        out_specs=[rs(D), rs(D), rs(LANES),
                   pl.BlockSpec((None, SUBLANES, LANES), lambda i: (i, 0, 0))],
        out_shape=[jax.ShapeDtypeStruct((n_tok, D), F32), jax.ShapeDtypeStruct((n_tok, D), BF16),
                   jax.ShapeDtypeStruct((n_tok, LANES), F32),
                   jax.ShapeDtypeStruct((n_tok // tm, SUBLANES, LANES), F32)],
        compiler_params=_cparams(("arbitrary",)),
        name="mixout",
    )(og, sz, om, sg, sm, xf, mod3, gn, wbg, wbm, wo, gpost, gffn, wrh, wrl, br)


def _sorted_rows(n_tok):
    n_steps = n_tok // TM_PROJ
    worst_pad = n_steps * N_EXP * (SUBLANES - 1) + N_EXP * TM_FFN
    per_step = -(-worst_pad // (n_steps * ZERO_ROWS)) * ZERO_ROWS
    n_rows = n_tok * TOP_K + per_step * n_steps
    assert n_rows % TM_FFN == 0
    return n_rows


def _seg_copies(src_ref, dst_ref, src0, dst0, length, sem):
    for b in reversed(range(SUBLANES.bit_length() - 1, TM_PROJ.bit_length())):
        size = 1 << b
        off = (length >> (b + 1)) << (b + 1)

        @pl.when((length & size) != 0)
        def _():
            pltpu.make_async_copy(
                src_ref.at[pl.ds(pl.multiple_of(src0 + off, SUBLANES), size)],
                dst_ref.at[pl.ds(pl.multiple_of(dst0 + off, SUBLANES), size)], sem).start()


def _wait_rows(src_ref, dst_ref, total, sem):
    for b in reversed(range(SUBLANES.bit_length() - 1, STAGE_ROWS.bit_length())):
        size = 1 << b

        @pl.when((total & size) != 0)
        def _():
            pltpu.make_async_copy(src_ref.at[pl.ds(0, size)], dst_ref.at[pl.ds(0, size)],
                                  sem).wait()


def _zero_rows(zero_ref, dst_ref, start, end, sem, wait):
    length = end - start
    for b in reversed(range(SUBLANES.bit_length() - 1, ZERO_ROWS.bit_length())):
        size = 1 << b
        off = (length >> (b + 1)) << (b + 1)

        @pl.when((length & size) != 0)
        def _():
            cp = pltpu.make_async_copy(
                zero_ref.at[pl.ds(0, size)],
                dst_ref.at[pl.ds(pl.multiple_of(start + off, SUBLANES), size)], sem)
            if wait:
                cp.wait()
            else:
                cp.start()


def _slot_onehot(route_t, n_slots):
    tm = route_t.shape[1]
    slot = lax.broadcasted_iota(jnp.int32, (n_slots, tm), 0).astype(F32)
    acc = jnp.zeros((n_slots, tm), F32)
    for kk in range(TOP_K):
        acc = jnp.where(slot == route_t[ROUTE_P + kk:ROUTE_P + kk + 1, :], 1.0, acc)
    return acc.astype(BF16)


def _dispatch_kernel(meta_ref, pad_ref, route_ref, h_ref, xs_out, stage, zero_ref, sems, zsem):
    i = pl.program_id(0)
    n_steps = pl.num_programs(0)
    n_rows = xs_out.shape[0]
    n_seg = n_steps * N_EXP
    slot = i % 2

    @pl.when(i == 0)
    def _():
        zero_ref[...] = jnp.zeros_like(zero_ref)

    def clear_uncovered(wait):
        per = -(-N_EXP // n_steps)
        for j in range(per):
            e = i * per + j

            @pl.when(e < N_EXP)
            def _():
                ec = jnp.minimum(e, N_EXP - 1)
                _zero_rows(zero_ref, xs_out, pad_ref[ec], pad_ref[N_EXP + 1 + ec], zsem, wait)

        total_end = pad_ref[N_EXP]
        tail_rows = n_rows - n_steps * TM_PROJ * TOP_K
        per_step = tail_rows // n_steps
        for j in range(per_step // ZERO_ROWS):
            seg_end = (n_rows - tail_rows) + i * per_step + (j + 1) * ZERO_ROWS
            zlen = jnp.clip(seg_end - total_end, 0, ZERO_ROWS)
            _zero_rows(zero_ref, xs_out, seg_end - zlen, seg_end, zsem, wait)

    clear_uncovered(wait=False)

    onehot = _slot_onehot(route_ref[...].T, STAGE_ROWS)
    stage[slot] = _dot(onehot, h_ref[...])

    def issue(e, carry):
        s = i * N_EXP + e
        _seg_copies(stage.at[slot], xs_out, meta_ref[s], meta_ref[2 * n_seg + s],
                    meta_ref[n_seg + s], sems.at[slot])
        return carry

    lax.fori_loop(0, N_EXP, issue, 0)
    clear_uncovered(wait=True)

    @pl.when(i >= 1)
    def _():
        _wait_rows(stage.at[1 - slot], xs_out, meta_ref[3 * n_seg + i - 1], sems.at[1 - slot])

    @pl.when(i == n_steps - 1)
    def _():
        _wait_rows(stage.at[slot], xs_out, meta_ref[3 * n_seg + i], sems.at[slot])


def _dispatch_call(meta, pad_info, route, h2, n_rows):
    n_tok = h2.shape[0]
    tm = TM_PROJ
    n_steps = n_tok // tm
    assert (n_rows - n_tok * TOP_K) % (n_steps * ZERO_ROWS) == 0 and TM_FFN <= 2 * ZERO_ROWS
    grid_spec = pltpu.PrefetchScalarGridSpec(
        num_scalar_prefetch=2,
        grid=(n_steps,),
        in_specs=[pl.BlockSpec((tm, LANES), lambda i, m, p: (i, 0)),
                  pl.BlockSpec((tm, D), lambda i, m, p: (i, 0))],
        out_specs=pl.BlockSpec(memory_space=pl.ANY),
        scratch_shapes=[pltpu.VMEM((2, STAGE_ROWS, D), F32), pltpu.VMEM((ZERO_ROWS, D), F32),
                        pltpu.SemaphoreType.DMA((2,)), pltpu.SemaphoreType.DMA(())],
    )
    return pl.pallas_call(
        _dispatch_kernel,
        grid_spec=grid_spec,
        out_shape=jax.ShapeDtypeStruct((n_rows, D), F32),
        compiler_params=_cparams(("arbitrary",)),
        name="dispatch",
    )(meta, pad_info, route, h2)


def _ffn_kernel(te_ref, nx_ref, nu_ref, x_ref, wg_hbm, wu_hbm, wd_hbm, bg0, bu0, bd0, bg1, bu1,
                bd1, y_ref, land, wgb, wub, wdb, sems):
    i = pl.program_id(0)
    tm = TM_FFN
    t0 = 2 * i
    t1 = t0 + 1
    e0 = te_ref[t0]
    e1 = te_ref[t1]
    used0 = t0 < nu_ref[0]
    used1 = t1 < nu_ref[0]
    first0 = jnp.logical_or(i == 0, e0 != te_ref[jnp.maximum(t0 - 1, 0)])
    joint = jnp.logical_and(used1, e1 == e0)
    w_hbm = (wg_hbm, wu_hbm, wd_hbm)

    def fetch(expert):
        for j in range(3):
            pltpu.make_async_copy(w_hbm[j].at[expert], land.at[j], sems.at[j]).start()

    def switch(t):
        for j, dst in enumerate((wgb, wub, wdb)):
            pltpu.make_async_copy(w_hbm[j].at[0], land.at[j], sems.at[j]).wait()
            dst[...] = land[j].astype(BF16)
        nxt = nx_ref[t]

        @pl.when(nxt >= 0)
        def _():
            fetch(jnp.maximum(nxt, 0))

    def compute(rows, bg, bu, bd):
        x = x_ref[rows, :].astype(BF16)
        gate = jnp.minimum(_dot(x, wgb[...]) + bg[...], SWIGLU_LIMIT)
        up = jnp.clip(_dot(x, wub[...]) + bu[...], -SWIGLU_LIMIT, SWIGLU_LIMIT)
        act = ((up + 1.0) * gate * _sigmoid(SWIGLU_ALPHA * gate)).astype(BF16)
        y_ref[rows, :] = _dot(act, wdb[...]) + bd[...]

    lo = slice(0, tm)
    hi = slice(tm, 2 * tm)

    @pl.when(jnp.logical_and(used0, i == 0))
    def _():
        fetch(e0)

    @pl.when(jnp.logical_and(used0, first0))
    def _():
        switch(t0)

    @pl.when(joint)
    def _():
        compute(slice(0, 2 * tm), bg0, bu0, bd0)

    @pl.when(jnp.logical_and(used0, jnp.logical_not(joint)))
    def _():
        compute(lo, bg0, bu0, bd0)

    @pl.when(jnp.logical_and(used1, jnp.logical_not(joint)))
    def _():
        switch(t1)
        compute(hi, bg1, bu1, bd1)

    @pl.when(jnp.logical_not(used0))
    def _():
        y_ref[lo, :] = jnp.zeros((tm, D), y_ref.dtype)

    @pl.when(jnp.logical_not(used1))
    def _():
        y_ref[hi, :] = jnp.zeros((tm, D), y_ref.dtype)


def _ffn_call(tile_exp, next_exp, n_used, xs, w_gate, w_up, w_down, b_gate, b_up, b_down):
    n_rows = xs.shape[0]
    tm = TM_FFN
    n_tiles = n_rows // tm
    assert D == D_FF
    biases = (b_gate.reshape(N_EXP, 1, D_FF), b_up.reshape(N_EXP, 1, D_FF),
              b_down.reshape(N_EXP, 1, D))

    assert n_tiles % 2 == 0

    def xrow(i, te, nx, nu):
        return (jnp.maximum(jnp.minimum(i, (nu[0] - 1) // 2), 0), 0)

    def bias(k):
        return lambda i, te, nx, nu: (te[2 * i + k], 0, 0)

    grid_spec = pltpu.PrefetchScalarGridSpec(
        num_scalar_prefetch=3,
        grid=(n_tiles // 2,),
        in_specs=[pl.BlockSpec((2 * tm, D), xrow),
                  pl.BlockSpec(memory_space=pl.ANY),
                  pl.BlockSpec(memory_space=pl.ANY),
                  pl.BlockSpec(memory_space=pl.ANY),
                  pl.BlockSpec((None, 1, D_FF), bias(0)),
                  pl.BlockSpec((None, 1, D_FF), bias(0)),
                  pl.BlockSpec((None, 1, D), bias(0)),
                  pl.BlockSpec((None, 1, D_FF), bias(1)),
                  pl.BlockSpec((None, 1, D_FF), bias(1)),
                  pl.BlockSpec((None, 1, D), bias(1))],
        out_specs=pl.BlockSpec((2 * tm, D), lambda i, te, nx, nu: (i, 0)),
        scratch_shapes=[pltpu.VMEM((3, D, D_FF), F32),
                        pltpu.VMEM((D, D_FF), BF16), pltpu.VMEM((D, D_FF), BF16),
                        pltpu.VMEM((D_FF, D), BF16), pltpu.SemaphoreType.DMA((3,))],
    )
    return pl.pallas_call(
        _ffn_kernel,
        grid_spec=grid_spec,
        out_shape=jax.ShapeDtypeStruct((n_rows, D), F32),
        compiler_params=_cparams(("arbitrary",)),
        name="ffn",
    )(tile_exp, next_exp, n_used, xs, w_gate, w_up, w_down, *(2 * biases))


def _combine_kernel(meta_ref, y_hbm, route_ref, x1_ref, mod_ref, gpost_ref, o_ref, stage, sems):
    i = pl.program_id(0)
    n_steps = pl.num_programs(0)
    n_seg = n_steps * N_EXP
    tm = x1_ref.shape[0]
    slot = i % 2

    def fetch(tile, into):
        def issue(e, carry):
            s = tile * N_EXP + e
            _seg_copies(y_hbm, stage.at[into], meta_ref[2 * n_seg + s], meta_ref[s],
                        meta_ref[n_seg + s], sems.at[into])
            return carry

        lax.fori_loop(0, N_EXP, issue, 0)

    @pl.when(i == 0)
    def _():
        stage[...] = jnp.zeros_like(stage)
        fetch(0, 0)

    @pl.when(i + 1 < n_steps)
    def _():
        fetch(i + 1, 1 - slot)

    _wait_rows(y_hbm, stage.at[slot], meta_ref[3 * n_seg + i], sems.at[slot])

    route = route_ref[...]
    col = lax.broadcasted_iota(jnp.int32, (tm, STAGE_ROWS), 1).astype(F32)
    wmat = jnp.zeros((tm, STAGE_ROWS), F32)
    for kk in range(TOP_K):
        hit = col == route[:, ROUTE_P + kk:ROUTE_P + kk + 1]
        wmat = jnp.where(hit, route[:, ROUTE_W + kk:ROUTE_W + kk + 1], wmat)
    moe = _dot(wmat.astype(BF16), stage[slot].astype(BF16))
    gt_f = mod_ref[...][:, 5 * D:6 * D]
    o_ref[...] = x1_ref[...] + gt_f * _rms(moe, gpost_ref[...])


def _combine_call(meta, y, route, x1, mod3, tiles_per_seq, gpost):
    n_tok = x1.shape[0]
    tm = TM_PROJ
    grid_spec = pltpu.PrefetchScalarGridSpec(
        num_scalar_prefetch=1,
        grid=(n_tok // tm,),
        in_specs=[pl.BlockSpec(memory_space=pl.ANY),
                  pl.BlockSpec((tm, LANES), lambda i, m: (i, 0)),
                  pl.BlockSpec((tm, D), lambda i, m: (i, 0)),
                  pl.BlockSpec((None, 1, 6 * D), lambda i, m: (i // tiles_per_seq, 0, 0)),
                  pl.BlockSpec((1, D), lambda i, m: (0, 0))],
        out_specs=pl.BlockSpec((tm, D), lambda i, m: (i, 0)),
        scratch_shapes=[pltpu.VMEM((2, STAGE_ROWS, D), F32), pltpu.SemaphoreType.DMA((2,))],
    )
    return pl.pallas_call(
        _combine_kernel,
        grid_spec=grid_spec,
        out_shape=jax.ShapeDtypeStruct((n_tok, D), F32),
        compiler_params=_cparams(("arbitrary",)),
        name="combine",
    )(meta, y, route, x1, mod3, gpost)


def _rope_swap(w):
    q = MLA_ROPE // 4
    return jnp.concatenate([-w[..., q:2 * q], w[..., 0:q], -w[..., 3 * q:4 * q],
                            w[..., 2 * q:3 * q]], axis=-1)


def _rope_tables(seq):
    rows = seq // GRID_W
    r, col = np.meshgrid(np.arange(rows, dtype=np.float64), np.arange(GRID_W, dtype=np.float64),
                         indexing="ij")
    half = MLA_ROPE // 2
    inv_freq = ROPE_BASE ** (-np.arange(0, half, 2, dtype=np.float64) / half)
    ar = r.reshape(-1)[:, None] * inv_freq
    ac = col.reshape(-1)[:, None] * inv_freq
    cos = np.concatenate([np.cos(ar), np.cos(ar), np.cos(ac), np.cos(ac)], axis=-1)
    sin = np.concatenate([np.sin(ar), np.sin(ar), np.sin(ac), np.sin(ac)], axis=-1)
    return cos, sin


def _head_slots(w3, lead):
    rows, heads, width = w3.shape
    return jnp.pad(w3, ((0, 0), (0, 0), (lead, HEAD_SLOT - lead - width))).reshape(
        rows, heads * HEAD_SLOT)


def kernel(x, c, ctx, c_ctx, w_mod, b_mod, g_pre_mix, g_post_mix, g_pre_ffn, g_post_ffn, w_in,
           gla_w_a2_f, gla_b_a_f, gla_w_a2_b, gla_b_a_b, gla_g_norm, mla_g_q, mla_w_uq, mla_g_kv,
           mla_w_uk, mla_w_uv, w_br_gla, w_br_mla, w_out, router_w, router_b, w_gate, b_gate,
           w_up, b_up, w_down, b_down):
    depth = w_mod.shape[0]
    assert depth == 1, "single-layer block"
    batch, seq, d = x.shape
    ctx_len = ctx.shape[1]
    assert d == D and seq % TM_PROJ == 0 and (batch * ctx_len) % TM_PROJ == 0
    assert TM_PROJ % ctx_len == 0 or ctx_len % TM_PROJ == 0
    n_tok = batch * seq

    cc = jnp.zeros((16, D), F32).at[:batch].set(c).at[batch].set(c_ctx)
    mod = _mod_call(cc, w_mod[0], b_mod[0])
    mod3 = mod.reshape(16, 1, 6 * D)

    wi = w_in[0]
    kr = wi[:, O_KR:O_KR + MLA_ROPE]
    small = jnp.concatenate([wi[:, O_AF:O_AF + GLA_RANK], wi[:, O_AB:O_AB + GLA_RANK], kr,
                             _rope_swap(kr), jnp.zeros((D, LANES - X_KRS - MLA_ROPE), F32)],
                            axis=1)
    w1 = jnp.concatenate([wi[:, O_Q:O_G], wi[:, O_DKV:O_DKV + MLA_KVR], small], axis=1).astype(BF16)
    w2 = jnp.concatenate([wi[:, O_G:O_G + GLA_V], wi[:, O_DQ:O_DQ + MLA_QR], wi[:, O_MG:]],
                         axis=1).astype(BF16)
    wa = jnp.zeros((LANES, 2 * GLA_QK), F32)
    wa = wa.at[X_AF:X_AF + GLA_RANK, 0:GLA_QK].set(gla_w_a2_f[0])
    wa = wa.at[X_AB:X_AB + GLA_RANK, GLA_QK:].set(gla_w_a2_b[0]).astype(BF16)
    ba = jnp.concatenate([gla_b_a_f[0], gla_b_a_b[0]]).reshape(1, 2 * GLA_QK)

    uk = mla_w_uk[0]
    uv = mla_w_uv[0]
    uq = mla_w_uq[0]
    wk_top = _head_slots(uk.reshape(MLA_KVR, MLA_H, MLA_NOPE), 0)
    place = np.zeros((LANES, MLA_H, HEAD_SLOT), np.float32)
    for j in range(MLA_ROPE):
        place[X_KR + j, :, MLA_NOPE + j] = 1.0
        place[X_KRS + j, :, MLA_NOPE + j] = 1.0
    wk = jnp.concatenate([wk_top, jnp.asarray(place.reshape(LANES, MLA_W))],
                         axis=0).astype(BF16)
    uv4 = uv.reshape(MLA_KVR, MLA_H // 2, 2, MLA_V)
    zv = jnp.zeros_like(uv4[:, :, 0])
    wv = jnp.stack([jnp.concatenate([uv4[:, :, 0], zv], axis=-1),
                    jnp.concatenate([zv, uv4[:, :, 1]], axis=-1)],
                   axis=2).reshape(MLA_KVR, MLA_W).astype(BF16)
    uq3 = uq.reshape(MLA_QR, MLA_H, MLA_QKD)
    wq_a = _head_slots(uq3, 0)
    wq_b = _head_slots(_rope_swap(uq3[:, :, MLA_NOPE:]), MLA_NOPE)
    wq = jnp.concatenate([wq_a, wq_b], axis=1).astype(BF16)

    cos, sin = _rope_tables(seq)
    zeros32 = np.zeros((seq, MLA_ROPE))
    t1_x = jnp.asarray(np.concatenate([zeros32, cos, sin, zeros32], axis=1), F32)
    t1_c = jnp.asarray(np.broadcast_to(
        np.concatenate([np.zeros(MLA_ROPE), np.ones(MLA_ROPE), np.zeros(2 * MLA_ROPE)]),
        (TM_PROJ, LANES)), F32)
    q_scale = MLA_SCALE * LOG2_E
    cq = jnp.asarray(np.concatenate([np.ones((seq, MLA_NOPE)), cos, zeros32], axis=1) * q_scale, F32)
    sq = jnp.asarray(np.concatenate([np.zeros((seq, MLA_NOPE)), sin, zeros32], axis=1) * q_scale,
                     F32)

    gpre = g_pre_mix[0].reshape(1, D)
    gkv = mla_g_kv[0].reshape(1, MLA_KVR)
    gq = mla_g_q[0].reshape(1, MLA_QR)
    tiles_per_seq = seq // TM_PROJ

    xf = x.reshape(n_tok, D)
    cf = ctx.reshape(batch * ctx_len, D)
    (gqx, gkx, gvx, lfx, lbx, mkx, mvx, sz, mq, sg, sm) = _inproj_call(
        True, xf, mod3, lambda i: i // tiles_per_seq, tiles_per_seq, gpre, w1, wa, ba, wk, wv,
        gkv, t1_x, extra=(w2, gq, wq, cq, sq))
    (gqc, gkc, gvc, lfc, lbc, mkc, mvc) = _inproj_call(
        False, cf, mod3, lambda i: batch, 1, gpre, w1, wa, ba, wk, wv, gkv, t1_c)

    og = _gla_call((gqx, gkx, gvx, lfx, lbx), (gqc, gkc, gvc, lfc, lbc), batch, seq, ctx_len)
    om = _mla_call(mq, mkx, mvx, mkc, mvc, batch, seq, ctx_len)

    wr = jnp.pad(router_w[0], ((0, 0), (0, LANES - N_EXP)))
    wrh = wr.astype(BF16)
    wrl = (wr - wrh.astype(F32)).astype(BF16)
    br =jnp.pad(router_b[0], (0, LANES - N_EXP)).reshape(1, LANES)
    x1, h2, route, cnt = _mixout_call(
        og, sz, om, sg, sm, xf, mod3, tiles_per_seq, gla_g_norm[0].reshape(1, GLA_DV),
        w_br_gla[0].astype(BF16), w_br_mla[0].astype(BF16), w_out[0].astype(BF16),
        g_post_mix[0].reshape(1, D), g_pre_ffn[0].reshape(1, D), wrh, wrl, br)

    seg = -(-cnt[:, 0, :N_EXP].astype(jnp.int32) // SUBLANES) * SUBLANES
    stage_start = jnp.cumsum(seg, axis=1) - seg
    in_group = jnp.cumsum(seg, axis=0) - seg
    group = jnp.sum(seg, axis=0)
    padded = -(-group // TM_FFN) * TM_FFN
    ends = jnp.cumsum(padded)
    starts = ends - padded
    sorted_row = starts[None, :] + in_group
    meta = jnp.concatenate([stage_start.reshape(-1), seg.reshape(-1), sorted_row.reshape(-1),
                            jnp.sum(seg, axis=1)]).astype(jnp.int32)
    pad_info = jnp.concatenate([starts + group, ends[-1:], ends]).astype(jnp.int32)
    n_rows = _sorted_rows(n_tok)
    tile_start = jnp.arange(n_rows // TM_FFN, dtype=jnp.int32) * TM_FFN
    tile_exp = jnp.minimum(jnp.sum((ends[None, :] <= tile_start[:, None]).astype(jnp.int32), axis=1),
                           N_EXP - 1)
    n_used = (ends[-1:] // TM_FFN).astype(jnp.int32)
    ids = jnp.arange(N_EXP, dtype=jnp.int32)
    later = jnp.logical_and(padded[None, :] > 0, ids[None, :] > ids[:, None])
    next_owner = jnp.min(jnp.where(later, ids[None, :], N_EXP), axis=1)
    next_owner = jnp.where(next_owner == N_EXP, -1, next_owner)
    next_exp = jnp.sum(jnp.where(tile_exp[:, None] == ids[None, :], next_owner[None, :], 0),
                       axis=1).astype(jnp.int32)

    xs = _dispatch_call(meta, pad_info, route, h2, n_rows)
    y = _ffn_call(tile_exp, next_exp, n_used, xs, w_gate[0], w_up[0], w_down[0], b_gate[0], b_up[0],
                  b_down[0])
    out = _combine_call(meta, y, route, x1, mod3, tiles_per_seq, g_post_ffn[0].reshape(1, D))
    return out.reshape(batch, seq, D)
```

```python
import functools

import jax
import jax.numpy as jnp
import numpy as np
from jax import lax
from jax.experimental import pallas as pl
from jax.experimental.pallas import tpu as pltpu

F32 = jnp.float32
BF16 = jnp.bfloat16

D = 1024
EPS = 1e-6
GRID_W = 64

GLA_H = 4
GLA_DK = 64
GLA_DV = 128
GLA_RANK = 16
GLA_TAU = 16.0
GLA_CHUNK = 64
GLA_BLOCK = 256
GLA_QK = GLA_H * GLA_DK
GLA_V = GLA_H * GLA_DV

MLA_H = 8
MLA_QR = 256
MLA_KVR = 128
MLA_NOPE = 64
MLA_ROPE = 32
MLA_V = 64
MLA_QKD = MLA_NOPE + MLA_ROPE
MLA_SCALE = MLA_QKD ** -0.5
LOG2_E = 1.4426950408889634
ROPE_BASE = 10000.0
HEAD_SLOT = 128
MLA_W = MLA_H * HEAD_SLOT
SUM_LANE_EVEN, SUM_LANE_ODD = MLA_V, 0

N_EXP = 32
TOP_K = 4
D_FF = 1024
SWIGLU_LIMIT = 7.0
SWIGLU_ALPHA = 1.702

LANES = 128
SUBLANES = 8
TM_PROJ = 512
TQ = 1024
TM_FFN = 512
ZERO_ROWS = 256
LONG_SEGMENT = 128
STAGE_ROWS = -(-(TM_PROJ * TOP_K + N_EXP * (SUBLANES - 1)) // 256) * 256
ROUTE_E, ROUTE_W, ROUTE_P = 0, TOP_K, 2 * TOP_K

_OFF = np.cumsum([0, GLA_QK, GLA_QK, GLA_V, GLA_V, GLA_RANK, GLA_RANK,
                  MLA_QR, MLA_KVR, MLA_ROPE, D, D])
(O_Q, O_K, O_V, O_G, O_AF, O_AB, O_DQ, O_DKV, O_KR, O_MG, O_MM, _) = _OFF.tolist()

X_AF, X_AB, X_KR, X_KRS = 0, 16, 32, 64

VMEM_LIMIT = 56 * 1024 * 1024


def _cparams(sem):
    return pltpu.CompilerParams(dimension_semantics=sem, vmem_limit_bytes=VMEM_LIMIT)


def _resident(arr):
    nd = arr.ndim
    return pl.BlockSpec(arr.shape, lambda *_: (0,) * nd, pipeline_mode=pl.Buffered(1))


def _rms(x, g):
    return x * lax.rsqrt(jnp.mean(x * x, axis=-1, keepdims=True) + EPS) * g


def _sigmoid(x):
    return 1.0 / (1.0 + jnp.exp(-x))


def _log_sigmoid(x):
    return jnp.minimum(x, 0.0) - jnp.log1p(jnp.exp(-jnp.abs(x)))


def _dot(a, b):
    return jnp.dot(a, b, preferred_element_type=F32)


def _dot_nt(a, b):
    return lax.dot_general(a, b, (((1,), (1,)), ((), ())), preferred_element_type=F32)


def _dot_tn(a, b):
    return lax.dot_general(a, b, (((0,), (0,)), ((), ())), preferred_element_type=F32)


def _mod_kernel(c_ref, w_ref, b_ref, o_ref):
    c = c_ref[...]
    s = (c * _sigmoid(c)).astype(BF16)
    o_ref[...] = _dot(s, w_ref[...].astype(BF16)) + b_ref[...]


def _mod_call(cc, w_mod, b_mod):
    n = w_mod.shape[1]
    tn = 1536
    return pl.pallas_call(
        _mod_kernel,
        grid=(n // tn,),
        in_specs=[pl.BlockSpec((16, D), lambda j: (0, 0)),
                  pl.BlockSpec((D, tn), lambda j: (0, j)),
                  pl.BlockSpec((1, tn), lambda j: (0, j))],
        out_specs=pl.BlockSpec((16, tn), lambda j: (0, j)),
        out_shape=jax.ShapeDtypeStruct((16, n), F32),
        compiler_params=_cparams(("arbitrary",)),
        name="mod",
    )(cc, w_mod, b_mod.reshape(1, n))


def _inproj_kernel(with_q, x_ref, mod_ref, gpre_ref, w1_ref, wa_ref, ba_ref, wk_ref,
                   wv_ref, gkv_ref, t1_ref, *rest):
    if with_q:
        (w2_ref, gq_ref, wq_ref, cq_ref, sq_ref,
         q_o, k_o, v_o, lf_o, lb_o, mk_o, mv_o, sz_o, mq_o, sg_o, sm_o) = rest
    else:
        (q_o, k_o, v_o, lf_o, lb_o, mk_o, mv_o) = rest
    x = x_ref[...]
    mod = mod_ref[...]
    sh = mod[:, 0:D]
    sc = mod[:, D:2 * D]
    h = (_rms(x, gpre_ref[...]) * (1.0 + sc) + sh).astype(BF16)

    z1 = _dot(h, w1_ref[...])
    q_o[...] = (z1[:, 0:GLA_QK] * (GLA_DK ** -0.5)).astype(BF16)
    k_o[...] = z1[:, GLA_QK:2 * GLA_QK].astype(BF16)
    v_o[...] = z1[:, 2 * GLA_QK:2 * GLA_QK + GLA_V].astype(BF16)
    o_dkv = 2 * GLA_QK + GLA_V
    ckv = _rms(z1[:, o_dkv:o_dkv + MLA_KVR], gkv_ref[...])
    xs = z1[:, o_dkv + MLA_KVR:o_dkv + MLA_KVR + LANES]

    la = _log_sigmoid(_dot(xs.astype(BF16), wa_ref[...]) + ba_ref[...]) * (1.0 / GLA_TAU)
    lf_o[...] = la[:, 0:GLA_QK]
    lb_o[...] = la[:, GLA_QK:2 * GLA_QK]

    lhs_k = jnp.concatenate([ckv, xs * t1_ref[...]], axis=-1).astype(BF16)
    mk_o[...] = _dot(lhs_k, wk_ref[...]).astype(BF16)
    lane = lax.broadcasted_iota(jnp.int32, (x.shape[0], MLA_W), 1)
    mv = _dot(ckv.astype(BF16), wv_ref[...])
    mv_o[...] = jnp.where(lane % (2 * HEAD_SLOT) == SUM_LANE_EVEN, 1.0,
                          jnp.where(lane % (2 * HEAD_SLOT) == HEAD_SLOT + SUM_LANE_ODD, 1.0,
                                    mv)).astype(BF16)

    if with_q:
        zg = _dot(h, w2_ref[:, 0:GLA_V])
        sz_o[...] = (zg * _sigmoid(zg)).astype(BF16)
        n = _rms(_dot(h, w2_ref[:, GLA_V:GLA_V + MLA_QR]), gq_ref[...]).astype(BF16)
        cq = jnp.concatenate([cq_ref[...]] * MLA_H, axis=-1)
        sq = jnp.concatenate([sq_ref[...]] * MLA_H, axis=-1)
        mq_o[...] = (_dot(n, wq_ref[:, 0:MLA_W]) * cq
                     + _dot(n, wq_ref[:, MLA_W:2 * MLA_W]) * sq).astype(BF16)
        o_mg = GLA_V + MLA_QR
        sg_o[...] = _sigmoid(_dot(h, w2_ref[:, o_mg:o_mg + D])).astype(BF16)
        sm_o[...] = _sigmoid(_dot(h, w2_ref[:, o_mg + D:o_mg + 2 * D])).astype(BF16)


def _inproj_call(with_q, xf, mod3, mod_row_fn, tiles_per_seq, gpre, w1, wa, ba, wk, wv,
                 gkv, t1, extra=()):
    n_tok = xf.shape[0]
    tm = TM_PROJ
    grid = (n_tok // tm,)
    row = lambda i: (i, 0)
    tab = lambda i: (i % tiles_per_seq, 0)
    in_specs = [
        pl.BlockSpec((tm, D), row),
        pl.BlockSpec((None, 1, 6 * D), lambda i: (mod_row_fn(i), 0, 0)),
        _resident(gpre), _resident(w1), _resident(wa), _resident(ba), _resident(wk),
        _resident(wv), _resident(gkv),
        pl.BlockSpec((tm, LANES), tab),
    ]
    widths = [(GLA_QK, BF16), (GLA_QK, BF16), (GLA_V, BF16), (GLA_QK, F32), (GLA_QK, F32),
              (MLA_W, BF16), (MLA_W, BF16)]
    args = [xf, mod3, gpre, w1, wa, ba, wk, wv, gkv, t1]
    if with_q:
        w2, gq, wq, cq, sq = extra
        in_specs += [_resident(w2), _resident(gq), _resident(wq),
                     pl.BlockSpec((tm, HEAD_SLOT), tab), pl.BlockSpec((tm, HEAD_SLOT), tab)]
        args += [w2, gq, wq, cq, sq]
        widths += [(GLA_V, BF16), (MLA_W, BF16), (D, BF16), (D, BF16)]
    return pl.pallas_call(
        functools.partial(_inproj_kernel, with_q),
        grid=grid,
        in_specs=in_specs,
        out_specs=[pl.BlockSpec((tm, w), row) for w, _ in widths],
        out_shape=[jax.ShapeDtypeStruct((n_tok, w), dt) for w, dt in widths],
        compiler_params=_cparams(("arbitrary",)),
        name="inproj_x" if with_q else "inproj_ctx",
    )(*args)


def _gla_kernel(qx, kx, vx, lfx, lbx, qc, kc, vc, lfc, lbc, o_ref, sf_ref, sb_ref):
    C = GLA_CHUNK
    R = GLA_BLOCK
    n_sub = R // C
    rr = lax.broadcasted_iota(jnp.int32, (R, R), 0)
    cc = lax.broadcasted_iota(jnp.int32, (R, R), 1)
    same = (rr // C) == (cc // C)
    tbd_f = jnp.logical_and(same, cc <= rr).astype(BF16)
    tbd_b = jnp.logical_and(same, cc >= rr).astype(BF16)
    row = lax.broadcasted_iota(jnp.int32, (C, GLA_QK), 0)
    col = lax.broadcasted_iota(jnp.int32, (C, GLA_QK), 1)
    head_qk = col // GLA_DK
    tri4_f = (col % C) <= row
    tri4_b = (col % C) >= row
    head_v = lax.broadcasted_iota(jnp.int32, (C, GLA_V), 1) // GLA_DV

    def stack_masked(x, head_of_lane):
        return jnp.concatenate(
            [jnp.where(head_of_lane == hh, x, jnp.zeros_like(x)) for hh in range(GLA_H)], axis=0)

    def block(q_ref, k_ref, v_ref, la_ref, row0, fwd, s_ref, emit):
        sl = pl.ds(row0, R)
        la = la_ref[sl, :]
        la_hi = la.astype(BF16)
        la_lo = (la - la_hi.astype(F32)).astype(BF16)
        tbd = tbd_f if fwd else tbd_b
        cum = _dot(tbd, la_hi) + _dot(tbd, la_lo)
        k = k_ref[sl, :].astype(F32)
        v = v_ref[sl, :]
        if emit:
            q = q_ref[sl, :].astype(F32)
        st = s_ref[...]
        outs = [None] * n_sub
        for ci in (range(n_sub) if fwd else reversed(range(n_sub))):
            rs = slice(ci * C, (ci + 1) * C)
            cum_c = cum[rs]
            tot = cum_c[C - 1:C, :] if fwd else cum_c[0:1, :]
            k_c = k[rs]
            v_c = v[rs]
            ke_bd = stack_masked((k_c * jnp.exp(tot - cum_c)).astype(BF16), head_qk)
            v_stack = jnp.concatenate(
                [v_c[:, hh * GLA_DV:(hh + 1) * GLA_DV] for hh in range(GLA_H)], axis=0)
            if emit:
                qd = (q[rs] * jnp.exp(cum_c)).astype(BF16)
                ki_bd = stack_masked((k_c * jnp.exp(-cum_c)).astype(BF16), head_qk)
                att = _dot_nt(qd, ki_bd)
                att = jnp.where(tri4_f if fwd else tri4_b, att, 0.0).astype(BF16)
                o_intra = _dot(att, stack_masked(v_c, head_v))
                oi = _dot_nt(stack_masked(qd, head_qk), st.astype(BF16))
                o_inter = jnp.concatenate([oi[hh * C:(hh + 1) * C] for hh in range(GLA_H)],
                                          axis=-1)
                outs[ci] = o_intra + o_inter
            st = st * jnp.exp(tot) + _dot_tn(v_stack, ke_bd)
        s_ref[...] = st
        if emit:
            o_ref[sl, :] += jnp.concatenate(outs, axis=0)

    sf_ref[...] = jnp.zeros_like(sf_ref)
    sb_ref[...] = jnp.zeros_like(sb_ref)
    o_ref[...] = jnp.zeros_like(o_ref)
    n_ctx = qc.shape[0] // R
    n_x = qx.shape[0] // R
    for i in range(n_ctx):
        block(qc, kc, vc, lfc, i * R, True, sf_ref, False)
        block(qc, kc, vc, lbc, (n_ctx - 1 - i) * R, False, sb_ref, False)

    def body(i, carry):
        block(qx, kx, vx, lfx, pl.multiple_of(i * R, R), True, sf_ref, True)
        block(qx, kx, vx, lbx, pl.multiple_of((n_x - 1 - i) * R, R), False, sb_ref, True)
        return carry

    lax.fori_loop(0, n_x, body, 0)


def _gla_call(fx, fc, batch, seq, ctx_len):
    qx, kx, vx, lfx, lbx = fx
    qc, kc, vc, lfc, lbc = fc

    def spec(rows, w):
        return pl.BlockSpec((rows, w), lambda b: (b, 0))

    return pl.pallas_call(
        _gla_kernel,
        grid=(batch,),
        in_specs=[spec(seq, GLA_QK), spec(seq, GLA_QK), spec(seq, GLA_V), spec(seq, GLA_QK),
                  spec(seq, GLA_QK),
                  spec(ctx_len, GLA_QK), spec(ctx_len, GLA_QK), spec(ctx_len, GLA_V),
                  spec(ctx_len, GLA_QK), spec(ctx_len, GLA_QK)],
        out_specs=spec(seq, GLA_V),
        out_shape=jax.ShapeDtypeStruct((batch * seq, GLA_V), F32),
        scratch_shapes=[pltpu.VMEM((GLA_DV, GLA_QK), F32), pltpu.VMEM((GLA_DV, GLA_QK), F32)],
        compiler_params=_cparams(("arbitrary",)),
        name="gla",
    )(qx, kx, vx, lfx, lbx, qc, kc, vc, lfc, lbc)


def _mla_kernel(q_ref, kx_ref, vx_ref, kc_ref, vc_ref, o_ref):
    lane = lax.broadcasted_iota(jnp.int32, (q_ref.shape[0], HEAD_SLOT), 1)
    for j in range(MLA_H // 2):
        pair = []
        for hh, sum_lane in ((2 * j, SUM_LANE_EVEN), (2 * j + 1, SUM_LANE_ODD)):
            sl = slice(hh * HEAD_SLOT, (hh + 1) * HEAD_SLOT)
            q = q_ref[:, sl]
            sx = _dot_nt(q, kx_ref[:, sl])
            sc = _dot_nt(q, kc_ref[:, sl])
            m = jnp.maximum(jnp.max(sx, axis=-1, keepdims=True),
                            jnp.max(sc, axis=-1, keepdims=True))
            px = jnp.exp2(sx - m).astype(BF16)
            pc = jnp.exp2(sc - m).astype(BF16)
            o = _dot(px, vx_ref[:, sl]) + _dot(pc, vc_ref[:, sl])
            pair.append(o / o[:, sum_lane:sum_lane + 1])
        o_ref[:, j * HEAD_SLOT:(j + 1) * HEAD_SLOT] = jnp.where(
            lane < MLA_V, pair[0], pair[1]).astype(BF16)


def _mla_call(mq, mkx, mvx, mkc, mvc, batch, seq, ctx_len):
    nq = seq // TQ
    return pl.pallas_call(
        _mla_kernel,
        grid=(batch, nq),
        in_specs=[pl.BlockSpec((TQ, MLA_W), lambda b, i: (b * nq + i, 0)),
                  pl.BlockSpec((seq, MLA_W), lambda b, i: (b, 0)),
                  pl.BlockSpec((seq, MLA_W), lambda b, i: (b, 0)),
                  pl.BlockSpec((ctx_len, MLA_W), lambda b, i: (b, 0)),
                  pl.BlockSpec((ctx_len, MLA_W), lambda b, i: (b, 0))],
        out_specs=pl.BlockSpec((TQ, MLA_H * MLA_V), lambda b, i: (b * nq + i, 0)),
        out_shape=jax.ShapeDtypeStruct((batch * seq, MLA_H * MLA_V), BF16),
        compiler_params=_cparams(("arbitrary", "arbitrary")),
        name="mla",
    )(mq, mkx, mvx, mkc, mvc)


def _mixout_kernel(og_ref, sz_ref, om_ref, sg_ref, sm_ref, x_ref, mod_ref, gn_ref, wbg_ref,
                   wbm_ref, wo_ref, gpost_ref, gffn_ref, wrh_ref, wrl_ref, br_ref,
                   x1_o, h2_o, route_o, cnt_o):
    tm = x_ref.shape[0]
    mod = mod_ref[...]
    gt_a = mod[:, 2 * D:3 * D]
    sh_f = mod[:, 3 * D:4 * D]
    sc_f = mod[:, 4 * D:5 * D]

    og = og_ref[...]
    gn = gn_ref[...]
    parts = [_rms(og[:, hh * GLA_DV:(hh + 1) * GLA_DV], gn) for hh in range(GLA_H)]
    a = (jnp.concatenate(parts, axis=-1) * sz_ref[...].astype(F32)).astype(BF16)
    br_g = _dot(a, wbg_ref[...])
    br_m = _dot(om_ref[...], wbm_ref[...])
    merged = (sg_ref[...].astype(F32) * br_g + sm_ref[...].astype(F32) * br_m).astype(BF16)
    mo = _dot(merged, wo_ref[...])
    x1 = x_ref[...] + gt_a * _rms(mo, gpost_ref[...])
    x1_o[...] = x1
    h2 = _rms(x1, gffn_ref[...]) * (1.0 + sc_f) + sh_f
    h2_o[...] = h2.astype(BF16)

    lane = lax.broadcasted_iota(jnp.int32, (tm, LANES), 1)
    h_hi = h2.astype(BF16)
    h_lo = (h2 - h_hi.astype(F32)).astype(BF16)
    logits = (_dot(h_hi, wrh_ref[...]) + _dot(h_lo, wrh_ref[...]) + _dot(h_hi, wrl_ref[...])
              + _dot(h_lo, wrl_ref[...]) + br_ref[...])
    neg = jnp.float32(-jnp.inf)
    lg = jnp.where(lane < N_EXP, logits, neg)
    lane_f = lane.astype(F32)
    hots, vals = [], []
    for _k in range(TOP_K):
        mx = jnp.max(lg, axis=-1, keepdims=True)
        idx = jnp.min(jnp.where(lg == mx, lane_f, float(LANES)), axis=-1, keepdims=True)
        hot = lane_f == idx
        lg = jnp.where(hot, neg, lg)
        hots.append(hot)
        vals.append(mx)
    es = [jnp.exp(v - vals[0]) for v in vals]
    den = es[0] + es[1] + es[2] + es[3]
    ws = [e / den for e in es]

    msum = jnp.zeros((tm, LANES), F32)
    for hot in hots:
        msum = msum + hot.astype(F32)
    rr = lax.broadcasted_iota(jnp.int32, (tm, tm), 0)
    cc = lax.broadcasted_iota(jnp.int32, (tm, tm), 1)
    lower = (cc < rr).astype(BF16)
    prior = _dot(lower, msum.astype(BF16))
    cnt = jnp.sum(msum, axis=0, keepdims=True)
    seg = jnp.floor((cnt + (SUBLANES - 1.0)) * (1.0 / SUBLANES)) * SUBLANES
    er = lax.broadcasted_iota(jnp.int32, (LANES, LANES), 0)
    ec = lax.broadcasted_iota(jnp.int32, (LANES, LANES), 1)
    before = (er < ec).astype(BF16)
    seg_start = _dot(jnp.broadcast_to(seg, (SUBLANES, LANES)).astype(BF16), before)[0:1, :]
    slot_of = prior + seg_start

    route = jnp.zeros((tm, LANES), F32)
    for kk in range(TOP_K):
        hotf = hots[kk].astype(F32)
        e_col = jnp.sum(hotf * lane_f, axis=-1, keepdims=True)
        p_col = jnp.sum(hotf * slot_of, axis=-1, keepdims=True)
        route = jnp.where(lane == ROUTE_E + kk, e_col, route)
        route = jnp.where(lane == ROUTE_W + kk, ws[kk], route)
        route = jnp.where(lane == ROUTE_P + kk, p_col, route)
    route_o[...] = route
    cnt_o[...] = jnp.broadcast_to(cnt, cnt_o.shape)


def _mixout_call(og, sz, om, sg, sm, xf, mod3, tiles_per_seq, gn, wbg, wbm, wo, gpost, gffn,
                 wrh, wrl, br):
    n_tok = xf.shape[0]
    tm = TM_PROJ
    row = lambda i: (i, 0)

    def rs(w):
        return pl.BlockSpec((tm, w), row)

    return pl.pallas_call(
        _mixout_kernel,
        grid=(n_tok // tm,),
        in_specs=[rs(GLA_V), rs(GLA_V), rs(MLA_H * MLA_V), rs(D), rs(D), rs(D),
                  pl.BlockSpec((None, 1, 6 * D), lambda i: (i // tiles_per_seq, 0, 0)),
                  _resident(gn), _resident(wbg), _resident(wbm), _resident(wo),
                  _resident(gpost), _resident(gffn), _resident(wrh), _resident(wrl),
                  _resident(br)],
        out_specs=[rs(D), rs(D), rs(LANES),
                   pl.BlockSpec((None, SUBLANES, LANES), lambda i: (i, 0, 0))],
        out_shape=[jax.ShapeDtypeStruct((n_tok, D), F32), jax.ShapeDtypeStruct((n_tok, D), BF16),
                   jax.ShapeDtypeStruct((n_tok, LANES), F32),
                   jax.ShapeDtypeStruct((n_tok // tm, SUBLANES, LANES), F32)],
        compiler_params=_cparams(("arbitrary",)),
        name="mixout",
    )(og, sz, om, sg, sm, xf, mod3, gn, wbg, wbm, wo, gpost, gffn, wrh, wrl, br)


def _sorted_rows(n_tok):
    n_steps = n_tok // TM_PROJ
    worst_pad = n_steps * N_EXP * (SUBLANES - 1) + N_EXP * TM_FFN
    per_step = -(-worst_pad // (n_steps * ZERO_ROWS)) * ZERO_ROWS
    n_rows = n_tok * TOP_K + per_step * n_steps
    assert n_rows % TM_FFN == 0
    return n_rows


def _seg_copies(src_ref, dst_ref, src0, dst0, length, sem):
    def copy_bit(b):
        size = 1 << b
        off = (length >> (b + 1)) << (b + 1)

        @pl.when((length & size) != 0)
        def _():
            pltpu.make_async_copy(
                src_ref.at[pl.ds(pl.multiple_of(src0 + off, SUBLANES), size)],
                dst_ref.at[pl.ds(pl.multiple_of(dst0 + off, SUBLANES), size)], sem).start()

    lo_bit = SUBLANES.bit_length() - 1
    hi_bit = TM_PROJ.bit_length() - 1
    split = LONG_SEGMENT.bit_length() - 1

    @pl.when(length >= LONG_SEGMENT)
    def _():
        for b in range(hi_bit, split - 1, -1):
            copy_bit(b)

    for b in range(split - 1, lo_bit - 1, -1):
        copy_bit(b)


def _wait_rows(src_ref, dst_ref, total, sem):
    for b in reversed(range(SUBLANES.bit_length() - 1, STAGE_ROWS.bit_length())):
        size = 1 << b

        @pl.when((total & size) != 0)
        def _():
            pltpu.make_async_copy(src_ref.at[pl.ds(0, size)], dst_ref.at[pl.ds(0, size)],
                                  sem).wait()


def _zero_rows(zero_ref, dst_ref, start, end, sem, wait):
    length = end - start
    for b in reversed(range(SUBLANES.bit_length() - 1, ZERO_ROWS.bit_length())):
        size = 1 << b
        off = (length >> (b + 1)) << (b + 1)

        @pl.when((length & size) != 0)
        def _():
            cp = pltpu.make_async_copy(
                zero_ref.at[pl.ds(0, size)],
                dst_ref.at[pl.ds(pl.multiple_of(start + off, SUBLANES), size)], sem)
            if wait:
                cp.wait()
            else:
                cp.start()


def _slot_onehot(route_t, n_slots):
    tm = route_t.shape[1]
    slot = lax.broadcasted_iota(jnp.int32, (n_slots, tm), 0).astype(F32)
    acc = jnp.zeros((n_slots, tm), F32)
    for kk in range(TOP_K):
        acc = jnp.where(slot == route_t[ROUTE_P + kk:ROUTE_P + kk + 1, :], 1.0, acc)
    return acc.astype(BF16)


def _dispatch_kernel(meta_ref, pad_ref, route_ref, h_ref, xs_out, stage, zero_ref, sems, zsem):
    i = pl.program_id(0)
    n_steps = pl.num_programs(0)
    n_rows = xs_out.shape[0]
    n_seg = n_steps * N_EXP
    slot = i % 2

    @pl.when(i == 0)
    def _():
        zero_ref[...] = jnp.zeros_like(zero_ref)

    def clear_uncovered(wait):
        per = -(-N_EXP // n_steps)
        for j in range(per):
            e = i * per + j

            @pl.when(e < N_EXP)
            def _():
                ec = jnp.minimum(e, N_EXP - 1)
                _zero_rows(zero_ref, xs_out, pad_ref[ec], pad_ref[N_EXP + 1 + ec], zsem, wait)

        total_end = pad_ref[N_EXP]
        tail_rows = n_rows - n_steps * TM_PROJ * TOP_K
        per_step = tail_rows // n_steps
        for j in range(per_step // ZERO_ROWS):
            seg_end = (n_rows - tail_rows) + i * per_step + (j + 1) * ZERO_ROWS
            zlen = jnp.clip(seg_end - total_end, 0, ZERO_ROWS)
            _zero_rows(zero_ref, xs_out, seg_end - zlen, seg_end, zsem, wait)

    clear_uncovered(wait=False)

    onehot = _slot_onehot(route_ref[...].T, STAGE_ROWS)
    stage[slot] = _dot(onehot, h_ref[...])

    def issue(e, carry):
        s = i * N_EXP + e
        _seg_copies(stage.at[slot], xs_out, meta_ref[s], meta_ref[2 * n_seg + s],
                    meta_ref[n_seg + s], sems.at[slot])
        return carry

    lax.fori_loop(0, N_EXP, issue, 0)
    clear_uncovered(wait=True)

    @pl.when(i >= 1)
    def _():
        _wait_rows(stage.at[1 - slot], xs_out, meta_ref[3 * n_seg + i - 1], sems.at[1 - slot])

    @pl.when(i == n_steps - 1)
    def _():
        _wait_rows(stage.at[slot], xs_out, meta_ref[3 * n_seg + i], sems.at[slot])


def _dispatch_call(meta, pad_info, route, h2, n_rows):
    n_tok = h2.shape[0]
    tm = TM_PROJ
    n_steps = n_tok // tm
    assert (n_rows - n_tok * TOP_K) % (n_steps * ZERO_ROWS) == 0 and TM_FFN <= 2 * ZERO_ROWS
    grid_spec = pltpu.PrefetchScalarGridSpec(
        num_scalar_prefetch=2,
        grid=(n_steps,),
        in_specs=[pl.BlockSpec((tm, LANES), lambda i, m, p: (i, 0)),
                  pl.BlockSpec((tm, D), lambda i, m, p: (i, 0))],
        out_specs=pl.BlockSpec(memory_space=pl.ANY),
        scratch_shapes=[pltpu.VMEM((2, STAGE_ROWS, D), F32), pltpu.VMEM((ZERO_ROWS, D), F32),
                        pltpu.SemaphoreType.DMA((2,)), pltpu.SemaphoreType.DMA(())],
    )
    return pl.pallas_call(
        _dispatch_kernel,
        grid_spec=grid_spec,
        out_shape=jax.ShapeDtypeStruct((n_rows, D), F32),
        compiler_params=_cparams(("arbitrary",)),
        name="dispatch",
    )(meta, pad_info, route, h2)


def _ffn_kernel(te_ref, nx_ref, nu_ref, x_ref, wg_hbm, wu_hbm, wd_hbm, bg0, bu0, bd0, bg1, bu1,
                bd1, y_ref, land, wgb, wub, wdb, sems):
    i = pl.program_id(0)
    tm = TM_FFN
    t0 = 2 * i
    t1 = t0 + 1
    e0 = te_ref[t0]
    e1 = te_ref[t1]
    used0 = t0 < nu_ref[0]
    used1 = t1 < nu_ref[0]
    first0 = jnp.logical_or(i == 0, e0 != te_ref[jnp.maximum(t0 - 1, 0)])
    joint = jnp.logical_and(used1, e1 == e0)
    w_hbm = (wg_hbm, wu_hbm, wd_hbm)

    def fetch(expert):
        for j in range(3):
            pltpu.make_async_copy(w_hbm[j].at[expert], land.at[j], sems.at[j]).start()

    def switch(t):
        for j, dst in enumerate((wgb, wub, wdb)):
            pltpu.make_async_copy(w_hbm[j].at[0], land.at[j], sems.at[j]).wait()
            dst[...] = land[j].astype(BF16)
        nxt = nx_ref[t]

        @pl.when(nxt >= 0)
        def _():
            fetch(jnp.maximum(nxt, 0))

    def compute(rows, bg, bu, bd):
        x = x_ref[rows, :].astype(BF16)
        gate = jnp.minimum(_dot(x, wgb[...]) + bg[...], SWIGLU_LIMIT)
        up = jnp.clip(_dot(x, wub[...]) + bu[...], -SWIGLU_LIMIT, SWIGLU_LIMIT)
        act = ((up + 1.0) * gate * _sigmoid(SWIGLU_ALPHA * gate)).astype(BF16)
        y_ref[rows, :] = _dot(act, wdb[...]) + bd[...]

    lo = slice(0, tm)
    hi = slice(tm, 2 * tm)

    @pl.when(jnp.logical_and(used0, i == 0))
    def _():
        fetch(e0)

    @pl.when(jnp.logical_and(used0, first0))
    def _():
        switch(t0)

    @pl.when(joint)
    def _():
        compute(slice(0, 2 * tm), bg0, bu0, bd0)

    @pl.when(jnp.logical_and(used0, jnp.logical_not(joint)))
    def _():
        compute(lo, bg0, bu0, bd0)

    @pl.when(jnp.logical_and(used1, jnp.logical_not(joint)))
    def _():
        switch(t1)
        compute(hi, bg1, bu1, bd1)

    @pl.when(jnp.logical_not(used0))
    def _():
        y_ref[lo, :] = jnp.zeros((tm, D), y_ref.dtype)

    @pl.when(jnp.logical_not(used1))
    def _():
        y_ref[hi, :] = jnp.zeros((tm, D), y_ref.dtype)


def _ffn_call(tile_exp, next_exp, n_used, xs, w_gate, w_up, w_down, b_gate, b_up, b_down):
    n_rows = xs.shape[0]
    tm = TM_FFN
    n_tiles = n_rows // tm
    assert D == D_FF
    biases = (b_gate.reshape(N_EXP, 1, D_FF), b_up.reshape(N_EXP, 1, D_FF),
              b_down.reshape(N_EXP, 1, D))

    assert n_tiles % 2 == 0

    def xrow(i, te, nx, nu):
        return (jnp.maximum(jnp.minimum(i, (nu[0] - 1) // 2), 0), 0)

    def bias(k):
        return lambda i, te, nx, nu: (te[2 * i + k], 0, 0)

    grid_spec = pltpu.PrefetchScalarGridSpec(
        num_scalar_prefetch=3,
        grid=(n_tiles // 2,),
        in_specs=[pl.BlockSpec((2 * tm, D), xrow),
                  pl.BlockSpec(memory_space=pl.ANY),
                  pl.BlockSpec(memory_space=pl.ANY),
                  pl.BlockSpec(memory_space=pl.ANY),
                  pl.BlockSpec((None, 1, D_FF), bias(0)),
                  pl.BlockSpec((None, 1, D_FF), bias(0)),
                  pl.BlockSpec((None, 1, D), bias(0)),
                  pl.BlockSpec((None, 1, D_FF), bias(1)),
                  pl.BlockSpec((None, 1, D_FF), bias(1)),
                  pl.BlockSpec((None, 1, D), bias(1))],
        out_specs=pl.BlockSpec((2 * tm, D), lambda i, te, nx, nu: (i, 0)),
        scratch_shapes=[pltpu.VMEM((3, D, D_FF), F32),
                        pltpu.VMEM((D, D_FF), BF16), pltpu.VMEM((D, D_FF), BF16),
                        pltpu.VMEM((D_FF, D), BF16), pltpu.SemaphoreType.DMA((3,))],
    )
    return pl.pallas_call(
        _ffn_kernel,
        grid_spec=grid_spec,
        out_shape=jax.ShapeDtypeStruct((n_rows, D), F32),
        compiler_params=_cparams(("arbitrary",)),
        name="ffn",
    )(tile_exp, next_exp, n_used, xs, w_gate, w_up, w_down, *(2 * biases))


def _combine_kernel(meta_ref, y_hbm, route_ref, x1_ref, mod_ref, gpost_ref, o_ref, stage, sems):
    i = pl.program_id(0)
    n_steps = pl.num_programs(0)
    n_seg = n_steps * N_EXP
    tm = x1_ref.shape[0]
    slot = i % 2

    def fetch(tile, into):
        def issue(e, carry):
            s = tile * N_EXP + e
            _seg_copies(y_hbm, stage.at[into], meta_ref[2 * n_seg + s], meta_ref[s],
                        meta_ref[n_seg + s], sems.at[into])
            return carry

        lax.fori_loop(0, N_EXP, issue, 0)

    @pl.when(i == 0)
    def _():
        stage[...] = jnp.zeros_like(stage)
        fetch(0, 0)

    @pl.when(i + 1 < n_steps)
    def _():
        fetch(i + 1, 1 - slot)

    _wait_rows(y_hbm, stage.at[slot], meta_ref[3 * n_seg + i], sems.at[slot])

    route = route_ref[...]
    col = lax.broadcasted_iota(jnp.int32, (tm, STAGE_ROWS), 1).astype(F32)
    wmat = jnp.zeros((tm, STAGE_ROWS), F32)
    for kk in range(TOP_K):
        hit = col == route[:, ROUTE_P + kk:ROUTE_P + kk + 1]
        wmat = jnp.where(hit, route[:, ROUTE_W + kk:ROUTE_W + kk + 1], wmat)
    moe = _dot(wmat.astype(BF16), stage[slot].astype(BF16))
    gt_f = mod_ref[...][:, 5 * D:6 * D]
    o_ref[...] = x1_ref[...] + gt_f * _rms(moe, gpost_ref[...])


def _combine_call(meta, y, route, x1, mod3, tiles_per_seq, gpost):
    n_tok = x1.shape[0]
    tm = TM_PROJ
    grid_spec = pltpu.PrefetchScalarGridSpec(
        num_scalar_prefetch=1,
        grid=(n_tok // tm,),
        in_specs=[pl.BlockSpec(memory_space=pl.ANY),
                  pl.BlockSpec((tm, LANES), lambda i, m: (i, 0)),
                  pl.BlockSpec((tm, D), lambda i, m: (i, 0)),
                  pl.BlockSpec((None, 1, 6 * D), lambda i, m: (i // tiles_per_seq, 0, 0)),
                  pl.BlockSpec((1, D), lambda i, m: (0, 0))],
        out_specs=pl.BlockSpec((tm, D), lambda i, m: (i, 0)),
        scratch_shapes=[pltpu.VMEM((2, STAGE_ROWS, D), F32), pltpu.SemaphoreType.DMA((2,))],
    )
    return pl.pallas_call(
        _combine_kernel,
        grid_spec=grid_spec,
        out_shape=jax.ShapeDtypeStruct((n_tok, D), F32),
        compiler_params=_cparams(("arbitrary",)),
        name="combine",
    )(meta, y, route, x1, mod3, gpost)


def _rope_swap(w):
    q = MLA_ROPE // 4
    return jnp.concatenate([-w[..., q:2 * q], w[..., 0:q], -w[..., 3 * q:4 * q],
                            w[..., 2 * q:3 * q]], axis=-1)


def _rope_tables(seq):
    rows = seq // GRID_W
    r, col = np.meshgrid(np.arange(rows, dtype=np.float64), np.arange(GRID_W, dtype=np.float64),
                         indexing="ij")
    half = MLA_ROPE // 2
    inv_freq = ROPE_BASE ** (-np.arange(0, half, 2, dtype=np.float64) / half)
    ar = r.reshape(-1)[:, None] * inv_freq
    ac = col.reshape(-1)[:, None] * inv_freq
    cos = np.concatenate([np.cos(ar), np.cos(ar), np.cos(ac), np.cos(ac)], axis=-1)
    sin = np.concatenate([np.sin(ar), np.sin(ar), np.sin(ac), np.sin(ac)], axis=-1)
    return cos, sin


def _head_slots(w3, lead):
    rows, heads, width = w3.shape
    return jnp.pad(w3, ((0, 0), (0, 0), (lead, HEAD_SLOT - lead - width))).reshape(
        rows, heads * HEAD_SLOT)


def kernel(x, c, ctx, c_ctx, w_mod, b_mod, g_pre_mix, g_post_mix, g_pre_ffn, g_post_ffn, w_in,
           gla_w_a2_f, gla_b_a_f, gla_w_a2_b, gla_b_a_b, gla_g_norm, mla_g_q, mla_w_uq, mla_g_kv,
           mla_w_uk, mla_w_uv, w_br_gla, w_br_mla, w_out, router_w, router_b, w_gate, b_gate,
           w_up, b_up, w_down, b_down):
    depth = w_mod.shape[0]
    assert depth == 1, "single-layer block"
    batch, seq, d = x.shape
    ctx_len = ctx.shape[1]
    assert d == D and seq % TM_PROJ == 0 and (batch * ctx_len) % TM_PROJ == 0
    assert TM_PROJ % ctx_len == 0 or ctx_len % TM_PROJ == 0
    n_tok = batch * seq

    cc = jnp.zeros((16, D), F32).at[:batch].set(c).at[batch].set(c_ctx)
    mod = _mod_call(cc, w_mod[0], b_mod[0])
    mod3 = mod.reshape(16, 1, 6 * D)

    wi = w_in[0]
    kr = wi[:, O_KR:O_KR + MLA_ROPE]
    small = jnp.concatenate([wi[:, O_AF:O_AF + GLA_RANK], wi[:, O_AB:O_AB + GLA_RANK], kr,
                             _rope_swap(kr), jnp.zeros((D, LANES - X_KRS - MLA_ROPE), F32)],
                            axis=1)
    w1 = jnp.concatenate([wi[:, O_Q:O_G], wi[:, O_DKV:O_DKV + MLA_KVR], small], axis=1).astype(BF16)
    w2 = jnp.concatenate([wi[:, O_G:O_G + GLA_V], wi[:, O_DQ:O_DQ + MLA_QR], wi[:, O_MG:]],
                         axis=1).astype(BF16)
    wa = jnp.zeros((LANES, 2 * GLA_QK), F32)
    wa = wa.at[X_AF:X_AF + GLA_RANK, 0:GLA_QK].set(gla_w_a2_f[0])
    wa = wa.at[X_AB:X_AB + GLA_RANK, GLA_QK:].set(gla_w_a2_b[0]).astype(BF16)
    ba = jnp.concatenate([gla_b_a_f[0], gla_b_a_b[0]]).reshape(1, 2 * GLA_QK)

    uk = mla_w_uk[0]
    uv = mla_w_uv[0]
    uq = mla_w_uq[0]
    wk_top = _head_slots(uk.reshape(MLA_KVR, MLA_H, MLA_NOPE), 0)
    place = np.zeros((LANES, MLA_H, HEAD_SLOT), np.float32)
    for j in range(MLA_ROPE):
        place[X_KR + j, :, MLA_NOPE + j] = 1.0
        place[X_KRS + j, :, MLA_NOPE + j] = 1.0
    wk = jnp.concatenate([wk_top, jnp.asarray(place.reshape(LANES, MLA_W))],
                         axis=0).astype(BF16)
    uv4 = uv.reshape(MLA_KVR, MLA_H // 2, 2, MLA_V)
    zv = jnp.zeros_like(uv4[:, :, 0])
    wv = jnp.stack([jnp.concatenate([uv4[:, :, 0], zv], axis=-1),
                    jnp.concatenate([zv, uv4[:, :, 1]], axis=-1)],
                   axis=2).reshape(MLA_KVR, MLA_W).astype(BF16)
    uq3 = uq.reshape(MLA_QR, MLA_H, MLA_QKD)
    wq_a = _head_slots(uq3, 0)
    wq_b = _head_slots(_rope_swap(uq3[:, :, MLA_NOPE:]), MLA_NOPE)
    wq = jnp.concatenate([wq_a, wq_b], axis=1).astype(BF16)

    cos, sin = _rope_tables(seq)
    zeros32 = np.zeros((seq, MLA_ROPE))
    t1_x = jnp.asarray(np.concatenate([zeros32, cos, sin, zeros32], axis=1), F32)
    t1_c = jnp.asarray(np.broadcast_to(
        np.concatenate([np.zeros(MLA_ROPE), np.ones(MLA_ROPE), np.zeros(2 * MLA_ROPE)]),
        (TM_PROJ, LANES)), F32)
    q_scale = MLA_SCALE * LOG2_E
    cq = jnp.asarray(np.concatenate([np.ones((seq, MLA_NOPE)), cos, zeros32], axis=1) * q_scale, F32)
    sq = jnp.asarray(np.concatenate([np.zeros((seq, MLA_NOPE)), sin, zeros32], axis=1) * q_scale,
                     F32)

    gpre = g_pre_mix[0].reshape(1, D)
    gkv = mla_g_kv[0].reshape(1, MLA_KVR)
    gq = mla_g_q[0].reshape(1, MLA_QR)
    tiles_per_seq = seq // TM_PROJ

    xf = x.reshape(n_tok, D)
    cf = ctx.reshape(batch * ctx_len, D)
    (gqx, gkx, gvx, lfx, lbx, mkx, mvx, sz, mq, sg, sm) = _inproj_call(
        True, xf, mod3, lambda i: i // tiles_per_seq, tiles_per_seq, gpre, w1, wa, ba, wk, wv,
        gkv, t1_x, extra=(w2, gq, wq, cq, sq))
    (gqc, gkc, gvc, lfc, lbc, mkc, mvc) = _inproj_call(
        False, cf, mod3, lambda i: batch, 1, gpre, w1, wa, ba, wk, wv, gkv, t1_c)

    og = _gla_call((gqx, gkx, gvx, lfx, lbx), (gqc, gkc, gvc, lfc, lbc), batch, seq, ctx_len)
    om = _mla_call(mq, mkx, mvx, mkc, mvc, batch, seq, ctx_len)

    wr = jnp.pad(router_w[0], ((0, 0), (0, LANES - N_EXP)))
    wrh = wr.astype(BF16)
    wrl = (wr - wrh.astype(F32)).astype(BF16)
    br =jnp.pad(router_b[0], (0, LANES - N_EXP)).reshape(1, LANES)
    x1, h2, route, cnt = _mixout_call(
        og, sz, om, sg, sm, xf, mod3, tiles_per_seq, gla_g_norm[0].reshape(1, GLA_DV),
        w_br_gla[0].astype(BF16), w_br_mla[0].astype(BF16), w_out[0].astype(BF16),
        g_post_mix[0].reshape(1, D), g_pre_ffn[0].reshape(1, D), wrh, wrl, br)

    seg = -(-cnt[:, 0, :N_EXP].astype(jnp.int32) // SUBLANES) * SUBLANES
    stage_start = jnp.cumsum(seg, axis=1) - seg
    in_group = jnp.cumsum(seg, axis=0) - seg
    group = jnp.sum(seg, axis=0)
    padded = -(-group // TM_FFN) * TM_FFN
    ends = jnp.cumsum(padded)
    starts = ends - padded
    sorted_row = starts[None, :] + in_group
    meta = jnp.concatenate([stage_start.reshape(-1), seg.reshape(-1), sorted_row.reshape(-1),
                            jnp.sum(seg, axis=1)]).astype(jnp.int32)
    pad_info = jnp.concatenate([starts + group, ends[-1:], ends]).astype(jnp.int32)
    n_rows = _sorted_rows(n_tok)
    tile_start = jnp.arange(n_rows // TM_FFN, dtype=jnp.int32) * TM_FFN
    tile_exp = jnp.minimum(jnp.sum((ends[None, :] <= tile_start[:, None]).astype(jnp.int32), axis=1),
                           N_EXP - 1)
    n_used = (ends[-1:] // TM_FFN).astype(jnp.int32)
    ids = jnp.arange(N_EXP, dtype=jnp.int32)
    later = jnp.logical_and(padded[None, :] > 0, ids[None, :] > ids[:, None])
    next_owner = jnp.min(jnp.where(later, ids[None, :], N_EXP), axis=1)
    next_owner = jnp.where(next_owner == N_EXP, -1, next_owner)
    next_exp = jnp.sum(jnp.where(tile_exp[:, None] == ids[None, :], next_owner[None, :], 0),
                       axis=1).astype(jnp.int32)

    xs = _dispatch_call(meta, pad_info, route, h2, n_rows)
    y = _ffn_call(tile_exp, next_exp, n_used, xs, w_gate[0], w_up[0], w_down[0], b_gate[0], b_up[0],
                  b_down[0])
    out = _combine_call(meta, y, route, x1, mod3, tiles_per_seq, g_post_ffn[0].reshape(1, D))
    return out.reshape(batch, seq, D)
```

```python
import functools

import jax
import jax.numpy as jnp
import numpy as np
from jax import lax
from jax.experimental import pallas as pl
from jax.experimental.pallas import tpu as pltpu

F32 = jnp.float32
BF16 = jnp.bfloat16

D = 1024
EPS = 1e-6
GRID_W = 64

GLA_H = 4
GLA_DK = 64
GLA_DV = 128
GLA_RANK = 16
GLA_TAU = 16.0
GLA_CHUNK = 64
GLA_BLOCK = 256
GLA_QK = GLA_H * GLA_DK
GLA_V = GLA_H * GLA_DV

MLA_H = 8
MLA_QR = 256
MLA_KVR = 128
MLA_NOPE = 64
MLA_ROPE = 32
MLA_V = 64
MLA_QKD = MLA_NOPE + MLA_ROPE
MLA_SCALE = MLA_QKD ** -0.5
LOG2_E = 1.4426950408889634
ROPE_BASE = 10000.0
HEAD_SLOT = 128
MLA_W = MLA_H * HEAD_SLOT
SUM_LANE_EVEN, SUM_LANE_ODD = MLA_V, 0

N_EXP = 32
TOP_K = 4
D_FF = 1024
SWIGLU_LIMIT = 7.0
SWIGLU_ALPHA = 1.702

LANES = 128
SUBLANES = 8
TM_PROJ = 512
TQ = 1024
TM_FFN = 512
ZERO_ROWS = 256
LONG_SEGMENT = 128
SEG_ALIGN = 16
STAGE_ROWS = -(-(TM_PROJ * TOP_K + N_EXP * (SEG_ALIGN - 1)) // 256) * 256
ROUTE_E, ROUTE_W, ROUTE_P = 0, TOP_K, 2 * TOP_K

_OFF = np.cumsum([0, GLA_QK, GLA_QK, GLA_V, GLA_V, GLA_RANK, GLA_RANK,
                  MLA_QR, MLA_KVR, MLA_ROPE, D, D])
(O_Q, O_K, O_V, O_G, O_AF, O_AB, O_DQ, O_DKV, O_KR, O_MG, O_MM, _) = _OFF.tolist()

X_AF, X_AB, X_KR, X_KRS = 0, 16, 32, 64

VMEM_LIMIT = 56 * 1024 * 1024


def _cparams(sem):
    return pltpu.CompilerParams(dimension_semantics=sem, vmem_limit_bytes=VMEM_LIMIT)


def _resident(arr):
    nd = arr.ndim
    return pl.BlockSpec(arr.shape, lambda *_: (0,) * nd, pipeline_mode=pl.Buffered(1))


def _rms(x, g):
    return x * lax.rsqrt(jnp.mean(x * x, axis=-1, keepdims=True) + EPS) * g


def _sigmoid(x):
    return 1.0 / (1.0 + jnp.exp(-x))


def _log_sigmoid(x):
    return jnp.minimum(x, 0.0) - jnp.log1p(jnp.exp(-jnp.abs(x)))


def _dot(a, b):
    return jnp.dot(a, b, preferred_element_type=F32)


def _dot_nt(a, b):
    return lax.dot_general(a, b, (((1,), (1,)), ((), ())), preferred_element_type=F32)


def _dot_tn(a, b):
    return lax.dot_general(a, b, (((0,), (0,)), ((), ())), preferred_element_type=F32)


def _mod_kernel(c_ref, w_ref, b_ref, o_ref):
    c = c_ref[...]
    s = (c * _sigmoid(c)).astype(BF16)
    o_ref[...] = _dot(s, w_ref[...].astype(BF16)) + b_ref[...]


def _mod_call(cc, w_mod, b_mod):
    n = w_mod.shape[1]
    tn = 1536
    return pl.pallas_call(
        _mod_kernel,
        grid=(n // tn,),
        in_specs=[pl.BlockSpec((16, D), lambda j: (0, 0)),
                  pl.BlockSpec((D, tn), lambda j: (0, j)),
                  pl.BlockSpec((1, tn), lambda j: (0, j))],
        out_specs=pl.BlockSpec((16, tn), lambda j: (0, j)),
        out_shape=jax.ShapeDtypeStruct((16, n), F32),
        compiler_params=_cparams(("arbitrary",)),
        name="mod",
    )(cc, w_mod, b_mod.reshape(1, n))


def _inproj_kernel(with_q, x_ref, mod_ref, gpre_ref, w1_ref, wa_ref, ba_ref, wk_ref,
                   wv_ref, gkv_ref, t1_ref, *rest):
    if with_q:
        (w2_ref, gq_ref, wq_ref, cq_ref, sq_ref,
         q_o, k_o, v_o, lf_o, lb_o, mk_o, mv_o, sz_o, mq_o, sg_o, sm_o) = rest
    else:
        (q_o, k_o, v_o, lf_o, lb_o, mk_o, mv_o) = rest
    x = x_ref[...]
    mod = mod_ref[...]
    sh = mod[:, 0:D]
    sc = mod[:, D:2 * D]
    h = (_rms(x, gpre_ref[...]) * (1.0 + sc) + sh).astype(BF16)

    z1 = _dot(h, w1_ref[...])
    q_o[...] = (z1[:, 0:GLA_QK] * (GLA_DK ** -0.5)).astype(BF16)
    k_o[...] = z1[:, GLA_QK:2 * GLA_QK].astype(BF16)
    v_o[...] = z1[:, 2 * GLA_QK:2 * GLA_QK + GLA_V].astype(BF16)
    o_dkv = 2 * GLA_QK + GLA_V
    ckv = _rms(z1[:, o_dkv:o_dkv + MLA_KVR], gkv_ref[...])
    xs = z1[:, o_dkv + MLA_KVR:o_dkv + MLA_KVR + LANES]

    la = _log_sigmoid(_dot(xs.astype(BF16), wa_ref[...]) + ba_ref[...]) * (1.0 / GLA_TAU)
    lf_o[...] = la[:, 0:GLA_QK]
    lb_o[...] = la[:, GLA_QK:2 * GLA_QK]

    lhs_k = jnp.concatenate([ckv, xs * t1_ref[...]], axis=-1).astype(BF16)
    mk_o[...] = _dot(lhs_k, wk_ref[...]).astype(BF16)
    lane = lax.broadcasted_iota(jnp.int32, (x.shape[0], MLA_W), 1)
    mv = _dot(ckv.astype(BF16), wv_ref[...])
    mv_o[...] = jnp.where(lane % (2 * HEAD_SLOT) == SUM_LANE_EVEN, 1.0,
                          jnp.where(lane % (2 * HEAD_SLOT) == HEAD_SLOT + SUM_LANE_ODD, 1.0,
                                    mv)).astype(BF16)

    if with_q:
        zg = _dot(h, w2_ref[:, 0:GLA_V])
        sz_o[...] = (zg * _sigmoid(zg)).astype(BF16)
        n = _rms(_dot(h, w2_ref[:, GLA_V:GLA_V + MLA_QR]), gq_ref[...]).astype(BF16)
        cq = jnp.concatenate([cq_ref[...]] * MLA_H, axis=-1)
        sq = jnp.concatenate([sq_ref[...]] * MLA_H, axis=-1)
        mq_o[...] = (_dot(n, wq_ref[:, 0:MLA_W]) * cq
                     + _dot(n, wq_ref[:, MLA_W:2 * MLA_W]) * sq).astype(BF16)
        o_mg = GLA_V + MLA_QR
        sg_o[...] = _sigmoid(_dot(h, w2_ref[:, o_mg:o_mg + D])).astype(BF16)
        sm_o[...] = _sigmoid(_dot(h, w2_ref[:, o_mg + D:o_mg + 2 * D])).astype(BF16)


def _inproj_call(with_q, xf, mod3, mod_row_fn, tiles_per_seq, gpre, w1, wa, ba, wk, wv,
                 gkv, t1, extra=()):
    n_tok = xf.shape[0]
    tm = TM_PROJ
    grid = (n_tok // tm,)
    row = lambda i: (i, 0)
    tab = lambda i: (i % tiles_per_seq, 0)
    in_specs = [
        pl.BlockSpec((tm, D), row),
        pl.BlockSpec((None, 1, 6 * D), lambda i: (mod_row_fn(i), 0, 0)),
        _resident(gpre), _resident(w1), _resident(wa), _resident(ba), _resident(wk),
        _resident(wv), _resident(gkv),
        pl.BlockSpec((tm, LANES), tab),
    ]
    widths = [(GLA_QK, BF16), (GLA_QK, BF16), (GLA_V, BF16), (GLA_QK, F32), (GLA_QK, F32),
              (MLA_W, BF16), (MLA_W, BF16)]
    args = [xf, mod3, gpre, w1, wa, ba, wk, wv, gkv, t1]
    if with_q:
        w2, gq, wq, cq, sq = extra
        in_specs += [_resident(w2), _resident(gq), _resident(wq),
                     pl.BlockSpec((tm, HEAD_SLOT), tab), pl.BlockSpec((tm, HEAD_SLOT), tab)]
        args += [w2, gq, wq, cq, sq]
        widths += [(GLA_V, BF16), (MLA_W, BF16), (D, BF16), (D, BF16)]
    return pl.pallas_call(
        functools.partial(_inproj_kernel, with_q),
        grid=grid,
        in_specs=in_specs,
        out_specs=[pl.BlockSpec((tm, w), row) for w, _ in widths],
        out_shape=[jax.ShapeDtypeStruct((n_tok, w), dt) for w, dt in widths],
        compiler_params=_cparams(("arbitrary",)),
        name="inproj_x" if with_q else "inproj_ctx",
    )(*args)


def _gla_kernel(qx, kx, vx, lfx, lbx, qc, kc, vc, lfc, lbc, o_ref, sf_ref, sb_ref):
    C = GLA_CHUNK
    R = GLA_BLOCK
    n_sub = R // C
    rr = lax.broadcasted_iota(jnp.int32, (R, R), 0)
    cc = lax.broadcasted_iota(jnp.int32, (R, R), 1)
    same = (rr // C) == (cc // C)
    tbd_f = jnp.logical_and(same, cc <= rr).astype(BF16)
    tbd_b = jnp.logical_and(same, cc >= rr).astype(BF16)
    row = lax.broadcasted_iota(jnp.int32, (C, GLA_QK), 0)
    col = lax.broadcasted_iota(jnp.int32, (C, GLA_QK), 1)
    head_qk = col // GLA_DK
    tri4_f = (col % C) <= row
    tri4_b = (col % C) >= row
    head_v = lax.broadcasted_iota(jnp.int32, (C, GLA_V), 1) // GLA_DV

    def stack_masked(x, head_of_lane):
        return jnp.concatenate(
            [jnp.where(head_of_lane == hh, x, jnp.zeros_like(x)) for hh in range(GLA_H)], axis=0)

    def block(q_ref, k_ref, v_ref, la_ref, row0, fwd, s_ref, emit):
        sl = pl.ds(row0, R)
        la = la_ref[sl, :]
        la_hi = la.astype(BF16)
        la_lo = (la - la_hi.astype(F32)).astype(BF16)
        tbd = tbd_f if fwd else tbd_b
        cum = _dot(tbd, la_hi) + _dot(tbd, la_lo)
        k = k_ref[sl, :].astype(F32)
        v = v_ref[sl, :]
        if emit:
            q = q_ref[sl, :].astype(F32)
        st = s_ref[...]
        outs = [None] * n_sub
        for ci in (range(n_sub) if fwd else reversed(range(n_sub))):
            rs = slice(ci * C, (ci + 1) * C)
            cum_c = cum[rs]
            tot = cum_c[C - 1:C, :] if fwd else cum_c[0:1, :]
            k_c = k[rs]
            v_c = v[rs]
            ke_bd = stack_masked((k_c * jnp.exp(tot - cum_c)).astype(BF16), head_qk)
            v_stack = jnp.concatenate(
                [v_c[:, hh * GLA_DV:(hh + 1) * GLA_DV] for hh in range(GLA_H)], axis=0)
            if emit:
                qd = (q[rs] * jnp.exp(cum_c)).astype(BF16)
                ki_bd = stack_masked((k_c * jnp.exp(-cum_c)).astype(BF16), head_qk)
                att = _dot_nt(qd, ki_bd)
                att = jnp.where(tri4_f if fwd else tri4_b, att, 0.0).astype(BF16)
                o_intra = _dot(att, stack_masked(v_c, head_v))
                oi = _dot_nt(stack_masked(qd, head_qk), st.astype(BF16))
                o_inter = jnp.concatenate([oi[hh * C:(hh + 1) * C] for hh in range(GLA_H)],
                                          axis=-1)
                outs[ci] = o_intra + o_inter
            st = st * jnp.exp(tot) + _dot_tn(v_stack, ke_bd)
        s_ref[...] = st
        if emit:
            o_ref[sl, :] += jnp.concatenate(outs, axis=0)

    sf_ref[...] = jnp.zeros_like(sf_ref)
    sb_ref[...] = jnp.zeros_like(sb_ref)
    o_ref[...] = jnp.zeros_like(o_ref)
    n_ctx = qc.shape[0] // R
    n_x = qx.shape[0] // R
    for i in range(n_ctx):
        block(qc, kc, vc, lfc, i * R, True, sf_ref, False)
        block(qc, kc, vc, lbc, (n_ctx - 1 - i) * R, False, sb_ref, False)

    def body(i, carry):
        block(qx, kx, vx, lfx, pl.multiple_of(i * R, R), True, sf_ref, True)
        block(qx, kx, vx, lbx, pl.multiple_of((n_x - 1 - i) * R, R), False, sb_ref, True)
        return carry

    lax.fori_loop(0, n_x, body, 0)


def _gla_call(fx, fc, batch, seq, ctx_len):
    qx, kx, vx, lfx, lbx = fx
    qc, kc, vc, lfc, lbc = fc

    def spec(rows, w):
        return pl.BlockSpec((rows, w), lambda b: (b, 0))

    return pl.pallas_call(
        _gla_kernel,
        grid=(batch,),
        in_specs=[spec(seq, GLA_QK), spec(seq, GLA_QK), spec(seq, GLA_V), spec(seq, GLA_QK),
                  spec(seq, GLA_QK),
                  spec(ctx_len, GLA_QK), spec(ctx_len, GLA_QK), spec(ctx_len, GLA_V),
                  spec(ctx_len, GLA_QK), spec(ctx_len, GLA_QK)],
        out_specs=spec(seq, GLA_V),
        out_shape=jax.ShapeDtypeStruct((batch * seq, GLA_V), F32),
        scratch_shapes=[pltpu.VMEM((GLA_DV, GLA_QK), F32), pltpu.VMEM((GLA_DV, GLA_QK), F32)],
        compiler_params=_cparams(("arbitrary",)),
        name="gla",
    )(qx, kx, vx, lfx, lbx, qc, kc, vc, lfc, lbc)


def _mla_kernel(q_ref, kx_ref, vx_ref, kc_ref, vc_ref, o_ref):
    lane = lax.broadcasted_iota(jnp.int32, (q_ref.shape[0], HEAD_SLOT), 1)
    for j in range(MLA_H // 2):
        pair = []
        for hh, sum_lane in ((2 * j, SUM_LANE_EVEN), (2 * j + 1, SUM_LANE_ODD)):
            sl = slice(hh * HEAD_SLOT, (hh + 1) * HEAD_SLOT)
            q = q_ref[:, sl]
            sx = _dot_nt(q, kx_ref[:, sl])
            sc = _dot_nt(q, kc_ref[:, sl])
            m = jnp.maximum(jnp.max(sx, axis=-1, keepdims=True),
                            jnp.max(sc, axis=-1, keepdims=True))
            px = jnp.exp2(sx - m).astype(BF16)
            pc = jnp.exp2(sc - m).astype(BF16)
            o = _dot(px, vx_ref[:, sl]) + _dot(pc, vc_ref[:, sl])
            pair.append(o / o[:, sum_lane:sum_lane + 1])
        o_ref[:, j * HEAD_SLOT:(j + 1) * HEAD_SLOT] = jnp.where(
            lane < MLA_V, pair[0], pair[1]).astype(BF16)


def _mla_call(mq, mkx, mvx, mkc, mvc, batch, seq, ctx_len):
    nq = seq // TQ
    return pl.pallas_call(
        _mla_kernel,
        grid=(batch, nq),
        in_specs=[pl.BlockSpec((TQ, MLA_W), lambda b, i: (b * nq + i, 0)),
                  pl.BlockSpec((seq, MLA_W), lambda b, i: (b, 0)),
                  pl.BlockSpec((seq, MLA_W), lambda b, i: (b, 0)),
                  pl.BlockSpec((ctx_len, MLA_W), lambda b, i: (b, 0)),
                  pl.BlockSpec((ctx_len, MLA_W), lambda b, i: (b, 0))],
        out_specs=pl.BlockSpec((TQ, MLA_H * MLA_V), lambda b, i: (b * nq + i, 0)),
        out_shape=jax.ShapeDtypeStruct((batch * seq, MLA_H * MLA_V), BF16),
        compiler_params=_cparams(("arbitrary", "arbitrary")),
        name="mla",
    )(mq, mkx, mvx, mkc, mvc)


def _mixout_kernel(og_ref, sz_ref, om_ref, sg_ref, sm_ref, x_ref, mod_ref, gn_ref, wbg_ref,
                   wbm_ref, wo_ref, gpost_ref, gffn_ref, wrh_ref, wrl_ref, br_ref,
                   x1_o, h2_o, route_o, cnt_o):
    tm = x_ref.shape[0]
    mod = mod_ref[...]
    gt_a = mod[:, 2 * D:3 * D]
    sh_f = mod[:, 3 * D:4 * D]
    sc_f = mod[:, 4 * D:5 * D]

    og = og_ref[...]
    gn = gn_ref[...]
    parts = [_rms(og[:, hh * GLA_DV:(hh + 1) * GLA_DV], gn) for hh in range(GLA_H)]
    a = (jnp.concatenate(parts, axis=-1) * sz_ref[...].astype(F32)).astype(BF16)
    br_g = _dot(a, wbg_ref[...])
    br_m = _dot(om_ref[...], wbm_ref[...])
    merged = (sg_ref[...].astype(F32) * br_g + sm_ref[...].astype(F32) * br_m).astype(BF16)
    mo = _dot(merged, wo_ref[...])
    x1 = x_ref[...] + gt_a * _rms(mo, gpost_ref[...])
    x1_o[...] = x1
    h2 = _rms(x1, gffn_ref[...]) * (1.0 + sc_f) + sh_f
    h2_o[...] = h2.astype(BF16)

    lane = lax.broadcasted_iota(jnp.int32, (tm, LANES), 1)
    h_hi = h2.astype(BF16)
    h_lo = (h2 - h_hi.astype(F32)).astype(BF16)
    logits = (_dot(h_hi, wrh_ref[...]) + _dot(h_lo, wrh_ref[...]) + _dot(h_hi, wrl_ref[...])
              + _dot(h_lo, wrl_ref[...]) + br_ref[...])
    neg = jnp.float32(-jnp.inf)
    lg = jnp.where(lane < N_EXP, logits, neg)
    lane_f = lane.astype(F32)
    hots, vals = [], []
    for _k in range(TOP_K):
        mx = jnp.max(lg, axis=-1, keepdims=True)
        idx = jnp.min(jnp.where(lg == mx, lane_f, float(LANES)), axis=-1, keepdims=True)
        hot = lane_f == idx
        lg = jnp.where(hot, neg, lg)
        hots.append(hot)
        vals.append(mx)
    es = [jnp.exp(v - vals[0]) for v in vals]
    den = es[0] + es[1] + es[2] + es[3]
    ws = [e / den for e in es]

    msum = jnp.zeros((tm, LANES), F32)
    for hot in hots:
        msum = msum + hot.astype(F32)
    rr = lax.broadcasted_iota(jnp.int32, (tm, tm), 0)
    cc = lax.broadcasted_iota(jnp.int32, (tm, tm), 1)
    lower = (cc < rr).astype(BF16)
    prior = _dot(lower, msum.astype(BF16))
    cnt = jnp.sum(msum, axis=0, keepdims=True)
    seg = jnp.floor((cnt + (SEG_ALIGN - 1.0)) * (1.0 / SEG_ALIGN)) * SEG_ALIGN
    er = lax.broadcasted_iota(jnp.int32, (LANES, LANES), 0)
    ec = lax.broadcasted_iota(jnp.int32, (LANES, LANES), 1)
    before = (er < ec).astype(BF16)
    seg_start = _dot(jnp.broadcast_to(seg, (SUBLANES, LANES)).astype(BF16), before)[0:1, :]
    slot_of = prior + seg_start

    route = jnp.zeros((tm, LANES), F32)
    for kk in range(TOP_K):
        hotf = hots[kk].astype(F32)
        e_col = jnp.sum(hotf * lane_f, axis=-1, keepdims=True)
        p_col = jnp.sum(hotf * slot_of, axis=-1, keepdims=True)
        route = jnp.where(lane == ROUTE_E + kk, e_col, route)
        route = jnp.where(lane == ROUTE_W + kk, ws[kk], route)
        route = jnp.where(lane == ROUTE_P + kk, p_col, route)
    route_o[...] = route
    cnt_o[...] = jnp.broadcast_to(cnt, cnt_o.shape)


def _mixout_call(og, sz, om, sg, sm, xf, mod3, tiles_per_seq, gn, wbg, wbm, wo, gpost, gffn,
                 wrh, wrl, br):
    n_tok = xf.shape[0]
    tm = TM_PROJ
    row = lambda i: (i, 0)

    def rs(w):
        return pl.BlockSpec((tm, w), row)

    return pl.pallas_call(
        _mixout_kernel,
        grid=(n_tok // tm,),
        in_specs=[rs(GLA_V), rs(GLA_V), rs(MLA_H * MLA_V), rs(D), rs(D), rs(D),
                  pl.BlockSpec((None, 1, 6 * D), lambda i: (i // tiles_per_seq, 0, 0)),
                  _resident(gn), _resident(wbg), _resident(wbm), _resident(wo),
                  _resident(gpost), _resident(gffn), _resident(wrh), _resident(wrl),
                  _resident(br)],
        out_specs=[rs(D), rs(D), rs(LANES),
                   pl.BlockSpec((None, SUBLANES, LANES), lambda i: (i, 0, 0))],
        out_shape=[jax.ShapeDtypeStruct((n_tok, D), F32), jax.ShapeDtypeStruct((n_tok, D), BF16),
                   jax.ShapeDtypeStruct((n_tok, LANES), F32),
                   jax.ShapeDtypeStruct((n_tok // tm, SUBLANES, LANES), F32)],
        compiler_params=_cparams(("arbitrary",)),
        name="mixout",
    )(og, sz, om, sg, sm, xf, mod3, gn, wbg, wbm, wo, gpost, gffn, wrh, wrl, br)


def _sorted_rows(n_tok):
    n_steps = n_tok // TM_PROJ
    worst_pad = n_steps * N_EXP * (SEG_ALIGN - 1) + N_EXP * TM_FFN
    per_step = -(-worst_pad // (n_steps * ZERO_ROWS)) * ZERO_ROWS
    n_rows = n_tok * TOP_K + per_step * n_steps
    assert n_rows % TM_FFN == 0
    return n_rows


def _seg_copies(src_ref, dst_ref, src0, dst0, length, sem):
    def copy_bit(b):
        size = 1 << b
        off = (length >> (b + 1)) << (b + 1)

        @pl.when((length & size) != 0)
        def _():
            pltpu.make_async_copy(
                src_ref.at[pl.ds(pl.multiple_of(src0 + off, SEG_ALIGN), size)],
                dst_ref.at[pl.ds(pl.multiple_of(dst0 + off, SEG_ALIGN), size)], sem).start()

    lo_bit = SEG_ALIGN.bit_length() - 1
    hi_bit = TM_PROJ.bit_length() - 1
    split = LONG_SEGMENT.bit_length() - 1

    @pl.when(length >= LONG_SEGMENT)
    def _():
        for b in range(hi_bit, split - 1, -1):
            copy_bit(b)

    for b in range(split - 1, lo_bit - 1, -1):
        copy_bit(b)


def _wait_rows(src_ref, dst_ref, total, sem):
    for b in reversed(range(SEG_ALIGN.bit_length() - 1, STAGE_ROWS.bit_length())):
        size = 1 << b

        @pl.when((total & size) != 0)
        def _():
            pltpu.make_async_copy(src_ref.at[pl.ds(0, size)], dst_ref.at[pl.ds(0, size)],
                                  sem).wait()


def _zero_rows(zero_ref, dst_ref, start, end, sem, wait):
    length = end - start
    for b in reversed(range(SEG_ALIGN.bit_length() - 1, ZERO_ROWS.bit_length())):
        size = 1 << b
        off = (length >> (b + 1)) << (b + 1)

        @pl.when((length & size) != 0)
        def _():
            cp = pltpu.make_async_copy(
                zero_ref.at[pl.ds(0, size)],
                dst_ref.at[pl.ds(pl.multiple_of(start + off, SEG_ALIGN), size)], sem)
            if wait:
                cp.wait()
            else:
                cp.start()


def _slot_onehot(route_t, n_slots):
    tm = route_t.shape[1]
    slot = lax.broadcasted_iota(jnp.int32, (n_slots, tm), 0).astype(F32)
    acc = jnp.zeros((n_slots, tm), F32)
    for kk in range(TOP_K):
        acc = jnp.where(slot == route_t[ROUTE_P + kk:ROUTE_P + kk + 1, :], 1.0, acc)
    return acc.astype(BF16)


def _dispatch_kernel(meta_ref, pad_ref, route_ref, h_ref, xs_out, stage, zero_ref, sems, zsem):
    i = pl.program_id(0)
    n_steps = pl.num_programs(0)
    n_rows = xs_out.shape[0]
    n_seg = n_steps * N_EXP
    slot = i % 2

    @pl.when(i == 0)
    def _():
        zero_ref[...] = jnp.zeros_like(zero_ref)

    def clear_uncovered(wait):
        per = -(-N_EXP // n_steps)
        for j in range(per):
            e = i * per + j

            @pl.when(e < N_EXP)
            def _():
                ec = jnp.minimum(e, N_EXP - 1)
                _zero_rows(zero_ref, xs_out, pad_ref[ec], pad_ref[N_EXP + 1 + ec], zsem, wait)

        total_end = pad_ref[N_EXP]
        tail_rows = n_rows - n_steps * TM_PROJ * TOP_K
        per_step = tail_rows // n_steps
        for j in range(per_step // ZERO_ROWS):
            seg_end = (n_rows - tail_rows) + i * per_step + (j + 1) * ZERO_ROWS
            zlen = jnp.clip(seg_end - total_end, 0, ZERO_ROWS)
            _zero_rows(zero_ref, xs_out, seg_end - zlen, seg_end, zsem, wait)

    clear_uncovered(wait=False)

    onehot = _slot_onehot(route_ref[...].T, STAGE_ROWS)
    stage[slot] = _dot(onehot, h_ref[...]).astype(BF16)

    def issue(e, carry):
        s = i * N_EXP + e
        _seg_copies(stage.at[slot], xs_out, meta_ref[s], meta_ref[2 * n_seg + s],
                    meta_ref[n_seg + s], sems.at[slot])
        return carry

    lax.fori_loop(0, N_EXP, issue, 0)
    clear_uncovered(wait=True)

    @pl.when(i >= 1)
    def _():
        _wait_rows(stage.at[1 - slot], xs_out, meta_ref[3 * n_seg + i - 1], sems.at[1 - slot])

    @pl.when(i == n_steps - 1)
    def _():
        _wait_rows(stage.at[slot], xs_out, meta_ref[3 * n_seg + i], sems.at[slot])


def _dispatch_call(meta, pad_info, route, h2, n_rows):
    n_tok = h2.shape[0]
    tm = TM_PROJ
    n_steps = n_tok // tm
    assert (n_rows - n_tok * TOP_K) % (n_steps * ZERO_ROWS) == 0 and TM_FFN <= 2 * ZERO_ROWS
    grid_spec = pltpu.PrefetchScalarGridSpec(
        num_scalar_prefetch=2,
        grid=(n_steps,),
        in_specs=[pl.BlockSpec((tm, LANES), lambda i, m, p: (i, 0)),
                  pl.BlockSpec((tm, D), lambda i, m, p: (i, 0))],
        out_specs=pl.BlockSpec(memory_space=pl.ANY),
        scratch_shapes=[pltpu.VMEM((2, STAGE_ROWS, D), BF16), pltpu.VMEM((ZERO_ROWS, D), BF16),
                        pltpu.SemaphoreType.DMA((2,)), pltpu.SemaphoreType.DMA(())],
    )
    return pl.pallas_call(
        _dispatch_kernel,
        grid_spec=grid_spec,
        out_shape=jax.ShapeDtypeStruct((n_rows, D), BF16),
        compiler_params=_cparams(("arbitrary",)),
        name="dispatch",
    )(meta, pad_info, route, h2)


def _ffn_kernel(te_ref, nx_ref, nu_ref, x_ref, wg_hbm, wu_hbm, wd_hbm, bg0, bu0, bd0, bg1, bu1,
                bd1, y_ref, land, wgb, wub, wdb, sems):
    i = pl.program_id(0)
    tm = TM_FFN
    t0 = 2 * i
    t1 = t0 + 1
    e0 = te_ref[t0]
    e1 = te_ref[t1]
    used0 = t0 < nu_ref[0]
    used1 = t1 < nu_ref[0]
    first0 = jnp.logical_or(i == 0, e0 != te_ref[jnp.maximum(t0 - 1, 0)])
    joint = jnp.logical_and(used1, e1 == e0)
    w_hbm = (wg_hbm, wu_hbm, wd_hbm)

    def fetch(expert):
        for j in range(3):
            pltpu.make_async_copy(w_hbm[j].at[expert], land.at[j], sems.at[j]).start()

    def switch(t):
        for j, dst in enumerate((wgb, wub, wdb)):
            pltpu.make_async_copy(w_hbm[j].at[0], land.at[j], sems.at[j]).wait()
            dst[...] = land[j].astype(BF16)
        nxt = nx_ref[t]

        @pl.when(nxt >= 0)
        def _():
            fetch(jnp.maximum(nxt, 0))

    def compute(rows, bg, bu, bd):
        x = x_ref[rows, :]
        gate = jnp.minimum(_dot(x, wgb[...]) + bg[...], SWIGLU_LIMIT)
        up = jnp.clip(_dot(x, wub[...]) + bu[...], -SWIGLU_LIMIT, SWIGLU_LIMIT)
        act = ((up + 1.0) * gate * _sigmoid(SWIGLU_ALPHA * gate)).astype(BF16)
        y_ref[rows, :] = (_dot(act, wdb[...]) + bd[...]).astype(BF16)

    lo = slice(0, tm)
    hi = slice(tm, 2 * tm)

    @pl.when(jnp.logical_and(used0, i == 0))
    def _():
        fetch(e0)

    @pl.when(jnp.logical_and(used0, first0))
    def _():
        switch(t0)

    @pl.when(joint)
    def _():
        compute(slice(0, 2 * tm), bg0, bu0, bd0)

    @pl.when(jnp.logical_and(used0, jnp.logical_not(joint)))
    def _():
        compute(lo, bg0, bu0, bd0)

    @pl.when(jnp.logical_and(used1, jnp.logical_not(joint)))
    def _():
        switch(t1)
        compute(hi, bg1, bu1, bd1)

    @pl.when(jnp.logical_not(used0))
    def _():
        y_ref[lo, :] = jnp.zeros((tm, D), y_ref.dtype)

    @pl.when(jnp.logical_not(used1))
    def _():
        y_ref[hi, :] = jnp.zeros((tm, D), y_ref.dtype)


def _ffn_call(tile_exp, next_exp, n_used, xs, w_gate, w_up, w_down, b_gate, b_up, b_down):
    n_rows = xs.shape[0]
    tm = TM_FFN
    n_tiles = n_rows // tm
    assert D == D_FF
    biases = (b_gate.reshape(N_EXP, 1, D_FF), b_up.reshape(N_EXP, 1, D_FF),
              b_down.reshape(N_EXP, 1, D))

    assert n_tiles % 2 == 0

    def xrow(i, te, nx, nu):
        return (jnp.maximum(jnp.minimum(i, (nu[0] - 1) // 2), 0), 0)

    def bias(k):
        return lambda i, te, nx, nu: (te[2 * i + k], 0, 0)

    grid_spec = pltpu.PrefetchScalarGridSpec(
        num_scalar_prefetch=3,
        grid=(n_tiles // 2,),
        in_specs=[pl.BlockSpec((2 * tm, D), xrow),
                  pl.BlockSpec(memory_space=pl.ANY),
                  pl.BlockSpec(memory_space=pl.ANY),
                  pl.BlockSpec(memory_space=pl.ANY),
                  pl.BlockSpec((None, 1, D_FF), bias(0)),
                  pl.BlockSpec((None, 1, D_FF), bias(0)),
                  pl.BlockSpec((None, 1, D), bias(0)),
                  pl.BlockSpec((None, 1, D_FF), bias(1)),
                  pl.BlockSpec((None, 1, D_FF), bias(1)),
                  pl.BlockSpec((None, 1, D), bias(1))],
        out_specs=pl.BlockSpec((2 * tm, D), lambda i, te, nx, nu: (i, 0)),
        scratch_shapes=[pltpu.VMEM((3, D, D_FF), F32),
                        pltpu.VMEM((D, D_FF), BF16), pltpu.VMEM((D, D_FF), BF16),
                        pltpu.VMEM((D_FF, D), BF16), pltpu.SemaphoreType.DMA((3,))],
    )
    return pl.pallas_call(
        _ffn_kernel,
        grid_spec=grid_spec,
        out_shape=jax.ShapeDtypeStruct((n_rows, D), BF16),
        compiler_params=_cparams(("arbitrary",)),
        name="ffn",
    )(tile_exp, next_exp, n_used, xs, w_gate, w_up, w_down, *(2 * biases))


def _combine_kernel(meta_ref, y_hbm, route_ref, x1_ref, mod_ref, gpost_ref, o_ref, stage, sems):
    i = pl.program_id(0)
    n_steps = pl.num_programs(0)
    n_seg = n_steps * N_EXP
    tm = x1_ref.shape[0]
    slot = i % 2

    def fetch(tile, into):
        def issue(e, carry):
            s = tile * N_EXP + e
            _seg_copies(y_hbm, stage.at[into], meta_ref[2 * n_seg + s], meta_ref[s],
                        meta_ref[n_seg + s], sems.at[into])
            return carry

        lax.fori_loop(0, N_EXP, issue, 0)

    @pl.when(i == 0)
    def _():
        stage[...] = jnp.zeros_like(stage)
        fetch(0, 0)

    @pl.when(i + 1 < n_steps)
    def _():
        fetch(i + 1, 1 - slot)

    _wait_rows(y_hbm, stage.at[slot], meta_ref[3 * n_seg + i], sems.at[slot])

    route = route_ref[...]
    col = lax.broadcasted_iota(jnp.int32, (tm, STAGE_ROWS), 1).astype(F32)
    wmat = jnp.zeros((tm, STAGE_ROWS), F32)
    for kk in range(TOP_K):
        hit = col == route[:, ROUTE_P + kk:ROUTE_P + kk + 1]
        wmat = jnp.where(hit, route[:, ROUTE_W + kk:ROUTE_W + kk + 1], wmat)
    moe = _dot(wmat.astype(BF16), stage[slot])
    gt_f = mod_ref[...][:, 5 * D:6 * D]
    o_ref[...] = x1_ref[...] + gt_f * _rms(moe, gpost_ref[...])


def _combine_call(meta, y, route, x1, mod3, tiles_per_seq, gpost):
    n_tok = x1.shape[0]
    tm = TM_PROJ
    grid_spec = pltpu.PrefetchScalarGridSpec(
        num_scalar_prefetch=1,
        grid=(n_tok // tm,),
        in_specs=[pl.BlockSpec(memory_space=pl.ANY),
                  pl.BlockSpec((tm, LANES), lambda i, m: (i, 0)),
                  pl.BlockSpec((tm, D), lambda i, m: (i, 0)),
                  pl.BlockSpec((None, 1, 6 * D), lambda i, m: (i // tiles_per_seq, 0, 0)),
                  pl.BlockSpec((1, D), lambda i, m: (0, 0))],
        out_specs=pl.BlockSpec((tm, D), lambda i, m: (i, 0)),
        scratch_shapes=[pltpu.VMEM((2, STAGE_ROWS, D), BF16), pltpu.SemaphoreType.DMA((2,))],
    )
    return pl.pallas_call(
        _combine_kernel,
        grid_spec=grid_spec,
        out_shape=jax.ShapeDtypeStruct((n_tok, D), F32),
        compiler_params=_cparams(("arbitrary",)),
        name="combine",
    )(meta, y, route, x1, mod3, gpost)


def _rope_swap(w):
    q = MLA_ROPE // 4
    return jnp.concatenate([-w[..., q:2 * q], w[..., 0:q], -w[..., 3 * q:4 * q],
                            w[..., 2 * q:3 * q]], axis=-1)


def _rope_tables(seq):
    rows = seq // GRID_W
    r, col = np.meshgrid(np.arange(rows, dtype=np.float64), np.arange(GRID_W, dtype=np.float64),
                         indexing="ij")
    half = MLA_ROPE // 2
    inv_freq = ROPE_BASE ** (-np.arange(0, half, 2, dtype=np.float64) / half)
    ar = r.reshape(-1)[:, None] * inv_freq
    ac = col.reshape(-1)[:, None] * inv_freq
    cos = np.concatenate([np.cos(ar), np.cos(ar), np.cos(ac), np.cos(ac)], axis=-1)
    sin = np.concatenate([np.sin(ar), np.sin(ar), np.sin(ac), np.sin(ac)], axis=-1)
    return cos, sin


def _head_slots(w3, lead):
    rows, heads, width = w3.shape
    return jnp.pad(w3, ((0, 0), (0, 0), (lead, HEAD_SLOT - lead - width))).reshape(
        rows, heads * HEAD_SLOT)


def kernel(x, c, ctx, c_ctx, w_mod, b_mod, g_pre_mix, g_post_mix, g_pre_ffn, g_post_ffn, w_in,
           gla_w_a2_f, gla_b_a_f, gla_w_a2_b, gla_b_a_b, gla_g_norm, mla_g_q, mla_w_uq, mla_g_kv,
           mla_w_uk, mla_w_uv, w_br_gla, w_br_mla, w_out, router_w, router_b, w_gate, b_gate,
           w_up, b_up, w_down, b_down):
    depth = w_mod.shape[0]
    assert depth == 1, "single-layer block"
    batch, seq, d = x.shape
    ctx_len = ctx.shape[1]
    assert d == D and seq % TM_PROJ == 0 and (batch * ctx_len) % TM_PROJ == 0
    assert TM_PROJ % ctx_len == 0 or ctx_len % TM_PROJ == 0
    n_tok = batch * seq

    cc = jnp.zeros((16, D), F32).at[:batch].set(c).at[batch].set(c_ctx)
    mod = _mod_call(cc, w_mod[0], b_mod[0])
    mod3 = mod.reshape(16, 1, 6 * D)

    wi = w_in[0]
    kr = wi[:, O_KR:O_KR + MLA_ROPE]
    small = jnp.concatenate([wi[:, O_AF:O_AF + GLA_RANK], wi[:, O_AB:O_AB + GLA_RANK], kr,
                             _rope_swap(kr), jnp.zeros((D, LANES - X_KRS - MLA_ROPE), F32)],
                            axis=1)
    w1 = jnp.concatenate([wi[:, O_Q:O_G], wi[:, O_DKV:O_DKV + MLA_KVR], small], axis=1).astype(BF16)
    w2 = jnp.concatenate([wi[:, O_G:O_G + GLA_V], wi[:, O_DQ:O_DQ + MLA_QR], wi[:, O_MG:]],
                         axis=1).astype(BF16)
    wa = jnp.zeros((LANES, 2 * GLA_QK), F32)
    wa = wa.at[X_AF:X_AF + GLA_RANK, 0:GLA_QK].set(gla_w_a2_f[0])
    wa = wa.at[X_AB:X_AB + GLA_RANK, GLA_QK:].set(gla_w_a2_b[0]).astype(BF16)
    ba = jnp.concatenate([gla_b_a_f[0], gla_b_a_b[0]]).reshape(1, 2 * GLA_QK)

    uk = mla_w_uk[0]
    uv = mla_w_uv[0]
    uq = mla_w_uq[0]
    wk_top = _head_slots(uk.reshape(MLA_KVR, MLA_H, MLA_NOPE), 0)
    place = np.zeros((LANES, MLA_H, HEAD_SLOT), np.float32)
    for j in range(MLA_ROPE):
        place[X_KR + j, :, MLA_NOPE + j] = 1.0
        place[X_KRS + j, :, MLA_NOPE + j] = 1.0
    wk = jnp.concatenate([wk_top, jnp.asarray(place.reshape(LANES, MLA_W))],
                         axis=0).astype(BF16)
    uv4 = uv.reshape(MLA_KVR, MLA_H // 2, 2, MLA_V)
    zv = jnp.zeros_like(uv4[:, :, 0])
    wv = jnp.stack([jnp.concatenate([uv4[:, :, 0], zv], axis=-1),
                    jnp.concatenate([zv, uv4[:, :, 1]], axis=-1)],
                   axis=2).reshape(MLA_KVR, MLA_W).astype(BF16)
    uq3 = uq.reshape(MLA_QR, MLA_H, MLA_QKD)
    wq_a = _head_slots(uq3, 0)
    wq_b = _head_slots(_rope_swap(uq3[:, :, MLA_NOPE:]), MLA_NOPE)
    wq = jnp.concatenate([wq_a, wq_b], axis=1).astype(BF16)

    cos, sin = _rope_tables(seq)
    zeros32 = np.zeros((seq, MLA_ROPE))
    t1_x = jnp.asarray(np.concatenate([zeros32, cos, sin, zeros32], axis=1), F32)
    t1_c = jnp.asarray(np.broadcast_to(
        np.concatenate([np.zeros(MLA_ROPE), np.ones(MLA_ROPE), np.zeros(2 * MLA_ROPE)]),
        (TM_PROJ, LANES)), F32)
    q_scale = MLA_SCALE * LOG2_E
    cq = jnp.asarray(np.concatenate([np.ones((seq, MLA_NOPE)), cos, zeros32], axis=1) * q_scale, F32)
    sq = jnp.asarray(np.concatenate([np.zeros((seq, MLA_NOPE)), sin, zeros32], axis=1) * q_scale,
                     F32)

    gpre = g_pre_mix[0].reshape(1, D)
    gkv = mla_g_kv[0].reshape(1, MLA_KVR)
    gq = mla_g_q[0].reshape(1, MLA_QR)
    tiles_per_seq = seq // TM_PROJ

    xf = x.reshape(n_tok, D)
    cf = ctx.reshape(batch * ctx_len, D)
    (gqx, gkx, gvx, lfx, lbx, mkx, mvx, sz, mq, sg, sm) = _inproj_call(
        True, xf, mod3, lambda i: i // tiles_per_seq, tiles_per_seq, gpre, w1, wa, ba, wk, wv,
        gkv, t1_x, extra=(w2, gq, wq, cq, sq))
    (gqc, gkc, gvc, lfc, lbc, mkc, mvc) = _inproj_call(
        False, cf, mod3, lambda i: batch, 1, gpre, w1, wa, ba, wk, wv, gkv, t1_c)

    og = _gla_call((gqx, gkx, gvx, lfx, lbx), (gqc, gkc, gvc, lfc, lbc), batch, seq, ctx_len)
    om = _mla_call(mq, mkx, mvx, mkc, mvc, batch, seq, ctx_len)

    wr = jnp.pad(router_w[0], ((0, 0), (0, LANES - N_EXP)))
    wrh = wr.astype(BF16)
    wrl = (wr - wrh.astype(F32)).astype(BF16)
    br =jnp.pad(router_b[0], (0, LANES - N_EXP)).reshape(1, LANES)
    x1, h2, route, cnt = _mixout_call(
        og, sz, om, sg, sm, xf, mod3, tiles_per_seq, gla_g_norm[0].reshape(1, GLA_DV),
        w_br_gla[0].astype(BF16), w_br_mla[0].astype(BF16), w_out[0].astype(BF16),
        g_post_mix[0].reshape(1, D), g_pre_ffn[0].reshape(1, D), wrh, wrl, br)

    seg = -(-cnt[:, 0, :N_EXP].astype(jnp.int32) // SEG_ALIGN) * SEG_ALIGN
    stage_start = jnp.cumsum(seg, axis=1) - seg
    in_group = jnp.cumsum(seg, axis=0) - seg
    group = jnp.sum(seg, axis=0)
    padded = -(-group // TM_FFN) * TM_FFN
    ends = jnp.cumsum(padded)
    starts = ends - padded
    sorted_row = starts[None, :] + in_group
    meta = jnp.concatenate([stage_start.reshape(-1), seg.reshape(-1), sorted_row.reshape(-1),
                            jnp.sum(seg, axis=1)]).astype(jnp.int32)
    pad_info = jnp.concatenate([starts + group, ends[-1:], ends]).astype(jnp.int32)
    n_rows = _sorted_rows(n_tok)
    tile_start = jnp.arange(n_rows // TM_FFN, dtype=jnp.int32) * TM_FFN
    tile_exp = jnp.minimum(jnp.sum((ends[None, :] <= tile_start[:, None]).astype(jnp.int32), axis=1),
                           N_EXP - 1)
    n_used = (ends[-1:] // TM_FFN).astype(jnp.int32)
    ids = jnp.arange(N_EXP, dtype=jnp.int32)
    later = jnp.logical_and(padded[None, :] > 0, ids[None, :] > ids[:, None])
    next_owner = jnp.min(jnp.where(later, ids[None, :], N_EXP), axis=1)
    next_owner = jnp.where(next_owner == N_EXP, -1, next_owner)
    next_exp = jnp.sum(jnp.where(tile_exp[:, None] == ids[None, :], next_owner[None, :], 0),
                       axis=1).astype(jnp.int32)

    xs = _dispatch_call(meta, pad_info, route, h2, n_rows)
    y = _ffn_call(tile_exp, next_exp, n_used, xs, w_gate[0], w_up[0], w_down[0], b_gate[0], b_up[0],
                  b_down[0])
    out = _combine_call(meta, y, route, x1, mod3, tiles_per_seq, g_post_ffn[0].reshape(1, D))
    return out.reshape(batch, seq, D)
```

```python
import functools

import jax
import jax.numpy as jnp
import numpy as np
from jax import lax
from jax.experimental import pallas as pl
from jax.experimental.pallas import tpu as pltpu

F32 = jnp.float32
BF16 = jnp.bfloat16

D = 1024
EPS = 1e-6
GRID_W = 64

GLA_H = 4
GLA_DK = 64
GLA_DV = 128
GLA_RANK = 16
GLA_TAU = 16.0
GLA_CHUNK = 64
GLA_BLOCK = 256
GLA_QK = GLA_H * GLA_DK
GLA_V = GLA_H * GLA_DV

MLA_H = 8
MLA_QR = 256
MLA_KVR = 128
MLA_NOPE = 64
MLA_ROPE = 32
MLA_V = 64
MLA_QKD = MLA_NOPE + MLA_ROPE
MLA_SCALE = MLA_QKD ** -0.5
LOG2_E = 1.4426950408889634
ROPE_BASE = 10000.0
HEAD_SLOT = 128
MLA_W = MLA_H * HEAD_SLOT
SUM_LANE_EVEN, SUM_LANE_ODD = MLA_V, 0

N_EXP = 32
TOP_K = 4
D_FF = 1024
SWIGLU_LIMIT = 7.0
SWIGLU_ALPHA = 1.702

LANES = 128
SUBLANES = 8
TM_PROJ = 512
TQ = 1024
TM_FFN = 512
ZERO_ROWS = 256
LONG_SEGMENT = 128
STAGE_ROWS = -(-(TM_PROJ * TOP_K + N_EXP * (SUBLANES - 1)) // 256) * 256
ROUTE_E, ROUTE_W, ROUTE_P = 0, TOP_K, 2 * TOP_K

_OFF = np.cumsum([0, GLA_QK, GLA_QK, GLA_V, GLA_V, GLA_RANK, GLA_RANK,
                  MLA_QR, MLA_KVR, MLA_ROPE, D, D])
(O_Q, O_K, O_V, O_G, O_AF, O_AB, O_DQ, O_DKV, O_KR, O_MG, O_MM, _) = _OFF.tolist()

X_AF, X_AB, X_KR, X_KRS = 0, 16, 32, 64

VMEM_LIMIT = 56 * 1024 * 1024


def _cparams(sem):
    return pltpu.CompilerParams(dimension_semantics=sem, vmem_limit_bytes=VMEM_LIMIT)


def _resident(arr):
    nd = arr.ndim
    return pl.BlockSpec(arr.shape, lambda *_: (0,) * nd, pipeline_mode=pl.Buffered(1))


def _rms(x, g):
    return x * lax.rsqrt(jnp.mean(x * x, axis=-1, keepdims=True) + EPS) * g


def _sigmoid(x):
    return 1.0 / (1.0 + jnp.exp(-x))


def _log_sigmoid(x):
    return jnp.minimum(x, 0.0) - jnp.log1p(jnp.exp(-jnp.abs(x)))


def _dot(a, b):
    return jnp.dot(a, b, preferred_element_type=F32)


def _dot_nt(a, b):
    return lax.dot_general(a, b, (((1,), (1,)), ((), ())), preferred_element_type=F32)


def _dot_tn(a, b):
    return lax.dot_general(a, b, (((0,), (0,)), ((), ())), preferred_element_type=F32)


def _mod_kernel(c_ref, w_ref, b_ref, o_ref):
    c = c_ref[...]
    s = (c * _sigmoid(c)).astype(BF16)
    o_ref[...] = _dot(s, w_ref[...].astype(BF16)) + b_ref[...]


def _mod_call(cc, w_mod, b_mod):
    n = w_mod.shape[1]
    tn = 1536
    return pl.pallas_call(
        _mod_kernel,
        grid=(n // tn,),
        in_specs=[pl.BlockSpec((16, D), lambda j: (0, 0)),
                  pl.BlockSpec((D, tn), lambda j: (0, j)),
                  pl.BlockSpec((1, tn), lambda j: (0, j))],
        out_specs=pl.BlockSpec((16, tn), lambda j: (0, j)),
        out_shape=jax.ShapeDtypeStruct((16, n), F32),
        compiler_params=_cparams(("arbitrary",)),
        name="mod",
    )(cc, w_mod, b_mod.reshape(1, n))


def _inproj_kernel(with_q, x_ref, mod_ref, gpre_ref, w1_ref, wa_ref, ba_ref, wk_ref,
                   wv_ref, gkv_ref, t1_ref, *rest):
    if with_q:
        (w2_ref, gq_ref, wq_ref, cq_ref, sq_ref,
         q_o, k_o, v_o, lf_o, lb_o, mk_o, mv_o, sz_o, mq_o, sg_o, sm_o) = rest
    else:
        (q_o, k_o, v_o, lf_o, lb_o, mk_o, mv_o) = rest
    x = x_ref[...]
    mod = mod_ref[...]
    sh = mod[:, 0:D]
    sc = mod[:, D:2 * D]
    h = (_rms(x, gpre_ref[...]) * (1.0 + sc) + sh).astype(BF16)

    z1 = _dot(h, w1_ref[...])
    q_o[...] = (z1[:, 0:GLA_QK] * (GLA_DK ** -0.5)).astype(BF16)
    k_o[...] = z1[:, GLA_QK:2 * GLA_QK].astype(BF16)
    v_o[...] = z1[:, 2 * GLA_QK:2 * GLA_QK + GLA_V].astype(BF16)
    o_dkv = 2 * GLA_QK + GLA_V
    ckv = _rms(z1[:, o_dkv:o_dkv + MLA_KVR], gkv_ref[...])
    xs = z1[:, o_dkv + MLA_KVR:o_dkv + MLA_KVR + LANES]

    la = _log_sigmoid(_dot(xs.astype(BF16), wa_ref[...]) + ba_ref[...]) * (1.0 / GLA_TAU)
    lf_o[...] = la[:, 0:GLA_QK]
    lb_o[...] = la[:, GLA_QK:2 * GLA_QK]

    lhs_k = jnp.concatenate([ckv, xs * t1_ref[...]], axis=-1).astype(BF16)
    mk_o[...] = _dot(lhs_k, wk_ref[...]).astype(BF16)
    lane = lax.broadcasted_iota(jnp.int32, (x.shape[0], MLA_W), 1)
    mv = _dot(ckv.astype(BF16), wv_ref[...])
    mv_o[...] = jnp.where(lane % (2 * HEAD_SLOT) == SUM_LANE_EVEN, 1.0,
                          jnp.where(lane % (2 * HEAD_SLOT) == HEAD_SLOT + SUM_LANE_ODD, 1.0,
                                    mv)).astype(BF16)

    if with_q:
        zg = _dot(h, w2_ref[:, 0:GLA_V])
        sz_o[...] = (zg * _sigmoid(zg)).astype(BF16)
        n = _rms(_dot(h, w2_ref[:, GLA_V:GLA_V + MLA_QR]), gq_ref[...]).astype(BF16)
        cq = jnp.concatenate([cq_ref[...]] * MLA_H, axis=-1)
        sq = jnp.concatenate([sq_ref[...]] * MLA_H, axis=-1)
        mq_o[...] = (_dot(n, wq_ref[:, 0:MLA_W]) * cq
                     + _dot(n, wq_ref[:, MLA_W:2 * MLA_W]) * sq).astype(BF16)
        o_mg = GLA_V + MLA_QR
        sg_o[...] = _sigmoid(_dot(h, w2_ref[:, o_mg:o_mg + D])).astype(BF16)
        sm_o[...] = _sigmoid(_dot(h, w2_ref[:, o_mg + D:o_mg + 2 * D])).astype(BF16)


def _inproj_call(with_q, xf, mod3, mod_row_fn, tiles_per_seq, gpre, w1, wa, ba, wk, wv,
                 gkv, t1, extra=()):
    n_tok = xf.shape[0]
    tm = TM_PROJ
    grid = (n_tok // tm,)
    row = lambda i: (i, 0)
    tab = lambda i: (i % tiles_per_seq, 0)
    in_specs = [
        pl.BlockSpec((tm, D), row),
        pl.BlockSpec((None, 1, 6 * D), lambda i: (mod_row_fn(i), 0, 0)),
        _resident(gpre), _resident(w1), _resident(wa), _resident(ba), _resident(wk),
        _resident(wv), _resident(gkv),
        pl.BlockSpec((tm, LANES), tab),
    ]
    widths = [(GLA_QK, BF16), (GLA_QK, BF16), (GLA_V, BF16), (GLA_QK, F32), (GLA_QK, F32),
              (MLA_W, BF16), (MLA_W, BF16)]
    args = [xf, mod3, gpre, w1, wa, ba, wk, wv, gkv, t1]
    if with_q:
        w2, gq, wq, cq, sq = extra
        in_specs += [_resident(w2), _resident(gq), _resident(wq),
                     pl.BlockSpec((tm, HEAD_SLOT), tab), pl.BlockSpec((tm, HEAD_SLOT), tab)]
        args += [w2, gq, wq, cq, sq]
        widths += [(GLA_V, BF16), (MLA_W, BF16), (D, BF16), (D, BF16)]
    return pl.pallas_call(
        functools.partial(_inproj_kernel, with_q),
        grid=grid,
        in_specs=in_specs,
        out_specs=[pl.BlockSpec((tm, w), row) for w, _ in widths],
        out_shape=[jax.ShapeDtypeStruct((n_tok, w), dt) for w, dt in widths],
        compiler_params=_cparams(("arbitrary",)),
        name="inproj_x" if with_q else "inproj_ctx",
    )(*args)


def _gla_kernel(qx, kx, vx, lfx, lbx, qc, kc, vc, lfc, lbc, o_ref, sf_ref, sb_ref):
    C = GLA_CHUNK
    R = GLA_BLOCK
    n_sub = R // C
    rr = lax.broadcasted_iota(jnp.int32, (R, R), 0)
    cc = lax.broadcasted_iota(jnp.int32, (R, R), 1)
    same = (rr // C) == (cc // C)
    tbd_f = jnp.logical_and(same, cc <= rr).astype(BF16)
    tbd_b = jnp.logical_and(same, cc >= rr).astype(BF16)
    row = lax.broadcasted_iota(jnp.int32, (C, GLA_QK), 0)
    col = lax.broadcasted_iota(jnp.int32, (C, GLA_QK), 1)
    head_qk = col // GLA_DK
    tri4_f = (col % C) <= row
    tri4_b = (col % C) >= row
    head_v = lax.broadcasted_iota(jnp.int32, (C, GLA_V), 1) // GLA_DV

    def stack_masked(x, head_of_lane):
        return jnp.concatenate(
            [jnp.where(head_of_lane == hh, x, jnp.zeros_like(x)) for hh in range(GLA_H)], axis=0)

    def block(q_ref, k_ref, v_ref, la_ref, row0, fwd, s_ref, emit):
        sl = pl.ds(row0, R)
        la = la_ref[sl, :]
        la_hi = la.astype(BF16)
        la_lo = (la - la_hi.astype(F32)).astype(BF16)
        tbd = tbd_f if fwd else tbd_b
        cum = _dot(tbd, la_hi) + _dot(tbd, la_lo)
        k = k_ref[sl, :].astype(F32)
        v = v_ref[sl, :]
        if emit:
            q = q_ref[sl, :].astype(F32)
        st = s_ref[...]
        outs = [None] * n_sub
        for ci in (range(n_sub) if fwd else reversed(range(n_sub))):
            rs = slice(ci * C, (ci + 1) * C)
            cum_c = cum[rs]
            tot = cum_c[C - 1:C, :] if fwd else cum_c[0:1, :]
            k_c = k[rs]
            v_c = v[rs]
            ke_bd = stack_masked((k_c * jnp.exp(tot - cum_c)).astype(BF16), head_qk)
            v_stack = jnp.concatenate(
                [v_c[:, hh * GLA_DV:(hh + 1) * GLA_DV] for hh in range(GLA_H)], axis=0)
            if emit:
                qd = (q[rs] * jnp.exp(cum_c)).astype(BF16)
                ki_bd = stack_masked((k_c * jnp.exp(-cum_c)).astype(BF16), head_qk)
                att = _dot_nt(qd, ki_bd)
                att = jnp.where(tri4_f if fwd else tri4_b, att, 0.0).astype(BF16)
                o_intra = _dot(att, stack_masked(v_c, head_v))
                oi = _dot_nt(stack_masked(qd, head_qk), st.astype(BF16))
                o_inter = jnp.concatenate([oi[hh * C:(hh + 1) * C] for hh in range(GLA_H)],
                                          axis=-1)
                outs[ci] = o_intra + o_inter
            st = st * jnp.exp(tot) + _dot_tn(v_stack, ke_bd)
        s_ref[...] = st
        if emit:
            o_ref[sl, :] += jnp.concatenate(outs, axis=0)

    sf_ref[...] = jnp.zeros_like(sf_ref)
    sb_ref[...] = jnp.zeros_like(sb_ref)
    o_ref[...] = jnp.zeros_like(o_ref)
    n_ctx = qc.shape[0] // R
    n_x = qx.shape[0] // R
    for i in range(n_ctx):
        block(qc, kc, vc, lfc, i * R, True, sf_ref, False)
        block(qc, kc, vc, lbc, (n_ctx - 1 - i) * R, False, sb_ref, False)

    def body(i, carry):
        block(qx, kx, vx, lfx, pl.multiple_of(i * R, R), True, sf_ref, True)
        block(qx, kx, vx, lbx, pl.multiple_of((n_x - 1 - i) * R, R), False, sb_ref, True)
        return carry

    lax.fori_loop(0, n_x, body, 0)


def _gla_call(fx, fc, batch, seq, ctx_len):
    qx, kx, vx, lfx, lbx = fx
    qc, kc, vc, lfc, lbc = fc

    def spec(rows, w):
        return pl.BlockSpec((rows, w), lambda b: (b, 0))

    return pl.pallas_call(
        _gla_kernel,
        grid=(batch,),
        in_specs=[spec(seq, GLA_QK), spec(seq, GLA_QK), spec(seq, GLA_V), spec(seq, GLA_QK),
                  spec(seq, GLA_QK),
                  spec(ctx_len, GLA_QK), spec(ctx_len, GLA_QK), spec(ctx_len, GLA_V),
                  spec(ctx_len, GLA_QK), spec(ctx_len, GLA_QK)],
        out_specs=spec(seq, GLA_V),
        out_shape=jax.ShapeDtypeStruct((batch * seq, GLA_V), F32),
        scratch_shapes=[pltpu.VMEM((GLA_DV, GLA_QK), F32), pltpu.VMEM((GLA_DV, GLA_QK), F32)],
        compiler_params=_cparams(("arbitrary",)),
        name="gla",
    )(qx, kx, vx, lfx, lbx, qc, kc, vc, lfc, lbc)


def _mla_kernel(q_ref, kx_ref, vx_ref, kc_ref, vc_ref, o_ref):
    lane = lax.broadcasted_iota(jnp.int32, (q_ref.shape[0], HEAD_SLOT), 1)
    for j in range(MLA_H // 2):
        pair = []
        for hh, sum_lane in ((2 * j, SUM_LANE_EVEN), (2 * j + 1, SUM_LANE_ODD)):
            sl = slice(hh * HEAD_SLOT, (hh + 1) * HEAD_SLOT)
            q = q_ref[:, sl]
            sx = _dot_nt(q, kx_ref[:, sl])
            sc = _dot_nt(q, kc_ref[:, sl])
            m = jnp.maximum(jnp.max(sx, axis=-1, keepdims=True),
                            jnp.max(sc, axis=-1, keepdims=True))
            px = jnp.exp2(sx - m).astype(BF16)
            pc = jnp.exp2(sc - m).astype(BF16)
            o = _dot(px, vx_ref[:, sl]) + _dot(pc, vc_ref[:, sl])
            pair.append(o / o[:, sum_lane:sum_lane + 1])
        o_ref[:, j * HEAD_SLOT:(j + 1) * HEAD_SLOT] = jnp.where(
            lane < MLA_V, pair[0], pair[1]).astype(BF16)


def _mla_call(mq, mkx, mvx, mkc, mvc, batch, seq, ctx_len):
    nq = seq // TQ
    return pl.pallas_call(
        _mla_kernel,
        grid=(batch, nq),
        in_specs=[pl.BlockSpec((TQ, MLA_W), lambda b, i: (b * nq + i, 0)),
                  pl.BlockSpec((seq, MLA_W), lambda b, i: (b, 0)),
                  pl.BlockSpec((seq, MLA_W), lambda b, i: (b, 0)),
                  pl.BlockSpec((ctx_len, MLA_W), lambda b, i: (b, 0)),
                  pl.BlockSpec((ctx_len, MLA_W), lambda b, i: (b, 0))],
        out_specs=pl.BlockSpec((TQ, MLA_H * MLA_V), lambda b, i: (b * nq + i, 0)),
        out_shape=jax.ShapeDtypeStruct((batch * seq, MLA_H * MLA_V), BF16),
        compiler_params=_cparams(("arbitrary", "arbitrary")),
        name="mla",
    )(mq, mkx, mvx, mkc, mvc)


def _mixout_kernel(og_ref, sz_ref, om_ref, sg_ref, sm_ref, x_ref, mod_ref, gn_ref, wbg_ref,
                   wbm_ref, wo_ref, gpost_ref, gffn_ref, wr_ref, br_ref,
                   x1_o, h2_o, route_o, cnt_o):
    tm = x_ref.shape[0]
    mod = mod_ref[...]
    gt_a = mod[:, 2 * D:3 * D]
    sh_f = mod[:, 3 * D:4 * D]
    sc_f = mod[:, 4 * D:5 * D]

    og = og_ref[...]
    gn = gn_ref[...]
    parts = [_rms(og[:, hh * GLA_DV:(hh + 1) * GLA_DV], gn) for hh in range(GLA_H)]
    a = (jnp.concatenate(parts, axis=-1) * sz_ref[...].astype(F32)).astype(BF16)
    br_g = _dot(a, wbg_ref[...])
    br_m = _dot(om_ref[...], wbm_ref[...])
    merged = (sg_ref[...].astype(F32) * br_g + sm_ref[...].astype(F32) * br_m).astype(BF16)
    mo = _dot(merged, wo_ref[...])
    x1 = x_ref[...] + gt_a * _rms(mo, gpost_ref[...])
    x1_o[...] = x1
    h2 = _rms(x1, gffn_ref[...]) * (1.0 + sc_f) + sh_f
    h2_o[...] = h2.astype(BF16)

    lane = lax.broadcasted_iota(jnp.int32, (tm, LANES), 1)
    h_hi = h2.astype(BF16)
    h_lo = (h2 - h_hi.astype(F32)).astype(BF16)
    part = _dot(h_hi, wr_ref[...]) + _dot(h_lo, wr_ref[...])
    logits = part[:, 0:LANES] + part[:, LANES:2 * LANES] + br_ref[...]
    neg = jnp.float32(-jnp.inf)
    lg = jnp.where(lane < N_EXP, logits, neg)
    lane_f = lane.astype(F32)
    hots, vals = [], []
    for _k in range(TOP_K):
        mx = jnp.max(lg, axis=-1, keepdims=True)
        idx = jnp.min(jnp.where(lg == mx, lane_f, float(LANES)), axis=-1, keepdims=True)
        hot = lane_f == idx
        lg = jnp.where(hot, neg, lg)
        hots.append(hot)
        vals.append(mx)
    es = [jnp.exp(v - vals[0]) for v in vals]
    den = es[0] + es[1] + es[2] + es[3]
    ws = [e / den for e in es]

    msum = jnp.zeros((tm, LANES), F32)
    for hot in hots:
        msum = msum + hot.astype(F32)
    rr = lax.broadcasted_iota(jnp.int32, (tm, tm), 0)
    cc = lax.broadcasted_iota(jnp.int32, (tm, tm), 1)
    lower = (cc < rr).astype(BF16)
    prior = _dot(lower, msum.astype(BF16))
    cnt = jnp.sum(msum, axis=0, keepdims=True)
    seg = jnp.floor((cnt + (SUBLANES - 1.0)) * (1.0 / SUBLANES)) * SUBLANES
    er = lax.broadcasted_iota(jnp.int32, (LANES, LANES), 0)
    ec = lax.broadcasted_iota(jnp.int32, (LANES, LANES), 1)
    before = (er < ec).astype(BF16)
    seg_start = _dot(jnp.broadcast_to(seg, (SUBLANES, LANES)).astype(BF16), before)[0:1, :]
    slot_of = prior + seg_start

    route = jnp.zeros((tm, LANES), F32)
    for kk in range(TOP_K):
        hotf = hots[kk].astype(F32)
        e_col = jnp.sum(hotf * lane_f, axis=-1, keepdims=True)
        p_col = jnp.sum(hotf * slot_of, axis=-1, keepdims=True)
        route = jnp.where(lane == ROUTE_E + kk, e_col, route)
        route = jnp.where(lane == ROUTE_W + kk, ws[kk], route)
        route = jnp.where(lane == ROUTE_P + kk, p_col, route)
    route_o[...] = route
    cnt_o[...] = jnp.broadcast_to(cnt, cnt_o.shape)


def _mixout_call(og, sz, om, sg, sm, xf, mod3, tiles_per_seq, gn, wbg, wbm, wo, gpost, gffn,
                 wr, br):
    n_tok = xf.shape[0]
    tm = TM_PROJ
    row = lambda i: (i, 0)

    def rs(w):
        return pl.BlockSpec((tm, w), row)

    return pl.pallas_call(
        _mixout_kernel,
        grid=(n_tok // tm,),
        in_specs=[rs(GLA_V), rs(GLA_V), rs(MLA_H * MLA_V), rs(D), rs(D), rs(D),
                  pl.BlockSpec((None, 1, 6 * D), lambda i: (i // tiles_per_seq, 0, 0)),
                  _resident(gn), _resident(wbg), _resident(wbm), _resident(wo),
                  _resident(gpost), _resident(gffn), _resident(wr), _resident(br)],
        out_specs=[rs(D), rs(D), rs(LANES),
                   pl.BlockSpec((None, SUBLANES, LANES), lambda i: (i, 0, 0))],
        out_shape=[jax.ShapeDtypeStruct((n_tok, D), F32), jax.ShapeDtypeStruct((n_tok, D), BF16),
                   jax.ShapeDtypeStruct((n_tok, LANES), F32),
                   jax.ShapeDtypeStruct((n_tok // tm, SUBLANES, LANES), F32)],
        compiler_params=_cparams(("arbitrary",)),
        name="mixout",
    )(og, sz, om, sg, sm, xf, mod3, gn, wbg, wbm, wo, gpost, gffn, wr, br)


def _sorted_rows(n_tok):
    n_steps = n_tok // TM_PROJ
    worst_pad = n_steps * N_EXP * (SUBLANES - 1) + N_EXP * TM_FFN
    per_step = -(-worst_pad // (n_steps * ZERO_ROWS)) * ZERO_ROWS
    n_rows = n_tok * TOP_K + per_step * n_steps
    assert n_rows % TM_FFN == 0
    return n_rows


def _seg_copies(src_ref, dst_ref, src0, dst0, length, sem):
    def copy_bit(b):
        size = 1 << b
        off = (length >> (b + 1)) << (b + 1)

        @pl.when((length & size) != 0)
        def _():
            pltpu.make_async_copy(
                src_ref.at[pl.ds(pl.multiple_of(src0 + off, SUBLANES), size)],
                dst_ref.at[pl.ds(pl.multiple_of(dst0 + off, SUBLANES), size)], sem).start()

    lo_bit = SUBLANES.bit_length() - 1
    hi_bit = TM_PROJ.bit_length() - 1
    split = LONG_SEGMENT.bit_length() - 1

    @pl.when(length >= LONG_SEGMENT)
    def _():
        for b in range(hi_bit, split - 1, -1):
            copy_bit(b)

    for b in range(split - 1, lo_bit - 1, -1):
        copy_bit(b)


def _wait_rows(src_ref, dst_ref, total, sem):
    for b in reversed(range(SUBLANES.bit_length() - 1, STAGE_ROWS.bit_length())):
        size = 1 << b

        @pl.when((total & size) != 0)
        def _():
            pltpu.make_async_copy(src_ref.at[pl.ds(0, size)], dst_ref.at[pl.ds(0, size)],
                                  sem).wait()


def _zero_rows(zero_ref, dst_ref, start, end, sem, wait):
    length = end - start
    for b in reversed(range(SUBLANES.bit_length() - 1, ZERO_ROWS.bit_length())):
        size = 1 << b
        off = (length >> (b + 1)) << (b + 1)

        @pl.when((length & size) != 0)
        def _():
            cp = pltpu.make_async_copy(
                zero_ref.at[pl.ds(0, size)],
                dst_ref.at[pl.ds(pl.multiple_of(start + off, SUBLANES), size)], sem)
            if wait:
                cp.wait()
            else:
                cp.start()


def _slot_onehot(route_t, n_slots):
    tm = route_t.shape[1]
    slot = lax.broadcasted_iota(jnp.int32, (n_slots, tm), 0).astype(F32)
    acc = jnp.zeros((n_slots, tm), F32)
    for kk in range(TOP_K):
        acc = jnp.where(slot == route_t[ROUTE_P + kk:ROUTE_P + kk + 1, :], 1.0, acc)
    return acc.astype(BF16)


def _dispatch_kernel(meta_ref, pad_ref, route_ref, h_ref, xs_out, stage, zero_ref, sems, zsem):
    i = pl.program_id(0)
    n_steps = pl.num_programs(0)
    n_rows = xs_out.shape[0]
    n_seg = n_steps * N_EXP
    slot = i % 2

    @pl.when(i == 0)
    def _():
        zero_ref[...] = jnp.zeros_like(zero_ref)

    def clear_uncovered(wait):
        per = -(-N_EXP // n_steps)
        for j in range(per):
            e = i * per + j

            @pl.when(e < N_EXP)
            def _():
                ec = jnp.minimum(e, N_EXP - 1)
                _zero_rows(zero_ref, xs_out, pad_ref[ec], pad_ref[N_EXP + 1 + ec], zsem, wait)

        total_end = pad_ref[N_EXP]
        tail_rows = n_rows - n_steps * TM_PROJ * TOP_K
        per_step = tail_rows // n_steps
        for j in range(per_step // ZERO_ROWS):
            seg_end = (n_rows - tail_rows) + i * per_step + (j + 1) * ZERO_ROWS
            zlen = jnp.clip(seg_end - total_end, 0, ZERO_ROWS)
            _zero_rows(zero_ref, xs_out, seg_end - zlen, seg_end, zsem, wait)

    clear_uncovered(wait=False)

    onehot = _slot_onehot(route_ref[...].T, STAGE_ROWS)
    stage[slot] = _dot(onehot, h_ref[...])

    def issue(e, carry):
        s = i * N_EXP + e
        _seg_copies(stage.at[slot], xs_out, meta_ref[s], meta_ref[2 * n_seg + s],
                    meta_ref[n_seg + s], sems.at[slot])
        return carry

    lax.fori_loop(0, N_EXP, issue, 0)
    clear_uncovered(wait=True)

    @pl.when(i >= 1)
    def _():
        _wait_rows(stage.at[1 - slot], xs_out, meta_ref[3 * n_seg + i - 1], sems.at[1 - slot])

    @pl.when(i == n_steps - 1)
    def _():
        _wait_rows(stage.at[slot], xs_out, meta_ref[3 * n_seg + i], sems.at[slot])


def _dispatch_call(meta, pad_info, route, h2, n_rows):
    n_tok = h2.shape[0]
    tm = TM_PROJ
    n_steps = n_tok // tm
    assert (n_rows - n_tok * TOP_K) % (n_steps * ZERO_ROWS) == 0 and TM_FFN <= 2 * ZERO_ROWS
    grid_spec = pltpu.PrefetchScalarGridSpec(
        num_scalar_prefetch=2,
        grid=(n_steps,),
        in_specs=[pl.BlockSpec((tm, LANES), lambda i, m, p: (i, 0)),
                  pl.BlockSpec((tm, D), lambda i, m, p: (i, 0))],
        out_specs=pl.BlockSpec(memory_space=pl.ANY),
        scratch_shapes=[pltpu.VMEM((2, STAGE_ROWS, D), F32), pltpu.VMEM((ZERO_ROWS, D), F32),
                        pltpu.SemaphoreType.DMA((2,)), pltpu.SemaphoreType.DMA(())],
    )
    return pl.pallas_call(
        _dispatch_kernel,
        grid_spec=grid_spec,
        out_shape=jax.ShapeDtypeStruct((n_rows, D), F32),
        compiler_params=_cparams(("arbitrary",)),
        name="dispatch",
    )(meta, pad_info, route, h2)


def _ffn_kernel(te_ref, nx_ref, nu_ref, x_ref, wg_hbm, wu_hbm, wd_hbm, bg0, bu0, bd0, bg1, bu1,
                bd1, y_ref, land, wgb, wub, wdb, sems):
    i = pl.program_id(0)
    tm = TM_FFN
    t0 = 2 * i
    t1 = t0 + 1
    e0 = te_ref[t0]
    e1 = te_ref[t1]
    used0 = t0 < nu_ref[0]
    used1 = t1 < nu_ref[0]
    first0 = jnp.logical_or(i == 0, e0 != te_ref[jnp.maximum(t0 - 1, 0)])
    joint = jnp.logical_and(used1, e1 == e0)
    w_hbm = (wg_hbm, wu_hbm, wd_hbm)

    def fetch(expert):
        for j in range(3):
            pltpu.make_async_copy(w_hbm[j].at[expert], land.at[j], sems.at[j]).start()

    def switch(t):
        for j, dst in enumerate((wgb, wub, wdb)):
            pltpu.make_async_copy(w_hbm[j].at[0], land.at[j], sems.at[j]).wait()
            dst[...] = land[j].astype(BF16)
        nxt = nx_ref[t]

        @pl.when(nxt >= 0)
        def _():
            fetch(jnp.maximum(nxt, 0))

    def compute(rows, bg, bu, bd):
        x = x_ref[rows, :].astype(BF16)
        gate = jnp.minimum(_dot(x, wgb[...]) + bg[...], SWIGLU_LIMIT)
        up = jnp.clip(_dot(x, wub[...]) + bu[...], -SWIGLU_LIMIT, SWIGLU_LIMIT)
        act = ((up + 1.0) * gate * _sigmoid(SWIGLU_ALPHA * gate)).astype(BF16)
        y_ref[rows, :] = _dot(act, wdb[...]) + bd[...]

    lo = slice(0, tm)
    hi = slice(tm, 2 * tm)

    @pl.when(jnp.logical_and(used0, i == 0))
    def _():
        fetch(e0)

    @pl.when(jnp.logical_and(used0, first0))
    def _():
        switch(t0)

    @pl.when(joint)
    def _():
        compute(slice(0, 2 * tm), bg0, bu0, bd0)

    @pl.when(jnp.logical_and(used0, jnp.logical_not(joint)))
    def _():
        compute(lo, bg0, bu0, bd0)

    @pl.when(jnp.logical_and(used1, jnp.logical_not(joint)))
    def _():
        switch(t1)
        compute(hi, bg1, bu1, bd1)

    @pl.when(jnp.logical_not(used0))
    def _():
        y_ref[lo, :] = jnp.zeros((tm, D), y_ref.dtype)

    @pl.when(jnp.logical_not(used1))
    def _():
        y_ref[hi, :] = jnp.zeros((tm, D), y_ref.dtype)


def _ffn_call(tile_exp, next_exp, n_used, xs, w_gate, w_up, w_down, b_gate, b_up, b_down):
    n_rows = xs.shape[0]
    tm = TM_FFN
    n_tiles = n_rows // tm
    assert D == D_FF
    biases = (b_gate.reshape(N_EXP, 1, D_FF), b_up.reshape(N_EXP, 1, D_FF),
              b_down.reshape(N_EXP, 1, D))

    assert n_tiles % 2 == 0

    def xrow(i, te, nx, nu):
        return (jnp.maximum(jnp.minimum(i, (nu[0] - 1) // 2), 0), 0)

    def bias(k):
        return lambda i, te, nx, nu: (te[2 * i + k], 0, 0)

    grid_spec = pltpu.PrefetchScalarGridSpec(
        num_scalar_prefetch=3,
        grid=(n_tiles // 2,),
        in_specs=[pl.BlockSpec((2 * tm, D), xrow),
                  pl.BlockSpec(memory_space=pl.ANY),
                  pl.BlockSpec(memory_space=pl.ANY),
                  pl.BlockSpec(memory_space=pl.ANY),
                  pl.BlockSpec((None, 1, D_FF), bias(0)),
                  pl.BlockSpec((None, 1, D_FF), bias(0)),
                  pl.BlockSpec((None, 1, D), bias(0)),
                  pl.BlockSpec((None, 1, D_FF), bias(1)),
                  pl.BlockSpec((None, 1, D_FF), bias(1)),
                  pl.BlockSpec((None, 1, D), bias(1))],
        out_specs=pl.BlockSpec((2 * tm, D), lambda i, te, nx, nu: (i, 0)),
        scratch_shapes=[pltpu.VMEM((3, D, D_FF), F32),
                        pltpu.VMEM((D, D_FF), BF16), pltpu.VMEM((D, D_FF), BF16),
                        pltpu.VMEM((D_FF, D), BF16), pltpu.SemaphoreType.DMA((3,))],
    )
    return pl.pallas_call(
        _ffn_kernel,
        grid_spec=grid_spec,
        out_shape=jax.ShapeDtypeStruct((n_rows, D), F32),
        compiler_params=_cparams(("arbitrary",)),
        name="ffn",
    )(tile_exp, next_exp, n_used, xs, w_gate, w_up, w_down, *(2 * biases))


def _combine_kernel(meta_ref, y_hbm, route_ref, x1_ref, mod_ref, gpost_ref, o_ref, stage, sems):
    i = pl.program_id(0)
    n_steps = pl.num_programs(0)
    n_seg = n_steps * N_EXP
    tm = x1_ref.shape[0]
    slot = i % 2

    def fetch(tile, into):
        def issue(e, carry):
            s = tile * N_EXP + e
            _seg_copies(y_hbm, stage.at[into], meta_ref[2 * n_seg + s], meta_ref[s],
                        meta_ref[n_seg + s], sems.at[into])
            return carry

        lax.fori_loop(0, N_EXP, issue, 0)

    @pl.when(i == 0)
    def _():
        stage[...] = jnp.zeros_like(stage)
        fetch(0, 0)

    @pl.when(i + 1 < n_steps)
    def _():
        fetch(i + 1, 1 - slot)

    _wait_rows(y_hbm, stage.at[slot], meta_ref[3 * n_seg + i], sems.at[slot])

    route = route_ref[...]
    col = lax.broadcasted_iota(jnp.int32, (tm, STAGE_ROWS), 1).astype(F32)
    wmat = jnp.zeros((tm, STAGE_ROWS), F32)
    for kk in range(TOP_K):
        hit = col == route[:, ROUTE_P + kk:ROUTE_P + kk + 1]
        wmat = jnp.where(hit, route[:, ROUTE_W + kk:ROUTE_W + kk + 1], wmat)
    moe = _dot(wmat.astype(BF16), stage[slot].astype(BF16))
    gt_f = mod_ref[...][:, 5 * D:6 * D]
    o_ref[...] = x1_ref[...] + gt_f * _rms(moe, gpost_ref[...])


def _combine_call(meta, y, route, x1, mod3, tiles_per_seq, gpost):
    n_tok = x1.shape[0]
    tm = TM_PROJ
    grid_spec = pltpu.PrefetchScalarGridSpec(
        num_scalar_prefetch=1,
        grid=(n_tok // tm,),
        in_specs=[pl.BlockSpec(memory_space=pl.ANY),
                  pl.BlockSpec((tm, LANES), lambda i, m: (i, 0)),
                  pl.BlockSpec((tm, D), lambda i, m: (i, 0)),
                  pl.BlockSpec((None, 1, 6 * D), lambda i, m: (i // tiles_per_seq, 0, 0)),
                  pl.BlockSpec((1, D), lambda i, m: (0, 0))],
        out_specs=pl.BlockSpec((tm, D), lambda i, m: (i, 0)),
        scratch_shapes=[pltpu.VMEM((2, STAGE_ROWS, D), F32), pltpu.SemaphoreType.DMA((2,))],
    )
    return pl.pallas_call(
        _combine_kernel,
        grid_spec=grid_spec,
        out_shape=jax.ShapeDtypeStruct((n_tok, D), F32),
        compiler_params=_cparams(("arbitrary",)),
        name="combine",
    )(meta, y, route, x1, mod3, gpost)


def _rope_swap(w):
    q = MLA_ROPE // 4
    return jnp.concatenate([-w[..., q:2 * q], w[..., 0:q], -w[..., 3 * q:4 * q],
                            w[..., 2 * q:3 * q]], axis=-1)


def _rope_tables(seq):
    rows = seq // GRID_W
    r, col = np.meshgrid(np.arange(rows, dtype=np.float64), np.arange(GRID_W, dtype=np.float64),
                         indexing="ij")
    half = MLA_ROPE // 2
    inv_freq = ROPE_BASE ** (-np.arange(0, half, 2, dtype=np.float64) / half)
    ar = r.reshape(-1)[:, None] * inv_freq
    ac = col.reshape(-1)[:, None] * inv_freq
    cos = np.concatenate([np.cos(ar), np.cos(ar), np.cos(ac), np.cos(ac)], axis=-1)
    sin = np.concatenate([np.sin(ar), np.sin(ar), np.sin(ac), np.sin(ac)], axis=-1)
    return cos, sin


def _head_slots(w3, lead):
    rows, heads, width = w3.shape
    return jnp.pad(w3, ((0, 0), (0, 0), (lead, HEAD_SLOT - lead - width))).reshape(
        rows, heads * HEAD_SLOT)


def kernel(x, c, ctx, c_ctx, w_mod, b_mod, g_pre_mix, g_post_mix, g_pre_ffn, g_post_ffn, w_in,
           gla_w_a2_f, gla_b_a_f, gla_w_a2_b, gla_b_a_b, gla_g_norm, mla_g_q, mla_w_uq, mla_g_kv,
           mla_w_uk, mla_w_uv, w_br_gla, w_br_mla, w_out, router_w, router_b, w_gate, b_gate,
           w_up, b_up, w_down, b_down):
    depth = w_mod.shape[0]
    assert depth == 1, "single-layer block"
    batch, seq, d = x.shape
    ctx_len = ctx.shape[1]
    assert d == D and seq % TM_PROJ == 0 and (batch * ctx_len) % TM_PROJ == 0
    assert TM_PROJ % ctx_len == 0 or ctx_len % TM_PROJ == 0
    n_tok = batch * seq

    cc = jnp.zeros((16, D), F32).at[:batch].set(c).at[batch].set(c_ctx)
    mod = _mod_call(cc, w_mod[0], b_mod[0])
    mod3 = mod.reshape(16, 1, 6 * D)

    wi = w_in[0]
    kr = wi[:, O_KR:O_KR + MLA_ROPE]
    small = jnp.concatenate([wi[:, O_AF:O_AF + GLA_RANK], wi[:, O_AB:O_AB + GLA_RANK], kr,
                             _rope_swap(kr), jnp.zeros((D, LANES - X_KRS - MLA_ROPE), F32)],
                            axis=1)
    w1 = jnp.concatenate([wi[:, O_Q:O_G], wi[:, O_DKV:O_DKV + MLA_KVR], small], axis=1).astype(BF16)
    w2 = jnp.concatenate([wi[:, O_G:O_G + GLA_V], wi[:, O_DQ:O_DQ + MLA_QR], wi[:, O_MG:]],
                         axis=1).astype(BF16)
    wa = jnp.zeros((LANES, 2 * GLA_QK), F32)
    wa = wa.at[X_AF:X_AF + GLA_RANK, 0:GLA_QK].set(gla_w_a2_f[0])
    wa = wa.at[X_AB:X_AB + GLA_RANK, GLA_QK:].set(gla_w_a2_b[0]).astype(BF16)
    ba = jnp.concatenate([gla_b_a_f[0], gla_b_a_b[0]]).reshape(1, 2 * GLA_QK)

    uk = mla_w_uk[0]
    uv = mla_w_uv[0]
    uq = mla_w_uq[0]
    wk_top = _head_slots(uk.reshape(MLA_KVR, MLA_H, MLA_NOPE), 0)
    place = np.zeros((LANES, MLA_H, HEAD_SLOT), np.float32)
    for j in range(MLA_ROPE):
        place[X_KR + j, :, MLA_NOPE + j] = 1.0
        place[X_KRS + j, :, MLA_NOPE + j] = 1.0
    wk = jnp.concatenate([wk_top, jnp.asarray(place.reshape(LANES, MLA_W))],
                         axis=0).astype(BF16)
    uv4 = uv.reshape(MLA_KVR, MLA_H // 2, 2, MLA_V)
    zv = jnp.zeros_like(uv4[:, :, 0])
    wv = jnp.stack([jnp.concatenate([uv4[:, :, 0], zv], axis=-1),
                    jnp.concatenate([zv, uv4[:, :, 1]], axis=-1)],
                   axis=2).reshape(MLA_KVR, MLA_W).astype(BF16)
    uq3 = uq.reshape(MLA_QR, MLA_H, MLA_QKD)
    wq_a = _head_slots(uq3, 0)
    wq_b = _head_slots(_rope_swap(uq3[:, :, MLA_NOPE:]), MLA_NOPE)
    wq = jnp.concatenate([wq_a, wq_b], axis=1).astype(BF16)

    cos, sin = _rope_tables(seq)
    zeros32 = np.zeros((seq, MLA_ROPE))
    t1_x = jnp.asarray(np.concatenate([zeros32, cos, sin, zeros32], axis=1), F32)
    t1_c = jnp.asarray(np.broadcast_to(
        np.concatenate([np.zeros(MLA_ROPE), np.ones(MLA_ROPE), np.zeros(2 * MLA_ROPE)]),
        (TM_PROJ, LANES)), F32)
    q_scale = MLA_SCALE * LOG2_E
    cq = jnp.asarray(np.concatenate([np.ones((seq, MLA_NOPE)), cos, zeros32], axis=1) * q_scale, F32)
    sq = jnp.asarray(np.concatenate([np.zeros((seq, MLA_NOPE)), sin, zeros32], axis=1) * q_scale,
                     F32)

    gpre = g_pre_mix[0].reshape(1, D)
    gkv = mla_g_kv[0].reshape(1, MLA_KVR)
    gq = mla_g_q[0].reshape(1, MLA_QR)
    tiles_per_seq = seq // TM_PROJ

    xf = x.reshape(n_tok, D)
    cf = ctx.reshape(batch * ctx_len, D)
    (gqx, gkx, gvx, lfx, lbx, mkx, mvx, sz, mq, sg, sm) = _inproj_call(
        True, xf, mod3, lambda i: i // tiles_per_seq, tiles_per_seq, gpre, w1, wa, ba, wk, wv,
        gkv, t1_x, extra=(w2, gq, wq, cq, sq))
    (gqc, gkc, gvc, lfc, lbc, mkc, mvc) = _inproj_call(
        False, cf, mod3, lambda i: batch, 1, gpre, w1, wa, ba, wk, wv, gkv, t1_c)

    og = _gla_call((gqx, gkx, gvx, lfx, lbx), (gqc, gkc, gvc, lfc, lbc), batch, seq, ctx_len)
    om = _mla_call(mq, mkx, mvx, mkc, mvc, batch, seq, ctx_len)

    wr = jnp.pad(router_w[0], ((0, 0), (0, LANES - N_EXP)))
    wrh = wr.astype(BF16)
    wr2 = jnp.concatenate([wrh, (wr - wrh.astype(F32)).astype(BF16)], axis=1)
    br =jnp.pad(router_b[0], (0, LANES - N_EXP)).reshape(1, LANES)
    x1, h2, route, cnt = _mixout_call(
        og, sz, om, sg, sm, xf, mod3, tiles_per_seq, gla_g_norm[0].reshape(1, GLA_DV),
        w_br_gla[0].astype(BF16), w_br_mla[0].astype(BF16), w_out[0].astype(BF16),
        g_post_mix[0].reshape(1, D), g_pre_ffn[0].reshape(1, D), wr2, br)

    seg = -(-cnt[:, 0, :N_EXP].astype(jnp.int32) // SUBLANES) * SUBLANES
    stage_start = jnp.cumsum(seg, axis=1) - seg
    in_group = jnp.cumsum(seg, axis=0) - seg
    group = jnp.sum(seg, axis=0)
    padded = -(-group // TM_FFN) * TM_FFN
    ends = jnp.cumsum(padded)
    starts = ends - padded
    sorted_row = starts[None, :] + in_group
    meta = jnp.concatenate([stage_start.reshape(-1), seg.reshape(-1), sorted_row.reshape(-1),
                            jnp.sum(seg, axis=1)]).astype(jnp.int32)
    pad_info = jnp.concatenate([starts + group, ends[-1:], ends]).astype(jnp.int32)
    n_rows = _sorted_rows(n_tok)
    tile_start = jnp.arange(n_rows // TM_FFN, dtype=jnp.int32) * TM_FFN
    tile_exp = jnp.minimum(jnp.sum((ends[None, :] <= tile_start[:, None]).astype(jnp.int32), axis=1),
                           N_EXP - 1)
    n_used = (ends[-1:] // TM_FFN).astype(jnp.int32)
    ids = jnp.arange(N_EXP, dtype=jnp.int32)
    later = jnp.logical_and(padded[None, :] > 0, ids[None, :] > ids[:, None])
    next_owner = jnp.min(jnp.where(later, ids[None, :], N_EXP), axis=1)
    next_owner = jnp.where(next_owner == N_EXP, -1, next_owner)
    next_exp = jnp.sum(jnp.where(tile_exp[:, None] == ids[None, :], next_owner[None, :], 0),
                       axis=1).astype(jnp.int32)

    xs = _dispatch_call(meta, pad_info, route, h2, n_rows)
    y = _ffn_call(tile_exp, next_exp, n_used, xs, w_gate[0], w_up[0], w_down[0], b_gate[0], b_up[0],
                  b_down[0])
    out = _combine_call(meta, y, route, x1, mod3, tiles_per_seq, g_post_ffn[0].reshape(1, D))
    return out.reshape(batch, seq, D)
```

```python
import functools

import jax
import jax.numpy as jnp
import numpy as np
from jax import lax
from jax.experimental import pallas as pl
from jax.experimental.pallas import tpu as pltpu

F32 = jnp.float32
BF16 = jnp.bfloat16

D = 1024
EPS = 1e-6
GRID_W = 64

GLA_H = 4
GLA_DK = 64
GLA_DV = 128
GLA_RANK = 16
GLA_TAU = 16.0
GLA_CHUNK = 64
GLA_BLOCK = 256
GLA_QK = GLA_H * GLA_DK
GLA_V = GLA_H * GLA_DV

MLA_H = 8
MLA_QR = 256
MLA_KVR = 128
MLA_NOPE = 64
MLA_ROPE = 32
MLA_V = 64
MLA_QKD = MLA_NOPE + MLA_ROPE
MLA_SCALE = MLA_QKD ** -0.5
LOG2_E = 1.4426950408889634
ROPE_BASE = 10000.0
HEAD_SLOT = 128
MLA_W = MLA_H * HEAD_SLOT
SUM_LANE_EVEN, SUM_LANE_ODD = MLA_V, 0

N_EXP = 32
TOP_K = 4
D_FF = 1024
SWIGLU_LIMIT = 7.0
SWIGLU_ALPHA = 1.702

LANES = 128
SUBLANES = 8
TM_PROJ = 512
TQ = 1024
TM_FFN = 512
ZERO_ROWS = 256
LONG_SEGMENT = 128
STAGE_ROWS = -(-(TM_PROJ * TOP_K + N_EXP * (SUBLANES - 1)) // 256) * 256
ROUTE_E, ROUTE_W, ROUTE_P = 0, TOP_K, 2 * TOP_K

_OFF = np.cumsum([0, GLA_QK, GLA_QK, GLA_V, GLA_V, GLA_RANK, GLA_RANK,
                  MLA_QR, MLA_KVR, MLA_ROPE, D, D])
(O_Q, O_K, O_V, O_G, O_AF, O_AB, O_DQ, O_DKV, O_KR, O_MG, O_MM, _) = _OFF.tolist()

X_AF, X_AB, X_KR, X_KRS = 0, 16, 32, 64

VMEM_LIMIT = 56 * 1024 * 1024


def _cparams(sem):
    return pltpu.CompilerParams(dimension_semantics=sem, vmem_limit_bytes=VMEM_LIMIT)


def _resident(arr):
    nd = arr.ndim
    return pl.BlockSpec(arr.shape, lambda *_: (0,) * nd, pipeline_mode=pl.Buffered(1))


def _rms(x, g):
    return x * lax.rsqrt(jnp.mean(x * x, axis=-1, keepdims=True) + EPS) * g


def _sigmoid(x):
    return 1.0 / (1.0 + jnp.exp(-x))


def _log_sigmoid(x):
    return jnp.minimum(x, 0.0) - jnp.log1p(jnp.exp(-jnp.abs(x)))


def _dot(a, b):
    return jnp.dot(a, b, preferred_element_type=F32)


def _dot_nt(a, b):
    return lax.dot_general(a, b, (((1,), (1,)), ((), ())), preferred_element_type=F32)


def _dot_tn(a, b):
    return lax.dot_general(a, b, (((0,), (0,)), ((), ())), preferred_element_type=F32)


def _mod_kernel(c_ref, w_ref, b_ref, o_ref):
    c = c_ref[...]
    s = (c * _sigmoid(c)).astype(BF16)
    o_ref[...] = _dot(s, w_ref[...].astype(BF16)) + b_ref[...]


def _mod_call(cc, w_mod, b_mod):
    n = w_mod.shape[1]
    tn = 1536
    return pl.pallas_call(
        _mod_kernel,
        grid=(n // tn,),
        in_specs=[pl.BlockSpec((16, D), lambda j: (0, 0)),
                  pl.BlockSpec((D, tn), lambda j: (0, j)),
                  pl.BlockSpec((1, tn), lambda j: (0, j))],
        out_specs=pl.BlockSpec((16, tn), lambda j: (0, j)),
        out_shape=jax.ShapeDtypeStruct((16, n), F32),
        compiler_params=_cparams(("arbitrary",)),
        name="mod",
    )(cc, w_mod, b_mod.reshape(1, n))


def _inproj_kernel(with_q, x_ref, mod_ref, gpre_ref, w1_ref, wa_ref, ba_ref, wk_ref,
                   wv_ref, gkv_ref, t1_ref, *rest):
    if with_q:
        (w2_ref, gq_ref, wq_ref, cq_ref, sq_ref,
         q_o, k_o, v_o, lf_o, lb_o, mk_o, mv_o, sz_o, mq_o, sg_o, sm_o) = rest
    else:
        (q_o, k_o, v_o, lf_o, lb_o, mk_o, mv_o) = rest
    x = x_ref[...]
    mod = mod_ref[...]
    sh = mod[:, 0:D]
    sc = mod[:, D:2 * D]
    h = (_rms(x, gpre_ref[...]) * (1.0 + sc) + sh).astype(BF16)

    z1 = _dot(h, w1_ref[...])
    q_o[...] = (z1[:, 0:GLA_QK] * (GLA_DK ** -0.5)).astype(BF16)
    k_o[...] = z1[:, GLA_QK:2 * GLA_QK].astype(BF16)
    v_o[...] = z1[:, 2 * GLA_QK:2 * GLA_QK + GLA_V].astype(BF16)
    o_dkv = 2 * GLA_QK + GLA_V
    ckv = _rms(z1[:, o_dkv:o_dkv + MLA_KVR], gkv_ref[...])
    xs = z1[:, o_dkv + MLA_KVR:o_dkv + MLA_KVR + LANES]

    la = _log_sigmoid(_dot(xs.astype(BF16), wa_ref[...]) + ba_ref[...]) * (1.0 / GLA_TAU)
    lf_o[...] = la[:, 0:GLA_QK]
    lb_o[...] = la[:, GLA_QK:2 * GLA_QK]

    lhs_k = jnp.concatenate([ckv, xs * t1_ref[...]], axis=-1).astype(BF16)
    mk_o[...] = _dot(lhs_k, wk_ref[...]).astype(BF16)
    lane = lax.broadcasted_iota(jnp.int32, (x.shape[0], MLA_W), 1)
    mv = _dot(ckv.astype(BF16), wv_ref[...])
    mv_o[...] = jnp.where(lane % (2 * HEAD_SLOT) == SUM_LANE_EVEN, 1.0,
                          jnp.where(lane % (2 * HEAD_SLOT) == HEAD_SLOT + SUM_LANE_ODD, 1.0,
                                    mv)).astype(BF16)

    if with_q:
        zg = _dot(h, w2_ref[:, 0:GLA_V])
        sz_o[...] = (zg * _sigmoid(zg)).astype(BF16)
        n = _rms(_dot(h, w2_ref[:, GLA_V:GLA_V + MLA_QR]), gq_ref[...]).astype(BF16)
        cq = jnp.concatenate([cq_ref[...]] * MLA_H, axis=-1)
        sq = jnp.concatenate([sq_ref[...]] * MLA_H, axis=-1)
        mq_o[...] = (_dot(n, wq_ref[:, 0:MLA_W]) * cq
                     + _dot(n, wq_ref[:, MLA_W:2 * MLA_W]) * sq).astype(BF16)
        o_mg = GLA_V + MLA_QR
        sg_o[...] = _sigmoid(_dot(h, w2_ref[:, o_mg:o_mg + D])).astype(BF16)
        sm_o[...] = _sigmoid(_dot(h, w2_ref[:, o_mg + D:o_mg + 2 * D])).astype(BF16)


def _inproj_call(with_q, xf, mod3, mod_row_fn, tiles_per_seq, gpre, w1, wa, ba, wk, wv,
                 gkv, t1, extra=()):
    n_tok = xf.shape[0]
    tm = TM_PROJ
    grid = (n_tok // tm,)
    row = lambda i: (i, 0)
    tab = lambda i: (i % tiles_per_seq, 0)
    in_specs = [
        pl.BlockSpec((tm, D), row),
        pl.BlockSpec((None, 1, 6 * D), lambda i: (mod_row_fn(i), 0, 0)),
        _resident(gpre), _resident(w1), _resident(wa), _resident(ba), _resident(wk),
        _resident(wv), _resident(gkv),
        pl.BlockSpec((tm, LANES), tab),
    ]
    widths = [(GLA_QK, BF16), (GLA_QK, BF16), (GLA_V, BF16), (GLA_QK, F32), (GLA_QK, F32),
              (MLA_W, BF16), (MLA_W, BF16)]
    args = [xf, mod3, gpre, w1, wa, ba, wk, wv, gkv, t1]
    if with_q:
        w2, gq, wq, cq, sq = extra
        in_specs += [_resident(w2), _resident(gq), _resident(wq),
                     pl.BlockSpec((tm, HEAD_SLOT), tab), pl.BlockSpec((tm, HEAD_SLOT), tab)]
        args += [w2, gq, wq, cq, sq]
        widths += [(GLA_V, BF16), (MLA_W, BF16), (D, BF16), (D, BF16)]
    return pl.pallas_call(
        functools.partial(_inproj_kernel, with_q),
        grid=grid,
        in_specs=in_specs,
        out_specs=[pl.BlockSpec((tm, w), row) for w, _ in widths],
        out_shape=[jax.ShapeDtypeStruct((n_tok, w), dt) for w, dt in widths],
        compiler_params=_cparams(("arbitrary",)),
        name="inproj_x" if with_q else "inproj_ctx",
    )(*args)


def _gla_kernel(qx, kx, vx, lfx, lbx, qc, kc, vc, lfc, lbc, o_ref, sf_ref, sb_ref):
    C = GLA_CHUNK
    R = GLA_BLOCK
    n_sub = R // C
    rr = lax.broadcasted_iota(jnp.int32, (R, R), 0)
    cc = lax.broadcasted_iota(jnp.int32, (R, R), 1)
    same = (rr // C) == (cc // C)
    tbd_f = jnp.logical_and(same, cc <= rr).astype(BF16)
    tbd_b = jnp.logical_and(same, cc >= rr).astype(BF16)
    row = lax.broadcasted_iota(jnp.int32, (C, GLA_QK), 0)
    col = lax.broadcasted_iota(jnp.int32, (C, GLA_QK), 1)
    head_qk = col // GLA_DK
    tri4_f = (col % C) <= row
    tri4_b = (col % C) >= row
    head_v = lax.broadcasted_iota(jnp.int32, (C, GLA_V), 1) // GLA_DV

    def stack_masked(x, head_of_lane):
        return jnp.concatenate(
            [jnp.where(head_of_lane == hh, x, jnp.zeros_like(x)) for hh in range(GLA_H)], axis=0)

    def block(q_ref, k_ref, v_ref, la_ref, row0, fwd, s_ref, emit):
        sl = pl.ds(row0, R)
        la = la_ref[sl, :]
        la_hi = la.astype(BF16)
        la_lo = (la - la_hi.astype(F32)).astype(BF16)
        tbd = tbd_f if fwd else tbd_b
        cum = _dot(tbd, la_hi) + _dot(tbd, la_lo)
        k = k_ref[sl, :].astype(F32)
        v = v_ref[sl, :]
        if emit:
            q = q_ref[sl, :].astype(F32)
        st = s_ref[...]
        outs = [None] * n_sub
        for ci in (range(n_sub) if fwd else reversed(range(n_sub))):
            rs = slice(ci * C, (ci + 1) * C)
            cum_c = cum[rs]
            tot = cum_c[C - 1:C, :] if fwd else cum_c[0:1, :]
            k_c = k[rs]
            v_c = v[rs]
            ke_bd = stack_masked((k_c * jnp.exp(tot - cum_c)).astype(BF16), head_qk)
            v_stack = jnp.concatenate(
                [v_c[:, hh * GLA_DV:(hh + 1) * GLA_DV] for hh in range(GLA_H)], axis=0)
            if emit:
                qd = (q[rs] * jnp.exp(cum_c)).astype(BF16)
                ki_bd = stack_masked((k_c * jnp.exp(-cum_c)).astype(BF16), head_qk)
                att = _dot_nt(qd, ki_bd)
                att = jnp.where(tri4_f if fwd else tri4_b, att, 0.0).astype(BF16)
                o_intra = _dot(att, stack_masked(v_c, head_v))
                oi = _dot_nt(stack_masked(qd, head_qk), st.astype(BF16))
                o_inter = jnp.concatenate([oi[hh * C:(hh + 1) * C] for hh in range(GLA_H)],
                                          axis=-1)
                outs[ci] = o_intra + o_inter
            st = st * jnp.exp(tot) + _dot_tn(v_stack, ke_bd)
        s_ref[...] = st
        if emit:
            o_ref[sl, :] += jnp.concatenate(outs, axis=0)

    sf_ref[...] = jnp.zeros_like(sf_ref)
    sb_ref[...] = jnp.zeros_like(sb_ref)
    o_ref[...] = jnp.zeros_like(o_ref)
    n_ctx = qc.shape[0] // R
    n_x = qx.shape[0] // R
    for i in range(n_ctx):
        block(qc, kc, vc, lfc, i * R, True, sf_ref, False)
        block(qc, kc, vc, lbc, (n_ctx - 1 - i) * R, False, sb_ref, False)

    def body(i, carry):
        block(qx, kx, vx, lfx, pl.multiple_of(i * R, R), True, sf_ref, True)
        block(qx, kx, vx, lbx, pl.multiple_of((n_x - 1 - i) * R, R), False, sb_ref, True)
        return carry

    lax.fori_loop(0, n_x, body, 0)


def _gla_call(fx, fc, batch, seq, ctx_len):
    qx, kx, vx, lfx, lbx = fx
    qc, kc, vc, lfc, lbc = fc

    def spec(rows, w):
        return pl.BlockSpec((rows, w), lambda b: (b, 0))

    return pl.pallas_call(
        _gla_kernel,
        grid=(batch,),
        in_specs=[spec(seq, GLA_QK), spec(seq, GLA_QK), spec(seq, GLA_V), spec(seq, GLA_QK),
                  spec(seq, GLA_QK),
                  spec(ctx_len, GLA_QK), spec(ctx_len, GLA_QK), spec(ctx_len, GLA_V),
                  spec(ctx_len, GLA_QK), spec(ctx_len, GLA_QK)],
        out_specs=spec(seq, GLA_V),
        out_shape=jax.ShapeDtypeStruct((batch * seq, GLA_V), F32),
        scratch_shapes=[pltpu.VMEM((GLA_DV, GLA_QK), F32), pltpu.VMEM((GLA_DV, GLA_QK), F32)],
        compiler_params=_cparams(("arbitrary",)),
        name="gla",
    )(qx, kx, vx, lfx, lbx, qc, kc, vc, lfc, lbc)


def _mla_kernel(q_ref, kx_ref, vx_ref, kc_ref, vc_ref, o_ref):
    lane = lax.broadcasted_iota(jnp.int32, (q_ref.shape[0], HEAD_SLOT), 1)
    for j in range(MLA_H // 2):
        pair = []
        for hh, sum_lane in ((2 * j, SUM_LANE_EVEN), (2 * j + 1, SUM_LANE_ODD)):
            sl = slice(hh * HEAD_SLOT, (hh + 1) * HEAD_SLOT)
            q = q_ref[:, sl]
            sx = _dot_nt(q, kx_ref[:, sl])
            sc = _dot_nt(q, kc_ref[:, sl])
            m = jnp.maximum(jnp.max(sx, axis=-1, keepdims=True),
                            jnp.max(sc, axis=-1, keepdims=True))
            px = jnp.exp2(sx - m).astype(BF16)
            pc = jnp.exp2(sc - m).astype(BF16)
            o = _dot(px, vx_ref[:, sl]) + _dot(pc, vc_ref[:, sl])
            pair.append(o / o[:, sum_lane:sum_lane + 1])
        o_ref[:, j * HEAD_SLOT:(j + 1) * HEAD_SLOT] = jnp.where(
            lane < MLA_V, pair[0], pair[1]).astype(BF16)


def _mla_call(mq, mkx, mvx, mkc, mvc, batch, seq, ctx_len):
    nq = seq // TQ
    return pl.pallas_call(
        _mla_kernel,
        grid=(batch, nq),
        in_specs=[pl.BlockSpec((TQ, MLA_W), lambda b, i: (b * nq + i, 0)),
                  pl.BlockSpec((seq, MLA_W), lambda b, i: (b, 0)),
                  pl.BlockSpec((seq, MLA_W), lambda b, i: (b, 0)),
                  pl.BlockSpec((ctx_len, MLA_W), lambda b, i: (b, 0)),
                  pl.BlockSpec((ctx_len, MLA_W), lambda b, i: (b, 0))],
        out_specs=pl.BlockSpec((TQ, MLA_H * MLA_V), lambda b, i: (b * nq + i, 0)),
        out_shape=jax.ShapeDtypeStruct((batch * seq, MLA_H * MLA_V), BF16),
        compiler_params=_cparams(("arbitrary", "arbitrary")),
        name="mla",
    )(mq, mkx, mvx, mkc, mvc)


def _mixout_kernel(og_ref, sz_ref, om_ref, sg_ref, sm_ref, x_ref, mod_ref, gn_ref, wbg_ref,
                   wbm_ref, wo_ref, gpost_ref, gffn_ref, wr_ref, br_ref,
                   x1_o, h2_o, route_o, cnt_o):
    tm = x_ref.shape[0]
    mod = mod_ref[...]
    gt_a = mod[:, 2 * D:3 * D]
    sh_f = mod[:, 3 * D:4 * D]
    sc_f = mod[:, 4 * D:5 * D]

    og = og_ref[...]
    gn = gn_ref[...]
    parts = [_rms(og[:, hh * GLA_DV:(hh + 1) * GLA_DV], gn) for hh in range(GLA_H)]
    a = (jnp.concatenate(parts, axis=-1) * sz_ref[...].astype(F32)).astype(BF16)
    br_g = _dot(a, wbg_ref[...])
    br_m = _dot(om_ref[...], wbm_ref[...])
    merged = (sg_ref[...].astype(F32) * br_g + sm_ref[...].astype(F32) * br_m).astype(BF16)
    mo = _dot(merged, wo_ref[...])
    x1 = x_ref[...] + gt_a * _rms(mo, gpost_ref[...])
    x1_o[...] = x1
    h2 = _rms(x1, gffn_ref[...]) * (1.0 + sc_f) + sh_f
    h2_o[...] = h2.astype(BF16)

    lane = lax.broadcasted_iota(jnp.int32, (tm, LANES), 1)
    h_hi = h2.astype(BF16)
    h_lo = (h2 - h_hi.astype(F32)).astype(BF16)
    part = _dot(h_hi, wr_ref[...]) + _dot(h_lo, wr_ref[...])
    logits = part[:, 0:LANES] + part[:, LANES:2 * LANES] + br_ref[...]
    neg = jnp.float32(-jnp.inf)
    lg = jnp.where(lane < N_EXP, logits, neg)
    lane_f = lane.astype(F32)
    hots, vals = [], []
    for _k in range(TOP_K):
        mx = jnp.max(lg, axis=-1, keepdims=True)
        idx = jnp.min(jnp.where(lg == mx, lane_f, float(LANES)), axis=-1, keepdims=True)
        hot = lane_f == idx
        lg = jnp.where(hot, neg, lg)
        hots.append(hot)
        vals.append(mx)
    es = [jnp.exp(v - vals[0]) for v in vals]
    den = es[0] + es[1] + es[2] + es[3]
    ws = [e / den for e in es]

    msum = jnp.zeros((tm, LANES), F32)
    for hot in hots:
        msum = msum + hot.astype(F32)
    rr = lax.broadcasted_iota(jnp.int32, (tm, tm), 0)
    cc = lax.broadcasted_iota(jnp.int32, (tm, tm), 1)
    lower = (cc < rr).astype(BF16)
    prior = _dot(lower, msum.astype(BF16))
    cnt = jnp.sum(msum, axis=0, keepdims=True)
    seg = jnp.floor((cnt + (SUBLANES - 1.0)) * (1.0 / SUBLANES)) * SUBLANES
    er = lax.broadcasted_iota(jnp.int32, (LANES, LANES), 0)
    ec = lax.broadcasted_iota(jnp.int32, (LANES, LANES), 1)
    before = (er < ec).astype(BF16)
    seg_start = _dot(jnp.broadcast_to(seg, (SUBLANES, LANES)).astype(BF16), before)[0:1, :]
    slot_of = prior + seg_start

    route = jnp.zeros((tm, LANES), F32)
    for kk in range(TOP_K):
        hotf = hots[kk].astype(F32)
        e_col = jnp.sum(hotf * lane_f, axis=-1, keepdims=True)
        p_col = jnp.sum(hotf * slot_of, axis=-1, keepdims=True)
        route = jnp.where(lane == ROUTE_E + kk, e_col, route)
        route = jnp.where(lane == ROUTE_W + kk, ws[kk], route)
        route = jnp.where(lane == ROUTE_P + kk, p_col, route)
    route_o[...] = route
    cnt_o[...] = jnp.broadcast_to(cnt, cnt_o.shape)


def _mixout_call(og, sz, om, sg, sm, xf, mod3, tiles_per_seq, gn, wbg, wbm, wo, gpost, gffn,
                 wr, br):
    n_tok = xf.shape[0]
    tm = TM_PROJ
    row = lambda i: (i, 0)

    def rs(w):
        return pl.BlockSpec((tm, w), row)

    return pl.pallas_call(
        _mixout_kernel,
        grid=(n_tok // tm,),
        in_specs=[rs(GLA_V), rs(GLA_V), rs(MLA_H * MLA_V), rs(D), rs(D), rs(D),
                  pl.BlockSpec((None, 1, 6 * D), lambda i: (i // tiles_per_seq, 0, 0)),
                  _resident(gn), _resident(wbg), _resident(wbm), _resident(wo),
                  _resident(gpost), _resident(gffn), _resident(wr), _resident(br)],
        out_specs=[rs(D), rs(D), rs(LANES),
                   pl.BlockSpec((None, SUBLANES, LANES), lambda i: (i, 0, 0))],
        out_shape=[jax.ShapeDtypeStruct((n_tok, D), F32), jax.ShapeDtypeStruct((n_tok, D), BF16),
                   jax.ShapeDtypeStruct((n_tok, LANES), F32),
                   jax.ShapeDtypeStruct((n_tok // tm, SUBLANES, LANES), F32)],
        compiler_params=_cparams(("arbitrary",)),
        name="mixout",
    )(og, sz, om, sg, sm, xf, mod3, gn, wbg, wbm, wo, gpost, gffn, wr, br)


def _sorted_rows(n_tok):
    n_steps = n_tok // TM_PROJ
    worst_pad = n_steps * N_EXP * (SUBLANES - 1) + N_EXP * TM_FFN
    per_step = -(-worst_pad // (n_steps * ZERO_ROWS)) * ZERO_ROWS
    n_rows = n_tok * TOP_K + per_step * n_steps
    assert n_rows % TM_FFN == 0
    return n_rows


def _seg_copies(src_ref, dst_ref, src0, dst0, length, sem):
    def copy_bit(b):
        size = 1 << b
        off = (length >> (b + 1)) << (b + 1)

        @pl.when((length & size) != 0)
        def _():
            pltpu.make_async_copy(
                src_ref.at[pl.ds(pl.multiple_of(src0 + off, SUBLANES), size)],
                dst_ref.at[pl.ds(pl.multiple_of(dst0 + off, SUBLANES), size)], sem).start()

    lo_bit = SUBLANES.bit_length() - 1
    hi_bit = TM_PROJ.bit_length() - 1
    split = LONG_SEGMENT.bit_length() - 1

    @pl.when(length >= LONG_SEGMENT)
    def _():
        for b in range(hi_bit, split - 1, -1):
            copy_bit(b)

    for b in range(split - 1, lo_bit - 1, -1):
        copy_bit(b)


def _wait_rows(src_ref, dst_ref, total, sem):
    for b in reversed(range(SUBLANES.bit_length() - 1, STAGE_ROWS.bit_length())):
        size = 1 << b

        @pl.when((total & size) != 0)
        def _():
            pltpu.make_async_copy(src_ref.at[pl.ds(0, size)], dst_ref.at[pl.ds(0, size)],
                                  sem).wait()


def _zero_rows(zero_ref, dst_ref, start, end, sem, wait):
    length = end - start
    for b in reversed(range(SUBLANES.bit_length() - 1, ZERO_ROWS.bit_length())):
        size = 1 << b
        off = (length >> (b + 1)) << (b + 1)

        @pl.when((length & size) != 0)
        def _():
            cp = pltpu.make_async_copy(
                zero_ref.at[pl.ds(0, size)],
                dst_ref.at[pl.ds(pl.multiple_of(start + off, SUBLANES), size)], sem)
            if wait:
                cp.wait()
            else:
                cp.start()


def _slot_onehot(route_t, n_slots):
    tm = route_t.shape[1]
    slot = lax.broadcasted_iota(jnp.int32, (n_slots, tm), 0).astype(F32)
    acc = jnp.zeros((n_slots, tm), F32)
    for kk in range(TOP_K):
        acc = jnp.where(slot == route_t[ROUTE_P + kk:ROUTE_P + kk + 1, :], 1.0, acc)
    return acc.astype(BF16)


def _dispatch_kernel(meta_ref, pad_ref, route_ref, h_ref, xs_out, stage, zero_ref, sems, zsem):
    i = pl.program_id(0)
    n_steps = pl.num_programs(0)
    n_rows = xs_out.shape[0]
    n_seg = n_steps * N_EXP
    slot = i % 2

    @pl.when(i == 0)
    def _():
        zero_ref[...] = jnp.zeros_like(zero_ref)

    def clear_uncovered(wait):
        per = -(-N_EXP // n_steps)
        for j in range(per):
            e = i * per + j

            @pl.when(e < N_EXP)
            def _():
                ec = jnp.minimum(e, N_EXP - 1)
                _zero_rows(zero_ref, xs_out, pad_ref[ec], pad_ref[N_EXP + 1 + ec], zsem, wait)

        total_end = pad_ref[N_EXP]
        tail_rows = n_rows - n_steps * TM_PROJ * TOP_K
        per_step = tail_rows // n_steps
        for j in range(per_step // ZERO_ROWS):
            seg_end = (n_rows - tail_rows) + i * per_step + (j + 1) * ZERO_ROWS
            zlen = jnp.clip(seg_end - total_end, 0, ZERO_ROWS)
            _zero_rows(zero_ref, xs_out, seg_end - zlen, seg_end, zsem, wait)

    clear_uncovered(wait=False)

    onehot = _slot_onehot(route_ref[...].T, STAGE_ROWS)
    stage[slot] = _dot(onehot, h_ref[...])

    def issue(e, carry):
        s = i * N_EXP + e
        _seg_copies(stage.at[slot], xs_out, meta_ref[s], meta_ref[2 * n_seg + s],
                    meta_ref[n_seg + s], sems.at[slot])
        return carry

    lax.fori_loop(0, N_EXP, issue, 0)
    clear_uncovered(wait=True)

    @pl.when(i >= 1)
    def _():
        _wait_rows(stage.at[1 - slot], xs_out, meta_ref[3 * n_seg + i - 1], sems.at[1 - slot])

    @pl.when(i == n_steps - 1)
    def _():
        _wait_rows(stage.at[slot], xs_out, meta_ref[3 * n_seg + i], sems.at[slot])


def _dispatch_call(meta, pad_info, route, h2, n_rows):
    n_tok = h2.shape[0]
    tm = TM_PROJ
    n_steps = n_tok // tm
    assert (n_rows - n_tok * TOP_K) % (n_steps * ZERO_ROWS) == 0 and TM_FFN <= 2 * ZERO_ROWS
    grid_spec = pltpu.PrefetchScalarGridSpec(
        num_scalar_prefetch=2,
        grid=(n_steps,),
        in_specs=[pl.BlockSpec((tm, LANES), lambda i, m, p: (i, 0)),
                  pl.BlockSpec((tm, D), lambda i, m, p: (i, 0))],
        out_specs=pl.BlockSpec(memory_space=pl.ANY),
        scratch_shapes=[pltpu.VMEM((2, STAGE_ROWS, D), F32), pltpu.VMEM((ZERO_ROWS, D), F32),
                        pltpu.SemaphoreType.DMA((2,)), pltpu.SemaphoreType.DMA(())],
    )
    return pl.pallas_call(
        _dispatch_kernel,
        grid_spec=grid_spec,
        out_shape=jax.ShapeDtypeStruct((n_rows, D), F32),
        compiler_params=_cparams(("arbitrary",)),
        name="dispatch",
    )(meta, pad_info, route, h2)


def _ffn_kernel(te_ref, nx_ref, vr_ref, nu_ref, x_ref, wg_hbm, wu_hbm, wd_hbm, bg0, bu0, bd0, bg1,
                bu1, bd1, y_ref, land, wgb, wub, wdb, sems):
    i = pl.program_id(0)
    tm = TM_FFN
    t0 = 2 * i
    t1 = t0 + 1
    e0 = te_ref[t0]
    e1 = te_ref[t1]
    used0 = t0 < nu_ref[0]
    used1 = t1 < nu_ref[0]
    first0 = jnp.logical_or(i == 0, e0 != te_ref[jnp.maximum(t0 - 1, 0)])
    same = jnp.logical_and(used1, e1 == e0)
    short1 = vr_ref[t1] <= tm // 2
    joint = jnp.logical_and(same, jnp.logical_not(short1))
    w_hbm = (wg_hbm, wu_hbm, wd_hbm)

    def fetch(expert):
        for j in range(3):
            pltpu.make_async_copy(w_hbm[j].at[expert], land.at[j], sems.at[j]).start()

    def switch(t):
        for j, dst in enumerate((wgb, wub, wdb)):
            pltpu.make_async_copy(w_hbm[j].at[0], land.at[j], sems.at[j]).wait()
            dst[...] = land[j].astype(BF16)
        nxt = nx_ref[t]

        @pl.when(nxt >= 0)
        def _():
            fetch(jnp.maximum(nxt, 0))

    def compute(rows, bg, bu, bd):
        x = x_ref[rows, :].astype(BF16)
        gate = jnp.minimum(_dot(x, wgb[...]) + bg[...], SWIGLU_LIMIT)
        up = jnp.clip(_dot(x, wub[...]) + bu[...], -SWIGLU_LIMIT, SWIGLU_LIMIT)
        act = ((up + 1.0) * gate * _sigmoid(SWIGLU_ALPHA * gate)).astype(BF16)
        y_ref[rows, :] = _dot(act, wdb[...]) + bd[...]

    lo = slice(0, tm)
    hi = slice(tm, 2 * tm)

    @pl.when(jnp.logical_and(used0, i == 0))
    def _():
        fetch(e0)

    @pl.when(jnp.logical_and(used0, first0))
    def _():
        switch(t0)

    def single(off, short, bg, bu, bd):
        @pl.when(short)
        def _():
            compute(slice(off, off + tm // 2), bg, bu, bd)
            y_ref[off + tm // 2:off + tm, :] = jnp.zeros((tm // 2, D), y_ref.dtype)

        @pl.when(jnp.logical_not(short))
        def _():
            compute(slice(off, off + tm), bg, bu, bd)

    @pl.when(joint)
    def _():
        compute(slice(0, 2 * tm), bg0, bu0, bd0)

    @pl.when(jnp.logical_and(used0, jnp.logical_not(joint)))
    def _():
        single(0, vr_ref[t0] <= tm // 2, bg0, bu0, bd0)

    @pl.when(jnp.logical_and(used1, jnp.logical_not(joint)))
    def _():
        @pl.when(jnp.logical_not(same))
        def _():
            switch(t1)

        single(tm, short1, bg1, bu1, bd1)

    @pl.when(jnp.logical_not(used0))
    def _():
        y_ref[lo, :] = jnp.zeros((tm, D), y_ref.dtype)

    @pl.when(jnp.logical_not(used1))
    def _():
        y_ref[hi, :] = jnp.zeros((tm, D), y_ref.dtype)


def _ffn_call(tile_exp, next_exp, tile_rows, n_used, xs, w_gate, w_up, w_down, b_gate, b_up,
              b_down):
    n_rows = xs.shape[0]
    tm = TM_FFN
    n_tiles = n_rows // tm
    assert D == D_FF
    biases = (b_gate.reshape(N_EXP, 1, D_FF), b_up.reshape(N_EXP, 1, D_FF),
              b_down.reshape(N_EXP, 1, D))

    assert n_tiles % 2 == 0

    def xrow(i, te, nx, vr, nu):
        return (jnp.maximum(jnp.minimum(i, (nu[0] - 1) // 2), 0), 0)

    def bias(k):
        return lambda i, te, nx, vr, nu: (te[2 * i + k], 0, 0)

    grid_spec = pltpu.PrefetchScalarGridSpec(
        num_scalar_prefetch=4,
        grid=(n_tiles // 2,),
        in_specs=[pl.BlockSpec((2 * tm, D), xrow),
                  pl.BlockSpec(memory_space=pl.ANY),
                  pl.BlockSpec(memory_space=pl.ANY),
                  pl.BlockSpec(memory_space=pl.ANY),
                  pl.BlockSpec((None, 1, D_FF), bias(0)),
                  pl.BlockSpec((None, 1, D_FF), bias(0)),
                  pl.BlockSpec((None, 1, D), bias(0)),
                  pl.BlockSpec((None, 1, D_FF), bias(1)),
                  pl.BlockSpec((None, 1, D_FF), bias(1)),
                  pl.BlockSpec((None, 1, D), bias(1))],
        out_specs=pl.BlockSpec((2 * tm, D), lambda i, te, nx, vr, nu: (i, 0)),
        scratch_shapes=[pltpu.VMEM((3, D, D_FF), F32),
                        pltpu.VMEM((D, D_FF), BF16), pltpu.VMEM((D, D_FF), BF16),
                        pltpu.VMEM((D_FF, D), BF16), pltpu.SemaphoreType.DMA((3,))],
    )
    return pl.pallas_call(
        _ffn_kernel,
        grid_spec=grid_spec,
        out_shape=jax.ShapeDtypeStruct((n_rows, D), F32),
        compiler_params=_cparams(("arbitrary",)),
        name="ffn",
    )(tile_exp, next_exp, tile_rows, n_used, xs, w_gate, w_up, w_down, *(2 * biases))


def _combine_kernel(meta_ref, y_hbm, route_ref, x1_ref, mod_ref, gpost_ref, o_ref, stage, sems):
    i = pl.program_id(0)
    n_steps = pl.num_programs(0)
    n_seg = n_steps * N_EXP
    tm = x1_ref.shape[0]
    slot = i % 2

    def fetch(tile, into):
        def issue(e, carry):
            s = tile * N_EXP + e
            _seg_copies(y_hbm, stage.at[into], meta_ref[2 * n_seg + s], meta_ref[s],
                        meta_ref[n_seg + s], sems.at[into])
            return carry

        lax.fori_loop(0, N_EXP, issue, 0)

    @pl.when(i == 0)
    def _():
        stage[...] = jnp.zeros_like(stage)
        fetch(0, 0)

    @pl.when(i + 1 < n_steps)
    def _():
        fetch(i + 1, 1 - slot)

    _wait_rows(y_hbm, stage.at[slot], meta_ref[3 * n_seg + i], sems.at[slot])

    route = route_ref[...]
    col = lax.broadcasted_iota(jnp.int32, (tm, STAGE_ROWS), 1).astype(F32)
    wmat = jnp.zeros((tm, STAGE_ROWS), F32)
    for kk in range(TOP_K):
        hit = col == route[:, ROUTE_P + kk:ROUTE_P + kk + 1]
        wmat = jnp.where(hit, route[:, ROUTE_W + kk:ROUTE_W + kk + 1], wmat)
    moe = _dot(wmat.astype(BF16), stage[slot].astype(BF16))
    gt_f = mod_ref[...][:, 5 * D:6 * D]
    o_ref[...] = x1_ref[...] + gt_f * _rms(moe, gpost_ref[...])


def _combine_call(meta, y, route, x1, mod3, tiles_per_seq, gpost):
    n_tok = x1.shape[0]
    tm = TM_PROJ
    grid_spec = pltpu.PrefetchScalarGridSpec(
        num_scalar_prefetch=1,
        grid=(n_tok // tm,),
        in_specs=[pl.BlockSpec(memory_space=pl.ANY),
                  pl.BlockSpec((tm, LANES), lambda i, m: (i, 0)),
                  pl.BlockSpec((tm, D), lambda i, m: (i, 0)),
                  pl.BlockSpec((None, 1, 6 * D), lambda i, m: (i // tiles_per_seq, 0, 0)),
                  pl.BlockSpec((1, D), lambda i, m: (0, 0))],
        out_specs=pl.BlockSpec((tm, D), lambda i, m: (i, 0)),
        scratch_shapes=[pltpu.VMEM((2, STAGE_ROWS, D), F32), pltpu.SemaphoreType.DMA((2,))],
    )
    return pl.pallas_call(
        _combine_kernel,
        grid_spec=grid_spec,
        out_shape=jax.ShapeDtypeStruct((n_tok, D), F32),
        compiler_params=_cparams(("arbitrary",)),
        name="combine",
    )(meta, y, route, x1, mod3, gpost)


def _rope_swap(w):
    q = MLA_ROPE // 4
    return jnp.concatenate([-w[..., q:2 * q], w[..., 0:q], -w[..., 3 * q:4 * q],
                            w[..., 2 * q:3 * q]], axis=-1)


def _rope_tables(seq):
    rows = seq // GRID_W
    r, col = np.meshgrid(np.arange(rows, dtype=np.float64), np.arange(GRID_W, dtype=np.float64),
                         indexing="ij")
    half = MLA_ROPE // 2
    inv_freq = ROPE_BASE ** (-np.arange(0, half, 2, dtype=np.float64) / half)
    ar = r.reshape(-1)[:, None] * inv_freq
    ac = col.reshape(-1)[:, None] * inv_freq
    cos = np.concatenate([np.cos(ar), np.cos(ar), np.cos(ac), np.cos(ac)], axis=-1)
    sin = np.concatenate([np.sin(ar), np.sin(ar), np.sin(ac), np.sin(ac)], axis=-1)
    return cos, sin


def _head_slots(w3, lead):
    rows, heads, width = w3.shape
    return jnp.pad(w3, ((0, 0), (0, 0), (lead, HEAD_SLOT - lead - width))).reshape(
        rows, heads * HEAD_SLOT)


def kernel(x, c, ctx, c_ctx, w_mod, b_mod, g_pre_mix, g_post_mix, g_pre_ffn, g_post_ffn, w_in,
           gla_w_a2_f, gla_b_a_f, gla_w_a2_b, gla_b_a_b, gla_g_norm, mla_g_q, mla_w_uq, mla_g_kv,
           mla_w_uk, mla_w_uv, w_br_gla, w_br_mla, w_out, router_w, router_b, w_gate, b_gate,
           w_up, b_up, w_down, b_down):
    depth = w_mod.shape[0]
    assert depth == 1, "single-layer block"
    batch, seq, d = x.shape
    ctx_len = ctx.shape[1]
    assert d == D and seq % TM_PROJ == 0 and (batch * ctx_len) % TM_PROJ == 0
    assert TM_PROJ % ctx_len == 0 or ctx_len % TM_PROJ == 0
    n_tok = batch * seq

    cc = jnp.zeros((16, D), F32).at[:batch].set(c).at[batch].set(c_ctx)
    mod = _mod_call(cc, w_mod[0], b_mod[0])
    mod3 = mod.reshape(16, 1, 6 * D)

    wi = w_in[0]
    kr = wi[:, O_KR:O_KR + MLA_ROPE]
    small = jnp.concatenate([wi[:, O_AF:O_AF + GLA_RANK], wi[:, O_AB:O_AB + GLA_RANK], kr,
                             _rope_swap(kr), jnp.zeros((D, LANES - X_KRS - MLA_ROPE), F32)],
                            axis=1)
    w1 = jnp.concatenate([wi[:, O_Q:O_G], wi[:, O_DKV:O_DKV + MLA_KVR], small], axis=1).astype(BF16)
    w2 = jnp.concatenate([wi[:, O_G:O_G + GLA_V], wi[:, O_DQ:O_DQ + MLA_QR], wi[:, O_MG:]],
                         axis=1).astype(BF16)
    wa = jnp.zeros((LANES, 2 * GLA_QK), F32)
    wa = wa.at[X_AF:X_AF + GLA_RANK, 0:GLA_QK].set(gla_w_a2_f[0])
    wa = wa.at[X_AB:X_AB + GLA_RANK, GLA_QK:].set(gla_w_a2_b[0]).astype(BF16)
    ba = jnp.concatenate([gla_b_a_f[0], gla_b_a_b[0]]).reshape(1, 2 * GLA_QK)

    uk = mla_w_uk[0]
    uv = mla_w_uv[0]
    uq = mla_w_uq[0]
    wk_top = _head_slots(uk.reshape(MLA_KVR, MLA_H, MLA_NOPE), 0)
    place = np.zeros((LANES, MLA_H, HEAD_SLOT), np.float32)
    for j in range(MLA_ROPE):
        place[X_KR + j, :, MLA_NOPE + j] = 1.0
        place[X_KRS + j, :, MLA_NOPE + j] = 1.0
    wk = jnp.concatenate([wk_top, jnp.asarray(place.reshape(LANES, MLA_W))],
                         axis=0).astype(BF16)
    uv4 = uv.reshape(MLA_KVR, MLA_H // 2, 2, MLA_V)
    zv = jnp.zeros_like(uv4[:, :, 0])
    wv = jnp.stack([jnp.concatenate([uv4[:, :, 0], zv], axis=-1),
                    jnp.concatenate([zv, uv4[:, :, 1]], axis=-1)],
                   axis=2).reshape(MLA_KVR, MLA_W).astype(BF16)
    uq3 = uq.reshape(MLA_QR, MLA_H, MLA_QKD)
    wq_a = _head_slots(uq3, 0)
    wq_b = _head_slots(_rope_swap(uq3[:, :, MLA_NOPE:]), MLA_NOPE)
    wq = jnp.concatenate([wq_a, wq_b], axis=1).astype(BF16)

    cos, sin = _rope_tables(seq)
    zeros32 = np.zeros((seq, MLA_ROPE))
    t1_x = jnp.asarray(np.concatenate([zeros32, cos, sin, zeros32], axis=1), F32)
    t1_c = jnp.asarray(np.broadcast_to(
        np.concatenate([np.zeros(MLA_ROPE), np.ones(MLA_ROPE), np.zeros(2 * MLA_ROPE)]),
        (TM_PROJ, LANES)), F32)
    q_scale = MLA_SCALE * LOG2_E
    cq = jnp.asarray(np.concatenate([np.ones((seq, MLA_NOPE)), cos, zeros32], axis=1) * q_scale, F32)
    sq = jnp.asarray(np.concatenate([np.zeros((seq, MLA_NOPE)), sin, zeros32], axis=1) * q_scale,
                     F32)

    gpre = g_pre_mix[0].reshape(1, D)
    gkv = mla_g_kv[0].reshape(1, MLA_KVR)
    gq = mla_g_q[0].reshape(1, MLA_QR)
    tiles_per_seq = seq // TM_PROJ

    xf = x.reshape(n_tok, D)
    cf = ctx.reshape(batch * ctx_len, D)
    (gqx, gkx, gvx, lfx, lbx, mkx, mvx, sz, mq, sg, sm) = _inproj_call(
        True, xf, mod3, lambda i: i // tiles_per_seq, tiles_per_seq, gpre, w1, wa, ba, wk, wv,
        gkv, t1_x, extra=(w2, gq, wq, cq, sq))
    (gqc, gkc, gvc, lfc, lbc, mkc, mvc) = _inproj_call(
        False, cf, mod3, lambda i: batch, 1, gpre, w1, wa, ba, wk, wv, gkv, t1_c)

    og = _gla_call((gqx, gkx, gvx, lfx, lbx), (gqc, gkc, gvc, lfc, lbc), batch, seq, ctx_len)
    om = _mla_call(mq, mkx, mvx, mkc, mvc, batch, seq, ctx_len)

    wr = jnp.pad(router_w[0], ((0, 0), (0, LANES - N_EXP)))
    wrh = wr.astype(BF16)
    wr2 = jnp.concatenate([wrh, (wr - wrh.astype(F32)).astype(BF16)], axis=1)
    br =jnp.pad(router_b[0], (0, LANES - N_EXP)).reshape(1, LANES)
    x1, h2, route, cnt = _mixout_call(
        og, sz, om, sg, sm, xf, mod3, tiles_per_seq, gla_g_norm[0].reshape(1, GLA_DV),
        w_br_gla[0].astype(BF16), w_br_mla[0].astype(BF16), w_out[0].astype(BF16),
        g_post_mix[0].reshape(1, D), g_pre_ffn[0].reshape(1, D), wr2, br)

    seg = -(-cnt[:, 0, :N_EXP].astype(jnp.int32) // SUBLANES) * SUBLANES
    stage_start = jnp.cumsum(seg, axis=1) - seg
    in_group = jnp.cumsum(seg, axis=0) - seg
    group = jnp.sum(seg, axis=0)
    padded = -(-group // TM_FFN) * TM_FFN
    ends = jnp.cumsum(padded)
    starts = ends - padded
    sorted_row = starts[None, :] + in_group
    meta = jnp.concatenate([stage_start.reshape(-1), seg.reshape(-1), sorted_row.reshape(-1),
                            jnp.sum(seg, axis=1)]).astype(jnp.int32)
    pad_info = jnp.concatenate([starts + group, ends[-1:], ends]).astype(jnp.int32)
    n_rows = _sorted_rows(n_tok)
    tile_start = jnp.arange(n_rows // TM_FFN, dtype=jnp.int32) * TM_FFN
    tile_exp = jnp.minimum(jnp.sum((ends[None, :] <= tile_start[:, None]).astype(jnp.int32), axis=1),
                           N_EXP - 1)
    n_used = (ends[-1:] // TM_FFN).astype(jnp.int32)
    ids = jnp.arange(N_EXP, dtype=jnp.int32)
    later = jnp.logical_and(padded[None, :] > 0, ids[None, :] > ids[:, None])
    next_owner = jnp.min(jnp.where(later, ids[None, :], N_EXP), axis=1)
    next_owner = jnp.where(next_owner == N_EXP, -1, next_owner)
    own = tile_exp[:, None] == ids[None, :]
    next_exp = jnp.sum(jnp.where(own, next_owner[None, :], 0), axis=1).astype(jnp.int32)
    data_end = jnp.sum(jnp.where(own, (starts + group)[None, :], 0), axis=1)
    tile_rows = jnp.clip(data_end - tile_start, 0, TM_FFN).astype(jnp.int32)

    xs = _dispatch_call(meta, pad_info, route, h2, n_rows)
    y = _ffn_call(tile_exp, next_exp, tile_rows, n_used, xs, w_gate[0], w_up[0], w_down[0],
                  b_gate[0], b_up[0], b_down[0])
    out = _combine_call(meta, y, route, x1, mod3, tiles_per_seq, g_post_ffn[0].reshape(1, D))
    return out.reshape(batch, seq, D)
```

```python
import functools

import jax
import jax.numpy as jnp
import numpy as np
from jax import lax
from jax.experimental import pallas as pl
from jax.experimental.pallas import tpu as pltpu

F32 = jnp.float32
BF16 = jnp.bfloat16

D = 1024
EPS = 1e-6
GRID_W = 64

GLA_H = 4
GLA_DK = 64
GLA_DV = 128
GLA_RANK = 16
GLA_TAU = 16.0
GLA_CHUNK = 64
GLA_BLOCK = 256
GLA_QK = GLA_H * GLA_DK
GLA_V = GLA_H * GLA_DV

MLA_H = 8
MLA_QR = 256
MLA_KVR = 128
MLA_NOPE = 64
MLA_ROPE = 32
MLA_V = 64
MLA_QKD = MLA_NOPE + MLA_ROPE
MLA_SCALE = MLA_QKD ** -0.5
LOG2_E = 1.4426950408889634
ROPE_BASE = 10000.0
HEAD_SLOT = 128
MLA_W = MLA_H * HEAD_SLOT
SUM_LANE_EVEN, SUM_LANE_ODD = MLA_V, 0

N_EXP = 32
TOP_K = 4
D_FF = 1024
SWIGLU_LIMIT = 7.0
SWIGLU_ALPHA = 1.702

LANES = 128
SUBLANES = 8
TM_PROJ = 512
TQ = 1024
TM_FFN = 512
FFN_SUB = 128
ZERO_ROWS = 256
LONG_SEGMENT = 128
STAGE_ROWS = -(-(TM_PROJ * TOP_K + N_EXP * (SUBLANES - 1)) // 256) * 256
ROUTE_E, ROUTE_W, ROUTE_P = 0, TOP_K, 2 * TOP_K

_OFF = np.cumsum([0, GLA_QK, GLA_QK, GLA_V, GLA_V, GLA_RANK, GLA_RANK,
                  MLA_QR, MLA_KVR, MLA_ROPE, D, D])
(O_Q, O_K, O_V, O_G, O_AF, O_AB, O_DQ, O_DKV, O_KR, O_MG, O_MM, _) = _OFF.tolist()

X_AF, X_AB, X_KR, X_KRS = 0, 16, 32, 64

VMEM_LIMIT = 56 * 1024 * 1024


def _cparams(sem):
    return pltpu.CompilerParams(dimension_semantics=sem, vmem_limit_bytes=VMEM_LIMIT)


def _resident(arr):
    nd = arr.ndim
    return pl.BlockSpec(arr.shape, lambda *_: (0,) * nd, pipeline_mode=pl.Buffered(1))


def _rms(x, g):
    return x * lax.rsqrt(jnp.mean(x * x, axis=-1, keepdims=True) + EPS) * g


def _sigmoid(x):
    return 1.0 / (1.0 + jnp.exp(-x))


def _log_sigmoid(x):
    return jnp.minimum(x, 0.0) - jnp.log1p(jnp.exp(-jnp.abs(x)))


def _dot(a, b):
    return jnp.dot(a, b, preferred_element_type=F32)


def _dot_nt(a, b):
    return lax.dot_general(a, b, (((1,), (1,)), ((), ())), preferred_element_type=F32)


def _dot_tn(a, b):
    return lax.dot_general(a, b, (((0,), (0,)), ((), ())), preferred_element_type=F32)


def _mod_kernel(c_ref, w_ref, b_ref, o_ref):
    c = c_ref[...]
    s = (c * _sigmoid(c)).astype(BF16)
    o_ref[...] = _dot(s, w_ref[...].astype(BF16)) + b_ref[...]


def _mod_call(cc, w_mod, b_mod):
    n = w_mod.shape[1]
    tn = 1536
    return pl.pallas_call(
        _mod_kernel,
        grid=(n // tn,),
        in_specs=[pl.BlockSpec((16, D), lambda j: (0, 0)),
                  pl.BlockSpec((D, tn), lambda j: (0, j)),
                  pl.BlockSpec((1, tn), lambda j: (0, j))],
        out_specs=pl.BlockSpec((16, tn), lambda j: (0, j)),
        out_shape=jax.ShapeDtypeStruct((16, n), F32),
        compiler_params=_cparams(("arbitrary",)),
        name="mod",
    )(cc, w_mod, b_mod.reshape(1, n))


def _inproj_kernel(with_q, x_ref, mod_ref, gpre_ref, w1_ref, wa_ref, ba_ref, wk_ref,
                   wv_ref, gkv_ref, t1_ref, *rest):
    if with_q:
        (w2_ref, gq_ref, wq_ref, cq_ref, sq_ref,
         q_o, k_o, v_o, lf_o, lb_o, mk_o, mv_o, sz_o, mq_o, sg_o, sm_o) = rest
    else:
        (q_o, k_o, v_o, lf_o, lb_o, mk_o, mv_o) = rest
    x = x_ref[...]
    mod = mod_ref[...]
    sh = mod[:, 0:D]
    sc = mod[:, D:2 * D]
    h = (_rms(x, gpre_ref[...]) * (1.0 + sc) + sh).astype(BF16)

    z1 = _dot(h, w1_ref[...])
    q_o[...] = (z1[:, 0:GLA_QK] * (GLA_DK ** -0.5)).astype(BF16)
    k_o[...] = z1[:, GLA_QK:2 * GLA_QK].astype(BF16)
    v_o[...] = z1[:, 2 * GLA_QK:2 * GLA_QK + GLA_V].astype(BF16)
    o_dkv = 2 * GLA_QK + GLA_V
    ckv = _rms(z1[:, o_dkv:o_dkv + MLA_KVR], gkv_ref[...])
    xs = z1[:, o_dkv + MLA_KVR:o_dkv + MLA_KVR + LANES]

    la = _log_sigmoid(_dot(xs.astype(BF16), wa_ref[...]) + ba_ref[...]) * (1.0 / GLA_TAU)
    lf_o[...] = la[:, 0:GLA_QK]
    lb_o[...] = la[:, GLA_QK:2 * GLA_QK]

    lhs_k = jnp.concatenate([ckv, xs * t1_ref[...]], axis=-1).astype(BF16)
    mk_o[...] = _dot(lhs_k, wk_ref[...]).astype(BF16)
    lane = lax.broadcasted_iota(jnp.int32, (x.shape[0], MLA_W), 1)
    mv = _dot(ckv.astype(BF16), wv_ref[...])
    mv_o[...] = jnp.where(lane % (2 * HEAD_SLOT) == SUM_LANE_EVEN, 1.0,
                          jnp.where(lane % (2 * HEAD_SLOT) == HEAD_SLOT + SUM_LANE_ODD, 1.0,
                                    mv)).astype(BF16)

    if with_q:
        zg = _dot(h, w2_ref[:, 0:GLA_V])
        sz_o[...] = (zg * _sigmoid(zg)).astype(BF16)
        n = _rms(_dot(h, w2_ref[:, GLA_V:GLA_V + MLA_QR]), gq_ref[...]).astype(BF16)
        cq = jnp.concatenate([cq_ref[...]] * MLA_H, axis=-1)
        sq = jnp.concatenate([sq_ref[...]] * MLA_H, axis=-1)
        mq_o[...] = (_dot(n, wq_ref[:, 0:MLA_W]) * cq
                     + _dot(n, wq_ref[:, MLA_W:2 * MLA_W]) * sq).astype(BF16)
        o_mg = GLA_V + MLA_QR
        sg_o[...] = _sigmoid(_dot(h, w2_ref[:, o_mg:o_mg + D])).astype(BF16)
        sm_o[...] = _sigmoid(_dot(h, w2_ref[:, o_mg + D:o_mg + 2 * D])).astype(BF16)


def _inproj_call(with_q, xf, mod3, mod_row_fn, tiles_per_seq, gpre, w1, wa, ba, wk, wv,
                 gkv, t1, extra=()):
    n_tok = xf.shape[0]
    tm = TM_PROJ
    grid = (n_tok // tm,)
    row = lambda i: (i, 0)
    tab = lambda i: (i % tiles_per_seq, 0)
    in_specs = [
        pl.BlockSpec((tm, D), row),
        pl.BlockSpec((None, 1, 6 * D), lambda i: (mod_row_fn(i), 0, 0)),
        _resident(gpre), _resident(w1), _resident(wa), _resident(ba), _resident(wk),
        _resident(wv), _resident(gkv),
        pl.BlockSpec((tm, LANES), tab),
    ]
    widths = [(GLA_QK, BF16), (GLA_QK, BF16), (GLA_V, BF16), (GLA_QK, F32), (GLA_QK, F32),
              (MLA_W, BF16), (MLA_W, BF16)]
    args = [xf, mod3, gpre, w1, wa, ba, wk, wv, gkv, t1]
    if with_q:
        w2, gq, wq, cq, sq = extra
        in_specs += [_resident(w2), _resident(gq), _resident(wq),
                     pl.BlockSpec((tm, HEAD_SLOT), tab), pl.BlockSpec((tm, HEAD_SLOT), tab)]
        args += [w2, gq, wq, cq, sq]
        widths += [(GLA_V, BF16), (MLA_W, BF16), (D, BF16), (D, BF16)]
    return pl.pallas_call(
        functools.partial(_inproj_kernel, with_q),
        grid=grid,
        in_specs=in_specs,
        out_specs=[pl.BlockSpec((tm, w), row) for w, _ in widths],
        out_shape=[jax.ShapeDtypeStruct((n_tok, w), dt) for w, dt in widths],
        compiler_params=_cparams(("arbitrary",)),
        name="inproj_x" if with_q else "inproj_ctx",
    )(*args)


def _gla_kernel(qx, kx, vx, lfx, lbx, qc, kc, vc, lfc, lbc, o_ref, sf_ref, sb_ref):
    C = GLA_CHUNK
    R = GLA_BLOCK
    n_sub = R // C
    rr = lax.broadcasted_iota(jnp.int32, (R, R), 0)
    cc = lax.broadcasted_iota(jnp.int32, (R, R), 1)
    same = (rr // C) == (cc // C)
    tbd_f = jnp.logical_and(same, cc <= rr).astype(BF16)
    tbd_b = jnp.logical_and(same, cc >= rr).astype(BF16)
    row = lax.broadcasted_iota(jnp.int32, (C, GLA_QK), 0)
    col = lax.broadcasted_iota(jnp.int32, (C, GLA_QK), 1)
    head_qk = col // GLA_DK
    tri4_f = (col % C) <= row
    tri4_b = (col % C) >= row
    head_v = lax.broadcasted_iota(jnp.int32, (C, GLA_V), 1) // GLA_DV

    def stack_masked(x, head_of_lane):
        return jnp.concatenate(
            [jnp.where(head_of_lane == hh, x, jnp.zeros_like(x)) for hh in range(GLA_H)], axis=0)

    def block(q_ref, k_ref, v_ref, la_ref, row0, fwd, s_ref, emit):
        sl = pl.ds(row0, R)
        la = la_ref[sl, :]
        la_hi = la.astype(BF16)
        la_lo = (la - la_hi.astype(F32)).astype(BF16)
        tbd = tbd_f if fwd else tbd_b
        cum = _dot(tbd, la_hi) + _dot(tbd, la_lo)
        k = k_ref[sl, :].astype(F32)
        v = v_ref[sl, :]
        if emit:
            q = q_ref[sl, :].astype(F32)
        st = s_ref[...]
        outs = [None] * n_sub
        for ci in (range(n_sub) if fwd else reversed(range(n_sub))):
            rs = slice(ci * C, (ci + 1) * C)
            cum_c = cum[rs]
            tot = cum_c[C - 1:C, :] if fwd else cum_c[0:1, :]
            k_c = k[rs]
            v_c = v[rs]
            ke_bd = stack_masked((k_c * jnp.exp(tot - cum_c)).astype(BF16), head_qk)
            v_stack = jnp.concatenate(
                [v_c[:, hh * GLA_DV:(hh + 1) * GLA_DV] for hh in range(GLA_H)], axis=0)
            if emit:
                qd = (q[rs] * jnp.exp(cum_c)).astype(BF16)
                ki_bd = stack_masked((k_c * jnp.exp(-cum_c)).astype(BF16), head_qk)
                att = _dot_nt(qd, ki_bd)
                att = jnp.where(tri4_f if fwd else tri4_b, att, 0.0).astype(BF16)
                o_intra = _dot(att, stack_masked(v_c, head_v))
                oi = _dot_nt(stack_masked(qd, head_qk), st.astype(BF16))
                o_inter = jnp.concatenate([oi[hh * C:(hh + 1) * C] for hh in range(GLA_H)],
                                          axis=-1)
                outs[ci] = o_intra + o_inter
            st = st * jnp.exp(tot) + _dot_tn(v_stack, ke_bd)
        s_ref[...] = st
        if emit:
            o_ref[sl, :] += jnp.concatenate(outs, axis=0)

    sf_ref[...] = jnp.zeros_like(sf_ref)
    sb_ref[...] = jnp.zeros_like(sb_ref)
    o_ref[...] = jnp.zeros_like(o_ref)
    n_ctx = qc.shape[0] // R
    n_x = qx.shape[0] // R
    for i in range(n_ctx):
        block(qc, kc, vc, lfc, i * R, True, sf_ref, False)
        block(qc, kc, vc, lbc, (n_ctx - 1 - i) * R, False, sb_ref, False)

    def body(i, carry):
        block(qx, kx, vx, lfx, pl.multiple_of(i * R, R), True, sf_ref, True)
        block(qx, kx, vx, lbx, pl.multiple_of((n_x - 1 - i) * R, R), False, sb_ref, True)
        return carry

    lax.fori_loop(0, n_x, body, 0)


def _gla_call(fx, fc, batch, seq, ctx_len):
    qx, kx, vx, lfx, lbx = fx
    qc, kc, vc, lfc, lbc = fc

    def spec(rows, w):
        return pl.BlockSpec((rows, w), lambda b: (b, 0))

    return pl.pallas_call(
        _gla_kernel,
        grid=(batch,),
        in_specs=[spec(seq, GLA_QK), spec(seq, GLA_QK), spec(seq, GLA_V), spec(seq, GLA_QK),
                  spec(seq, GLA_QK),
                  spec(ctx_len, GLA_QK), spec(ctx_len, GLA_QK), spec(ctx_len, GLA_V),
                  spec(ctx_len, GLA_QK), spec(ctx_len, GLA_QK)],
        out_specs=spec(seq, GLA_V),
        out_shape=jax.ShapeDtypeStruct((batch * seq, GLA_V), F32),
        scratch_shapes=[pltpu.VMEM((GLA_DV, GLA_QK), F32), pltpu.VMEM((GLA_DV, GLA_QK), F32)],
        compiler_params=_cparams(("arbitrary",)),
        name="gla",
    )(qx, kx, vx, lfx, lbx, qc, kc, vc, lfc, lbc)


def _mla_kernel(q_ref, kx_ref, vx_ref, kc_ref, vc_ref, o_ref):
    lane = lax.broadcasted_iota(jnp.int32, (q_ref.shape[0], HEAD_SLOT), 1)
    for j in range(MLA_H // 2):
        pair = []
        for hh, sum_lane in ((2 * j, SUM_LANE_EVEN), (2 * j + 1, SUM_LANE_ODD)):
            sl = slice(hh * HEAD_SLOT, (hh + 1) * HEAD_SLOT)
            q = q_ref[:, sl]
            sx = _dot_nt(q, kx_ref[:, sl])
            sc = _dot_nt(q, kc_ref[:, sl])
            m = jnp.maximum(jnp.max(sx, axis=-1, keepdims=True),
                            jnp.max(sc, axis=-1, keepdims=True))
            px = jnp.exp2(sx - m).astype(BF16)
            pc = jnp.exp2(sc - m).astype(BF16)
            o = _dot(px, vx_ref[:, sl]) + _dot(pc, vc_ref[:, sl])
            pair.append(o / o[:, sum_lane:sum_lane + 1])
        o_ref[:, j * HEAD_SLOT:(j + 1) * HEAD_SLOT] = jnp.where(
            lane < MLA_V, pair[0], pair[1]).astype(BF16)


def _mla_call(mq, mkx, mvx, mkc, mvc, batch, seq, ctx_len):
    nq = seq // TQ
    return pl.pallas_call(
        _mla_kernel,
        grid=(batch, nq),
        in_specs=[pl.BlockSpec((TQ, MLA_W), lambda b, i: (b * nq + i, 0)),
                  pl.BlockSpec((seq, MLA_W), lambda b, i: (b, 0)),
                  pl.BlockSpec((seq, MLA_W), lambda b, i: (b, 0)),
                  pl.BlockSpec((ctx_len, MLA_W), lambda b, i: (b, 0)),
                  pl.BlockSpec((ctx_len, MLA_W), lambda b, i: (b, 0))],
        out_specs=pl.BlockSpec((TQ, MLA_H * MLA_V), lambda b, i: (b * nq + i, 0)),
        out_shape=jax.ShapeDtypeStruct((batch * seq, MLA_H * MLA_V), BF16),
        compiler_params=_cparams(("arbitrary", "arbitrary")),
        name="mla",
    )(mq, mkx, mvx, mkc, mvc)


def _mixout_kernel(og_ref, sz_ref, om_ref, sg_ref, sm_ref, x_ref, mod_ref, gn_ref, wbg_ref,
                   wbm_ref, wo_ref, gpost_ref, gffn_ref, wr_ref, br_ref,
                   x1_o, h2_o, route_o, cnt_o):
    tm = x_ref.shape[0]
    mod = mod_ref[...]
    gt_a = mod[:, 2 * D:3 * D]
    sh_f = mod[:, 3 * D:4 * D]
    sc_f = mod[:, 4 * D:5 * D]

    og = og_ref[...]
    gn = gn_ref[...]
    parts = [_rms(og[:, hh * GLA_DV:(hh + 1) * GLA_DV], gn) for hh in range(GLA_H)]
    a = (jnp.concatenate(parts, axis=-1) * sz_ref[...].astype(F32)).astype(BF16)
    br_g = _dot(a, wbg_ref[...])
    br_m = _dot(om_ref[...], wbm_ref[...])
    merged = (sg_ref[...].astype(F32) * br_g + sm_ref[...].astype(F32) * br_m).astype(BF16)
    mo = _dot(merged, wo_ref[...])
    x1 = x_ref[...] + gt_a * _rms(mo, gpost_ref[...])
    x1_o[...] = x1
    h2 = _rms(x1, gffn_ref[...]) * (1.0 + sc_f) + sh_f
    h2_o[...] = h2.astype(BF16)

    lane = lax.broadcasted_iota(jnp.int32, (tm, LANES), 1)
    h_hi = h2.astype(BF16)
    h_lo = (h2 - h_hi.astype(F32)).astype(BF16)
    part = _dot(h_hi, wr_ref[...]) + _dot(h_lo, wr_ref[...])
    logits = part[:, 0:LANES] + part[:, LANES:2 * LANES] + br_ref[...]
    neg = jnp.float32(-jnp.inf)
    lg = jnp.where(lane < N_EXP, logits, neg)
    lane_f = lane.astype(F32)
    hots, vals = [], []
    for _k in range(TOP_K):
        mx = jnp.max(lg, axis=-1, keepdims=True)
        idx = jnp.min(jnp.where(lg == mx, lane_f, float(LANES)), axis=-1, keepdims=True)
        hot = lane_f == idx
        lg = jnp.where(hot, neg, lg)
        hots.append(hot)
        vals.append(mx)
    es = [jnp.exp(v - vals[0]) for v in vals]
    den = es[0] + es[1] + es[2] + es[3]
    ws = [e / den for e in es]

    msum = jnp.zeros((tm, LANES), F32)
    for hot in hots:
        msum = msum + hot.astype(F32)
    rr = lax.broadcasted_iota(jnp.int32, (tm, tm), 0)
    cc = lax.broadcasted_iota(jnp.int32, (tm, tm), 1)
    lower = (cc < rr).astype(BF16)
    prior = _dot(lower, msum.astype(BF16))
    cnt = jnp.sum(msum, axis=0, keepdims=True)
    seg = jnp.floor((cnt + (SUBLANES - 1.0)) * (1.0 / SUBLANES)) * SUBLANES
    er = lax.broadcasted_iota(jnp.int32, (LANES, LANES), 0)
    ec = lax.broadcasted_iota(jnp.int32, (LANES, LANES), 1)
    before = (er < ec).astype(BF16)
    seg_start = _dot(jnp.broadcast_to(seg, (SUBLANES, LANES)).astype(BF16), before)[0:1, :]
    slot_of = prior + seg_start

    route = jnp.zeros((tm, LANES), F32)
    for kk in range(TOP_K):
        hotf = hots[kk].astype(F32)
        e_col = jnp.sum(hotf * lane_f, axis=-1, keepdims=True)
        p_col = jnp.sum(hotf * slot_of, axis=-1, keepdims=True)
        route = jnp.where(lane == ROUTE_E + kk, e_col, route)
        route = jnp.where(lane == ROUTE_W + kk, ws[kk], route)
        route = jnp.where(lane == ROUTE_P + kk, p_col, route)
    route_o[...] = route
    cnt_o[...] = jnp.broadcast_to(cnt, cnt_o.shape)


def _mixout_call(og, sz, om, sg, sm, xf, mod3, tiles_per_seq, gn, wbg, wbm, wo, gpost, gffn,
                 wr, br):
    n_tok = xf.shape[0]
    tm = TM_PROJ
    row = lambda i: (i, 0)

    def rs(w):
        return pl.BlockSpec((tm, w), row)

    return pl.pallas_call(
        _mixout_kernel,
        grid=(n_tok // tm,),
        in_specs=[rs(GLA_V), rs(GLA_V), rs(MLA_H * MLA_V), rs(D), rs(D), rs(D),
                  pl.BlockSpec((None, 1, 6 * D), lambda i: (i // tiles_per_seq, 0, 0)),
                  _resident(gn), _resident(wbg), _resident(wbm), _resident(wo),
                  _resident(gpost), _resident(gffn), _resident(wr), _resident(br)],
        out_specs=[rs(D), rs(D), rs(LANES),
                   pl.BlockSpec((None, SUBLANES, LANES), lambda i: (i, 0, 0))],
        out_shape=[jax.ShapeDtypeStruct((n_tok, D), F32), jax.ShapeDtypeStruct((n_tok, D), BF16),
                   jax.ShapeDtypeStruct((n_tok, LANES), F32),
                   jax.ShapeDtypeStruct((n_tok // tm, SUBLANES, LANES), F32)],
        compiler_params=_cparams(("arbitrary",)),
        name="mixout",
    )(og, sz, om, sg, sm, xf, mod3, gn, wbg, wbm, wo, gpost, gffn, wr, br)


def _sorted_rows(n_tok):
    n_steps = n_tok // TM_PROJ
    worst_pad = n_steps * N_EXP * (SUBLANES - 1) + N_EXP * TM_FFN
    per_step = -(-worst_pad // (n_steps * ZERO_ROWS)) * ZERO_ROWS
    n_rows = n_tok * TOP_K + per_step * n_steps
    assert n_rows % TM_FFN == 0
    return n_rows


def _seg_copies(src_ref, dst_ref, src0, dst0, length, sem):
    def copy_bit(b):
        size = 1 << b
        off = (length >> (b + 1)) << (b + 1)

        @pl.when((length & size) != 0)
        def _():
            pltpu.make_async_copy(
                src_ref.at[pl.ds(pl.multiple_of(src0 + off, SUBLANES), size)],
                dst_ref.at[pl.ds(pl.multiple_of(dst0 + off, SUBLANES), size)], sem).start()

    lo_bit = SUBLANES.bit_length() - 1
    hi_bit = TM_PROJ.bit_length() - 1
    split = LONG_SEGMENT.bit_length() - 1

    @pl.when(length >= LONG_SEGMENT)
    def _():
        for b in range(hi_bit, split - 1, -1):
            copy_bit(b)

    for b in range(split - 1, lo_bit - 1, -1):
        copy_bit(b)


def _wait_rows(src_ref, dst_ref, total, sem):
    for b in reversed(range(SUBLANES.bit_length() - 1, STAGE_ROWS.bit_length())):
        size = 1 << b

        @pl.when((total & size) != 0)
        def _():
            pltpu.make_async_copy(src_ref.at[pl.ds(0, size)], dst_ref.at[pl.ds(0, size)],
                                  sem).wait()


def _zero_rows(zero_ref, dst_ref, start, end, sem, wait):
    length = end - start
    for b in reversed(range(SUBLANES.bit_length() - 1, ZERO_ROWS.bit_length())):
        size = 1 << b
        off = (length >> (b + 1)) << (b + 1)

        @pl.when((length & size) != 0)
        def _():
            cp = pltpu.make_async_copy(
                zero_ref.at[pl.ds(0, size)],
                dst_ref.at[pl.ds(pl.multiple_of(start + off, SUBLANES), size)], sem)
            if wait:
                cp.wait()
            else:
                cp.start()


def _slot_onehot(route_t, n_slots):
    tm = route_t.shape[1]
    slot = lax.broadcasted_iota(jnp.int32, (n_slots, tm), 0).astype(F32)
    acc = jnp.zeros((n_slots, tm), F32)
    for kk in range(TOP_K):
        acc = jnp.where(slot == route_t[ROUTE_P + kk:ROUTE_P + kk + 1, :], 1.0, acc)
    return acc.astype(BF16)


def _dispatch_kernel(meta_ref, pad_ref, route_ref, h_ref, xs_out, stage, zero_ref, sems, zsem):
    i = pl.program_id(0)
    n_steps = pl.num_programs(0)
    n_rows = xs_out.shape[0]
    n_seg = n_steps * N_EXP
    slot = i % 2

    @pl.when(i == 0)
    def _():
        zero_ref[...] = jnp.zeros_like(zero_ref)

    def clear_uncovered(wait):
        per = -(-N_EXP // n_steps)
        for j in range(per):
            e = i * per + j

            @pl.when(e < N_EXP)
            def _():
                ec = jnp.minimum(e, N_EXP - 1)
                _zero_rows(zero_ref, xs_out, pad_ref[ec], pad_ref[N_EXP + 1 + ec], zsem, wait)

        total_end = pad_ref[N_EXP]
        tail_rows = n_rows - n_steps * TM_PROJ * TOP_K
        per_step = tail_rows // n_steps
        for j in range(per_step // ZERO_ROWS):
            seg_end = (n_rows - tail_rows) + i * per_step + (j + 1) * ZERO_ROWS
            zlen = jnp.clip(seg_end - total_end, 0, ZERO_ROWS)
            _zero_rows(zero_ref, xs_out, seg_end - zlen, seg_end, zsem, wait)

    clear_uncovered(wait=False)

    onehot = _slot_onehot(route_ref[...].T, STAGE_ROWS)
    stage[slot] = _dot(onehot, h_ref[...])

    def issue(e, carry):
        s = i * N_EXP + e
        _seg_copies(stage.at[slot], xs_out, meta_ref[s], meta_ref[2 * n_seg + s],
                    meta_ref[n_seg + s], sems.at[slot])
        return carry

    lax.fori_loop(0, N_EXP, issue, 0)
    clear_uncovered(wait=True)

    @pl.when(i >= 1)
    def _():
        _wait_rows(stage.at[1 - slot], xs_out, meta_ref[3 * n_seg + i - 1], sems.at[1 - slot])

    @pl.when(i == n_steps - 1)
    def _():
        _wait_rows(stage.at[slot], xs_out, meta_ref[3 * n_seg + i], sems.at[slot])


def _dispatch_call(meta, pad_info, route, h2, n_rows):
    n_tok = h2.shape[0]
    tm = TM_PROJ
    n_steps = n_tok // tm
    assert (n_rows - n_tok * TOP_K) % (n_steps * ZERO_ROWS) == 0 and TM_FFN <= 2 * ZERO_ROWS
    grid_spec = pltpu.PrefetchScalarGridSpec(
        num_scalar_prefetch=2,
        grid=(n_steps,),
        in_specs=[pl.BlockSpec((tm, LANES), lambda i, m, p: (i, 0)),
                  pl.BlockSpec((tm, D), lambda i, m, p: (i, 0))],
        out_specs=pl.BlockSpec(memory_space=pl.ANY),
        scratch_shapes=[pltpu.VMEM((2, STAGE_ROWS, D), F32), pltpu.VMEM((ZERO_ROWS, D), F32),
                        pltpu.SemaphoreType.DMA((2,)), pltpu.SemaphoreType.DMA(())],
    )
    return pl.pallas_call(
        _dispatch_kernel,
        grid_spec=grid_spec,
        out_shape=jax.ShapeDtypeStruct((n_rows, D), F32),
        compiler_params=_cparams(("arbitrary",)),
        name="dispatch",
    )(meta, pad_info, route, h2)


def _ffn_kernel(te_ref, nx_ref, vr_ref, nu_ref, x_ref, wg_hbm, wu_hbm, wd_hbm, bg0, bu0, bd0, bg1,
                bu1, bd1, y_ref, land, wgb, wub, wdb, sems):
    i = pl.program_id(0)
    tm = TM_FFN
    t0 = 2 * i
    t1 = t0 + 1
    e0 = te_ref[t0]
    e1 = te_ref[t1]
    used0 = t0 < nu_ref[0]
    used1 = t1 < nu_ref[0]
    first0 = jnp.logical_or(i == 0, e0 != te_ref[jnp.maximum(t0 - 1, 0)])
    same = jnp.logical_and(used1, e1 == e0)
    n_sub = tm // FFN_SUB
    parts0 = (vr_ref[t0] + FFN_SUB - 1) // FFN_SUB
    parts1 = (vr_ref[t1] + FFN_SUB - 1) // FFN_SUB
    joint = jnp.logical_and(same, parts1 == n_sub)
    w_hbm = (wg_hbm, wu_hbm, wd_hbm)

    def fetch(expert):
        for j in range(3):
            pltpu.make_async_copy(w_hbm[j].at[expert], land.at[j], sems.at[j]).start()

    def switch(t):
        for j, dst in enumerate((wgb, wub, wdb)):
            pltpu.make_async_copy(w_hbm[j].at[0], land.at[j], sems.at[j]).wait()
            dst[...] = land[j].astype(BF16)
        nxt = nx_ref[t]

        @pl.when(nxt >= 0)
        def _():
            fetch(jnp.maximum(nxt, 0))

    def compute(rows, bg, bu, bd):
        x = x_ref[rows, :].astype(BF16)
        gate = jnp.minimum(_dot(x, wgb[...]) + bg[...], SWIGLU_LIMIT)
        up = jnp.clip(_dot(x, wub[...]) + bu[...], -SWIGLU_LIMIT, SWIGLU_LIMIT)
        act = ((up + 1.0) * gate * _sigmoid(SWIGLU_ALPHA * gate)).astype(BF16)
        y_ref[rows, :] = _dot(act, wdb[...]) + bd[...]

    lo = slice(0, tm)
    hi = slice(tm, 2 * tm)

    @pl.when(jnp.logical_and(used0, i == 0))
    def _():
        fetch(e0)

    @pl.when(jnp.logical_and(used0, first0))
    def _():
        switch(t0)

    def single(off, parts, bg, bu, bd):
        for n in range(1, n_sub + 1):
            @pl.when(parts == n)
            def _():
                compute(slice(off, off + n * FFN_SUB), bg, bu, bd)
                if n < n_sub:
                    y_ref[off + n * FFN_SUB:off + tm, :] = jnp.zeros((tm - n * FFN_SUB, D),
                                                                     y_ref.dtype)

    @pl.when(joint)
    def _():
        compute(slice(0, 2 * tm), bg0, bu0, bd0)

    @pl.when(jnp.logical_and(used0, jnp.logical_not(joint)))
    def _():
        single(0, parts0, bg0, bu0, bd0)

    @pl.when(jnp.logical_and(used1, jnp.logical_not(joint)))
    def _():
        @pl.when(jnp.logical_not(same))
        def _():
            switch(t1)

        single(tm, parts1, bg1, bu1, bd1)

    @pl.when(jnp.logical_not(used0))
    def _():
        y_ref[lo, :] = jnp.zeros((tm, D), y_ref.dtype)

    @pl.when(jnp.logical_not(used1))
    def _():
        y_ref[hi, :] = jnp.zeros((tm, D), y_ref.dtype)


def _ffn_call(tile_exp, next_exp, tile_rows, n_used, xs, w_gate, w_up, w_down, b_gate, b_up,
              b_down):
    n_rows = xs.shape[0]
    tm = TM_FFN
    n_tiles = n_rows // tm
    assert D == D_FF
    biases = (b_gate.reshape(N_EXP, 1, D_FF), b_up.reshape(N_EXP, 1, D_FF),
              b_down.reshape(N_EXP, 1, D))

    assert n_tiles % 2 == 0

    def xrow(i, te, nx, vr, nu):
        return (jnp.maximum(jnp.minimum(i, (nu[0] - 1) // 2), 0), 0)

    def bias(k):
        return lambda i, te, nx, vr, nu: (te[2 * i + k], 0, 0)

    grid_spec = pltpu.PrefetchScalarGridSpec(
        num_scalar_prefetch=4,
        grid=(n_tiles // 2,),
        in_specs=[pl.BlockSpec((2 * tm, D), xrow),
                  pl.BlockSpec(memory_space=pl.ANY),
                  pl.BlockSpec(memory_space=pl.ANY),
                  pl.BlockSpec(memory_space=pl.ANY),
                  pl.BlockSpec((None, 1, D_FF), bias(0)),
                  pl.BlockSpec((None, 1, D_FF), bias(0)),
                  pl.BlockSpec((None, 1, D), bias(0)),
                  pl.BlockSpec((None, 1, D_FF), bias(1)),
                  pl.BlockSpec((None, 1, D_FF), bias(1)),
                  pl.BlockSpec((None, 1, D), bias(1))],
        out_specs=pl.BlockSpec((2 * tm, D), lambda i, te, nx, vr, nu: (i, 0)),
        scratch_shapes=[pltpu.VMEM((3, D, D_FF), F32),
                        pltpu.VMEM((D, D_FF), BF16), pltpu.VMEM((D, D_FF), BF16),
                        pltpu.VMEM((D_FF, D), BF16), pltpu.SemaphoreType.DMA((3,))],
    )
    return pl.pallas_call(
        _ffn_kernel,
        grid_spec=grid_spec,
        out_shape=jax.ShapeDtypeStruct((n_rows, D), F32),
        compiler_params=_cparams(("arbitrary",)),
        name="ffn",
    )(tile_exp, next_exp, tile_rows, n_used, xs, w_gate, w_up, w_down, *(2 * biases))


def _combine_kernel(meta_ref, y_hbm, route_ref, x1_ref, mod_ref, gpost_ref, o_ref, stage, sems):
    i = pl.program_id(0)
    n_steps = pl.num_programs(0)
    n_seg = n_steps * N_EXP
    tm = x1_ref.shape[0]
    slot = i % 2

    def fetch(tile, into):
        def issue(e, carry):
            s = tile * N_EXP + e
            _seg_copies(y_hbm, stage.at[into], meta_ref[2 * n_seg + s], meta_ref[s],
                        meta_ref[n_seg + s], sems.at[into])
            return carry

        lax.fori_loop(0, N_EXP, issue, 0)

    @pl.when(i == 0)
    def _():
        stage[...] = jnp.zeros_like(stage)
        fetch(0, 0)

    @pl.when(i + 1 < n_steps)
    def _():
        fetch(i + 1, 1 - slot)

    _wait_rows(y_hbm, stage.at[slot], meta_ref[3 * n_seg + i], sems.at[slot])

    route = route_ref[...]
    col = lax.broadcasted_iota(jnp.int32, (tm, STAGE_ROWS), 1).astype(F32)
    wmat = jnp.zeros((tm, STAGE_ROWS), F32)
    for kk in range(TOP_K):
        hit = col == route[:, ROUTE_P + kk:ROUTE_P + kk + 1]
        wmat = jnp.where(hit, route[:, ROUTE_W + kk:ROUTE_W + kk + 1], wmat)
    moe = _dot(wmat.astype(BF16), stage[slot].astype(BF16))
    gt_f = mod_ref[...][:, 5 * D:6 * D]
    o_ref[...] = x1_ref[...] + gt_f * _rms(moe, gpost_ref[...])


def _combine_call(meta, y, route, x1, mod3, tiles_per_seq, gpost):
    n_tok = x1.shape[0]
    tm = TM_PROJ
    grid_spec = pltpu.PrefetchScalarGridSpec(
        num_scalar_prefetch=1,
        grid=(n_tok // tm,),
        in_specs=[pl.BlockSpec(memory_space=pl.ANY),
                  pl.BlockSpec((tm, LANES), lambda i, m: (i, 0)),
                  pl.BlockSpec((tm, D), lambda i, m: (i, 0)),
                  pl.BlockSpec((None, 1, 6 * D), lambda i, m: (i // tiles_per_seq, 0, 0)),
                  pl.BlockSpec((1, D), lambda i, m: (0, 0))],
        out_specs=pl.BlockSpec((tm, D), lambda i, m: (i, 0)),
        scratch_shapes=[pltpu.VMEM((2, STAGE_ROWS, D), F32), pltpu.SemaphoreType.DMA((2,))],
    )
    return pl.pallas_call(
        _combine_kernel,
        grid_spec=grid_spec,
        out_shape=jax.ShapeDtypeStruct((n_tok, D), F32),
        compiler_params=_cparams(("arbitrary",)),
        name="combine",
    )(meta, y, route, x1, mod3, gpost)


def _rope_swap(w):
    q = MLA_ROPE // 4
    return jnp.concatenate([-w[..., q:2 * q], w[..., 0:q], -w[..., 3 * q:4 * q],
                            w[..., 2 * q:3 * q]], axis=-1)


def _rope_tables(seq):
    rows = seq // GRID_W
    r, col = np.meshgrid(np.arange(rows, dtype=np.float64), np.arange(GRID_W, dtype=np.float64),
                         indexing="ij")
    half = MLA_ROPE // 2
    inv_freq = ROPE_BASE ** (-np.arange(0, half, 2, dtype=np.float64) / half)
    ar = r.reshape(-1)[:, None] * inv_freq
    ac = col.reshape(-1)[:, None] * inv_freq
    cos = np.concatenate([np.cos(ar), np.cos(ar), np.cos(ac), np.cos(ac)], axis=-1)
    sin = np.concatenate([np.sin(ar), np.sin(ar), np.sin(ac), np.sin(ac)], axis=-1)
    return cos, sin


def _head_slots(w3, lead):
    rows, heads, width = w3.shape
    return jnp.pad(w3, ((0, 0), (0, 0), (lead, HEAD_SLOT - lead - width))).reshape(
        rows, heads * HEAD_SLOT)


def kernel(x, c, ctx, c_ctx, w_mod, b_mod, g_pre_mix, g_post_mix, g_pre_ffn, g_post_ffn, w_in,
           gla_w_a2_f, gla_b_a_f, gla_w_a2_b, gla_b_a_b, gla_g_norm, mla_g_q, mla_w_uq, mla_g_kv,
           mla_w_uk, mla_w_uv, w_br_gla, w_br_mla, w_out, router_w, router_b, w_gate, b_gate,
           w_up, b_up, w_down, b_down):
    depth = w_mod.shape[0]
    assert depth == 1, "single-layer block"
    batch, seq, d = x.shape
    ctx_len = ctx.shape[1]
    assert d == D and seq % TM_PROJ == 0 and (batch * ctx_len) % TM_PROJ == 0
    assert TM_PROJ % ctx_len == 0 or ctx_len % TM_PROJ == 0
    n_tok = batch * seq

    cc = jnp.zeros((16, D), F32).at[:batch].set(c).at[batch].set(c_ctx)
    mod = _mod_call(cc, w_mod[0], b_mod[0])
    mod3 = mod.reshape(16, 1, 6 * D)

    wi = w_in[0]
    kr = wi[:, O_KR:O_KR + MLA_ROPE]
    small = jnp.concatenate([wi[:, O_AF:O_AF + GLA_RANK], wi[:, O_AB:O_AB + GLA_RANK], kr,
                             _rope_swap(kr), jnp.zeros((D, LANES - X_KRS - MLA_ROPE), F32)],
                            axis=1)
    w1 = jnp.concatenate([wi[:, O_Q:O_G], wi[:, O_DKV:O_DKV + MLA_KVR], small], axis=1).astype(BF16)
    w2 = jnp.concatenate([wi[:, O_G:O_G + GLA_V], wi[:, O_DQ:O_DQ + MLA_QR], wi[:, O_MG:]],
                         axis=1).astype(BF16)
    wa = jnp.zeros((LANES, 2 * GLA_QK), F32)
    wa = wa.at[X_AF:X_AF + GLA_RANK, 0:GLA_QK].set(gla_w_a2_f[0])
    wa = wa.at[X_AB:X_AB + GLA_RANK, GLA_QK:].set(gla_w_a2_b[0]).astype(BF16)
    ba = jnp.concatenate([gla_b_a_f[0], gla_b_a_b[0]]).reshape(1, 2 * GLA_QK)

    uk = mla_w_uk[0]
    uv = mla_w_uv[0]
    uq = mla_w_uq[0]
    wk_top = _head_slots(uk.reshape(MLA_KVR, MLA_H, MLA_NOPE), 0)
    place = np.zeros((LANES, MLA_H, HEAD_SLOT), np.float32)
    for j in range(MLA_ROPE):
        place[X_KR + j, :, MLA_NOPE + j] = 1.0
        place[X_KRS + j, :, MLA_NOPE + j] = 1.0
    wk = jnp.concatenate([wk_top, jnp.asarray(place.reshape(LANES, MLA_W))],
                         axis=0).astype(BF16)
    uv4 = uv.reshape(MLA_KVR, MLA_H // 2, 2, MLA_V)
    zv = jnp.zeros_like(uv4[:, :, 0])
    wv = jnp.stack([jnp.concatenate([uv4[:, :, 0], zv], axis=-1),
                    jnp.concatenate([zv, uv4[:, :, 1]], axis=-1)],
                   axis=2).reshape(MLA_KVR, MLA_W).astype(BF16)
    uq3 = uq.reshape(MLA_QR, MLA_H, MLA_QKD)
    wq_a = _head_slots(uq3, 0)
    wq_b = _head_slots(_rope_swap(uq3[:, :, MLA_NOPE:]), MLA_NOPE)
    wq = jnp.concatenate([wq_a, wq_b], axis=1).astype(BF16)

    cos, sin = _rope_tables(seq)
    zeros32 = np.zeros((seq, MLA_ROPE))
    t1_x = jnp.asarray(np.concatenate([zeros32, cos, sin, zeros32], axis=1), F32)
    t1_c = jnp.asarray(np.broadcast_to(
        np.concatenate([np.zeros(MLA_ROPE), np.ones(MLA_ROPE), np.zeros(2 * MLA_ROPE)]),
        (TM_PROJ, LANES)), F32)
    q_scale = MLA_SCALE * LOG2_E
    cq = jnp.asarray(np.concatenate([np.ones((seq, MLA_NOPE)), cos, zeros32], axis=1) * q_scale, F32)
    sq = jnp.asarray(np.concatenate([np.zeros((seq, MLA_NOPE)), sin, zeros32], axis=1) * q_scale,
                     F32)

    gpre = g_pre_mix[0].reshape(1, D)
    gkv = mla_g_kv[0].reshape(1, MLA_KVR)
    gq = mla_g_q[0].reshape(1, MLA_QR)
    tiles_per_seq = seq // TM_PROJ

    xf = x.reshape(n_tok, D)
    cf = ctx.reshape(batch * ctx_len, D)
    (gqx, gkx, gvx, lfx, lbx, mkx, mvx, sz, mq, sg, sm) = _inproj_call(
        True, xf, mod3, lambda i: i // tiles_per_seq, tiles_per_seq, gpre, w1, wa, ba, wk, wv,
        gkv, t1_x, extra=(w2, gq, wq, cq, sq))
    (gqc, gkc, gvc, lfc, lbc, mkc, mvc) = _inproj_call(
        False, cf, mod3, lambda i: batch, 1, gpre, w1, wa, ba, wk, wv, gkv, t1_c)

    og = _gla_call((gqx, gkx, gvx, lfx, lbx), (gqc, gkc, gvc, lfc, lbc), batch, seq, ctx_len)
    om = _mla_call(mq, mkx, mvx, mkc, mvc, batch, seq, ctx_len)

    wr = jnp.pad(router_w[0], ((0, 0), (0, LANES - N_EXP)))
    wrh = wr.astype(BF16)
    wr2 = jnp.concatenate([wrh, (wr - wrh.astype(F32)).astype(BF16)], axis=1)
    br =jnp.pad(router_b[0], (0, LANES - N_EXP)).reshape(1, LANES)
    x1, h2, route, cnt = _mixout_call(
        og, sz, om, sg, sm, xf, mod3, tiles_per_seq, gla_g_norm[0].reshape(1, GLA_DV),
        w_br_gla[0].astype(BF16), w_br_mla[0].astype(BF16), w_out[0].astype(BF16),
        g_post_mix[0].reshape(1, D), g_pre_ffn[0].reshape(1, D), wr2, br)

    seg = -(-cnt[:, 0, :N_EXP].astype(jnp.int32) // SUBLANES) * SUBLANES
    stage_start = jnp.cumsum(seg, axis=1) - seg
    in_group = jnp.cumsum(seg, axis=0) - seg
    group = jnp.sum(seg, axis=0)
    padded = -(-group // TM_FFN) * TM_FFN
    ends = jnp.cumsum(padded)
    starts = ends - padded
    sorted_row = starts[None, :] + in_group
    meta = jnp.concatenate([stage_start.reshape(-1), seg.reshape(-1), sorted_row.reshape(-1),
                            jnp.sum(seg, axis=1)]).astype(jnp.int32)
    pad_info = jnp.concatenate([starts + group, ends[-1:], ends]).astype(jnp.int32)
    n_rows = _sorted_rows(n_tok)
    tile_start = jnp.arange(n_rows // TM_FFN, dtype=jnp.int32) * TM_FFN
    tile_exp = jnp.minimum(jnp.sum((ends[None, :] <= tile_start[:, None]).astype(jnp.int32), axis=1),
                           N_EXP - 1)
    n_used = (ends[-1:] // TM_FFN).astype(jnp.int32)
    ids = jnp.arange(N_EXP, dtype=jnp.int32)
    later = jnp.logical_and(padded[None, :] > 0, ids[None, :] > ids[:, None])
    next_owner = jnp.min(jnp.where(later, ids[None, :], N_EXP), axis=1)
    next_owner = jnp.where(next_owner == N_EXP, -1, next_owner)
    own = tile_exp[:, None] == ids[None, :]
    next_exp = jnp.sum(jnp.where(own, next_owner[None, :], 0), axis=1).astype(jnp.int32)
    data_end = jnp.sum(jnp.where(own, (starts + group)[None, :], 0), axis=1)
    tile_rows = jnp.clip(data_end - tile_start, 0, TM_FFN).astype(jnp.int32)

    xs = _dispatch_call(meta, pad_info, route, h2, n_rows)
    y = _ffn_call(tile_exp, next_exp, tile_rows, n_used, xs, w_gate[0], w_up[0], w_down[0],
                  b_gate[0], b_up[0], b_down[0])
    out = _combine_call(meta, y, route, x1, mod3, tiles_per_seq, g_post_ffn[0].reshape(1, D))
    return out.reshape(batch, seq, D)
```

```python
import functools

import jax
import jax.numpy as jnp
import numpy as np
from jax import lax
from jax.experimental import pallas as pl
from jax.experimental.pallas import tpu as pltpu

F32 = jnp.float32
BF16 = jnp.bfloat16

D = 1024
EPS = 1e-6
GRID_W = 64

GLA_H = 4
GLA_DK = 64
GLA_DV = 128
GLA_RANK = 16
GLA_TAU = 16.0
GLA_CHUNK = 64
GLA_BLOCK = 256
GLA_QK = GLA_H * GLA_DK
GLA_V = GLA_H * GLA_DV

MLA_H = 8
MLA_QR = 256
MLA_KVR = 128
MLA_NOPE = 64
MLA_ROPE = 32
MLA_V = 64
MLA_QKD = MLA_NOPE + MLA_ROPE
MLA_SCALE = MLA_QKD ** -0.5
LOG2_E = 1.4426950408889634
ROPE_BASE = 10000.0
HEAD_SLOT = 128
MLA_W = MLA_H * HEAD_SLOT
SUM_LANE_EVEN, SUM_LANE_ODD = MLA_V, 0

N_EXP = 32
TOP_K = 4
D_FF = 1024
SWIGLU_LIMIT = 7.0
SWIGLU_ALPHA = 1.702

LANES = 128
SUBLANES = 8
TM_PROJ = 512
TQ = 1024
TM_FFN = 512
FFN_SUB = 128
ZERO_ROWS = 256
LONG_SEGMENT = 128
STAGE_ROWS = -(-(TM_PROJ * TOP_K + N_EXP * (SUBLANES - 1)) // 256) * 256
ROUTE_E, ROUTE_W, ROUTE_P = 0, TOP_K, 2 * TOP_K
ROUTE_ROWS = 16

_OFF = np.cumsum([0, GLA_QK, GLA_QK, GLA_V, GLA_V, GLA_RANK, GLA_RANK,
                  MLA_QR, MLA_KVR, MLA_ROPE, D, D])
(O_Q, O_K, O_V, O_G, O_AF, O_AB, O_DQ, O_DKV, O_KR, O_MG, O_MM, _) = _OFF.tolist()

X_AF, X_AB, X_KR, X_KRS = 0, 16, 32, 64

VMEM_LIMIT = 56 * 1024 * 1024


def _cparams(sem):
    return pltpu.CompilerParams(dimension_semantics=sem, vmem_limit_bytes=VMEM_LIMIT)


def _resident(arr):
    nd = arr.ndim
    return pl.BlockSpec(arr.shape, lambda *_: (0,) * nd, pipeline_mode=pl.Buffered(1))


def _rms(x, g):
    return x * lax.rsqrt(jnp.mean(x * x, axis=-1, keepdims=True) + EPS) * g


def _sigmoid(x):
    return 1.0 / (1.0 + jnp.exp(-x))


def _log_sigmoid(x):
    return jnp.minimum(x, 0.0) - jnp.log1p(jnp.exp(-jnp.abs(x)))


def _dot(a, b):
    return jnp.dot(a, b, preferred_element_type=F32)


def _dot_nt(a, b):
    return lax.dot_general(a, b, (((1,), (1,)), ((), ())), preferred_element_type=F32)


def _dot_tn(a, b):
    return lax.dot_general(a, b, (((0,), (0,)), ((), ())), preferred_element_type=F32)


def _mod_kernel(c_ref, w_ref, b_ref, o_ref):
    c = c_ref[...]
    s = (c * _sigmoid(c)).astype(BF16)
    o_ref[...] = _dot(s, w_ref[...].astype(BF16)) + b_ref[...]


def _mod_call(cc, w_mod, b_mod):
    n = w_mod.shape[1]
    tn = 1536
    return pl.pallas_call(
        _mod_kernel,
        grid=(n // tn,),
        in_specs=[pl.BlockSpec((16, D), lambda j: (0, 0)),
                  pl.BlockSpec((D, tn), lambda j: (0, j)),
                  pl.BlockSpec((1, tn), lambda j: (0, j))],
        out_specs=pl.BlockSpec((16, tn), lambda j: (0, j)),
        out_shape=jax.ShapeDtypeStruct((16, n), F32),
        compiler_params=_cparams(("arbitrary",)),
        name="mod",
    )(cc, w_mod, b_mod.reshape(1, n))


def _inproj_kernel(with_q, x_ref, mod_ref, gpre_ref, w1_ref, wa_ref, ba_ref, wk_ref,
                   wv_ref, gkv_ref, t1_ref, *rest):
    if with_q:
        (w2_ref, gq_ref, wq_ref, cq_ref, sq_ref,
         q_o, k_o, v_o, lf_o, lb_o, mk_o, mv_o, sz_o, mq_o, sg_o, sm_o) = rest
    else:
        (q_o, k_o, v_o, lf_o, lb_o, mk_o, mv_o) = rest
    x = x_ref[...]
    mod = mod_ref[...]
    sh = mod[:, 0:D]
    sc = mod[:, D:2 * D]
    h = (_rms(x, gpre_ref[...]) * (1.0 + sc) + sh).astype(BF16)

    z1 = _dot(h, w1_ref[...])
    q_o[...] = (z1[:, 0:GLA_QK] * (GLA_DK ** -0.5)).astype(BF16)
    k_o[...] = z1[:, GLA_QK:2 * GLA_QK].astype(BF16)
    v_o[...] = z1[:, 2 * GLA_QK:2 * GLA_QK + GLA_V].astype(BF16)
    o_dkv = 2 * GLA_QK + GLA_V
    ckv = _rms(z1[:, o_dkv:o_dkv + MLA_KVR], gkv_ref[...])
    xs = z1[:, o_dkv + MLA_KVR:o_dkv + MLA_KVR + LANES]

    la = _log_sigmoid(_dot(xs.astype(BF16), wa_ref[...]) + ba_ref[...]) * (1.0 / GLA_TAU)
    lf_o[...] = la[:, 0:GLA_QK]
    lb_o[...] = la[:, GLA_QK:2 * GLA_QK]

    lhs_k = jnp.concatenate([ckv, xs * t1_ref[...]], axis=-1).astype(BF16)
    mk_o[...] = _dot(lhs_k, wk_ref[...]).astype(BF16)
    lane = lax.broadcasted_iota(jnp.int32, (x.shape[0], MLA_W), 1)
    mv = _dot(ckv.astype(BF16), wv_ref[...])
    mv_o[...] = jnp.where(lane % (2 * HEAD_SLOT) == SUM_LANE_EVEN, 1.0,
                          jnp.where(lane % (2 * HEAD_SLOT) == HEAD_SLOT + SUM_LANE_ODD, 1.0,
                                    mv)).astype(BF16)

    if with_q:
        zg = _dot(h, w2_ref[:, 0:GLA_V])
        sz_o[...] = (zg * _sigmoid(zg)).astype(BF16)
        n = _rms(_dot(h, w2_ref[:, GLA_V:GLA_V + MLA_QR]), gq_ref[...]).astype(BF16)
        cq = jnp.concatenate([cq_ref[...]] * MLA_H, axis=-1)
        sq = jnp.concatenate([sq_ref[...]] * MLA_H, axis=-1)
        mq_o[...] = (_dot(n, wq_ref[:, 0:MLA_W]) * cq
                     + _dot(n, wq_ref[:, MLA_W:2 * MLA_W]) * sq).astype(BF16)
        o_mg = GLA_V + MLA_QR
        sg_o[...] = _sigmoid(_dot(h, w2_ref[:, o_mg:o_mg + D])).astype(BF16)
        sm_o[...] = _sigmoid(_dot(h, w2_ref[:, o_mg + D:o_mg + 2 * D])).astype(BF16)


def _inproj_call(with_q, xf, mod3, mod_row_fn, tiles_per_seq, gpre, w1, wa, ba, wk, wv,
                 gkv, t1, extra=()):
    n_tok = xf.shape[0]
    tm = TM_PROJ
    grid = (n_tok // tm,)
    row = lambda i: (i, 0)
    tab = lambda i: (i % tiles_per_seq, 0)
    in_specs = [
        pl.BlockSpec((tm, D), row),
        pl.BlockSpec((None, 1, 6 * D), lambda i: (mod_row_fn(i), 0, 0)),
        _resident(gpre), _resident(w1), _resident(wa), _resident(ba), _resident(wk),
        _resident(wv), _resident(gkv),
        pl.BlockSpec((tm, LANES), tab),
    ]
    widths = [(GLA_QK, BF16), (GLA_QK, BF16), (GLA_V, BF16), (GLA_QK, F32), (GLA_QK, F32),
              (MLA_W, BF16), (MLA_W, BF16)]
    args = [xf, mod3, gpre, w1, wa, ba, wk, wv, gkv, t1]
    if with_q:
        w2, gq, wq, cq, sq = extra
        in_specs += [_resident(w2), _resident(gq), _resident(wq),
                     pl.BlockSpec((tm, HEAD_SLOT), tab), pl.BlockSpec((tm, HEAD_SLOT), tab)]
        args += [w2, gq, wq, cq, sq]
        widths += [(GLA_V, BF16), (MLA_W, BF16), (D, BF16), (D, BF16)]
    return pl.pallas_call(
        functools.partial(_inproj_kernel, with_q),
        grid=grid,
        in_specs=in_specs,
        out_specs=[pl.BlockSpec((tm, w), row) for w, _ in widths],
        out_shape=[jax.ShapeDtypeStruct((n_tok, w), dt) for w, dt in widths],
        compiler_params=_cparams(("arbitrary",)),
        name="inproj_x" if with_q else "inproj_ctx",
    )(*args)


def _gla_kernel(qx, kx, vx, lfx, lbx, qc, kc, vc, lfc, lbc, o_ref, sf_ref, sb_ref):
    C = GLA_CHUNK
    R = GLA_BLOCK
    n_sub = R // C
    rr = lax.broadcasted_iota(jnp.int32, (R, R), 0)
    cc = lax.broadcasted_iota(jnp.int32, (R, R), 1)
    same = (rr // C) == (cc // C)
    tbd_f = jnp.logical_and(same, cc <= rr).astype(BF16)
    tbd_b = jnp.logical_and(same, cc >= rr).astype(BF16)
    row = lax.broadcasted_iota(jnp.int32, (C, GLA_QK), 0)
    col = lax.broadcasted_iota(jnp.int32, (C, GLA_QK), 1)
    head_qk = col // GLA_DK
    tri4_f = (col % C) <= row
    tri4_b = (col % C) >= row
    head_v = lax.broadcasted_iota(jnp.int32, (C, GLA_V), 1) // GLA_DV

    def stack_masked(x, head_of_lane):
        return jnp.concatenate(
            [jnp.where(head_of_lane == hh, x, jnp.zeros_like(x)) for hh in range(GLA_H)], axis=0)

    def block(q_ref, k_ref, v_ref, la_ref, row0, fwd, s_ref, emit):
        sl = pl.ds(row0, R)
        la = la_ref[sl, :]
        la_hi = la.astype(BF16)
        la_lo = (la - la_hi.astype(F32)).astype(BF16)
        tbd = tbd_f if fwd else tbd_b
        cum = _dot(tbd, la_hi) + _dot(tbd, la_lo)
        k = k_ref[sl, :].astype(F32)
        v = v_ref[sl, :]
        if emit:
            q = q_ref[sl, :].astype(F32)
        st = s_ref[...]
        outs = [None] * n_sub
        for ci in (range(n_sub) if fwd else reversed(range(n_sub))):
            rs = slice(ci * C, (ci + 1) * C)
            cum_c = cum[rs]
            tot = cum_c[C - 1:C, :] if fwd else cum_c[0:1, :]
            k_c = k[rs]
            v_c = v[rs]
            ke_bd = stack_masked((k_c * jnp.exp(tot - cum_c)).astype(BF16), head_qk)
            v_stack = jnp.concatenate(
                [v_c[:, hh * GLA_DV:(hh + 1) * GLA_DV] for hh in range(GLA_H)], axis=0)
            if emit:
                qd = (q[rs] * jnp.exp(cum_c)).astype(BF16)
                ki_bd = stack_masked((k_c * jnp.exp(-cum_c)).astype(BF16), head_qk)
                att = _dot_nt(qd, ki_bd)
                att = jnp.where(tri4_f if fwd else tri4_b, att, 0.0).astype(BF16)
                o_intra = _dot(att, stack_masked(v_c, head_v))
                oi = _dot_nt(stack_masked(qd, head_qk), st.astype(BF16))
                o_inter = jnp.concatenate([oi[hh * C:(hh + 1) * C] for hh in range(GLA_H)],
                                          axis=-1)
                outs[ci] = o_intra + o_inter
            st = st * jnp.exp(tot) + _dot_tn(v_stack, ke_bd)
        s_ref[...] = st
        if emit:
            o_ref[sl, :] += jnp.concatenate(outs, axis=0)

    sf_ref[...] = jnp.zeros_like(sf_ref)
    sb_ref[...] = jnp.zeros_like(sb_ref)
    o_ref[...] = jnp.zeros_like(o_ref)
    n_ctx = qc.shape[0] // R
    n_x = qx.shape[0] // R
    for i in range(n_ctx):
        block(qc, kc, vc, lfc, i * R, True, sf_ref, False)
        block(qc, kc, vc, lbc, (n_ctx - 1 - i) * R, False, sb_ref, False)

    def body(i, carry):
        block(qx, kx, vx, lfx, pl.multiple_of(i * R, R), True, sf_ref, True)
        block(qx, kx, vx, lbx, pl.multiple_of((n_x - 1 - i) * R, R), False, sb_ref, True)
        return carry

    lax.fori_loop(0, n_x, body, 0)


def _gla_call(fx, fc, batch, seq, ctx_len):
    qx, kx, vx, lfx, lbx = fx
    qc, kc, vc, lfc, lbc = fc

    def spec(rows, w):
        return pl.BlockSpec((rows, w), lambda b: (b, 0))

    return pl.pallas_call(
        _gla_kernel,
        grid=(batch,),
        in_specs=[spec(seq, GLA_QK), spec(seq, GLA_QK), spec(seq, GLA_V), spec(seq, GLA_QK),
                  spec(seq, GLA_QK),
                  spec(ctx_len, GLA_QK), spec(ctx_len, GLA_QK), spec(ctx_len, GLA_V),
                  spec(ctx_len, GLA_QK), spec(ctx_len, GLA_QK)],
        out_specs=spec(seq, GLA_V),
        out_shape=jax.ShapeDtypeStruct((batch * seq, GLA_V), F32),
        scratch_shapes=[pltpu.VMEM((GLA_DV, GLA_QK), F32), pltpu.VMEM((GLA_DV, GLA_QK), F32)],
        compiler_params=_cparams(("arbitrary",)),
        name="gla",
    )(qx, kx, vx, lfx, lbx, qc, kc, vc, lfc, lbc)


def _mla_kernel(q_ref, kx_ref, vx_ref, kc_ref, vc_ref, o_ref):
    lane = lax.broadcasted_iota(jnp.int32, (q_ref.shape[0], HEAD_SLOT), 1)
    for j in range(MLA_H // 2):
        pair = []
        for hh, sum_lane in ((2 * j, SUM_LANE_EVEN), (2 * j + 1, SUM_LANE_ODD)):
            sl = slice(hh * HEAD_SLOT, (hh + 1) * HEAD_SLOT)
            q = q_ref[:, sl]
            sx = _dot_nt(q, kx_ref[:, sl])
            sc = _dot_nt(q, kc_ref[:, sl])
            m = jnp.maximum(jnp.max(sx, axis=-1, keepdims=True),
                            jnp.max(sc, axis=-1, keepdims=True))
            px = jnp.exp2(sx - m).astype(BF16)
            pc = jnp.exp2(sc - m).astype(BF16)
            o = _dot(px, vx_ref[:, sl]) + _dot(pc, vc_ref[:, sl])
            pair.append(o / o[:, sum_lane:sum_lane + 1])
        o_ref[:, j * HEAD_SLOT:(j + 1) * HEAD_SLOT] = jnp.where(
            lane < MLA_V, pair[0], pair[1]).astype(BF16)


def _mla_call(mq, mkx, mvx, mkc, mvc, batch, seq, ctx_len):
    nq = seq // TQ
    return pl.pallas_call(
        _mla_kernel,
        grid=(batch, nq),
        in_specs=[pl.BlockSpec((TQ, MLA_W), lambda b, i: (b * nq + i, 0)),
                  pl.BlockSpec((seq, MLA_W), lambda b, i: (b, 0)),
                  pl.BlockSpec((seq, MLA_W), lambda b, i: (b, 0)),
                  pl.BlockSpec((ctx_len, MLA_W), lambda b, i: (b, 0)),
                  pl.BlockSpec((ctx_len, MLA_W), lambda b, i: (b, 0))],
        out_specs=pl.BlockSpec((TQ, MLA_H * MLA_V), lambda b, i: (b * nq + i, 0)),
        out_shape=jax.ShapeDtypeStruct((batch * seq, MLA_H * MLA_V), BF16),
        compiler_params=_cparams(("arbitrary", "arbitrary")),
        name="mla",
    )(mq, mkx, mvx, mkc, mvc)


def _mixout_kernel(og_ref, sz_ref, om_ref, sg_ref, sm_ref, x_ref, mod_ref, gn_ref, wbg_ref,
                   wbm_ref, wo_ref, gpost_ref, gffn_ref, wr_ref, br_ref,
                   x1_o, h2_o, route_o, routet_o, cnt_o):
    tm = x_ref.shape[0]
    mod = mod_ref[...]
    gt_a = mod[:, 2 * D:3 * D]
    sh_f = mod[:, 3 * D:4 * D]
    sc_f = mod[:, 4 * D:5 * D]

    og = og_ref[...]
    gn = gn_ref[...]
    parts = [_rms(og[:, hh * GLA_DV:(hh + 1) * GLA_DV], gn) for hh in range(GLA_H)]
    a = (jnp.concatenate(parts, axis=-1) * sz_ref[...].astype(F32)).astype(BF16)
    br_g = _dot(a, wbg_ref[...])
    br_m = _dot(om_ref[...], wbm_ref[...])
    merged = (sg_ref[...].astype(F32) * br_g + sm_ref[...].astype(F32) * br_m).astype(BF16)
    mo = _dot(merged, wo_ref[...])
    x1 = x_ref[...] + gt_a * _rms(mo, gpost_ref[...])
    x1_o[...] = x1
    h2 = _rms(x1, gffn_ref[...]) * (1.0 + sc_f) + sh_f
    h2_o[...] = h2.astype(BF16)

    h_hi = h2.astype(BF16)
    h_lo = (h2 - h_hi.astype(F32)).astype(BF16)
    part = _dot_nt(wr_ref[...], h_hi) + _dot_nt(wr_ref[...], h_lo)
    logits = part[0:LANES, :] + part[LANES:2 * LANES, :] + br_ref[...]
    row = lax.broadcasted_iota(jnp.int32, (LANES, tm), 0)
    row_f = row.astype(F32)
    neg = jnp.float32(-jnp.inf)
    lg = jnp.where(row < N_EXP, logits, neg)
    hots, vals, ids = [], [], []
    for _k in range(TOP_K):
        mx = jnp.max(lg, axis=0, keepdims=True)
        idx = jnp.min(jnp.where(lg == mx, row_f, float(LANES)), axis=0, keepdims=True)
        hot = row_f == idx
        lg = jnp.where(hot, neg, lg)
        hots.append(hot)
        vals.append(mx)
        ids.append(idx)
    es = [jnp.exp(v - vals[0]) for v in vals]
    den = es[0] + es[1] + es[2] + es[3]
    ws = [e / den for e in es]

    msum = jnp.zeros((LANES, tm), F32)
    for hot in hots:
        msum = msum + hot.astype(F32)
    rr = lax.broadcasted_iota(jnp.int32, (tm, tm), 0)
    cc = lax.broadcasted_iota(jnp.int32, (tm, tm), 1)
    upper = (rr < cc).astype(BF16)
    prior = _dot(msum.astype(BF16), upper)
    cnt = jnp.broadcast_to(jnp.sum(msum, axis=1, keepdims=True), (LANES, LANES))
    seg = jnp.floor((cnt + (SUBLANES - 1.0)) * (1.0 / SUBLANES)) * SUBLANES
    er = lax.broadcasted_iota(jnp.int32, (LANES, LANES), 0)
    ec = lax.broadcasted_iota(jnp.int32, (LANES, LANES), 1)
    seg_start = _dot((ec < er).astype(BF16), seg.astype(BF16))
    slot_of = prior + jnp.concatenate([seg_start] * (tm // LANES), axis=1)

    rows = list(ids) + ws + [jnp.sum(hot.astype(F32) * slot_of, axis=0, keepdims=True)
                             for hot in hots]
    route_t = jnp.concatenate(rows + [jnp.zeros((LANES - 3 * TOP_K, tm), F32)], axis=0)
    routet_o[...] = route_t[0:ROUTE_ROWS, :]
    route_o[...] = route_t.T
    cnt_o[...] = cnt


def _mixout_call(og, sz, om, sg, sm, xf, mod3, tiles_per_seq, gn, wbg, wbm, wo, gpost, gffn,
                 wr, br):
    n_tok = xf.shape[0]
    tm = TM_PROJ
    row = lambda i: (i, 0)

    def rs(w):
        return pl.BlockSpec((tm, w), row)

    return pl.pallas_call(
        _mixout_kernel,
        grid=(n_tok // tm,),
        in_specs=[rs(GLA_V), rs(GLA_V), rs(MLA_H * MLA_V), rs(D), rs(D), rs(D),
                  pl.BlockSpec((None, 1, 6 * D), lambda i: (i // tiles_per_seq, 0, 0)),
                  _resident(gn), _resident(wbg), _resident(wbm), _resident(wo),
                  _resident(gpost), _resident(gffn), _resident(wr), _resident(br)],
        out_specs=[rs(D), rs(D), rs(LANES),
                   pl.BlockSpec((None, ROUTE_ROWS, tm), lambda i: (i, 0, 0)),
                   pl.BlockSpec((None, LANES, LANES), lambda i: (i, 0, 0))],
        out_shape=[jax.ShapeDtypeStruct((n_tok, D), F32), jax.ShapeDtypeStruct((n_tok, D), BF16),
                   jax.ShapeDtypeStruct((n_tok, LANES), F32),
                   jax.ShapeDtypeStruct((n_tok // tm, ROUTE_ROWS, tm), F32),
                   jax.ShapeDtypeStruct((n_tok // tm, LANES, LANES), F32)],
        compiler_params=_cparams(("arbitrary",)),
        name="mixout",
    )(og, sz, om, sg, sm, xf, mod3, gn, wbg, wbm, wo, gpost, gffn, wr, br)


def _sorted_rows(n_tok):
    n_steps = n_tok // TM_PROJ
    worst_pad = n_steps * N_EXP * (SUBLANES - 1) + N_EXP * TM_FFN
    per_step = -(-worst_pad // (n_steps * ZERO_ROWS)) * ZERO_ROWS
    n_rows = n_tok * TOP_K + per_step * n_steps
    assert n_rows % TM_FFN == 0
    return n_rows


def _seg_copies(src_ref, dst_ref, src0, dst0, length, sem):
    def copy_bit(b):
        size = 1 << b
        off = (length >> (b + 1)) << (b + 1)

        @pl.when((length & size) != 0)
        def _():
            pltpu.make_async_copy(
                src_ref.at[pl.ds(pl.multiple_of(src0 + off, SUBLANES), size)],
                dst_ref.at[pl.ds(pl.multiple_of(dst0 + off, SUBLANES), size)], sem).start()

    lo_bit = SUBLANES.bit_length() - 1
    hi_bit = TM_PROJ.bit_length() - 1
    split = LONG_SEGMENT.bit_length() - 1

    @pl.when(length >= LONG_SEGMENT)
    def _():
        for b in range(hi_bit, split - 1, -1):
            copy_bit(b)

    for b in range(split - 1, lo_bit - 1, -1):
        copy_bit(b)


def _wait_rows(src_ref, dst_ref, total, sem):
    for b in reversed(range(SUBLANES.bit_length() - 1, STAGE_ROWS.bit_length())):
        size = 1 << b

        @pl.when((total & size) != 0)
        def _():
            pltpu.make_async_copy(src_ref.at[pl.ds(0, size)], dst_ref.at[pl.ds(0, size)],
                                  sem).wait()


def _zero_rows(zero_ref, dst_ref, start, end, sem, wait):
    length = end - start
    for b in reversed(range(SUBLANES.bit_length() - 1, ZERO_ROWS.bit_length())):
        size = 1 << b
        off = (length >> (b + 1)) << (b + 1)

        @pl.when((length & size) != 0)
        def _():
            cp = pltpu.make_async_copy(
                zero_ref.at[pl.ds(0, size)],
                dst_ref.at[pl.ds(pl.multiple_of(start + off, SUBLANES), size)], sem)
            if wait:
                cp.wait()
            else:
                cp.start()


def _slot_onehot(route_t, n_slots):
    tm = route_t.shape[1]
    slot = lax.broadcasted_iota(jnp.int32, (n_slots, tm), 0).astype(F32)
    acc = jnp.zeros((n_slots, tm), F32)
    for kk in range(TOP_K):
        acc = jnp.where(slot == route_t[ROUTE_P + kk:ROUTE_P + kk + 1, :], 1.0, acc)
    return acc.astype(BF16)


def _dispatch_kernel(meta_ref, pad_ref, route_ref, h_ref, xs_out, stage, zero_ref, sems, zsem):
    i = pl.program_id(0)
    n_steps = pl.num_programs(0)
    n_rows = xs_out.shape[0]
    n_seg = n_steps * N_EXP
    slot = i % 2

    @pl.when(i == 0)
    def _():
        zero_ref[...] = jnp.zeros_like(zero_ref)

    def clear_uncovered(wait):
        per = -(-N_EXP // n_steps)
        for j in range(per):
            e = i * per + j

            @pl.when(e < N_EXP)
            def _():
                ec = jnp.minimum(e, N_EXP - 1)
                _zero_rows(zero_ref, xs_out, pad_ref[ec], pad_ref[N_EXP + 1 + ec], zsem, wait)

        total_end = pad_ref[N_EXP]
        tail_rows = n_rows - n_steps * TM_PROJ * TOP_K
        per_step = tail_rows // n_steps
        for j in range(per_step // ZERO_ROWS):
            seg_end = (n_rows - tail_rows) + i * per_step + (j + 1) * ZERO_ROWS
            zlen = jnp.clip(seg_end - total_end, 0, ZERO_ROWS)
            _zero_rows(zero_ref, xs_out, seg_end - zlen, seg_end, zsem, wait)

    clear_uncovered(wait=False)

    onehot = _slot_onehot(route_ref[...], STAGE_ROWS)
    stage[slot] = _dot(onehot, h_ref[...])

    def issue(e, carry):
        s = i * N_EXP + e
        _seg_copies(stage.at[slot], xs_out, meta_ref[s], meta_ref[2 * n_seg + s],
                    meta_ref[n_seg + s], sems.at[slot])
        return carry

    lax.fori_loop(0, N_EXP, issue, 0)
    clear_uncovered(wait=True)

    @pl.when(i >= 1)
    def _():
        _wait_rows(stage.at[1 - slot], xs_out, meta_ref[3 * n_seg + i - 1], sems.at[1 - slot])

    @pl.when(i == n_steps - 1)
    def _():
        _wait_rows(stage.at[slot], xs_out, meta_ref[3 * n_seg + i], sems.at[slot])


def _dispatch_call(meta, pad_info, route, h2, n_rows):
    n_tok = h2.shape[0]
    tm = TM_PROJ
    n_steps = n_tok // tm
    assert (n_rows - n_tok * TOP_K) % (n_steps * ZERO_ROWS) == 0 and TM_FFN <= 2 * ZERO_ROWS
    grid_spec = pltpu.PrefetchScalarGridSpec(
        num_scalar_prefetch=2,
        grid=(n_steps,),
        in_specs=[pl.BlockSpec((None, ROUTE_ROWS, tm), lambda i, m, p: (i, 0, 0)),
                  pl.BlockSpec((tm, D), lambda i, m, p: (i, 0))],
        out_specs=pl.BlockSpec(memory_space=pl.ANY),
        scratch_shapes=[pltpu.VMEM((2, STAGE_ROWS, D), F32), pltpu.VMEM((ZERO_ROWS, D), F32),
                        pltpu.SemaphoreType.DMA((2,)), pltpu.SemaphoreType.DMA(())],
    )
    return pl.pallas_call(
        _dispatch_kernel,
        grid_spec=grid_spec,
        out_shape=jax.ShapeDtypeStruct((n_rows, D), F32),
        compiler_params=_cparams(("arbitrary",)),
        name="dispatch",
    )(meta, pad_info, route, h2)


def _ffn_kernel(te_ref, nx_ref, vr_ref, nu_ref, x_ref, wg_hbm, wu_hbm, wd_hbm, bg0, bu0, bd0, bg1,
                bu1, bd1, y_ref, land, wgb, wub, wdb, sems):
    i = pl.program_id(0)
    tm = TM_FFN
    t0 = 2 * i
    t1 = t0 + 1
    e0 = te_ref[t0]
    e1 = te_ref[t1]
    used0 = t0 < nu_ref[0]
    used1 = t1 < nu_ref[0]
    first0 = jnp.logical_or(i == 0, e0 != te_ref[jnp.maximum(t0 - 1, 0)])
    same = jnp.logical_and(used1, e1 == e0)
    n_sub = tm // FFN_SUB
    parts0 = (vr_ref[t0] + FFN_SUB - 1) // FFN_SUB
    parts1 = (vr_ref[t1] + FFN_SUB - 1) // FFN_SUB
    joint = jnp.logical_and(same, parts1 == n_sub)
    w_hbm = (wg_hbm, wu_hbm, wd_hbm)

    def fetch(expert):
        for j in range(3):
            pltpu.make_async_copy(w_hbm[j].at[expert], land.at[j], sems.at[j]).start()

    def switch(t):
        for j, dst in enumerate((wgb, wub, wdb)):
            pltpu.make_async_copy(w_hbm[j].at[0], land.at[j], sems.at[j]).wait()
            dst[...] = land[j].astype(BF16)
        nxt = nx_ref[t]

        @pl.when(nxt >= 0)
        def _():
            fetch(jnp.maximum(nxt, 0))

    def compute(rows, bg, bu, bd):
        x = x_ref[rows, :].astype(BF16)
        gate = jnp.minimum(_dot(x, wgb[...]) + bg[...], SWIGLU_LIMIT)
        up = jnp.clip(_dot(x, wub[...]) + bu[...], -SWIGLU_LIMIT, SWIGLU_LIMIT)
        act = ((up + 1.0) * gate * _sigmoid(SWIGLU_ALPHA * gate)).astype(BF16)
        y_ref[rows, :] = _dot(act, wdb[...]) + bd[...]

    lo = slice(0, tm)
    hi = slice(tm, 2 * tm)

    @pl.when(jnp.logical_and(used0, i == 0))
    def _():
        fetch(e0)

    @pl.when(jnp.logical_and(used0, first0))
    def _():
        switch(t0)

    def single(off, parts, bg, bu, bd):
        for n in range(1, n_sub + 1):
            @pl.when(parts == n)
            def _():
                compute(slice(off, off + n * FFN_SUB), bg, bu, bd)
                if n < n_sub:
                    y_ref[off + n * FFN_SUB:off + tm, :] = jnp.zeros((tm - n * FFN_SUB, D),
                                                                     y_ref.dtype)

    @pl.when(joint)
    def _():
        compute(slice(0, 2 * tm), bg0, bu0, bd0)

    @pl.when(jnp.logical_and(used0, jnp.logical_not(joint)))
    def _():
        single(0, parts0, bg0, bu0, bd0)

    @pl.when(jnp.logical_and(used1, jnp.logical_not(joint)))
    def _():
        @pl.when(jnp.logical_not(same))
        def _():
            switch(t1)

        single(tm, parts1, bg1, bu1, bd1)

    @pl.when(jnp.logical_not(used0))
    def _():
        y_ref[lo, :] = jnp.zeros((tm, D), y_ref.dtype)

    @pl.when(jnp.logical_not(used1))
    def _():
        y_ref[hi, :] = jnp.zeros((tm, D), y_ref.dtype)


def _ffn_call(tile_exp, next_exp, tile_rows, n_used, xs, w_gate, w_up, w_down, b_gate, b_up,
              b_down):
    n_rows = xs.shape[0]
    tm = TM_FFN
    n_tiles = n_rows // tm
    assert D == D_FF
    biases = (b_gate.reshape(N_EXP, 1, D_FF), b_up.reshape(N_EXP, 1, D_FF),
              b_down.reshape(N_EXP, 1, D))

    assert n_tiles % 2 == 0

    def xrow(i, te, nx, vr, nu):
        return (jnp.maximum(jnp.minimum(i, (nu[0] - 1) // 2), 0), 0)

    def bias(k):
        return lambda i, te, nx, vr, nu: (te[2 * i + k], 0, 0)

    grid_spec = pltpu.PrefetchScalarGridSpec(
        num_scalar_prefetch=4,
        grid=(n_tiles // 2,),
        in_specs=[pl.BlockSpec((2 * tm, D), xrow),
                  pl.BlockSpec(memory_space=pl.ANY),
                  pl.BlockSpec(memory_space=pl.ANY),
                  pl.BlockSpec(memory_space=pl.ANY),
                  pl.BlockSpec((None, 1, D_FF), bias(0)),
                  pl.BlockSpec((None, 1, D_FF), bias(0)),
                  pl.BlockSpec((None, 1, D), bias(0)),
                  pl.BlockSpec((None, 1, D_FF), bias(1)),
                  pl.BlockSpec((None, 1, D_FF), bias(1)),
                  pl.BlockSpec((None, 1, D), bias(1))],
        out_specs=pl.BlockSpec((2 * tm, D), lambda i, te, nx, vr, nu: (i, 0)),
        scratch_shapes=[pltpu.VMEM((3, D, D_FF), F32),
                        pltpu.VMEM((D, D_FF), BF16), pltpu.VMEM((D, D_FF), BF16),
                        pltpu.VMEM((D_FF, D), BF16), pltpu.SemaphoreType.DMA((3,))],
    )
    return pl.pallas_call(
        _ffn_kernel,
        grid_spec=grid_spec,
        out_shape=jax.ShapeDtypeStruct((n_rows, D), F32),
        compiler_params=_cparams(("arbitrary",)),
        name="ffn",
    )(tile_exp, next_exp, tile_rows, n_used, xs, w_gate, w_up, w_down, *(2 * biases))


def _combine_kernel(meta_ref, y_hbm, route_ref, x1_ref, mod_ref, gpost_ref, o_ref, stage, sems):
    i = pl.program_id(0)
    n_steps = pl.num_programs(0)
    n_seg = n_steps * N_EXP
    tm = x1_ref.shape[0]
    slot = i % 2

    def fetch(tile, into):
        def issue(e, carry):
            s = tile * N_EXP + e
            _seg_copies(y_hbm, stage.at[into], meta_ref[2 * n_seg + s], meta_ref[s],
                        meta_ref[n_seg + s], sems.at[into])
            return carry

        lax.fori_loop(0, N_EXP, issue, 0)

    @pl.when(i == 0)
    def _():
        stage[...] = jnp.zeros_like(stage)
        fetch(0, 0)

    @pl.when(i + 1 < n_steps)
    def _():
        fetch(i + 1, 1 - slot)

    _wait_rows(y_hbm, stage.at[slot], meta_ref[3 * n_seg + i], sems.at[slot])

    route = route_ref[...]
    col = lax.broadcasted_iota(jnp.int32, (tm, STAGE_ROWS), 1).astype(F32)
    wmat = jnp.zeros((tm, STAGE_ROWS), F32)
    for kk in range(TOP_K):
        hit = col == route[:, ROUTE_P + kk:ROUTE_P + kk + 1]
        wmat = jnp.where(hit, route[:, ROUTE_W + kk:ROUTE_W + kk + 1], wmat)
    moe = _dot(wmat.astype(BF16), stage[slot].astype(BF16))
    gt_f = mod_ref[...][:, 5 * D:6 * D]
    o_ref[...] = x1_ref[...] + gt_f * _rms(moe, gpost_ref[...])


def _combine_call(meta, y, route, x1, mod3, tiles_per_seq, gpost):
    n_tok = x1.shape[0]
    tm = TM_PROJ
    grid_spec = pltpu.PrefetchScalarGridSpec(
        num_scalar_prefetch=1,
        grid=(n_tok // tm,),
        in_specs=[pl.BlockSpec(memory_space=pl.ANY),
                  pl.BlockSpec((tm, LANES), lambda i, m: (i, 0)),
                  pl.BlockSpec((tm, D), lambda i, m: (i, 0)),
                  pl.BlockSpec((None, 1, 6 * D), lambda i, m: (i // tiles_per_seq, 0, 0)),
                  pl.BlockSpec((1, D), lambda i, m: (0, 0))],
        out_specs=pl.BlockSpec((tm, D), lambda i, m: (i, 0)),
        scratch_shapes=[pltpu.VMEM((2, STAGE_ROWS, D), F32), pltpu.SemaphoreType.DMA((2,))],
    )
    return pl.pallas_call(
        _combine_kernel,
        grid_spec=grid_spec,
        out_shape=jax.ShapeDtypeStruct((n_tok, D), F32),
        compiler_params=_cparams(("arbitrary",)),
        name="combine",
    )(meta, y, route, x1, mod3, gpost)


def _rope_swap(w):
    q = MLA_ROPE // 4
    return jnp.concatenate([-w[..., q:2 * q], w[..., 0:q], -w[..., 3 * q:4 * q],
                            w[..., 2 * q:3 * q]], axis=-1)


def _rope_tables(seq):
    rows = seq // GRID_W
    r, col = np.meshgrid(np.arange(rows, dtype=np.float64), np.arange(GRID_W, dtype=np.float64),
                         indexing="ij")
    half = MLA_ROPE // 2
    inv_freq = ROPE_BASE ** (-np.arange(0, half, 2, dtype=np.float64) / half)
    ar = r.reshape(-1)[:, None] * inv_freq
    ac = col.reshape(-1)[:, None] * inv_freq
    cos = np.concatenate([np.cos(ar), np.cos(ar), np.cos(ac), np.cos(ac)], axis=-1)
    sin = np.concatenate([np.sin(ar), np.sin(ar), np.sin(ac), np.sin(ac)], axis=-1)
    return cos, sin


def _head_slots(w3, lead):
    rows, heads, width = w3.shape
    return jnp.pad(w3, ((0, 0), (0, 0), (lead, HEAD_SLOT - lead - width))).reshape(
        rows, heads * HEAD_SLOT)


def kernel(x, c, ctx, c_ctx, w_mod, b_mod, g_pre_mix, g_post_mix, g_pre_ffn, g_post_ffn, w_in,
           gla_w_a2_f, gla_b_a_f, gla_w_a2_b, gla_b_a_b, gla_g_norm, mla_g_q, mla_w_uq, mla_g_kv,
           mla_w_uk, mla_w_uv, w_br_gla, w_br_mla, w_out, router_w, router_b, w_gate, b_gate,
           w_up, b_up, w_down, b_down):
    depth = w_mod.shape[0]
    assert depth == 1, "single-layer block"
    batch, seq, d = x.shape
    ctx_len = ctx.shape[1]
    assert d == D and seq % TM_PROJ == 0 and (batch * ctx_len) % TM_PROJ == 0
    assert TM_PROJ % ctx_len == 0 or ctx_len % TM_PROJ == 0
    n_tok = batch * seq

    cc = jnp.zeros((16, D), F32).at[:batch].set(c).at[batch].set(c_ctx)
    mod = _mod_call(cc, w_mod[0], b_mod[0])
    mod3 = mod.reshape(16, 1, 6 * D)

    wi = w_in[0]
    kr = wi[:, O_KR:O_KR + MLA_ROPE]
    small = jnp.concatenate([wi[:, O_AF:O_AF + GLA_RANK], wi[:, O_AB:O_AB + GLA_RANK], kr,
                             _rope_swap(kr), jnp.zeros((D, LANES - X_KRS - MLA_ROPE), F32)],
                            axis=1)
    w1 = jnp.concatenate([wi[:, O_Q:O_G], wi[:, O_DKV:O_DKV + MLA_KVR], small], axis=1).astype(BF16)
    w2 = jnp.concatenate([wi[:, O_G:O_G + GLA_V], wi[:, O_DQ:O_DQ + MLA_QR], wi[:, O_MG:]],
                         axis=1).astype(BF16)
    wa = jnp.zeros((LANES, 2 * GLA_QK), F32)
    wa = wa.at[X_AF:X_AF + GLA_RANK, 0:GLA_QK].set(gla_w_a2_f[0])
    wa = wa.at[X_AB:X_AB + GLA_RANK, GLA_QK:].set(gla_w_a2_b[0]).astype(BF16)
    ba = jnp.concatenate([gla_b_a_f[0], gla_b_a_b[0]]).reshape(1, 2 * GLA_QK)

    uk = mla_w_uk[0]
    uv = mla_w_uv[0]
    uq = mla_w_uq[0]
    wk_top = _head_slots(uk.reshape(MLA_KVR, MLA_H, MLA_NOPE), 0)
    place = np.zeros((LANES, MLA_H, HEAD_SLOT), np.float32)
    for j in range(MLA_ROPE):
        place[X_KR + j, :, MLA_NOPE + j] = 1.0
        place[X_KRS + j, :, MLA_NOPE + j] = 1.0
    wk = jnp.concatenate([wk_top, jnp.asarray(place.reshape(LANES, MLA_W))],
                         axis=0).astype(BF16)
    uv4 = uv.reshape(MLA_KVR, MLA_H // 2, 2, MLA_V)
    zv = jnp.zeros_like(uv4[:, :, 0])
    wv = jnp.stack([jnp.concatenate([uv4[:, :, 0], zv], axis=-1),
                    jnp.concatenate([zv, uv4[:, :, 1]], axis=-1)],
                   axis=2).reshape(MLA_KVR, MLA_W).astype(BF16)
    uq3 = uq.reshape(MLA_QR, MLA_H, MLA_QKD)
    wq_a = _head_slots(uq3, 0)
    wq_b = _head_slots(_rope_swap(uq3[:, :, MLA_NOPE:]), MLA_NOPE)
    wq = jnp.concatenate([wq_a, wq_b], axis=1).astype(BF16)

    cos, sin = _rope_tables(seq)
    zeros32 = np.zeros((seq, MLA_ROPE))
    t1_x = jnp.asarray(np.concatenate([zeros32, cos, sin, zeros32], axis=1), F32)
    t1_c = jnp.asarray(np.broadcast_to(
        np.concatenate([np.zeros(MLA_ROPE), np.ones(MLA_ROPE), np.zeros(2 * MLA_ROPE)]),
        (TM_PROJ, LANES)), F32)
    q_scale = MLA_SCALE * LOG2_E
    cq = jnp.asarray(np.concatenate([np.ones((seq, MLA_NOPE)), cos, zeros32], axis=1) * q_scale, F32)
    sq = jnp.asarray(np.concatenate([np.zeros((seq, MLA_NOPE)), sin, zeros32], axis=1) * q_scale,
                     F32)

    gpre = g_pre_mix[0].reshape(1, D)
    gkv = mla_g_kv[0].reshape(1, MLA_KVR)
    gq = mla_g_q[0].reshape(1, MLA_QR)
    tiles_per_seq = seq // TM_PROJ

    xf = x.reshape(n_tok, D)
    cf = ctx.reshape(batch * ctx_len, D)
    (gqx, gkx, gvx, lfx, lbx, mkx, mvx, sz, mq, sg, sm) = _inproj_call(
        True, xf, mod3, lambda i: i // tiles_per_seq, tiles_per_seq, gpre, w1, wa, ba, wk, wv,
        gkv, t1_x, extra=(w2, gq, wq, cq, sq))
    (gqc, gkc, gvc, lfc, lbc, mkc, mvc) = _inproj_call(
        False, cf, mod3, lambda i: batch, 1, gpre, w1, wa, ba, wk, wv, gkv, t1_c)

    og = _gla_call((gqx, gkx, gvx, lfx, lbx), (gqc, gkc, gvc, lfc, lbc), batch, seq, ctx_len)
    om = _mla_call(mq, mkx, mvx, mkc, mvc, batch, seq, ctx_len)

    wr = jnp.pad(router_w[0].T, ((0, LANES - N_EXP), (0, 0)))
    wrh = wr.astype(BF16)
    wr2 = jnp.concatenate([wrh, (wr - wrh.astype(F32)).astype(BF16)], axis=0)
    br = jnp.pad(router_b[0], (0, LANES - N_EXP)).reshape(LANES, 1)
    x1, h2, route, route_t, cnt = _mixout_call(
        og, sz, om, sg, sm, xf, mod3, tiles_per_seq, gla_g_norm[0].reshape(1, GLA_DV),
        w_br_gla[0].astype(BF16), w_br_mla[0].astype(BF16), w_out[0].astype(BF16),
        g_post_mix[0].reshape(1, D), g_pre_ffn[0].reshape(1, D), wr2, br)

    seg = -(-cnt[:, :N_EXP, 0].astype(jnp.int32) // SUBLANES) * SUBLANES
    stage_start = jnp.cumsum(seg, axis=1) - seg
    in_group = jnp.cumsum(seg, axis=0) - seg
    group = jnp.sum(seg, axis=0)
    padded = -(-group // TM_FFN) * TM_FFN
    ends = jnp.cumsum(padded)
    starts = ends - padded
    sorted_row = starts[None, :] + in_group
    meta = jnp.concatenate([stage_start.reshape(-1), seg.reshape(-1), sorted_row.reshape(-1),
                            jnp.sum(seg, axis=1)]).astype(jnp.int32)
    pad_info = jnp.concatenate([starts + group, ends[-1:], ends]).astype(jnp.int32)
    n_rows = _sorted_rows(n_tok)
    tile_start = jnp.arange(n_rows // TM_FFN, dtype=jnp.int32) * TM_FFN
    tile_exp = jnp.minimum(jnp.sum((ends[None, :] <= tile_start[:, None]).astype(jnp.int32), axis=1),
                           N_EXP - 1)
    n_used = (ends[-1:] // TM_FFN).astype(jnp.int32)
    ids = jnp.arange(N_EXP, dtype=jnp.int32)
    later = jnp.logical_and(padded[None, :] > 0, ids[None, :] > ids[:, None])
    next_owner = jnp.min(jnp.where(later, ids[None, :], N_EXP), axis=1)
    next_owner = jnp.where(next_owner == N_EXP, -1, next_owner)
    own = tile_exp[:, None] == ids[None, :]
    next_exp = jnp.sum(jnp.where(own, next_owner[None, :], 0), axis=1).astype(jnp.int32)
    data_end = jnp.sum(jnp.where(own, (starts + group)[None, :], 0), axis=1)
    tile_rows = jnp.clip(data_end - tile_start, 0, TM_FFN).astype(jnp.int32)

    xs = _dispatch_call(meta, pad_info, route_t, h2, n_rows)
    y = _ffn_call(tile_exp, next_exp, tile_rows, n_used, xs, w_gate[0], w_up[0], w_down[0],
                  b_gate[0], b_up[0], b_down[0])
    out = _combine_call(meta, y, route, x1, mod3, tiles_per_seq, g_post_ffn[0].reshape(1, D))
    return out.reshape(batch, seq, D)
```

```python
import functools

import jax
import jax.numpy as jnp
import numpy as np
from jax import lax
from jax.experimental import pallas as pl
from jax.experimental.pallas import tpu as pltpu

F32 = jnp.float32
BF16 = jnp.bfloat16

D = 1024
EPS = 1e-6
GRID_W = 64

GLA_H = 4
GLA_DK = 64
GLA_DV = 128
GLA_RANK = 16
GLA_TAU = 16.0
GLA_CHUNK = 64
GLA_BLOCK = 256
GLA_QK = GLA_H * GLA_DK
GLA_V = GLA_H * GLA_DV

MLA_H = 8
MLA_QR = 256
MLA_KVR = 128
MLA_NOPE = 64
MLA_ROPE = 32
MLA_V = 64
MLA_QKD = MLA_NOPE + MLA_ROPE
MLA_SCALE = MLA_QKD ** -0.5
LOG2_E = 1.4426950408889634
ROPE_BASE = 10000.0
HEAD_SLOT = 128
MLA_W = MLA_H * HEAD_SLOT
SUM_LANE_EVEN, SUM_LANE_ODD = MLA_V, 0

N_EXP = 32
TOP_K = 4
D_FF = 1024
SWIGLU_LIMIT = 7.0
SWIGLU_ALPHA = 1.702

LANES = 128
SUBLANES = 8
TM_PROJ = 512
TQ = 1024
TM_FFN = 512
WEIGHT_DMA_PRIORITY = 1
FFN_SUB = 128
ZERO_ROWS = 256
LONG_SEGMENT = 128
STAGE_ROWS = -(-(TM_PROJ * TOP_K + N_EXP * (SUBLANES - 1)) // 256) * 256
ROUTE_E, ROUTE_W, ROUTE_P = 0, TOP_K, 2 * TOP_K
ROUTE_ROWS = 16

_OFF = np.cumsum([0, GLA_QK, GLA_QK, GLA_V, GLA_V, GLA_RANK, GLA_RANK,
                  MLA_QR, MLA_KVR, MLA_ROPE, D, D])
(O_Q, O_K, O_V, O_G, O_AF, O_AB, O_DQ, O_DKV, O_KR, O_MG, O_MM, _) = _OFF.tolist()

X_AF, X_AB, X_KR, X_KRS = 0, 16, 32, 64

VMEM_LIMIT = 56 * 1024 * 1024


def _cparams(sem):
    return pltpu.CompilerParams(dimension_semantics=sem, vmem_limit_bytes=VMEM_LIMIT)


def _resident(arr):
    nd = arr.ndim
    return pl.BlockSpec(arr.shape, lambda *_: (0,) * nd, pipeline_mode=pl.Buffered(1))


def _rms(x, g):
    return x * lax.rsqrt(jnp.mean(x * x, axis=-1, keepdims=True) + EPS) * g


def _sigmoid(x):
    return 1.0 / (1.0 + jnp.exp(-x))


def _log_sigmoid(x):
    return jnp.minimum(x, 0.0) - jnp.log1p(jnp.exp(-jnp.abs(x)))


def _dot(a, b):
    return jnp.dot(a, b, preferred_element_type=F32)


def _dot_nt(a, b):
    return lax.dot_general(a, b, (((1,), (1,)), ((), ())), preferred_element_type=F32)


def _dot_tn(a, b):
    return lax.dot_general(a, b, (((0,), (0,)), ((), ())), preferred_element_type=F32)


def _mod_kernel(c_ref, w_ref, b_ref, o_ref):
    c = c_ref[...]
    s = (c * _sigmoid(c)).astype(BF16)
    o_ref[...] = _dot(s, w_ref[...].astype(BF16)) + b_ref[...]


def _mod_call(cc, w_mod, b_mod):
    n = w_mod.shape[1]
    tn = 1536
    return pl.pallas_call(
        _mod_kernel,
        grid=(n // tn,),
        in_specs=[pl.BlockSpec((16, D), lambda j: (0, 0)),
                  pl.BlockSpec((D, tn), lambda j: (0, j)),
                  pl.BlockSpec((1, tn), lambda j: (0, j))],
        out_specs=pl.BlockSpec((16, tn), lambda j: (0, j)),
        out_shape=jax.ShapeDtypeStruct((16, n), F32),
        compiler_params=_cparams(("arbitrary",)),
        name="mod",
    )(cc, w_mod, b_mod.reshape(1, n))


def _inproj_kernel(with_q, x_ref, mod_ref, gpre_ref, w1_ref, wa_ref, ba_ref, wk_ref,
                   wv_ref, gkv_ref, t1_ref, *rest):
    if with_q:
        (w2_ref, gq_ref, wq_ref, cq_ref, sq_ref,
         q_o, k_o, v_o, lf_o, lb_o, mk_o, mv_o, sz_o, mq_o, sg_o, sm_o) = rest
    else:
        (q_o, k_o, v_o, lf_o, lb_o, mk_o, mv_o) = rest
    x = x_ref[...]
    mod = mod_ref[...]
    sh = mod[:, 0:D]
    sc = mod[:, D:2 * D]
    h = (_rms(x, gpre_ref[...]) * (1.0 + sc) + sh).astype(BF16)

    z1 = _dot(h, w1_ref[...])
    q_o[...] = (z1[:, 0:GLA_QK] * (GLA_DK ** -0.5)).astype(BF16)
    k_o[...] = z1[:, GLA_QK:2 * GLA_QK].astype(BF16)
    v_o[...] = z1[:, 2 * GLA_QK:2 * GLA_QK + GLA_V].astype(BF16)
    o_dkv = 2 * GLA_QK + GLA_V
    ckv = _rms(z1[:, o_dkv:o_dkv + MLA_KVR], gkv_ref[...])
    xs = z1[:, o_dkv + MLA_KVR:o_dkv + MLA_KVR + LANES]

    la = _log_sigmoid(_dot(xs.astype(BF16), wa_ref[...]) + ba_ref[...]) * (1.0 / GLA_TAU)
    lf_o[...] = la[:, 0:GLA_QK]
    lb_o[...] = la[:, GLA_QK:2 * GLA_QK]

    lhs_k = jnp.concatenate([ckv, xs * t1_ref[...]], axis=-1).astype(BF16)
    mk_o[...] = _dot(lhs_k, wk_ref[...]).astype(BF16)
    lane = lax.broadcasted_iota(jnp.int32, (x.shape[0], MLA_W), 1)
    mv = _dot(ckv.astype(BF16), wv_ref[...])
    mv_o[...] = jnp.where(lane % (2 * HEAD_SLOT) == SUM_LANE_EVEN, 1.0,
                          jnp.where(lane % (2 * HEAD_SLOT) == HEAD_SLOT + SUM_LANE_ODD, 1.0,
                                    mv)).astype(BF16)

    if with_q:
        zg = _dot(h, w2_ref[:, 0:GLA_V])
        sz_o[...] = (zg * _sigmoid(zg)).astype(BF16)
        n = _rms(_dot(h, w2_ref[:, GLA_V:GLA_V + MLA_QR]), gq_ref[...]).astype(BF16)
        cq = jnp.concatenate([cq_ref[...]] * MLA_H, axis=-1)
        sq = jnp.concatenate([sq_ref[...]] * MLA_H, axis=-1)
        mq_o[...] = (_dot(n, wq_ref[:, 0:MLA_W]) * cq
                     + _dot(n, wq_ref[:, MLA_W:2 * MLA_W]) * sq).astype(BF16)
        o_mg = GLA_V + MLA_QR
        sg_o[...] = _sigmoid(_dot(h, w2_ref[:, o_mg:o_mg + D])).astype(BF16)
        sm_o[...] = _sigmoid(_dot(h, w2_ref[:, o_mg + D:o_mg + 2 * D])).astype(BF16)


def _inproj_call(with_q, xf, mod3, mod_row_fn, tiles_per_seq, gpre, w1, wa, ba, wk, wv,
                 gkv, t1, extra=()):
    n_tok = xf.shape[0]
    tm = TM_PROJ
    grid = (n_tok // tm,)
    row = lambda i: (i, 0)
    tab = lambda i: (i % tiles_per_seq, 0)
    in_specs = [
        pl.BlockSpec((tm, D), row),
        pl.BlockSpec((None, 1, 6 * D), lambda i: (mod_row_fn(i), 0, 0)),
        _resident(gpre), _resident(w1), _resident(wa), _resident(ba), _resident(wk),
        _resident(wv), _resident(gkv),
        pl.BlockSpec((tm, LANES), tab),
    ]
    widths = [(GLA_QK, BF16), (GLA_QK, BF16), (GLA_V, BF16), (GLA_QK, F32), (GLA_QK, F32),
              (MLA_W, BF16), (MLA_W, BF16)]
    args = [xf, mod3, gpre, w1, wa, ba, wk, wv, gkv, t1]
    if with_q:
        w2, gq, wq, cq, sq = extra
        in_specs += [_resident(w2), _resident(gq), _resident(wq),
                     pl.BlockSpec((tm, HEAD_SLOT), tab), pl.BlockSpec((tm, HEAD_SLOT), tab)]
        args += [w2, gq, wq, cq, sq]
        widths += [(GLA_V, BF16), (MLA_W, BF16), (D, BF16), (D, BF16)]
    return pl.pallas_call(
        functools.partial(_inproj_kernel, with_q),
        grid=grid,
        in_specs=in_specs,
        out_specs=[pl.BlockSpec((tm, w), row) for w, _ in widths],
        out_shape=[jax.ShapeDtypeStruct((n_tok, w), dt) for w, dt in widths],
        compiler_params=_cparams(("arbitrary",)),
        name="inproj_x" if with_q else "inproj_ctx",
    )(*args)


def _gla_kernel(qx, kx, vx, lfx, lbx, qc, kc, vc, lfc, lbc, o_ref, sf_ref, sb_ref):
    C = GLA_CHUNK
    R = GLA_BLOCK
    n_sub = R // C
    rr = lax.broadcasted_iota(jnp.int32, (R, R), 0)
    cc = lax.broadcasted_iota(jnp.int32, (R, R), 1)
    same = (rr // C) == (cc // C)
    tbd_f = jnp.logical_and(same, cc <= rr).astype(BF16)
    tbd_b = jnp.logical_and(same, cc >= rr).astype(BF16)
    row = lax.broadcasted_iota(jnp.int32, (C, GLA_QK), 0)
    col = lax.broadcasted_iota(jnp.int32, (C, GLA_QK), 1)
    head_qk = col // GLA_DK
    tri4_f = (col % C) <= row
    tri4_b = (col % C) >= row
    head_v = lax.broadcasted_iota(jnp.int32, (C, GLA_V), 1) // GLA_DV

    def stack_masked(x, head_of_lane):
        return jnp.concatenate(
            [jnp.where(head_of_lane == hh, x, jnp.zeros_like(x)) for hh in range(GLA_H)], axis=0)

    def block(q_ref, k_ref, v_ref, la_ref, row0, fwd, s_ref, emit):
        sl = pl.ds(row0, R)
        la = la_ref[sl, :]
        la_hi = la.astype(BF16)
        la_lo = (la - la_hi.astype(F32)).astype(BF16)
        tbd = tbd_f if fwd else tbd_b
        cum = _dot(tbd, la_hi) + _dot(tbd, la_lo)
        k = k_ref[sl, :].astype(F32)
        v = v_ref[sl, :]
        if emit:
            q = q_ref[sl, :].astype(F32)
        st = s_ref[...]
        outs = [None] * n_sub
        for ci in (range(n_sub) if fwd else reversed(range(n_sub))):
            rs = slice(ci * C, (ci + 1) * C)
            cum_c = cum[rs]
            tot = cum_c[C - 1:C, :] if fwd else cum_c[0:1, :]
            k_c = k[rs]
            v_c = v[rs]
            ke_bd = stack_masked((k_c * jnp.exp(tot - cum_c)).astype(BF16), head_qk)
            v_stack = jnp.concatenate(
                [v_c[:, hh * GLA_DV:(hh + 1) * GLA_DV] for hh in range(GLA_H)], axis=0)
            if emit:
                qd = (q[rs] * jnp.exp(cum_c)).astype(BF16)
                ki_bd = stack_masked((k_c * jnp.exp(-cum_c)).astype(BF16), head_qk)
                att = _dot_nt(qd, ki_bd)
                att = jnp.where(tri4_f if fwd else tri4_b, att, 0.0).astype(BF16)
                o_intra = _dot(att, stack_masked(v_c, head_v))
                oi = _dot_nt(stack_masked(qd, head_qk), st.astype(BF16))
                o_inter = jnp.concatenate([oi[hh * C:(hh + 1) * C] for hh in range(GLA_H)],
                                          axis=-1)
                outs[ci] = o_intra + o_inter
            st = st * jnp.exp(tot) + _dot_tn(v_stack, ke_bd)
        s_ref[...] = st
        if emit:
            o_ref[sl, :] += jnp.concatenate(outs, axis=0)

    sf_ref[...] = jnp.zeros_like(sf_ref)
    sb_ref[...] = jnp.zeros_like(sb_ref)
    o_ref[...] = jnp.zeros_like(o_ref)
    n_ctx = qc.shape[0] // R
    n_x = qx.shape[0] // R
    for i in range(n_ctx):
        block(qc, kc, vc, lfc, i * R, True, sf_ref, False)
        block(qc, kc, vc, lbc, (n_ctx - 1 - i) * R, False, sb_ref, False)

    def body(i, carry):
        block(qx, kx, vx, lfx, pl.multiple_of(i * R, R), True, sf_ref, True)
        block(qx, kx, vx, lbx, pl.multiple_of((n_x - 1 - i) * R, R), False, sb_ref, True)
        return carry

    lax.fori_loop(0, n_x, body, 0)


def _gla_call(fx, fc, batch, seq, ctx_len):
    qx, kx, vx, lfx, lbx = fx
    qc, kc, vc, lfc, lbc = fc

    def spec(rows, w):
        return pl.BlockSpec((rows, w), lambda b: (b, 0))

    return pl.pallas_call(
        _gla_kernel,
        grid=(batch,),
        in_specs=[spec(seq, GLA_QK), spec(seq, GLA_QK), spec(seq, GLA_V), spec(seq, GLA_QK),
                  spec(seq, GLA_QK),
                  spec(ctx_len, GLA_QK), spec(ctx_len, GLA_QK), spec(ctx_len, GLA_V),
                  spec(ctx_len, GLA_QK), spec(ctx_len, GLA_QK)],
        out_specs=spec(seq, GLA_V),
        out_shape=jax.ShapeDtypeStruct((batch * seq, GLA_V), F32),
        scratch_shapes=[pltpu.VMEM((GLA_DV, GLA_QK), F32), pltpu.VMEM((GLA_DV, GLA_QK), F32)],
        compiler_params=_cparams(("arbitrary",)),
        name="gla",
    )(qx, kx, vx, lfx, lbx, qc, kc, vc, lfc, lbc)


def _mla_kernel(q_ref, kx_ref, vx_ref, kc_ref, vc_ref, o_ref):
    lane = lax.broadcasted_iota(jnp.int32, (q_ref.shape[0], HEAD_SLOT), 1)
    for j in range(MLA_H // 2):
        pair = []
        for hh, sum_lane in ((2 * j, SUM_LANE_EVEN), (2 * j + 1, SUM_LANE_ODD)):
            sl = slice(hh * HEAD_SLOT, (hh + 1) * HEAD_SLOT)
            q = q_ref[:, sl]
            sx = _dot_nt(q, kx_ref[:, sl])
            sc = _dot_nt(q, kc_ref[:, sl])
            m = jnp.maximum(jnp.max(sx, axis=-1, keepdims=True),
                            jnp.max(sc, axis=-1, keepdims=True))
            px = jnp.exp2(sx - m).astype(BF16)
            pc = jnp.exp2(sc - m).astype(BF16)
            o = _dot(px, vx_ref[:, sl]) + _dot(pc, vc_ref[:, sl])
            pair.append(o / o[:, sum_lane:sum_lane + 1])
        o_ref[:, j * HEAD_SLOT:(j + 1) * HEAD_SLOT] = jnp.where(
            lane < MLA_V, pair[0], pair[1]).astype(BF16)


def _mla_call(mq, mkx, mvx, mkc, mvc, batch, seq, ctx_len):
    nq = seq // TQ
    return pl.pallas_call(
        _mla_kernel,
        grid=(batch, nq),
        in_specs=[pl.BlockSpec((TQ, MLA_W), lambda b, i: (b * nq + i, 0)),
                  pl.BlockSpec((seq, MLA_W), lambda b, i: (b, 0)),
                  pl.BlockSpec((seq, MLA_W), lambda b, i: (b, 0)),
                  pl.BlockSpec((ctx_len, MLA_W), lambda b, i: (b, 0)),
                  pl.BlockSpec((ctx_len, MLA_W), lambda b, i: (b, 0))],
        out_specs=pl.BlockSpec((TQ, MLA_H * MLA_V), lambda b, i: (b * nq + i, 0)),
        out_shape=jax.ShapeDtypeStruct((batch * seq, MLA_H * MLA_V), BF16),
        compiler_params=_cparams(("arbitrary", "arbitrary")),
        name="mla",
    )(mq, mkx, mvx, mkc, mvc)


def _mixout_kernel(og_ref, sz_ref, om_ref, sg_ref, sm_ref, x_ref, mod_ref, gn_ref, wbg_ref,
                   wbm_ref, wo_ref, gpost_ref, gffn_ref, wr_ref, br_ref,
                   x1_o, h2_o, route_o, routet_o, cnt_o):
    tm = x_ref.shape[0]
    mod = mod_ref[...]
    gt_a = mod[:, 2 * D:3 * D]
    sh_f = mod[:, 3 * D:4 * D]
    sc_f = mod[:, 4 * D:5 * D]

    og = og_ref[...]
    gn = gn_ref[...]
    parts = [_rms(og[:, hh * GLA_DV:(hh + 1) * GLA_DV], gn) for hh in range(GLA_H)]
    a = (jnp.concatenate(parts, axis=-1) * sz_ref[...].astype(F32)).astype(BF16)
    br_g = _dot(a, wbg_ref[...])
    br_m = _dot(om_ref[...], wbm_ref[...])
    merged = (sg_ref[...].astype(F32) * br_g + sm_ref[...].astype(F32) * br_m).astype(BF16)
    mo = _dot(merged, wo_ref[...])
    x1 = x_ref[...] + gt_a * _rms(mo, gpost_ref[...])
    x1_o[...] = x1
    h2 = _rms(x1, gffn_ref[...]) * (1.0 + sc_f) + sh_f
    h2_o[...] = h2.astype(BF16)

    h_hi = h2.astype(BF16)
    h_lo = (h2 - h_hi.astype(F32)).astype(BF16)
    part = _dot_nt(wr_ref[...], h_hi) + _dot_nt(wr_ref[...], h_lo)
    logits = part[0:LANES, :] + part[LANES:2 * LANES, :] + br_ref[...]
    row = lax.broadcasted_iota(jnp.int32, (LANES, tm), 0)
    row_f = row.astype(F32)
    neg = jnp.float32(-jnp.inf)
    lg = jnp.where(row < N_EXP, logits, neg)
    hots, vals, ids = [], [], []
    for _k in range(TOP_K):
        mx = jnp.max(lg, axis=0, keepdims=True)
        idx = jnp.min(jnp.where(lg == mx, row_f, float(LANES)), axis=0, keepdims=True)
        hot = row_f == idx
        lg = jnp.where(hot, neg, lg)
        hots.append(hot)
        vals.append(mx)
        ids.append(idx)
    es = [jnp.exp(v - vals[0]) for v in vals]
    den = es[0] + es[1] + es[2] + es[3]
    ws = [e / den for e in es]

    msum = jnp.zeros((LANES, tm), F32)
    for hot in hots:
        msum = msum + hot.astype(F32)
    rr = lax.broadcasted_iota(jnp.int32, (tm, tm), 0)
    cc = lax.broadcasted_iota(jnp.int32, (tm, tm), 1)
    upper = (rr < cc).astype(BF16)
    prior = _dot(msum.astype(BF16), upper)
    cnt = jnp.broadcast_to(jnp.sum(msum, axis=1, keepdims=True), (LANES, LANES))
    seg = jnp.floor((cnt + (SUBLANES - 1.0)) * (1.0 / SUBLANES)) * SUBLANES
    er = lax.broadcasted_iota(jnp.int32, (LANES, LANES), 0)
    ec = lax.broadcasted_iota(jnp.int32, (LANES, LANES), 1)
    seg_start = _dot((ec < er).astype(BF16), seg.astype(BF16))
    slot_of = prior + jnp.concatenate([seg_start] * (tm // LANES), axis=1)

    rows = list(ids) + ws + [jnp.sum(hot.astype(F32) * slot_of, axis=0, keepdims=True)
                             for hot in hots]
    route_t = jnp.concatenate(rows + [jnp.zeros((LANES - 3 * TOP_K, tm), F32)], axis=0)
    routet_o[...] = route_t[0:ROUTE_ROWS, :]
    route_o[...] = route_t.T
    cnt_o[...] = cnt


def _mixout_call(og, sz, om, sg, sm, xf, mod3, tiles_per_seq, gn, wbg, wbm, wo, gpost, gffn,
                 wr, br):
    n_tok = xf.shape[0]
    tm = TM_PROJ
    row = lambda i: (i, 0)

    def rs(w):
        return pl.BlockSpec((tm, w), row)

    return pl.pallas_call(
        _mixout_kernel,
        grid=(n_tok // tm,),
        in_specs=[rs(GLA_V), rs(GLA_V), rs(MLA_H * MLA_V), rs(D), rs(D), rs(D),
                  pl.BlockSpec((None, 1, 6 * D), lambda i: (i // tiles_per_seq, 0, 0)),
                  _resident(gn), _resident(wbg), _resident(wbm), _resident(wo),
                  _resident(gpost), _resident(gffn), _resident(wr), _resident(br)],
        out_specs=[rs(D), rs(D), rs(LANES),
                   pl.BlockSpec((None, ROUTE_ROWS, tm), lambda i: (i, 0, 0)),
                   pl.BlockSpec((None, LANES, LANES), lambda i: (i, 0, 0))],
        out_shape=[jax.ShapeDtypeStruct((n_tok, D), F32), jax.ShapeDtypeStruct((n_tok, D), BF16),
                   jax.ShapeDtypeStruct((n_tok, LANES), F32),
                   jax.ShapeDtypeStruct((n_tok // tm, ROUTE_ROWS, tm), F32),
                   jax.ShapeDtypeStruct((n_tok // tm, LANES, LANES), F32)],
        compiler_params=_cparams(("arbitrary",)),
        name="mixout",
    )(og, sz, om, sg, sm, xf, mod3, gn, wbg, wbm, wo, gpost, gffn, wr, br)


def _sorted_rows(n_tok):
    n_steps = n_tok // TM_PROJ
    worst_pad = n_steps * N_EXP * (SUBLANES - 1) + N_EXP * TM_FFN
    per_step = -(-worst_pad // (n_steps * ZERO_ROWS)) * ZERO_ROWS
    n_rows = n_tok * TOP_K + per_step * n_steps
    assert n_rows % TM_FFN == 0
    return n_rows


def _seg_copies(src_ref, dst_ref, src0, dst0, length, sem):
    def copy_bit(b):
        size = 1 << b
        off = (length >> (b + 1)) << (b + 1)

        @pl.when((length & size) != 0)
        def _():
            pltpu.make_async_copy(
                src_ref.at[pl.ds(pl.multiple_of(src0 + off, SUBLANES), size)],
                dst_ref.at[pl.ds(pl.multiple_of(dst0 + off, SUBLANES), size)], sem).start()

    lo_bit = SUBLANES.bit_length() - 1
    hi_bit = TM_PROJ.bit_length() - 1
    split = LONG_SEGMENT.bit_length() - 1

    @pl.when(length >= LONG_SEGMENT)
    def _():
        for b in range(hi_bit, split - 1, -1):
            copy_bit(b)

    for b in range(split - 1, lo_bit - 1, -1):
        copy_bit(b)


def _wait_rows(src_ref, dst_ref, total, sem):
    for b in reversed(range(SUBLANES.bit_length() - 1, STAGE_ROWS.bit_length())):
        size = 1 << b

        @pl.when((total & size) != 0)
        def _():
            pltpu.make_async_copy(src_ref.at[pl.ds(0, size)], dst_ref.at[pl.ds(0, size)],
                                  sem).wait()


def _zero_rows(zero_ref, dst_ref, start, end, sem, wait):
    length = end - start
    for b in reversed(range(SUBLANES.bit_length() - 1, ZERO_ROWS.bit_length())):
        size = 1 << b
        off = (length >> (b + 1)) << (b + 1)

        @pl.when((length & size) != 0)
        def _():
            cp = pltpu.make_async_copy(
                zero_ref.at[pl.ds(0, size)],
                dst_ref.at[pl.ds(pl.multiple_of(start + off, SUBLANES), size)], sem)
            if wait:
                cp.wait()
            else:
                cp.start()


def _slot_onehot(route_t, n_slots):
    tm = route_t.shape[1]
    slot = lax.broadcasted_iota(jnp.int32, (n_slots, tm), 0).astype(F32)
    acc = jnp.zeros((n_slots, tm), F32)
    for kk in range(TOP_K):
        acc = jnp.where(slot == route_t[ROUTE_P + kk:ROUTE_P + kk + 1, :], 1.0, acc)
    return acc.astype(BF16)


def _dispatch_kernel(meta_ref, pad_ref, route_ref, h_ref, xs_out, stage, zero_ref, sems, zsem):
    i = pl.program_id(0)
    n_steps = pl.num_programs(0)
    n_rows = xs_out.shape[0]
    n_seg = n_steps * N_EXP
    slot = i % 2

    @pl.when(i == 0)
    def _():
        zero_ref[...] = jnp.zeros_like(zero_ref)

    def clear_uncovered(wait):
        per = -(-N_EXP // n_steps)
        for j in range(per):
            e = i * per + j

            @pl.when(e < N_EXP)
            def _():
                ec = jnp.minimum(e, N_EXP - 1)
                _zero_rows(zero_ref, xs_out, pad_ref[ec], pad_ref[N_EXP + 1 + ec], zsem, wait)

        total_end = pad_ref[N_EXP]
        tail_rows = n_rows - n_steps * TM_PROJ * TOP_K
        per_step = tail_rows // n_steps
        for j in range(per_step // ZERO_ROWS):
            seg_end = (n_rows - tail_rows) + i * per_step + (j + 1) * ZERO_ROWS
            zlen = jnp.clip(seg_end - total_end, 0, ZERO_ROWS)
            _zero_rows(zero_ref, xs_out, seg_end - zlen, seg_end, zsem, wait)

    clear_uncovered(wait=False)

    onehot = _slot_onehot(route_ref[...], STAGE_ROWS)
    stage[slot] = _dot(onehot, h_ref[...])

    def issue(e, carry):
        s = i * N_EXP + e
        _seg_copies(stage.at[slot], xs_out, meta_ref[s], meta_ref[2 * n_seg + s],
                    meta_ref[n_seg + s], sems.at[slot])
        return carry

    lax.fori_loop(0, N_EXP, issue, 0)
    clear_uncovered(wait=True)

    @pl.when(i >= 1)
    def _():
        _wait_rows(stage.at[1 - slot], xs_out, meta_ref[3 * n_seg + i - 1], sems.at[1 - slot])

    @pl.when(i == n_steps - 1)
    def _():
        _wait_rows(stage.at[slot], xs_out, meta_ref[3 * n_seg + i], sems.at[slot])


def _dispatch_call(meta, pad_info, route, h2, n_rows):
    n_tok = h2.shape[0]
    tm = TM_PROJ
    n_steps = n_tok // tm
    assert (n_rows - n_tok * TOP_K) % (n_steps * ZERO_ROWS) == 0 and TM_FFN <= 2 * ZERO_ROWS
    grid_spec = pltpu.PrefetchScalarGridSpec(
        num_scalar_prefetch=2,
        grid=(n_steps,),
        in_specs=[pl.BlockSpec((None, ROUTE_ROWS, tm), lambda i, m, p: (i, 0, 0)),
                  pl.BlockSpec((tm, D), lambda i, m, p: (i, 0))],
        out_specs=pl.BlockSpec(memory_space=pl.ANY),
        scratch_shapes=[pltpu.VMEM((2, STAGE_ROWS, D), F32), pltpu.VMEM((ZERO_ROWS, D), F32),
                        pltpu.SemaphoreType.DMA((2,)), pltpu.SemaphoreType.DMA(())],
    )
    return pl.pallas_call(
        _dispatch_kernel,
        grid_spec=grid_spec,
        out_shape=jax.ShapeDtypeStruct((n_rows, D), F32),
        compiler_params=_cparams(("arbitrary",)),
        name="dispatch",
    )(meta, pad_info, route, h2)


def _ffn_kernel(te_ref, nx_ref, vr_ref, nu_ref, x_ref, wg_hbm, wu_hbm, wd_hbm, bg0, bu0, bd0, bg1,
                bu1, bd1, y_ref, land, wgb, wub, wdb, sems):
    i = pl.program_id(0)
    tm = TM_FFN
    t0 = 2 * i
    t1 = t0 + 1
    e0 = te_ref[t0]
    e1 = te_ref[t1]
    used0 = t0 < nu_ref[0]
    used1 = t1 < nu_ref[0]
    first0 = jnp.logical_or(i == 0, e0 != te_ref[jnp.maximum(t0 - 1, 0)])
    same = jnp.logical_and(used1, e1 == e0)
    n_sub = tm // FFN_SUB
    parts0 = (vr_ref[t0] + FFN_SUB - 1) // FFN_SUB
    parts1 = (vr_ref[t1] + FFN_SUB - 1) // FFN_SUB
    joint = jnp.logical_and(same, parts1 == n_sub)
    w_hbm = (wg_hbm, wu_hbm, wd_hbm)

    def fetch(expert):
        for j in range(3):
            pltpu.make_async_copy(w_hbm[j].at[expert], land.at[j],
                                  sems.at[j]).start(priority=WEIGHT_DMA_PRIORITY)

    def switch(t):
        for j, dst in enumerate((wgb, wub, wdb)):
            pltpu.make_async_copy(w_hbm[j].at[0], land.at[j], sems.at[j]).wait()
            dst[...] = land[j].astype(BF16)
        nxt = nx_ref[t]

        @pl.when(nxt >= 0)
        def _():
            fetch(jnp.maximum(nxt, 0))

    def compute(rows, bg, bu, bd):
        x = x_ref[rows, :].astype(BF16)
        gate = jnp.minimum(_dot(x, wgb[...]) + bg[...], SWIGLU_LIMIT)
        up = jnp.clip(_dot(x, wub[...]) + bu[...], -SWIGLU_LIMIT, SWIGLU_LIMIT)
        act = ((up + 1.0) * gate * _sigmoid(SWIGLU_ALPHA * gate)).astype(BF16)
        y_ref[rows, :] = _dot(act, wdb[...]) + bd[...]

    lo = slice(0, tm)
    hi = slice(tm, 2 * tm)

    @pl.when(jnp.logical_and(used0, i == 0))
    def _():
        fetch(e0)

    @pl.when(jnp.logical_and(used0, first0))
    def _():
        switch(t0)

    def single(off, parts, bg, bu, bd):
        for n in range(1, n_sub + 1):
            @pl.when(parts == n)
            def _():
                compute(slice(off, off + n * FFN_SUB), bg, bu, bd)
                if n < n_sub:
                    y_ref[off + n * FFN_SUB:off + tm, :] = jnp.zeros((tm - n * FFN_SUB, D),
                                                                     y_ref.dtype)

    @pl.when(joint)
    def _():
        compute(slice(0, 2 * tm), bg0, bu0, bd0)

    @pl.when(jnp.logical_and(used0, jnp.logical_not(joint)))
    def _():
        single(0, parts0, bg0, bu0, bd0)

    @pl.when(jnp.logical_and(used1, jnp.logical_not(joint)))
    def _():
        @pl.when(jnp.logical_not(same))
        def _():
            switch(t1)

        single(tm, parts1, bg1, bu1, bd1)

    @pl.when(jnp.logical_not(used0))
    def _():
        y_ref[lo, :] = jnp.zeros((tm, D), y_ref.dtype)

    @pl.when(jnp.logical_not(used1))
    def _():
        y_ref[hi, :] = jnp.zeros((tm, D), y_ref.dtype)


def _ffn_call(tile_exp, next_exp, tile_rows, n_used, xs, w_gate, w_up, w_down, b_gate, b_up,
              b_down):
    n_rows = xs.shape[0]
    tm = TM_FFN
    n_tiles = n_rows // tm
    assert D == D_FF
    biases = (b_gate.reshape(N_EXP, 1, D_FF), b_up.reshape(N_EXP, 1, D_FF),
              b_down.reshape(N_EXP, 1, D))

    assert n_tiles % 2 == 0

    def xrow(i, te, nx, vr, nu):
        return (jnp.maximum(jnp.minimum(i, (nu[0] - 1) // 2), 0), 0)

    def bias(k):
        return lambda i, te, nx, vr, nu: (te[2 * i + k], 0, 0)

    grid_spec = pltpu.PrefetchScalarGridSpec(
        num_scalar_prefetch=4,
        grid=(n_tiles // 2,),
        in_specs=[pl.BlockSpec((2 * tm, D), xrow),
                  pl.BlockSpec(memory_space=pl.ANY),
                  pl.BlockSpec(memory_space=pl.ANY),
                  pl.BlockSpec(memory_space=pl.ANY),
                  pl.BlockSpec((None, 1, D_FF), bias(0)),
                  pl.BlockSpec((None, 1, D_FF), bias(0)),
                  pl.BlockSpec((None, 1, D), bias(0)),
                  pl.BlockSpec((None, 1, D_FF), bias(1)),
                  pl.BlockSpec((None, 1, D_FF), bias(1)),
                  pl.BlockSpec((None, 1, D), bias(1))],
        out_specs=pl.BlockSpec((2 * tm, D), lambda i, te, nx, vr, nu: (i, 0)),
        scratch_shapes=[pltpu.VMEM((3, D, D_FF), F32),
                        pltpu.VMEM((D, D_FF), BF16), pltpu.VMEM((D, D_FF), BF16),
                        pltpu.VMEM((D_FF, D), BF16), pltpu.SemaphoreType.DMA((3,))],
    )
    return pl.pallas_call(
        _ffn_kernel,
        grid_spec=grid_spec,
        out_shape=jax.ShapeDtypeStruct((n_rows, D), F32),
        compiler_params=_cparams(("arbitrary",)),
        name="ffn",
    )(tile_exp, next_exp, tile_rows, n_used, xs, w_gate, w_up, w_down, *(2 * biases))


def _combine_kernel(meta_ref, y_hbm, route_ref, x1_ref, mod_ref, gpost_ref, o_ref, stage, sems):
    i = pl.program_id(0)
    n_steps = pl.num_programs(0)
    n_seg = n_steps * N_EXP
    tm = x1_ref.shape[0]
    slot = i % 2

    def fetch(tile, into):
        def issue(e, carry):
            s = tile * N_EXP + e
            _seg_copies(y_hbm, stage.at[into], meta_ref[2 * n_seg + s], meta_ref[s],
                        meta_ref[n_seg + s], sems.at[into])
            return carry

        lax.fori_loop(0, N_EXP, issue, 0)

    @pl.when(i == 0)
    def _():
        stage[...] = jnp.zeros_like(stage)
        fetch(0, 0)

    @pl.when(i + 1 < n_steps)
    def _():
        fetch(i + 1, 1 - slot)

    _wait_rows(y_hbm, stage.at[slot], meta_ref[3 * n_seg + i], sems.at[slot])

    route = route_ref[...]
    col = lax.broadcasted_iota(jnp.int32, (tm, STAGE_ROWS), 1).astype(F32)
    wmat = jnp.zeros((tm, STAGE_ROWS), F32)
    for kk in range(TOP_K):
        hit = col == route[:, ROUTE_P + kk:ROUTE_P + kk + 1]
        wmat = jnp.where(hit, route[:, ROUTE_W + kk:ROUTE_W + kk + 1], wmat)
    moe = _dot(wmat.astype(BF16), stage[slot].astype(BF16))
    gt_f = mod_ref[...][:, 5 * D:6 * D]
    o_ref[...] = x1_ref[...] + gt_f * _rms(moe, gpost_ref[...])


def _combine_call(meta, y, route, x1, mod3, tiles_per_seq, gpost):
    n_tok = x1.shape[0]
    tm = TM_PROJ
    grid_spec = pltpu.PrefetchScalarGridSpec(
        num_scalar_prefetch=1,
        grid=(n_tok // tm,),
        in_specs=[pl.BlockSpec(memory_space=pl.ANY),
                  pl.BlockSpec((tm, LANES), lambda i, m: (i, 0)),
                  pl.BlockSpec((tm, D), lambda i, m: (i, 0)),
                  pl.BlockSpec((None, 1, 6 * D), lambda i, m: (i // tiles_per_seq, 0, 0)),
                  pl.BlockSpec((1, D), lambda i, m: (0, 0))],
        out_specs=pl.BlockSpec((tm, D), lambda i, m: (i, 0)),
        scratch_shapes=[pltpu.VMEM((2, STAGE_ROWS, D), F32), pltpu.SemaphoreType.DMA((2,))],
    )
    return pl.pallas_call(
        _combine_kernel,
        grid_spec=grid_spec,
        out_shape=jax.ShapeDtypeStruct((n_tok, D), F32),
        compiler_params=_cparams(("arbitrary",)),
        name="combine",
    )(meta, y, route, x1, mod3, gpost)


def _rope_swap(w):
    q = MLA_ROPE // 4
    return jnp.concatenate([-w[..., q:2 * q], w[..., 0:q], -w[..., 3 * q:4 * q],
                            w[..., 2 * q:3 * q]], axis=-1)


def _rope_tables(seq):
    rows = seq // GRID_W
    r, col = np.meshgrid(np.arange(rows, dtype=np.float64), np.arange(GRID_W, dtype=np.float64),
                         indexing="ij")
    half = MLA_ROPE // 2
    inv_freq = ROPE_BASE ** (-np.arange(0, half, 2, dtype=np.float64) / half)
    ar = r.reshape(-1)[:, None] * inv_freq
    ac = col.reshape(-1)[:, None] * inv_freq
    cos = np.concatenate([np.cos(ar), np.cos(ar), np.cos(ac), np.cos(ac)], axis=-1)
    sin = np.concatenate([np.sin(ar), np.sin(ar), np.sin(ac), np.sin(ac)], axis=-1)
    return cos, sin


def _head_slots(w3, lead):
    rows, heads, width = w3.shape
    return jnp.pad(w3, ((0, 0), (0, 0), (lead, HEAD_SLOT - lead - width))).reshape(
        rows, heads * HEAD_SLOT)


def kernel(x, c, ctx, c_ctx, w_mod, b_mod, g_pre_mix, g_post_mix, g_pre_ffn, g_post_ffn, w_in,
           gla_w_a2_f, gla_b_a_f, gla_w_a2_b, gla_b_a_b, gla_g_norm, mla_g_q, mla_w_uq, mla_g_kv,
           mla_w_uk, mla_w_uv, w_br_gla, w_br_mla, w_out, router_w, router_b, w_gate, b_gate,
           w_up, b_up, w_down, b_down):
    depth = w_mod.shape[0]
    assert depth == 1, "single-layer block"
    batch, seq, d = x.shape
    ctx_len = ctx.shape[1]
    assert d == D and seq % TM_PROJ == 0 and (batch * ctx_len) % TM_PROJ == 0
    assert TM_PROJ % ctx_len == 0 or ctx_len % TM_PROJ == 0
    n_tok = batch * seq

    cc = jnp.zeros((16, D), F32).at[:batch].set(c).at[batch].set(c_ctx)
    mod = _mod_call(cc, w_mod[0], b_mod[0])
    mod3 = mod.reshape(16, 1, 6 * D)

    wi = w_in[0]
    kr = wi[:, O_KR:O_KR + MLA_ROPE]
    small = jnp.concatenate([wi[:, O_AF:O_AF + GLA_RANK], wi[:, O_AB:O_AB + GLA_RANK], kr,
                             _rope_swap(kr), jnp.zeros((D, LANES - X_KRS - MLA_ROPE), F32)],
                            axis=1)
    w1 = jnp.concatenate([wi[:, O_Q:O_G], wi[:, O_DKV:O_DKV + MLA_KVR], small], axis=1).astype(BF16)
    w2 = jnp.concatenate([wi[:, O_G:O_G + GLA_V], wi[:, O_DQ:O_DQ + MLA_QR], wi[:, O_MG:]],
                         axis=1).astype(BF16)
    wa = jnp.zeros((LANES, 2 * GLA_QK), F32)
    wa = wa.at[X_AF:X_AF + GLA_RANK, 0:GLA_QK].set(gla_w_a2_f[0])
    wa = wa.at[X_AB:X_AB + GLA_RANK, GLA_QK:].set(gla_w_a2_b[0]).astype(BF16)
    ba = jnp.concatenate([gla_b_a_f[0], gla_b_a_b[0]]).reshape(1, 2 * GLA_QK)

    uk = mla_w_uk[0]
    uv = mla_w_uv[0]
    uq = mla_w_uq[0]
    wk_top = _head_slots(uk.reshape(MLA_KVR, MLA_H, MLA_NOPE), 0)
    place = np.zeros((LANES, MLA_H, HEAD_SLOT), np.float32)
    for j in range(MLA_ROPE):
        place[X_KR + j, :, MLA_NOPE + j] = 1.0
        place[X_KRS + j, :, MLA_NOPE + j] = 1.0
    wk = jnp.concatenate([wk_top, jnp.asarray(place.reshape(LANES, MLA_W))],
                         axis=0).astype(BF16)
    uv4 = uv.reshape(MLA_KVR, MLA_H // 2, 2, MLA_V)
    zv = jnp.zeros_like(uv4[:, :, 0])
    wv = jnp.stack([jnp.concatenate([uv4[:, :, 0], zv], axis=-1),
                    jnp.concatenate([zv, uv4[:, :, 1]], axis=-1)],
                   axis=2).reshape(MLA_KVR, MLA_W).astype(BF16)
    uq3 = uq.reshape(MLA_QR, MLA_H, MLA_QKD)
    wq_a = _head_slots(uq3, 0)
    wq_b = _head_slots(_rope_swap(uq3[:, :, MLA_NOPE:]), MLA_NOPE)
    wq = jnp.concatenate([wq_a, wq_b], axis=1).astype(BF16)

    cos, sin = _rope_tables(seq)
    zeros32 = np.zeros((seq, MLA_ROPE))
    t1_x = jnp.asarray(np.concatenate([zeros32, cos, sin, zeros32], axis=1), F32)
    t1_c = jnp.asarray(np.broadcast_to(
        np.concatenate([np.zeros(MLA_ROPE), np.ones(MLA_ROPE), np.zeros(2 * MLA_ROPE)]),
        (TM_PROJ, LANES)), F32)
    q_scale = MLA_SCALE * LOG2_E
    cq = jnp.asarray(np.concatenate([np.ones((seq, MLA_NOPE)), cos, zeros32], axis=1) * q_scale, F32)
    sq = jnp.asarray(np.concatenate([np.zeros((seq, MLA_NOPE)), sin, zeros32], axis=1) * q_scale,
                     F32)

    gpre = g_pre_mix[0].reshape(1, D)
    gkv = mla_g_kv[0].reshape(1, MLA_KVR)
    gq = mla_g_q[0].reshape(1, MLA_QR)
    tiles_per_seq = seq // TM_PROJ

    xf = x.reshape(n_tok, D)
    cf = ctx.reshape(batch * ctx_len, D)
    (gqx, gkx, gvx, lfx, lbx, mkx, mvx, sz, mq, sg, sm) = _inproj_call(
        True, xf, mod3, lambda i: i // tiles_per_seq, tiles_per_seq, gpre, w1, wa, ba, wk, wv,
        gkv, t1_x, extra=(w2, gq, wq, cq, sq))
    (gqc, gkc, gvc, lfc, lbc, mkc, mvc) = _inproj_call(
        False, cf, mod3, lambda i: batch, 1, gpre, w1, wa, ba, wk, wv, gkv, t1_c)

    og = _gla_call((gqx, gkx, gvx, lfx, lbx), (gqc, gkc, gvc, lfc, lbc), batch, seq, ctx_len)
    om = _mla_call(mq, mkx, mvx, mkc, mvc, batch, seq, ctx_len)

    wr = jnp.pad(router_w[0].T, ((0, LANES - N_EXP), (0, 0)))
    wrh = wr.astype(BF16)
    wr2 = jnp.concatenate([wrh, (wr - wrh.astype(F32)).astype(BF16)], axis=0)
    br = jnp.pad(router_b[0], (0, LANES - N_EXP)).reshape(LANES, 1)
    x1, h2, route, route_t, cnt = _mixout_call(
        og, sz, om, sg, sm, xf, mod3, tiles_per_seq, gla_g_norm[0].reshape(1, GLA_DV),
        w_br_gla[0].astype(BF16), w_br_mla[0].astype(BF16), w_out[0].astype(BF16),
        g_post_mix[0].reshape(1, D), g_pre_ffn[0].reshape(1, D), wr2, br)

    seg = -(-cnt[:, :N_EXP, 0].astype(jnp.int32) // SUBLANES) * SUBLANES
    stage_start = jnp.cumsum(seg, axis=1) - seg
    in_group = jnp.cumsum(seg, axis=0) - seg
    group = jnp.sum(seg, axis=0)
    padded = -(-group // TM_FFN) * TM_FFN
    ends = jnp.cumsum(padded)
    starts = ends - padded
    sorted_row = starts[None, :] + in_group
    meta = jnp.concatenate([stage_start.reshape(-1), seg.reshape(-1), sorted_row.reshape(-1),
                            jnp.sum(seg, axis=1)]).astype(jnp.int32)
    pad_info = jnp.concatenate([starts + group, ends[-1:], ends]).astype(jnp.int32)
    n_rows = _sorted_rows(n_tok)
    tile_start = jnp.arange(n_rows // TM_FFN, dtype=jnp.int32) * TM_FFN
    tile_exp = jnp.minimum(jnp.sum((ends[None, :] <= tile_start[:, None]).astype(jnp.int32), axis=1),
                           N_EXP - 1)
    n_used = (ends[-1:] // TM_FFN).astype(jnp.int32)
    ids = jnp.arange(N_EXP, dtype=jnp.int32)
    later = jnp.logical_and(padded[None, :] > 0, ids[None, :] > ids[:, None])
    next_owner = jnp.min(jnp.where(later, ids[None, :], N_EXP), axis=1)
    next_owner = jnp.where(next_owner == N_EXP, -1, next_owner)
    own = tile_exp[:, None] == ids[None, :]
    next_exp = jnp.sum(jnp.where(own, next_owner[None, :], 0), axis=1).astype(jnp.int32)
    data_end = jnp.sum(jnp.where(own, (starts + group)[None, :], 0), axis=1)
    tile_rows = jnp.clip(data_end - tile_start, 0, TM_FFN).astype(jnp.int32)

    xs = _dispatch_call(meta, pad_info, route_t, h2, n_rows)
    y = _ffn_call(tile_exp, next_exp, tile_rows, n_used, xs, w_gate[0], w_up[0], w_down[0],
                  b_gate[0], b_up[0], b_down[0])
    out = _combine_call(meta, y, route, x1, mod3, tiles_per_seq, g_post_ffn[0].reshape(1, D))
    return out.reshape(batch, seq, D)
```

```python
import functools

import jax
import jax.numpy as jnp
import numpy as np
from jax import lax
from jax.experimental import pallas as pl
from jax.experimental.pallas import tpu as pltpu

F32 = jnp.float32
BF16 = jnp.bfloat16

D = 1024
EPS = 1e-6
GRID_W = 64

GLA_H = 4
GLA_DK = 64
GLA_DV = 128
GLA_RANK = 16
GLA_TAU = 16.0
GLA_CHUNK = 64
GLA_BLOCK = 256
GLA_QK = GLA_H * GLA_DK
GLA_V = GLA_H * GLA_DV

MLA_H = 8
MLA_QR = 256
MLA_KVR = 128
MLA_NOPE = 64
MLA_ROPE = 32
MLA_V = 64
MLA_QKD = MLA_NOPE + MLA_ROPE
MLA_SCALE = MLA_QKD ** -0.5
LOG2_E = 1.4426950408889634
ROPE_BASE = 10000.0
HEAD_SLOT = 128
MLA_W = MLA_H * HEAD_SLOT
SUM_LANE_EVEN, SUM_LANE_ODD = MLA_V, 0

N_EXP = 32
TOP_K = 4
D_FF = 1024
SWIGLU_LIMIT = 7.0
SWIGLU_ALPHA = 1.702

LANES = 128
SUBLANES = 8
TM_PROJ = 512
TQ = 1024
TM_FFN = 512
SECOND_DMA_QUEUE = 1
FFN_SUB = 128
ZERO_ROWS = 256
LONG_SEGMENT = 128
STAGE_ROWS = -(-(TM_PROJ * TOP_K + N_EXP * (SUBLANES - 1)) // 256) * 256
ROUTE_E, ROUTE_W, ROUTE_P = 0, TOP_K, 2 * TOP_K
ROUTE_ROWS = 16

_OFF = np.cumsum([0, GLA_QK, GLA_QK, GLA_V, GLA_V, GLA_RANK, GLA_RANK,
                  MLA_QR, MLA_KVR, MLA_ROPE, D, D])
(O_Q, O_K, O_V, O_G, O_AF, O_AB, O_DQ, O_DKV, O_KR, O_MG, O_MM, _) = _OFF.tolist()

X_AF, X_AB, X_KR, X_KRS = 0, 16, 32, 64

VMEM_LIMIT = 56 * 1024 * 1024


def _cparams(sem):
    return pltpu.CompilerParams(dimension_semantics=sem, vmem_limit_bytes=VMEM_LIMIT)


def _resident(arr):
    nd = arr.ndim
    return pl.BlockSpec(arr.shape, lambda *_: (0,) * nd, pipeline_mode=pl.Buffered(1))


def _rms(x, g):
    return x * lax.rsqrt(jnp.mean(x * x, axis=-1, keepdims=True) + EPS) * g


def _sigmoid(x):
    return 1.0 / (1.0 + jnp.exp(-x))


def _log_sigmoid(x):
    return jnp.minimum(x, 0.0) - jnp.log1p(jnp.exp(-jnp.abs(x)))


def _dot(a, b):
    return jnp.dot(a, b, preferred_element_type=F32)


def _dot_nt(a, b):
    return lax.dot_general(a, b, (((1,), (1,)), ((), ())), preferred_element_type=F32)


def _dot_tn(a, b):
    return lax.dot_general(a, b, (((0,), (0,)), ((), ())), preferred_element_type=F32)


def _mod_kernel(c_ref, w_ref, b_ref, o_ref):
    c = c_ref[...]
    s = (c * _sigmoid(c)).astype(BF16)
    o_ref[...] = _dot(s, w_ref[...].astype(BF16)) + b_ref[...]


def _mod_call(cc, w_mod, b_mod):
    n = w_mod.shape[1]
    tn = 1536
    return pl.pallas_call(
        _mod_kernel,
        grid=(n // tn,),
        in_specs=[pl.BlockSpec((16, D), lambda j: (0, 0)),
                  pl.BlockSpec((D, tn), lambda j: (0, j)),
                  pl.BlockSpec((1, tn), lambda j: (0, j))],
        out_specs=pl.BlockSpec((16, tn), lambda j: (0, j)),
        out_shape=jax.ShapeDtypeStruct((16, n), F32),
        compiler_params=_cparams(("arbitrary",)),
        name="mod",
    )(cc, w_mod, b_mod.reshape(1, n))


def _inproj_kernel(with_q, x_ref, mod_ref, gpre_ref, w1_ref, wa_ref, ba_ref, wk_ref,
                   wv_ref, gkv_ref, t1_ref, *rest):
    if with_q:
        (w2_ref, gq_ref, wq_ref, cq_ref, sq_ref,
         q_o, k_o, v_o, lf_o, lb_o, mk_o, mv_o, sz_o, mq_o, sg_o, sm_o) = rest
    else:
        (q_o, k_o, v_o, lf_o, lb_o, mk_o, mv_o) = rest
    x = x_ref[...]
    mod = mod_ref[...]
    sh = mod[:, 0:D]
    sc = mod[:, D:2 * D]
    h = (_rms(x, gpre_ref[...]) * (1.0 + sc) + sh).astype(BF16)

    z1 = _dot(h, w1_ref[...])
    q_o[...] = (z1[:, 0:GLA_QK] * (GLA_DK ** -0.5)).astype(BF16)
    k_o[...] = z1[:, GLA_QK:2 * GLA_QK].astype(BF16)
    v_o[...] = z1[:, 2 * GLA_QK:2 * GLA_QK + GLA_V].astype(BF16)
    o_dkv = 2 * GLA_QK + GLA_V
    ckv = _rms(z1[:, o_dkv:o_dkv + MLA_KVR], gkv_ref[...])
    xs = z1[:, o_dkv + MLA_KVR:o_dkv + MLA_KVR + LANES]

    la = _log_sigmoid(_dot(xs.astype(BF16), wa_ref[...]) + ba_ref[...]) * (1.0 / GLA_TAU)
    lf_o[...] = la[:, 0:GLA_QK]
    lb_o[...] = la[:, GLA_QK:2 * GLA_QK]

    lhs_k = jnp.concatenate([ckv, xs * t1_ref[...]], axis=-1).astype(BF16)
    mk_o[...] = _dot(lhs_k, wk_ref[...]).astype(BF16)
    lane = lax.broadcasted_iota(jnp.int32, (x.shape[0], MLA_W), 1)
    mv = _dot(ckv.astype(BF16), wv_ref[...])
    mv_o[...] = jnp.where(lane % (2 * HEAD_SLOT) == SUM_LANE_EVEN, 1.0,
                          jnp.where(lane % (2 * HEAD_SLOT) == HEAD_SLOT + SUM_LANE_ODD, 1.0,
                                    mv)).astype(BF16)

    if with_q:
        zg = _dot(h, w2_ref[:, 0:GLA_V])
        sz_o[...] = (zg * _sigmoid(zg)).astype(BF16)
        n = _rms(_dot(h, w2_ref[:, GLA_V:GLA_V + MLA_QR]), gq_ref[...]).astype(BF16)
        cq = jnp.concatenate([cq_ref[...]] * MLA_H, axis=-1)
        sq = jnp.concatenate([sq_ref[...]] * MLA_H, axis=-1)
        mq_o[...] = (_dot(n, wq_ref[:, 0:MLA_W]) * cq
                     + _dot(n, wq_ref[:, MLA_W:2 * MLA_W]) * sq).astype(BF16)
        o_mg = GLA_V + MLA_QR
        sg_o[...] = _sigmoid(_dot(h, w2_ref[:, o_mg:o_mg + D])).astype(BF16)
        sm_o[...] = _sigmoid(_dot(h, w2_ref[:, o_mg + D:o_mg + 2 * D])).astype(BF16)


def _inproj_call(with_q, xf, mod3, mod_row_fn, tiles_per_seq, gpre, w1, wa, ba, wk, wv,
                 gkv, t1, extra=()):
    n_tok = xf.shape[0]
    tm = TM_PROJ
    grid = (n_tok // tm,)
    row = lambda i: (i, 0)
    tab = lambda i: (i % tiles_per_seq, 0)
    in_specs = [
        pl.BlockSpec((tm, D), row),
        pl.BlockSpec((None, 1, 6 * D), lambda i: (mod_row_fn(i), 0, 0)),
        _resident(gpre), _resident(w1), _resident(wa), _resident(ba), _resident(wk),
        _resident(wv), _resident(gkv),
        pl.BlockSpec((tm, LANES), tab),
    ]
    widths = [(GLA_QK, BF16), (GLA_QK, BF16), (GLA_V, BF16), (GLA_QK, F32), (GLA_QK, F32),
              (MLA_W, BF16), (MLA_W, BF16)]
    args = [xf, mod3, gpre, w1, wa, ba, wk, wv, gkv, t1]
    if with_q:
        w2, gq, wq, cq, sq = extra
        in_specs += [_resident(w2), _resident(gq), _resident(wq),
                     pl.BlockSpec((tm, HEAD_SLOT), tab), pl.BlockSpec((tm, HEAD_SLOT), tab)]
        args += [w2, gq, wq, cq, sq]
        widths += [(GLA_V, BF16), (MLA_W, BF16), (D, BF16), (D, BF16)]
    return pl.pallas_call(
        functools.partial(_inproj_kernel, with_q),
        grid=grid,
        in_specs=in_specs,
        out_specs=[pl.BlockSpec((tm, w), row) for w, _ in widths],
        out_shape=[jax.ShapeDtypeStruct((n_tok, w), dt) for w, dt in widths],
        compiler_params=_cparams(("arbitrary",)),
        name="inproj_x" if with_q else "inproj_ctx",
    )(*args)


def _gla_kernel(qx, kx, vx, lfx, lbx, qc, kc, vc, lfc, lbc, o_ref, sf_ref, sb_ref):
    C = GLA_CHUNK
    R = GLA_BLOCK
    n_sub = R // C
    rr = lax.broadcasted_iota(jnp.int32, (R, R), 0)
    cc = lax.broadcasted_iota(jnp.int32, (R, R), 1)
    same = (rr // C) == (cc // C)
    tbd_f = jnp.logical_and(same, cc <= rr).astype(BF16)
    tbd_b = jnp.logical_and(same, cc >= rr).astype(BF16)
    row = lax.broadcasted_iota(jnp.int32, (C, GLA_QK), 0)
    col = lax.broadcasted_iota(jnp.int32, (C, GLA_QK), 1)
    head_qk = col // GLA_DK
    tri4_f = (col % C) <= row
    tri4_b = (col % C) >= row
    head_v = lax.broadcasted_iota(jnp.int32, (C, GLA_V), 1) // GLA_DV

    def stack_masked(x, head_of_lane):
        return jnp.concatenate(
            [jnp.where(head_of_lane == hh, x, jnp.zeros_like(x)) for hh in range(GLA_H)], axis=0)

    def block(q_ref, k_ref, v_ref, la_ref, row0, fwd, s_ref, emit):
        sl = pl.ds(row0, R)
        la = la_ref[sl, :]
        la_hi = la.astype(BF16)
        la_lo = (la - la_hi.astype(F32)).astype(BF16)
        tbd = tbd_f if fwd else tbd_b
        cum = _dot(tbd, la_hi) + _dot(tbd, la_lo)
        k = k_ref[sl, :].astype(F32)
        v = v_ref[sl, :]
        if emit:
            q = q_ref[sl, :].astype(F32)
        st = s_ref[...]
        outs = [None] * n_sub
        for ci in (range(n_sub) if fwd else reversed(range(n_sub))):
            rs = slice(ci * C, (ci + 1) * C)
            cum_c = cum[rs]
            tot = cum_c[C - 1:C, :] if fwd else cum_c[0:1, :]
            k_c = k[rs]
            v_c = v[rs]
            ke_bd = stack_masked((k_c * jnp.exp(tot - cum_c)).astype(BF16), head_qk)
            v_stack = jnp.concatenate(
                [v_c[:, hh * GLA_DV:(hh + 1) * GLA_DV] for hh in range(GLA_H)], axis=0)
            if emit:
                qd = (q[rs] * jnp.exp(cum_c)).astype(BF16)
                ki_bd = stack_masked((k_c * jnp.exp(-cum_c)).astype(BF16), head_qk)
                att = _dot_nt(qd, ki_bd)
                att = jnp.where(tri4_f if fwd else tri4_b, att, 0.0).astype(BF16)
                o_intra = _dot(att, stack_masked(v_c, head_v))
                oi = _dot_nt(stack_masked(qd, head_qk), st.astype(BF16))
                o_inter = jnp.concatenate([oi[hh * C:(hh + 1) * C] for hh in range(GLA_H)],
                                          axis=-1)
                outs[ci] = o_intra + o_inter
            st = st * jnp.exp(tot) + _dot_tn(v_stack, ke_bd)
        s_ref[...] = st
        if emit:
            o_ref[sl, :] += jnp.concatenate(outs, axis=0)

    sf_ref[...] = jnp.zeros_like(sf_ref)
    sb_ref[...] = jnp.zeros_like(sb_ref)
    o_ref[...] = jnp.zeros_like(o_ref)
    n_ctx = qc.shape[0] // R
    n_x = qx.shape[0] // R
    for i in range(n_ctx):
        block(qc, kc, vc, lfc, i * R, True, sf_ref, False)
        block(qc, kc, vc, lbc, (n_ctx - 1 - i) * R, False, sb_ref, False)

    def body(i, carry):
        block(qx, kx, vx, lfx, pl.multiple_of(i * R, R), True, sf_ref, True)
        block(qx, kx, vx, lbx, pl.multiple_of((n_x - 1 - i) * R, R), False, sb_ref, True)
        return carry

    lax.fori_loop(0, n_x, body, 0)


def _gla_call(fx, fc, batch, seq, ctx_len):
    qx, kx, vx, lfx, lbx = fx
    qc, kc, vc, lfc, lbc = fc

    def spec(rows, w):
        return pl.BlockSpec((rows, w), lambda b: (b, 0))

    return pl.pallas_call(
        _gla_kernel,
        grid=(batch,),
        in_specs=[spec(seq, GLA_QK), spec(seq, GLA_QK), spec(seq, GLA_V), spec(seq, GLA_QK),
                  spec(seq, GLA_QK),
                  spec(ctx_len, GLA_QK), spec(ctx_len, GLA_QK), spec(ctx_len, GLA_V),
                  spec(ctx_len, GLA_QK), spec(ctx_len, GLA_QK)],
        out_specs=spec(seq, GLA_V),
        out_shape=jax.ShapeDtypeStruct((batch * seq, GLA_V), F32),
        scratch_shapes=[pltpu.VMEM((GLA_DV, GLA_QK), F32), pltpu.VMEM((GLA_DV, GLA_QK), F32)],
        compiler_params=_cparams(("arbitrary",)),
        name="gla",
    )(qx, kx, vx, lfx, lbx, qc, kc, vc, lfc, lbc)


def _mla_kernel(q_ref, kx_ref, vx_ref, kc_ref, vc_ref, o_ref):
    lane = lax.broadcasted_iota(jnp.int32, (q_ref.shape[0], HEAD_SLOT), 1)
    for j in range(MLA_H // 2):
        pair = []
        for hh, sum_lane in ((2 * j, SUM_LANE_EVEN), (2 * j + 1, SUM_LANE_ODD)):
            sl = slice(hh * HEAD_SLOT, (hh + 1) * HEAD_SLOT)
            q = q_ref[:, sl]
            sx = _dot_nt(q, kx_ref[:, sl])
            sc = _dot_nt(q, kc_ref[:, sl])
            m = jnp.maximum(jnp.max(sx, axis=-1, keepdims=True),
                            jnp.max(sc, axis=-1, keepdims=True))
            px = jnp.exp2(sx - m).astype(BF16)
            pc = jnp.exp2(sc - m).astype(BF16)
            o = _dot(px, vx_ref[:, sl]) + _dot(pc, vc_ref[:, sl])
            pair.append(o / o[:, sum_lane:sum_lane + 1])
        o_ref[:, j * HEAD_SLOT:(j + 1) * HEAD_SLOT] = jnp.where(
            lane < MLA_V, pair[0], pair[1]).astype(BF16)


def _mla_call(mq, mkx, mvx, mkc, mvc, batch, seq, ctx_len):
    nq = seq // TQ
    return pl.pallas_call(
        _mla_kernel,
        grid=(batch, nq),
        in_specs=[pl.BlockSpec((TQ, MLA_W), lambda b, i: (b * nq + i, 0)),
                  pl.BlockSpec((seq, MLA_W), lambda b, i: (b, 0)),
                  pl.BlockSpec((seq, MLA_W), lambda b, i: (b, 0)),
                  pl.BlockSpec((ctx_len, MLA_W), lambda b, i: (b, 0)),
                  pl.BlockSpec((ctx_len, MLA_W), lambda b, i: (b, 0))],
        out_specs=pl.BlockSpec((TQ, MLA_H * MLA_V), lambda b, i: (b * nq + i, 0)),
        out_shape=jax.ShapeDtypeStruct((batch * seq, MLA_H * MLA_V), BF16),
        compiler_params=_cparams(("arbitrary", "arbitrary")),
        name="mla",
    )(mq, mkx, mvx, mkc, mvc)


def _mixout_kernel(og_ref, sz_ref, om_ref, sg_ref, sm_ref, x_ref, mod_ref, gn_ref, wbg_ref,
                   wbm_ref, wo_ref, gpost_ref, gffn_ref, wr_ref, br_ref,
                   x1_o, h2_o, route_o, routet_o, cnt_o):
    tm = x_ref.shape[0]
    mod = mod_ref[...]
    gt_a = mod[:, 2 * D:3 * D]
    sh_f = mod[:, 3 * D:4 * D]
    sc_f = mod[:, 4 * D:5 * D]

    og = og_ref[...]
    gn = gn_ref[...]
    parts = [_rms(og[:, hh * GLA_DV:(hh + 1) * GLA_DV], gn) for hh in range(GLA_H)]
    a = (jnp.concatenate(parts, axis=-1) * sz_ref[...].astype(F32)).astype(BF16)
    br_g = _dot(a, wbg_ref[...])
    br_m = _dot(om_ref[...], wbm_ref[...])
    merged = (sg_ref[...].astype(F32) * br_g + sm_ref[...].astype(F32) * br_m).astype(BF16)
    mo = _dot(merged, wo_ref[...])
    x1 = x_ref[...] + gt_a * _rms(mo, gpost_ref[...])
    x1_o[...] = x1
    h2 = _rms(x1, gffn_ref[...]) * (1.0 + sc_f) + sh_f
    h2_o[...] = h2.astype(BF16)

    h_hi = h2.astype(BF16)
    h_lo = (h2 - h_hi.astype(F32)).astype(BF16)
    part = _dot_nt(wr_ref[...], h_hi) + _dot_nt(wr_ref[...], h_lo)
    logits = part[0:LANES, :] + part[LANES:2 * LANES, :] + br_ref[...]
    row = lax.broadcasted_iota(jnp.int32, (LANES, tm), 0)
    row_f = row.astype(F32)
    neg = jnp.float32(-jnp.inf)
    lg = jnp.where(row < N_EXP, logits, neg)
    hots, vals, ids = [], [], []
    for _k in range(TOP_K):
        mx = jnp.max(lg, axis=0, keepdims=True)
        idx = jnp.min(jnp.where(lg == mx, row_f, float(LANES)), axis=0, keepdims=True)
        hot = row_f == idx
        lg = jnp.where(hot, neg, lg)
        hots.append(hot)
        vals.append(mx)
        ids.append(idx)
    es = [jnp.exp(v - vals[0]) for v in vals]
    den = es[0] + es[1] + es[2] + es[3]
    ws = [e / den for e in es]

    msum = jnp.zeros((LANES, tm), F32)
    for hot in hots:
        msum = msum + hot.astype(F32)
    rr = lax.broadcasted_iota(jnp.int32, (tm, tm), 0)
    cc = lax.broadcasted_iota(jnp.int32, (tm, tm), 1)
    upper = (rr < cc).astype(BF16)
    prior = _dot(msum.astype(BF16), upper)
    cnt = jnp.broadcast_to(jnp.sum(msum, axis=1, keepdims=True), (LANES, LANES))
    seg = jnp.floor((cnt + (SUBLANES - 1.0)) * (1.0 / SUBLANES)) * SUBLANES
    er = lax.broadcasted_iota(jnp.int32, (LANES, LANES), 0)
    ec = lax.broadcasted_iota(jnp.int32, (LANES, LANES), 1)
    seg_start = _dot((ec < er).astype(BF16), seg.astype(BF16))
    slot_of = prior + jnp.concatenate([seg_start] * (tm // LANES), axis=1)

    rows = list(ids) + ws + [jnp.sum(hot.astype(F32) * slot_of, axis=0, keepdims=True)
                             for hot in hots]
    route_t = jnp.concatenate(rows + [jnp.zeros((LANES - 3 * TOP_K, tm), F32)], axis=0)
    routet_o[...] = route_t[0:ROUTE_ROWS, :]
    route_o[...] = route_t.T
    cnt_o[...] = cnt


def _mixout_call(og, sz, om, sg, sm, xf, mod3, tiles_per_seq, gn, wbg, wbm, wo, gpost, gffn,
                 wr, br):
    n_tok = xf.shape[0]
    tm = TM_PROJ
    row = lambda i: (i, 0)

    def rs(w):
        return pl.BlockSpec((tm, w), row)

    return pl.pallas_call(
        _mixout_kernel,
        grid=(n_tok // tm,),
        in_specs=[rs(GLA_V), rs(GLA_V), rs(MLA_H * MLA_V), rs(D), rs(D), rs(D),
                  pl.BlockSpec((None, 1, 6 * D), lambda i: (i // tiles_per_seq, 0, 0)),
                  _resident(gn), _resident(wbg), _resident(wbm), _resident(wo),
                  _resident(gpost), _resident(gffn), _resident(wr), _resident(br)],
        out_specs=[rs(D), rs(D), rs(LANES),
                   pl.BlockSpec((None, ROUTE_ROWS, tm), lambda i: (i, 0, 0)),
                   pl.BlockSpec((None, LANES, LANES), lambda i: (i, 0, 0))],
        out_shape=[jax.ShapeDtypeStruct((n_tok, D), F32), jax.ShapeDtypeStruct((n_tok, D), BF16),
                   jax.ShapeDtypeStruct((n_tok, LANES), F32),
                   jax.ShapeDtypeStruct((n_tok // tm, ROUTE_ROWS, tm), F32),
                   jax.ShapeDtypeStruct((n_tok // tm, LANES, LANES), F32)],
        compiler_params=_cparams(("arbitrary",)),
        name="mixout",
    )(og, sz, om, sg, sm, xf, mod3, gn, wbg, wbm, wo, gpost, gffn, wr, br)


def _sorted_rows(n_tok):
    n_steps = n_tok // TM_PROJ
    worst_pad = n_steps * N_EXP * (SUBLANES - 1) + N_EXP * TM_FFN
    per_step = -(-worst_pad // (n_steps * ZERO_ROWS)) * ZERO_ROWS
    n_rows = n_tok * TOP_K + per_step * n_steps
    assert n_rows % TM_FFN == 0
    return n_rows


def _seg_copies(src_ref, dst_ref, src0, dst0, length, sem, priority=0):
    def copy_bit(b):
        size = 1 << b
        off = (length >> (b + 1)) << (b + 1)

        @pl.when((length & size) != 0)
        def _():
            pltpu.make_async_copy(
                src_ref.at[pl.ds(pl.multiple_of(src0 + off, SUBLANES), size)],
                dst_ref.at[pl.ds(pl.multiple_of(dst0 + off, SUBLANES), size)],
                sem).start(priority=priority)

    lo_bit = SUBLANES.bit_length() - 1
    hi_bit = TM_PROJ.bit_length() - 1
    split = LONG_SEGMENT.bit_length() - 1

    @pl.when(length >= LONG_SEGMENT)
    def _():
        for b in range(hi_bit, split - 1, -1):
            copy_bit(b)

    for b in range(split - 1, lo_bit - 1, -1):
        copy_bit(b)


def _wait_rows(src_ref, dst_ref, total, sem):
    for b in reversed(range(SUBLANES.bit_length() - 1, STAGE_ROWS.bit_length())):
        size = 1 << b

        @pl.when((total & size) != 0)
        def _():
            pltpu.make_async_copy(src_ref.at[pl.ds(0, size)], dst_ref.at[pl.ds(0, size)],
                                  sem).wait()


def _zero_rows(zero_ref, dst_ref, start, end, sem, wait):
    length = end - start
    for b in reversed(range(SUBLANES.bit_length() - 1, ZERO_ROWS.bit_length())):
        size = 1 << b
        off = (length >> (b + 1)) << (b + 1)

        @pl.when((length & size) != 0)
        def _():
            cp = pltpu.make_async_copy(
                zero_ref.at[pl.ds(0, size)],
                dst_ref.at[pl.ds(pl.multiple_of(start + off, SUBLANES), size)], sem)
            if wait:
                cp.wait()
            else:
                cp.start()


def _slot_onehot(route_t, n_slots):
    tm = route_t.shape[1]
    slot = lax.broadcasted_iota(jnp.int32, (n_slots, tm), 0).astype(F32)
    acc = jnp.zeros((n_slots, tm), F32)
    for kk in range(TOP_K):
        acc = jnp.where(slot == route_t[ROUTE_P + kk:ROUTE_P + kk + 1, :], 1.0, acc)
    return acc.astype(BF16)


def _dispatch_kernel(meta_ref, pad_ref, route_ref, h_ref, xs_out, stage, zero_ref, sems, zsem):
    i = pl.program_id(0)
    n_steps = pl.num_programs(0)
    n_rows = xs_out.shape[0]
    n_seg = n_steps * N_EXP
    slot = i % 2

    @pl.when(i == 0)
    def _():
        zero_ref[...] = jnp.zeros_like(zero_ref)

    def clear_uncovered(wait):
        per = -(-N_EXP // n_steps)
        for j in range(per):
            e = i * per + j

            @pl.when(e < N_EXP)
            def _():
                ec = jnp.minimum(e, N_EXP - 1)
                _zero_rows(zero_ref, xs_out, pad_ref[ec], pad_ref[N_EXP + 1 + ec], zsem, wait)

        total_end = pad_ref[N_EXP]
        tail_rows = n_rows - n_steps * TM_PROJ * TOP_K
        per_step = tail_rows // n_steps
        for j in range(per_step // ZERO_ROWS):
            seg_end = (n_rows - tail_rows) + i * per_step + (j + 1) * ZERO_ROWS
            zlen = jnp.clip(seg_end - total_end, 0, ZERO_ROWS)
            _zero_rows(zero_ref, xs_out, seg_end - zlen, seg_end, zsem, wait)

    clear_uncovered(wait=False)

    onehot = _slot_onehot(route_ref[...], STAGE_ROWS)
    stage[slot] = _dot(onehot, h_ref[...])

    def issue(e, carry):
        s = i * N_EXP + e
        _seg_copies(stage.at[slot], xs_out, meta_ref[s], meta_ref[2 * n_seg + s],
                    meta_ref[n_seg + s], sems.at[slot])
        return carry

    lax.fori_loop(0, N_EXP, issue, 0)
    clear_uncovered(wait=True)

    @pl.when(i >= 1)
    def _():
        _wait_rows(stage.at[1 - slot], xs_out, meta_ref[3 * n_seg + i - 1], sems.at[1 - slot])

    @pl.when(i == n_steps - 1)
    def _():
        _wait_rows(stage.at[slot], xs_out, meta_ref[3 * n_seg + i], sems.at[slot])


def _dispatch_call(meta, pad_info, route, h2, n_rows):
    n_tok = h2.shape[0]
    tm = TM_PROJ
    n_steps = n_tok // tm
    assert (n_rows - n_tok * TOP_K) % (n_steps * ZERO_ROWS) == 0 and TM_FFN <= 2 * ZERO_ROWS
    grid_spec = pltpu.PrefetchScalarGridSpec(
        num_scalar_prefetch=2,
        grid=(n_steps,),
        in_specs=[pl.BlockSpec((None, ROUTE_ROWS, tm), lambda i, m, p: (i, 0, 0)),
                  pl.BlockSpec((tm, D), lambda i, m, p: (i, 0))],
        out_specs=pl.BlockSpec(memory_space=pl.ANY),
        scratch_shapes=[pltpu.VMEM((2, STAGE_ROWS, D), F32), pltpu.VMEM((ZERO_ROWS, D), F32),
                        pltpu.SemaphoreType.DMA((2,)), pltpu.SemaphoreType.DMA(())],
    )
    return pl.pallas_call(
        _dispatch_kernel,
        grid_spec=grid_spec,
        out_shape=jax.ShapeDtypeStruct((n_rows, D), F32),
        compiler_params=_cparams(("arbitrary",)),
        name="dispatch",
    )(meta, pad_info, route, h2)


def _ffn_kernel(te_ref, nx_ref, vr_ref, nu_ref, x_ref, wg_hbm, wu_hbm, wd_hbm, bg0, bu0, bd0, bg1,
                bu1, bd1, y_ref, land, wgb, wub, wdb, sems):
    i = pl.program_id(0)
    tm = TM_FFN
    t0 = 2 * i
    t1 = t0 + 1
    e0 = te_ref[t0]
    e1 = te_ref[t1]
    used0 = t0 < nu_ref[0]
    used1 = t1 < nu_ref[0]
    first0 = jnp.logical_or(i == 0, e0 != te_ref[jnp.maximum(t0 - 1, 0)])
    same = jnp.logical_and(used1, e1 == e0)
    n_sub = tm // FFN_SUB
    parts0 = (vr_ref[t0] + FFN_SUB - 1) // FFN_SUB
    parts1 = (vr_ref[t1] + FFN_SUB - 1) // FFN_SUB
    joint = jnp.logical_and(same, parts1 == n_sub)
    w_hbm = (wg_hbm, wu_hbm, wd_hbm)

    def fetch(expert):
        for j in range(3):
            pltpu.make_async_copy(w_hbm[j].at[expert], land.at[j], sems.at[j]).start()

    def switch(t):
        for j, dst in enumerate((wgb, wub, wdb)):
            pltpu.make_async_copy(w_hbm[j].at[0], land.at[j], sems.at[j]).wait()
            dst[...] = land[j].astype(BF16)
        nxt = nx_ref[t]

        @pl.when(nxt >= 0)
        def _():
            fetch(jnp.maximum(nxt, 0))

    def compute(rows, bg, bu, bd):
        x = x_ref[rows, :].astype(BF16)
        gate = jnp.minimum(_dot(x, wgb[...]) + bg[...], SWIGLU_LIMIT)
        up = jnp.clip(_dot(x, wub[...]) + bu[...], -SWIGLU_LIMIT, SWIGLU_LIMIT)
        act = ((up + 1.0) * gate * _sigmoid(SWIGLU_ALPHA * gate)).astype(BF16)
        y_ref[rows, :] = _dot(act, wdb[...]) + bd[...]

    lo = slice(0, tm)
    hi = slice(tm, 2 * tm)

    @pl.when(jnp.logical_and(used0, i == 0))
    def _():
        fetch(e0)

    @pl.when(jnp.logical_and(used0, first0))
    def _():
        switch(t0)

    def single(off, parts, bg, bu, bd):
        for n in range(1, n_sub + 1):
            @pl.when(parts == n)
            def _():
                compute(slice(off, off + n * FFN_SUB), bg, bu, bd)
                if n < n_sub:
                    y_ref[off + n * FFN_SUB:off + tm, :] = jnp.zeros((tm - n * FFN_SUB, D),
                                                                     y_ref.dtype)

    @pl.when(joint)
    def _():
        compute(slice(0, 2 * tm), bg0, bu0, bd0)

    @pl.when(jnp.logical_and(used0, jnp.logical_not(joint)))
    def _():
        single(0, parts0, bg0, bu0, bd0)

    @pl.when(jnp.logical_and(used1, jnp.logical_not(joint)))
    def _():
        @pl.when(jnp.logical_not(same))
        def _():
            switch(t1)

        single(tm, parts1, bg1, bu1, bd1)

    @pl.when(jnp.logical_not(used0))
    def _():
        y_ref[lo, :] = jnp.zeros((tm, D), y_ref.dtype)

    @pl.when(jnp.logical_not(used1))
    def _():
        y_ref[hi, :] = jnp.zeros((tm, D), y_ref.dtype)


def _ffn_call(tile_exp, next_exp, tile_rows, n_used, xs, w_gate, w_up, w_down, b_gate, b_up,
              b_down):
    n_rows = xs.shape[0]
    tm = TM_FFN
    n_tiles = n_rows // tm
    assert D == D_FF
    biases = (b_gate.reshape(N_EXP, 1, D_FF), b_up.reshape(N_EXP, 1, D_FF),
              b_down.reshape(N_EXP, 1, D))

    assert n_tiles % 2 == 0

    def xrow(i, te, nx, vr, nu):
        return (jnp.maximum(jnp.minimum(i, (nu[0] - 1) // 2), 0), 0)

    def bias(k):
        return lambda i, te, nx, vr, nu: (te[2 * i + k], 0, 0)

    grid_spec = pltpu.PrefetchScalarGridSpec(
        num_scalar_prefetch=4,
        grid=(n_tiles // 2,),
        in_specs=[pl.BlockSpec((2 * tm, D), xrow),
                  pl.BlockSpec(memory_space=pl.ANY),
                  pl.BlockSpec(memory_space=pl.ANY),
                  pl.BlockSpec(memory_space=pl.ANY),
                  pl.BlockSpec((None, 1, D_FF), bias(0)),
                  pl.BlockSpec((None, 1, D_FF), bias(0)),
                  pl.BlockSpec((None, 1, D), bias(0)),
                  pl.BlockSpec((None, 1, D_FF), bias(1)),
                  pl.BlockSpec((None, 1, D_FF), bias(1)),
                  pl.BlockSpec((None, 1, D), bias(1))],
        out_specs=pl.BlockSpec((2 * tm, D), lambda i, te, nx, vr, nu: (i, 0)),
        scratch_shapes=[pltpu.VMEM((3, D, D_FF), F32),
                        pltpu.VMEM((D, D_FF), BF16), pltpu.VMEM((D, D_FF), BF16),
                        pltpu.VMEM((D_FF, D), BF16), pltpu.SemaphoreType.DMA((3,))],
    )
    return pl.pallas_call(
        _ffn_kernel,
        grid_spec=grid_spec,
        out_shape=jax.ShapeDtypeStruct((n_rows, D), F32),
        compiler_params=_cparams(("arbitrary",)),
        name="ffn",
    )(tile_exp, next_exp, tile_rows, n_used, xs, w_gate, w_up, w_down, *(2 * biases))


def _combine_kernel(meta_ref, y_hbm, route_ref, x1_ref, mod_ref, gpost_ref, o_ref, stage, sems):
    i = pl.program_id(0)
    n_steps = pl.num_programs(0)
    n_seg = n_steps * N_EXP
    tm = x1_ref.shape[0]
    slot = i % 2

    def fetch(tile, into):
        def issue(e, carry):
            s = tile * N_EXP + e
            _seg_copies(y_hbm, stage.at[into], meta_ref[2 * n_seg + s], meta_ref[s],
                        meta_ref[n_seg + s], sems.at[into], priority=SECOND_DMA_QUEUE)
            return carry

        lax.fori_loop(0, N_EXP, issue, 0)

    @pl.when(i == 0)
    def _():
        stage[...] = jnp.zeros_like(stage)
        fetch(0, 0)

    @pl.when(i + 1 < n_steps)
    def _():
        fetch(i + 1, 1 - slot)

    _wait_rows(y_hbm, stage.at[slot], meta_ref[3 * n_seg + i], sems.at[slot])

    route = route_ref[...]
    col = lax.broadcasted_iota(jnp.int32, (tm, STAGE_ROWS), 1).astype(F32)
    wmat = jnp.zeros((tm, STAGE_ROWS), F32)
    for kk in range(TOP_K):
        hit = col == route[:, ROUTE_P + kk:ROUTE_P + kk + 1]
        wmat = jnp.where(hit, route[:, ROUTE_W + kk:ROUTE_W + kk + 1], wmat)
    moe = _dot(wmat.astype(BF16), stage[slot].astype(BF16))
    gt_f = mod_ref[...][:, 5 * D:6 * D]
    o_ref[...] = x1_ref[...] + gt_f * _rms(moe, gpost_ref[...])


def _combine_call(meta, y, route, x1, mod3, tiles_per_seq, gpost):
    n_tok = x1.shape[0]
    tm = TM_PROJ
    grid_spec = pltpu.PrefetchScalarGridSpec(
        num_scalar_prefetch=1,
        grid=(n_tok // tm,),
        in_specs=[pl.BlockSpec(memory_space=pl.ANY),
                  pl.BlockSpec((tm, LANES), lambda i, m: (i, 0)),
                  pl.BlockSpec((tm, D), lambda i, m: (i, 0)),
                  pl.BlockSpec((None, 1, 6 * D), lambda i, m: (i // tiles_per_seq, 0, 0)),
                  pl.BlockSpec((1, D), lambda i, m: (0, 0))],
        out_specs=pl.BlockSpec((tm, D), lambda i, m: (i, 0)),
        scratch_shapes=[pltpu.VMEM((2, STAGE_ROWS, D), F32), pltpu.SemaphoreType.DMA((2,))],
    )
    return pl.pallas_call(
        _combine_kernel,
        grid_spec=grid_spec,
        out_shape=jax.ShapeDtypeStruct((n_tok, D), F32),
        compiler_params=_cparams(("arbitrary",)),
        name="combine",
    )(meta, y, route, x1, mod3, gpost)


def _rope_swap(w):
    q = MLA_ROPE // 4
    return jnp.concatenate([-w[..., q:2 * q], w[..., 0:q], -w[..., 3 * q:4 * q],
                            w[..., 2 * q:3 * q]], axis=-1)


def _rope_tables(seq):
    rows = seq // GRID_W
    r, col = np.meshgrid(np.arange(rows, dtype=np.float64), np.arange(GRID_W, dtype=np.float64),
                         indexing="ij")
    half = MLA_ROPE // 2
    inv_freq = ROPE_BASE ** (-np.arange(0, half, 2, dtype=np.float64) / half)
    ar = r.reshape(-1)[:, None] * inv_freq
    ac = col.reshape(-1)[:, None] * inv_freq
    cos = np.concatenate([np.cos(ar), np.cos(ar), np.cos(ac), np.cos(ac)], axis=-1)
    sin = np.concatenate([np.sin(ar), np.sin(ar), np.sin(ac), np.sin(ac)], axis=-1)
    return cos, sin


def _head_slots(w3, lead):
    rows, heads, width = w3.shape
    return jnp.pad(w3, ((0, 0), (0, 0), (lead, HEAD_SLOT - lead - width))).reshape(
        rows, heads * HEAD_SLOT)


def kernel(x, c, ctx, c_ctx, w_mod, b_mod, g_pre_mix, g_post_mix, g_pre_ffn, g_post_ffn, w_in,
           gla_w_a2_f, gla_b_a_f, gla_w_a2_b, gla_b_a_b, gla_g_norm, mla_g_q, mla_w_uq, mla_g_kv,
           mla_w_uk, mla_w_uv, w_br_gla, w_br_mla, w_out, router_w, router_b, w_gate, b_gate,
           w_up, b_up, w_down, b_down):
    depth = w_mod.shape[0]
    assert depth == 1, "single-layer block"
    batch, seq, d = x.shape
    ctx_len = ctx.shape[1]
    assert d == D and seq % TM_PROJ == 0 and (batch * ctx_len) % TM_PROJ == 0
    assert TM_PROJ % ctx_len == 0 or ctx_len % TM_PROJ == 0
    n_tok = batch * seq

    cc = jnp.zeros((16, D), F32).at[:batch].set(c).at[batch].set(c_ctx)
    mod = _mod_call(cc, w_mod[0], b_mod[0])
    mod3 = mod.reshape(16, 1, 6 * D)

    wi = w_in[0]
    kr = wi[:, O_KR:O_KR + MLA_ROPE]
    small = jnp.concatenate([wi[:, O_AF:O_AF + GLA_RANK], wi[:, O_AB:O_AB + GLA_RANK], kr,
                             _rope_swap(kr), jnp.zeros((D, LANES - X_KRS - MLA_ROPE), F32)],
                            axis=1)
    w1 = jnp.concatenate([wi[:, O_Q:O_G], wi[:, O_DKV:O_DKV + MLA_KVR], small], axis=1).astype(BF16)
    w2 = jnp.concatenate([wi[:, O_G:O_G + GLA_V], wi[:, O_DQ:O_DQ + MLA_QR], wi[:, O_MG:]],
                         axis=1).astype(BF16)
    wa = jnp.zeros((LANES, 2 * GLA_QK), F32)
    wa = wa.at[X_AF:X_AF + GLA_RANK, 0:GLA_QK].set(gla_w_a2_f[0])
    wa = wa.at[X_AB:X_AB + GLA_RANK, GLA_QK:].set(gla_w_a2_b[0]).astype(BF16)
    ba = jnp.concatenate([gla_b_a_f[0], gla_b_a_b[0]]).reshape(1, 2 * GLA_QK)

    uk = mla_w_uk[0]
    uv = mla_w_uv[0]
    uq = mla_w_uq[0]
    wk_top = _head_slots(uk.reshape(MLA_KVR, MLA_H, MLA_NOPE), 0)
    place = np.zeros((LANES, MLA_H, HEAD_SLOT), np.float32)
    for j in range(MLA_ROPE):
        place[X_KR + j, :, MLA_NOPE + j] = 1.0
        place[X_KRS + j, :, MLA_NOPE + j] = 1.0
    wk = jnp.concatenate([wk_top, jnp.asarray(place.reshape(LANES, MLA_W))],
                         axis=0).astype(BF16)
    uv4 = uv.reshape(MLA_KVR, MLA_H // 2, 2, MLA_V)
    zv = jnp.zeros_like(uv4[:, :, 0])
    wv = jnp.stack([jnp.concatenate([uv4[:, :, 0], zv], axis=-1),
                    jnp.concatenate([zv, uv4[:, :, 1]], axis=-1)],
                   axis=2).reshape(MLA_KVR, MLA_W).astype(BF16)
    uq3 = uq.reshape(MLA_QR, MLA_H, MLA_QKD)
    wq_a = _head_slots(uq3, 0)
    wq_b = _head_slots(_rope_swap(uq3[:, :, MLA_NOPE:]), MLA_NOPE)
    wq = jnp.concatenate([wq_a, wq_b], axis=1).astype(BF16)

    cos, sin = _rope_tables(seq)
    zeros32 = np.zeros((seq, MLA_ROPE))
    t1_x = jnp.asarray(np.concatenate([zeros32, cos, sin, zeros32], axis=1), F32)
    t1_c = jnp.asarray(np.broadcast_to(
        np.concatenate([np.zeros(MLA_ROPE), np.ones(MLA_ROPE), np.zeros(2 * MLA_ROPE)]),
        (TM_PROJ, LANES)), F32)
    q_scale = MLA_SCALE * LOG2_E
    cq = jnp.asarray(np.concatenate([np.ones((seq, MLA_NOPE)), cos, zeros32], axis=1) * q_scale, F32)
    sq = jnp.asarray(np.concatenate([np.zeros((seq, MLA_NOPE)), sin, zeros32], axis=1) * q_scale,
                     F32)

    gpre = g_pre_mix[0].reshape(1, D)
    gkv = mla_g_kv[0].reshape(1, MLA_KVR)
    gq = mla_g_q[0].reshape(1, MLA_QR)
    tiles_per_seq = seq // TM_PROJ

    xf = x.reshape(n_tok, D)
    cf = ctx.reshape(batch * ctx_len, D)
    (gqx, gkx, gvx, lfx, lbx, mkx, mvx, sz, mq, sg, sm) = _inproj_call(
        True, xf, mod3, lambda i: i // tiles_per_seq, tiles_per_seq, gpre, w1, wa, ba, wk, wv,
        gkv, t1_x, extra=(w2, gq, wq, cq, sq))
    (gqc, gkc, gvc, lfc, lbc, mkc, mvc) = _inproj_call(
        False, cf, mod3, lambda i: batch, 1, gpre, w1, wa, ba, wk, wv, gkv, t1_c)

    og = _gla_call((gqx, gkx, gvx, lfx, lbx), (gqc, gkc, gvc, lfc, lbc), batch, seq, ctx_len)
    om = _mla_call(mq, mkx, mvx, mkc, mvc, batch, seq, ctx_len)

    wr = jnp.pad(router_w[0].T, ((0, LANES - N_EXP), (0, 0)))
    wrh = wr.astype(BF16)
    wr2 = jnp.concatenate([wrh, (wr - wrh.astype(F32)).astype(BF16)], axis=0)
    br = jnp.pad(router_b[0], (0, LANES - N_EXP)).reshape(LANES, 1)
    x1, h2, route, route_t, cnt = _mixout_call(
        og, sz, om, sg, sm, xf, mod3, tiles_per_seq, gla_g_norm[0].reshape(1, GLA_DV),
        w_br_gla[0].astype(BF16), w_br_mla[0].astype(BF16), w_out[0].astype(BF16),
        g_post_mix[0].reshape(1, D), g_pre_ffn[0].reshape(1, D), wr2, br)

    seg = -(-cnt[:, :N_EXP, 0].astype(jnp.int32) // SUBLANES) * SUBLANES
    stage_start = jnp.cumsum(seg, axis=1) - seg
    in_group = jnp.cumsum(seg, axis=0) - seg
    group = jnp.sum(seg, axis=0)
    padded = -(-group // TM_FFN) * TM_FFN
    ends = jnp.cumsum(padded)
    starts = ends - padded
    sorted_row = starts[None, :] + in_group
    meta = jnp.concatenate([stage_start.reshape(-1), seg.reshape(-1), sorted_row.reshape(-1),
                            jnp.sum(seg, axis=1)]).astype(jnp.int32)
    pad_info = jnp.concatenate([starts + group, ends[-1:], ends]).astype(jnp.int32)
    n_rows = _sorted_rows(n_tok)
    tile_start = jnp.arange(n_rows // TM_FFN, dtype=jnp.int32) * TM_FFN
    tile_exp = jnp.minimum(jnp.sum((ends[None, :] <= tile_start[:, None]).astype(jnp.int32), axis=1),
                           N_EXP - 1)
    n_used = (ends[-1:] // TM_FFN).astype(jnp.int32)
    ids = jnp.arange(N_EXP, dtype=jnp.int32)
    later = jnp.logical_and(padded[None, :] > 0, ids[None, :] > ids[:, None])
    next_owner = jnp.min(jnp.where(later, ids[None, :], N_EXP), axis=1)
    next_owner = jnp.where(next_owner == N_EXP, -1, next_owner)
    own = tile_exp[:, None] == ids[None, :]
    next_exp = jnp.sum(jnp.where(own, next_owner[None, :], 0), axis=1).astype(jnp.int32)
    data_end = jnp.sum(jnp.where(own, (starts + group)[None, :], 0), axis=1)
    tile_rows = jnp.clip(data_end - tile_start, 0, TM_FFN).astype(jnp.int32)

    xs = _dispatch_call(meta, pad_info, route_t, h2, n_rows)
    y = _ffn_call(tile_exp, next_exp, tile_rows, n_used, xs, w_gate[0], w_up[0], w_down[0],
                  b_gate[0], b_up[0], b_down[0])
    out = _combine_call(meta, y, route, x1, mod3, tiles_per_seq, g_post_ffn[0].reshape(1, D))
    return out.reshape(batch, seq, D)
```
